```python
import jax
import jax.numpy as jnp
from jax import lax
import numpy as np

D_MODEL = 1024
BATCH = 8
SEQ = 2048
DEPTH = 4

N_META = 16
D_MIX = 2 * D_MODEL
W_GRP = D_MIX // 4
CONV_K = 4
D_FF = ((8 * D_MODEL // 3 + 127) // 128) * 128
EPS = 1e-6
CHUNK = 64
LEAD_PAD = CHUNK - N_META

LRU_HEAD_DIM = 64
LRU_HEADS = W_GRP // LRU_HEAD_DIM
LRU_C = 8.0

GDN_HEAD_DIM = 128
GDN_HEADS = W_GRP // GDN_HEAD_DIM

SSD_HEAD_DIM = 64
SSD_HEADS = W_GRP // SSD_HEAD_DIM
SSD_GROUPS = 2
SSD_STATE = 128

S5_GROUP_CH = 16
S5_GROUPS = W_GRP // S5_GROUP_CH
S5_STATE = 64

IN_SPLITS = (W_GRP, W_GRP, 3 * W_GRP, W_GRP, GDN_HEADS, GDN_HEADS,
             W_GRP, W_GRP + 2 * SSD_GROUPS * SSD_STATE, SSD_HEADS, W_GRP)
D_IN = sum(IN_SPLITS)

kernel_name = "hymba_style_parallel_hybrid_trunk"

F32 = jnp.float32


def rms_norm(x, g):
    xf = x.astype(F32)
    y = xf * lax.rsqrt(jnp.mean(xf * xf, axis=-1, keepdims=True) + EPS)
    return (y * g.astype(F32)).astype(x.dtype)


def l2_norm(x):
    return x * lax.rsqrt(jnp.sum(x * x, axis=-1, keepdims=True) + EPS)


def causal_dwconv(x, w):
    ch = x.shape[-1]
    return lax.conv_general_dilated(
        x, w[:, None, :].astype(x.dtype), window_strides=(1,),
        padding=[(w.shape[0] - 1, 0)], dimension_numbers=("NWC", "WIO", "NWC"),
        feature_group_count=ch)


def swiglu(x, w_gate, w_up, w_down):
    return (jax.nn.silu(x @ w_gate) * (x @ w_up)) @ w_down


def front_pad(t, n):
    return jnp.pad(t, [(0, 0), (n, 0)] + [(0, 0)] * (t.ndim - 2))


def _linear_combine(e1, e2):
    a1, b1 = e1
    a2, b2 = e2
    return a1 * a2, a2 * b1 + b2


def _complex_linear_combine(e1, e2):
    a1r, a1i, b1r, b1i = e1
    a2r, a2i, b2r, b2i = e2
    return (a1r * a2r - a1i * a2i, a1r * a2i + a1i * a2r,
            a2r * b1r - a2i * b1i + b2r, a2r * b1i + a2i * b1r + b2i)


def rglru_mixer(u_x, u_gate, conv_w, conv_b, w_a, b_a, w_i, b_i, lam, norm_g):
    bsz, t, _ = u_x.shape
    xc = (causal_dwconv(u_x, conv_w) + conv_b).astype(F32)
    xh = xc.reshape(bsz, t, LRU_HEADS, LRU_HEAD_DIM)
    r = jax.nn.sigmoid(jnp.einsum("btsi,sij->btsj", xh, w_a.astype(F32)).reshape(bsz, t, W_GRP) + b_a)
    ig = jax.nn.sigmoid(jnp.einsum("btsi,sij->btsj", xh, w_i.astype(F32)).reshape(bsz, t, W_GRP) + b_i)
    log_a = -LRU_C * r * jax.nn.softplus(-lam.astype(F32))
    a = jnp.exp(log_a)
    b = jnp.sqrt(-jnp.expm1(2.0 * log_a)) * (ig * xc)
    _, h = lax.associative_scan(_linear_combine, (a, b), axis=1)
    y = jax.nn.gelu(u_gate.astype(F32)) * h
    return rms_norm(y, norm_g)


def chunk_gated_delta_rule(q, k, v, beta, g):
    bsz, tp, nh, dk = q.shape
    dv = v.shape[-1]
    nc = tp // CHUNK

    def blk(z):
        z = z.reshape((bsz, nc, CHUNK) + z.shape[2:])
        return jnp.moveaxis(jnp.moveaxis(z, 1, 0), 2, 3)

    q, k, v, beta, g = (blk(z) for z in (q, k, v, beta, g))
    g = jnp.cumsum(g, axis=-1)
    incl = jnp.tril(jnp.ones((CHUNK, CHUNK), dtype=bool))
    strict = jnp.tril(jnp.ones((CHUNK, CHUNK), dtype=bool), -1)
    decay = jnp.exp(jnp.where(incl, g[..., :, None] - g[..., None, :], -jnp.inf))
    k_beta = k * beta[..., None]
    lmat = jnp.where(strict, jnp.einsum("nbhcd,nbhsd->nbhcs", k_beta, k) * decay, 0.0)
    eye = jnp.eye(CHUNK, dtype=F32)
    rhs = jnp.concatenate([v * beta[..., None], k_beta * jnp.exp(g)[..., None]], axis=-1)
    sol = lax.linalg.triangular_solve(eye + lmat, rhs, left_side=True, lower=True,
                                      unit_diagonal=True)
    u, w = sol[..., :dv], sol[..., dv:]
    attn = jnp.einsum("nbhcd,nbhsd->nbhcs", q, k) * decay
    q_dec = q * jnp.exp(g)[..., None]
    k_dec = k * jnp.exp(g[..., -1:] - g)[..., None]
    last = jnp.exp(g[..., -1])

    def step(s, inp):
        u_n, w_n, attn_n, q_n, k_n, last_n = inp
        v_new = u_n - jnp.einsum("bhcd,bhde->bhce", w_n, s)
        o_n = jnp.einsum("bhcd,bhde->bhce", q_n, s) + jnp.einsum("bhcs,bhse->bhce", attn_n, v_new)
        s = s * last_n[..., None, None] + jnp.einsum("bhcd,bhce->bhde", k_n, v_new)
        return s, o_n

    s0 = jnp.zeros((bsz, nh, dk, dv), F32)
    _, o = lax.scan(step, s0, (u, w, attn, q_dec, k_dec, last))
    return jnp.moveaxis(o, 0, 1).transpose(0, 1, 3, 2, 4).reshape(bsz, tp, nh, dv)


def gdn_mixer(u_qkv, u_z, u_beta, u_alpha, conv_w, a_log, dt_bias, norm_g):
    bsz, t, _ = u_qkv.shape
    qkv = jax.nn.silu(causal_dwconv(u_qkv, conv_w).astype(F32))
    q, k, v = jnp.split(qkv, 3, axis=-1)
    hs = (bsz, t, GDN_HEADS, GDN_HEAD_DIM)
    q = l2_norm(q.reshape(hs)) * (GDN_HEAD_DIM ** -0.5)
    k = l2_norm(k.reshape(hs))
    v = v.reshape(hs)
    beta = jax.nn.sigmoid(u_beta.astype(F32))
    g = -jnp.exp(a_log.astype(F32)) * jax.nn.softplus(u_alpha.astype(F32) + dt_bias)
    o = chunk_gated_delta_rule(*(front_pad(z, LEAD_PAD) for z in (q, k, v, beta, g)))[:, LEAD_PAD:]
    o = rms_norm(o, norm_g) * jax.nn.silu(u_z.astype(F32).reshape(hs))
    return o.reshape(bsz, t, W_GRP)


def ssd_chunked(x, a_dt, bm, cm):
    bsz, tp, nh, hd = x.shape
    nc = tp // CHUNK
    hpg = nh // SSD_GROUPS
    x = x.reshape(bsz, nc, CHUNK, SSD_GROUPS, hpg, hd)
    a = a_dt.reshape(bsz, nc, CHUNK, SSD_GROUPS, hpg).transpose(0, 1, 3, 4, 2)
    bm = bm.reshape(bsz, nc, CHUNK, SSD_GROUPS, SSD_STATE)
    cm = cm.reshape(bsz, nc, CHUNK, SSD_GROUPS, SSD_STATE)
    a_cum = jnp.cumsum(a, axis=-1)
    incl = jnp.tril(jnp.ones((CHUNK, CHUNK), dtype=bool))
    lmat = jnp.exp(jnp.where(incl, a_cum[..., :, None] - a_cum[..., None, :], -jnp.inf))
    cb = jnp.einsum("bclgn,bcsgn->bcgls", cm, bm)
    y_diag = jnp.einsum("bcgls,bcgels,bcsgep->bclgep", cb, lmat, x)
    decay_states = jnp.exp(a_cum[..., -1:] - a_cum)
    states = jnp.einsum("bclgn,bcgel,bclgep->bcgepn", bm, decay_states, x)
    chunk_decay = jnp.exp(a_cum[..., -1])

    def step(s, inp):
        st, dec = inp
        return s * dec[..., None, None] + st, s

    _, s_in = lax.scan(step, jnp.zeros_like(states[:, 0]),
                       (jnp.moveaxis(states, 1, 0), jnp.moveaxis(chunk_decay, 1, 0)))
    s_in = jnp.moveaxis(s_in, 0, 1)
    y_off = jnp.einsum("bclgn,bcgepn,bcgel->bclgep", cm, s_in, jnp.exp(a_cum))
    return (y_diag + y_off).reshape(bsz, tp, nh, hd)


def ssd_mixer(u_z, u_xbc, u_dt, conv_w, conv_b, a_log, dt_bias, d_skip, norm_g):
    bsz, t, _ = u_z.shape
    xbc = jax.nn.silu((causal_dwconv(u_xbc, conv_w) + conv_b).astype(F32))
    xs, bm, cm = jnp.split(xbc, [W_GRP, W_GRP + SSD_GROUPS * SSD_STATE], axis=-1)
    xs = xs.reshape(bsz, t, SSD_HEADS, SSD_HEAD_DIM)
    bm = bm.reshape(bsz, t, SSD_GROUPS, SSD_STATE)
    cm = cm.reshape(bsz, t, SSD_GROUPS, SSD_STATE)
    dt = jax.nn.softplus(u_dt.astype(F32) + dt_bias)
    a = -jnp.exp(a_log.astype(F32))
    y = ssd_chunked(*(front_pad(z, LEAD_PAD) for z in (xs * dt[..., None], dt * a, bm, cm)))[:, LEAD_PAD:]
    y = y + d_skip[:, None] * xs
    gs = (bsz, t, SSD_GROUPS, W_GRP // SSD_GROUPS)
    y = y.reshape(gs) * jax.nn.silu(u_z.astype(F32).reshape(gs))
    y = rms_norm(y, norm_g.reshape(SSD_GROUPS, W_GRP // SSD_GROUPS))
    return y.reshape(bsz, t, W_GRP)


def s5_mixer(u, a_re, a_im, log_dt, b_re, b_im, c_re, c_im, d_skip, w_glu, norm_g):
    bsz, t, _ = u.shape
    uf = u.astype(F32)
    ug = uf.reshape(bsz, t, S5_GROUPS, S5_GROUP_CH)
    lam_re = jnp.minimum(a_re.astype(F32), -1e-4)
    lam_im = a_im.astype(F32)
    dt = jnp.exp(log_dt.astype(F32))[:, None]
    mag = jnp.exp(dt * lam_re)
    ab_re = mag * jnp.cos(dt * lam_im)
    ab_im = mag * jnp.sin(dt * lam_im)
    den = lam_re * lam_re + lam_im * lam_im
    f_re = ((ab_re - 1.0) * lam_re + ab_im * lam_im) / den
    f_im = (ab_im * lam_re - (ab_re - 1.0) * lam_im) / den
    bb_re = f_re[..., None] * b_re - f_im[..., None] * b_im
    bb_im = f_re[..., None] * b_im + f_im[..., None] * b_re
    bu_re = jnp.einsum("btgi,gpi->tbgp", ug, bb_re)
    bu_im = jnp.einsum("btgi,gpi->tbgp", ug, bb_im)
    shp = (t, 1, S5_GROUPS, S5_STATE)
    _, _, s_re, s_im = lax.associative_scan(
        _complex_linear_combine,
        (jnp.broadcast_to(ab_re, shp), jnp.broadcast_to(ab_im, shp), bu_re, bu_im), axis=0)
    y = jnp.einsum("tbgp,gip->btgi", s_re, c_re) - jnp.einsum("tbgp,gip->btgi", s_im, c_im)
    y = y.reshape(bsz, t, W_GRP) + d_skip * uf
    y = jax.nn.gelu(y)
    y = y * jax.nn.sigmoid(y @ w_glu.astype(F32))
    return rms_norm(y, norm_g)


def hybrid_mixer(h, w_in, w_out, lru_p, gdn_p, ssd_p, s5_p):
    proj = h @ w_in
    (a_x, a_gate, b_qkv, b_z, b_beta, b_alpha, c_z, c_xbc, c_dt, d_u) = jnp.split(
        proj, np.cumsum(IN_SPLITS)[:-1].tolist(), axis=-1)
    y_a = rglru_mixer(a_x, a_gate, *lru_p)
    y_b = gdn_mixer(b_qkv, b_z, b_beta, b_alpha, *gdn_p)
    y_c = ssd_mixer(c_z, c_xbc, c_dt, *ssd_p)
    y_d = s5_mixer(d_u, *s5_p)
    y = jnp.concatenate([y_a, y_b, y_c, y_d], axis=-1).astype(h.dtype)
    return y @ w_out


def _fwd_setup_inputs(seed: int = 0) -> dict:
    key = jax.random.key(seed)
    ks = iter(jax.random.split(key, 64))
    L = DEPTH

    def nrm(shape, scale):
        return jax.random.normal(next(ks), shape, F32) * scale

    def gain(shape):
        return 1.0 + 0.02 * jax.random.normal(next(ks), shape, F32)

    def unif(shape, lo, hi):
        return jax.random.uniform(next(ks), shape, F32, lo, hi)

    def dt_bias(shape):
        dt = jnp.exp(unif(shape, float(np.log(1e-3)), float(np.log(1e-1))))
        return dt + jnp.log(-jnp.expm1(-dt))

    a0 = unif((L, W_GRP), 0.9, 0.999) ** (1.0 / LRU_C)
    xbc_w = W_GRP + 2 * SSD_GROUPS * SSD_STATE
    return {
        "x": nrm((BATCH, SEQ, D_MODEL), 1.0),
        "meta_tokens": nrm((N_META, D_MODEL), 1.0),
        "ffn1_norm": gain((L, D_MODEL)),
        "ffn1_w_gate": nrm((L, D_MODEL, D_FF), D_MODEL ** -0.5),
        "ffn1_w_up": nrm((L, D_MODEL, D_FF), D_MODEL ** -0.5),
        "ffn1_w_down": nrm((L, D_FF, D_MODEL), D_FF ** -0.5),
        "mix_norm": gain((L, D_MODEL)),
        "w_in": nrm((L, D_MODEL, D_IN), D_MODEL ** -0.5),
        "w_out": nrm((L, D_MIX, D_MODEL), D_MIX ** -0.5),
        "lru_conv_w": nrm((L, CONV_K, W_GRP), CONV_K ** -0.5),
        "lru_conv_b": nrm((L, W_GRP), 0.02),
        "lru_w_a": nrm((L, LRU_HEADS, LRU_HEAD_DIM, LRU_HEAD_DIM), LRU_HEAD_DIM ** -0.5),
        "lru_b_a": nrm((L, W_GRP), 0.02),
        "lru_w_i": nrm((L, LRU_HEADS, LRU_HEAD_DIM, LRU_HEAD_DIM), LRU_HEAD_DIM ** -0.5),
        "lru_b_i": nrm((L, W_GRP), 0.02),
        "lru_lambda": jnp.log(a0) - jnp.log1p(-a0),
        "lru_norm": gain((L, W_GRP)),
        "gdn_conv_w": nrm((L, CONV_K, 3 * W_GRP), CONV_K ** -0.5),
        "gdn_a_log": jnp.log(unif((L, GDN_HEADS), 1.0, 16.0)),
        "gdn_dt_bias": dt_bias((L, GDN_HEADS)),
        "gdn_norm": gain((L, GDN_HEAD_DIM)),
        "ssd_conv_w": nrm((L, CONV_K, xbc_w), CONV_K ** -0.5),
        "ssd_conv_b": nrm((L, xbc_w), 0.02),
        "ssd_a_log": jnp.log(unif((L, SSD_HEADS), 1.0, 16.0)),
        "ssd_dt_bias": dt_bias((L, SSD_HEADS)),
        "ssd_d": gain((L, SSD_HEADS)),
        "ssd_norm": gain((L, W_GRP)),
        "s5_a_re": -0.5 + nrm((L, S5_GROUPS, S5_STATE), 0.01),
        "s5_a_im": jnp.pi * jnp.arange(S5_STATE, dtype=F32) + nrm((L, S5_GROUPS, S5_STATE), 0.01),
        "s5_log_dt": unif((L, S5_GROUPS), float(np.log(1e-3)), float(np.log(1e-1))),
        "s5_b_re": nrm((L, S5_GROUPS, S5_STATE, S5_GROUP_CH), (2 * S5_GROUP_CH) ** -0.5),
        "s5_b_im": nrm((L, S5_GROUPS, S5_STATE, S5_GROUP_CH), (2 * S5_GROUP_CH) ** -0.5),
        "s5_c_re": nrm((L, S5_GROUPS, S5_GROUP_CH, S5_STATE), (2 * S5_STATE) ** -0.5),
        "s5_c_im": nrm((L, S5_GROUPS, S5_GROUP_CH, S5_STATE), (2 * S5_STATE) ** -0.5),
        "s5_d": nrm((L, W_GRP), 1.0),
        "s5_w_glu": nrm((L, W_GRP, W_GRP), W_GRP ** -0.5),
        "s5_norm": gain((L, W_GRP)),
        "ffn2_norm": gain((L, D_MODEL)),
        "ffn2_w_gate": nrm((L, D_MODEL, D_FF), D_MODEL ** -0.5),
        "ffn2_w_up": nrm((L, D_MODEL, D_FF), D_MODEL ** -0.5),
        "ffn2_w_down": nrm((L, D_FF, D_MODEL), D_FF ** -0.5),
        "final_norm": gain((D_MODEL,)),
    }


def _fwd_reference(x, meta_tokens, ffn1_norm, ffn1_w_gate, ffn1_w_up, ffn1_w_down, mix_norm, w_in, w_out,
              lru_conv_w, lru_conv_b, lru_w_a, lru_b_a, lru_w_i, lru_b_i, lru_lambda, lru_norm,
              gdn_conv_w, gdn_a_log, gdn_dt_bias, gdn_norm,
              ssd_conv_w, ssd_conv_b, ssd_a_log, ssd_dt_bias, ssd_d, ssd_norm,
              s5_a_re, s5_a_im, s5_log_dt, s5_b_re, s5_b_im, s5_c_re, s5_c_im, s5_d, s5_w_glu, s5_norm,
              ffn2_norm, ffn2_w_gate, ffn2_w_up, ffn2_w_down, final_norm):
    bsz = x.shape[0]
    meta = jnp.broadcast_to(meta_tokens.astype(x.dtype)[None], (bsz, N_META, D_MODEL))
    h = jnp.concatenate([meta, x], axis=1)
    for l in range(DEPTH):
        h = h + 0.5 * swiglu(rms_norm(h, ffn1_norm[l]), ffn1_w_gate[l], ffn1_w_up[l], ffn1_w_down[l])
        lru_p = (lru_conv_w[l], lru_conv_b[l], lru_w_a[l], lru_b_a[l], lru_w_i[l], lru_b_i[l],
                 lru_lambda[l], lru_norm[l])
        gdn_p = (gdn_conv_w[l], gdn_a_log[l], gdn_dt_bias[l], gdn_norm[l])
        ssd_p = (ssd_conv_w[l], ssd_conv_b[l], ssd_a_log[l], ssd_dt_bias[l], ssd_d[l], ssd_norm[l])
        s5_p = (s5_a_re[l], s5_a_im[l], s5_log_dt[l], s5_b_re[l], s5_b_im[l], s5_c_re[l], s5_c_im[l],
                s5_d[l], s5_w_glu[l], s5_norm[l])
        h = h + hybrid_mixer(rms_norm(h, mix_norm[l]), w_in[l], w_out[l], lru_p, gdn_p, ssd_p, s5_p)
        h = h + 0.5 * swiglu(rms_norm(h, ffn2_norm[l]), ffn2_w_gate[l], ffn2_w_up[l], ffn2_w_down[l])
    return rms_norm(h, final_norm)[:, N_META:]


import jax as _jax
import jax.numpy as _jnp

TWIN_FORMAT = 'train_step'
FWD_PARAMS = ['x', 'meta_tokens', 'ffn1_norm', 'ffn1_w_gate', 'ffn1_w_up', 'ffn1_w_down', 'mix_norm', 'w_in', 'w_out', 'lru_conv_w', 'lru_conv_b', 'lru_w_a', 'lru_b_a', 'lru_w_i', 'lru_b_i', 'lru_lambda', 'lru_norm', 'gdn_conv_w', 'gdn_a_log', 'gdn_dt_bias', 'gdn_norm', 'ssd_conv_w', 'ssd_conv_b', 'ssd_a_log', 'ssd_dt_bias', 'ssd_d', 'ssd_norm', 's5_a_re', 's5_a_im', 's5_log_dt', 's5_b_re', 's5_b_im', 's5_c_re', 's5_c_im', 's5_d', 's5_w_glu', 's5_norm', 'ffn2_norm', 'ffn2_w_gate', 'ffn2_w_up', 'ffn2_w_down', 'final_norm']
TWIN_WEIGHTS = ['meta_tokens', 'ffn1_norm', 'ffn1_w_gate', 'ffn1_w_up', 'ffn1_w_down', 'mix_norm', 'w_in', 'w_out', 'lru_conv_w', 'lru_conv_b', 'lru_w_a', 'lru_b_a', 'lru_w_i', 'lru_b_i', 'lru_lambda', 'lru_norm', 'gdn_conv_w', 'gdn_a_log', 'gdn_dt_bias', 'gdn_norm', 'ssd_conv_w', 'ssd_conv_b', 'ssd_a_log', 'ssd_dt_bias', 'ssd_d', 'ssd_norm', 's5_a_re', 's5_a_im', 's5_log_dt', 's5_b_re', 's5_b_im', 's5_c_re', 's5_c_im', 's5_d', 's5_w_glu', 's5_norm', 'ffn2_norm', 'ffn2_w_gate', 'ffn2_w_up', 'ffn2_w_down', 'final_norm']
TWIN_DIFF_INPUT = 'x'
TWIN_INPUTS = ['x', 'meta_tokens', 'ffn1_norm', 'ffn1_w_gate', 'ffn1_w_up', 'ffn1_w_down', 'mix_norm', 'w_in', 'w_out', 'lru_conv_w', 'lru_conv_b', 'lru_w_a', 'lru_b_a', 'lru_w_i', 'lru_b_i', 'lru_lambda', 'lru_norm', 'gdn_conv_w', 'gdn_a_log', 'gdn_dt_bias', 'gdn_norm', 'ssd_conv_w', 'ssd_conv_b', 'ssd_a_log', 'ssd_dt_bias', 'ssd_d', 'ssd_norm', 's5_a_re', 's5_a_im', 's5_log_dt', 's5_b_re', 's5_b_im', 's5_c_re', 's5_c_im', 's5_d', 's5_w_glu', 's5_norm', 'ffn2_norm', 'ffn2_w_gate', 'ffn2_w_up', 'ffn2_w_down', 'final_norm', 'loss_target', 'm_meta_tokens', 'm_ffn1_norm', 'm_ffn1_w_gate', 'm_ffn1_w_up', 'm_ffn1_w_down', 'm_mix_norm', 'm_w_in', 'm_w_out', 'm_lru_conv_w', 'm_lru_conv_b', 'm_lru_w_a', 'm_lru_b_a', 'm_lru_w_i', 'm_lru_b_i', 'm_lru_lambda', 'm_lru_norm', 'm_gdn_conv_w', 'm_gdn_a_log', 'm_gdn_dt_bias', 'm_gdn_norm', 'm_ssd_conv_w', 'm_ssd_conv_b', 'm_ssd_a_log', 'm_ssd_dt_bias', 'm_ssd_d', 'm_ssd_norm', 'm_s5_a_re', 'm_s5_a_im', 'm_s5_log_dt', 'm_s5_b_re', 'm_s5_b_im', 'm_s5_c_re', 'm_s5_c_im', 'm_s5_d', 'm_s5_w_glu', 'm_s5_norm', 'm_ffn2_norm', 'm_ffn2_w_gate', 'm_ffn2_w_up', 'm_ffn2_w_down', 'm_final_norm', 'v_meta_tokens', 'v_ffn1_norm', 'v_ffn1_w_gate', 'v_ffn1_w_up', 'v_ffn1_w_down', 'v_mix_norm', 'v_w_in', 'v_w_out', 'v_lru_conv_w', 'v_lru_conv_b', 'v_lru_w_a', 'v_lru_b_a', 'v_lru_w_i', 'v_lru_b_i', 'v_lru_lambda', 'v_lru_norm', 'v_gdn_conv_w', 'v_gdn_a_log', 'v_gdn_dt_bias', 'v_gdn_norm', 'v_ssd_conv_w', 'v_ssd_conv_b', 'v_ssd_a_log', 'v_ssd_dt_bias', 'v_ssd_d', 'v_ssd_norm', 'v_s5_a_re', 'v_s5_a_im', 'v_s5_log_dt', 'v_s5_b_re', 'v_s5_b_im', 'v_s5_c_re', 'v_s5_c_im', 'v_s5_d', 'v_s5_w_glu', 'v_s5_norm', 'v_ffn2_norm', 'v_ffn2_w_gate', 'v_ffn2_w_up', 'v_ffn2_w_down', 'v_final_norm']
TWIN_OUTPUTS = ['loss', 'grad_x', 'grad_meta_tokens', 'grad_ffn1_norm', 'grad_ffn1_w_gate', 'grad_ffn1_w_up', 'grad_ffn1_w_down', 'grad_mix_norm', 'grad_w_in', 'grad_w_out', 'grad_lru_conv_w', 'grad_lru_conv_b', 'grad_lru_w_a', 'grad_lru_b_a', 'grad_lru_w_i', 'grad_lru_b_i', 'grad_lru_lambda', 'grad_lru_norm', 'grad_gdn_conv_w', 'grad_gdn_a_log', 'grad_gdn_dt_bias', 'grad_gdn_norm', 'grad_ssd_conv_w', 'grad_ssd_conv_b', 'grad_ssd_a_log', 'grad_ssd_dt_bias', 'grad_ssd_d', 'grad_ssd_norm', 'grad_s5_a_re', 'grad_s5_a_im', 'grad_s5_log_dt', 'grad_s5_b_re', 'grad_s5_b_im', 'grad_s5_c_re', 'grad_s5_c_im', 'grad_s5_d', 'grad_s5_w_glu', 'grad_s5_norm', 'grad_ffn2_norm', 'grad_ffn2_w_gate', 'grad_ffn2_w_up', 'grad_ffn2_w_down', 'grad_final_norm', 'delta_meta_tokens', 'delta_ffn1_norm', 'delta_ffn1_w_gate', 'delta_ffn1_w_up', 'delta_ffn1_w_down', 'delta_mix_norm', 'delta_w_in', 'delta_w_out', 'delta_lru_conv_w', 'delta_lru_conv_b', 'delta_lru_w_a', 'delta_lru_b_a', 'delta_lru_w_i', 'delta_lru_b_i', 'delta_lru_lambda', 'delta_lru_norm', 'delta_gdn_conv_w', 'delta_gdn_a_log', 'delta_gdn_dt_bias', 'delta_gdn_norm', 'delta_ssd_conv_w', 'delta_ssd_conv_b', 'delta_ssd_a_log', 'delta_ssd_dt_bias', 'delta_ssd_d', 'delta_ssd_norm', 'delta_s5_a_re', 'delta_s5_a_im', 'delta_s5_log_dt', 'delta_s5_b_re', 'delta_s5_b_im', 'delta_s5_c_re', 'delta_s5_c_im', 'delta_s5_d', 'delta_s5_w_glu', 'delta_s5_norm', 'delta_ffn2_norm', 'delta_ffn2_w_gate', 'delta_ffn2_w_up', 'delta_ffn2_w_down', 'delta_final_norm', 'new_m_meta_tokens', 'new_m_ffn1_norm', 'new_m_ffn1_w_gate', 'new_m_ffn1_w_up', 'new_m_ffn1_w_down', 'new_m_mix_norm', 'new_m_w_in', 'new_m_w_out', 'new_m_lru_conv_w', 'new_m_lru_conv_b', 'new_m_lru_w_a', 'new_m_lru_b_a', 'new_m_lru_w_i', 'new_m_lru_b_i', 'new_m_lru_lambda', 'new_m_lru_norm', 'new_m_gdn_conv_w', 'new_m_gdn_a_log', 'new_m_gdn_dt_bias', 'new_m_gdn_norm', 'new_m_ssd_conv_w', 'new_m_ssd_conv_b', 'new_m_ssd_a_log', 'new_m_ssd_dt_bias', 'new_m_ssd_d', 'new_m_ssd_norm', 'new_m_s5_a_re', 'new_m_s5_a_im', 'new_m_s5_log_dt', 'new_m_s5_b_re', 'new_m_s5_b_im', 'new_m_s5_c_re', 'new_m_s5_c_im', 'new_m_s5_d', 'new_m_s5_w_glu', 'new_m_s5_norm', 'new_m_ffn2_norm', 'new_m_ffn2_w_gate', 'new_m_ffn2_w_up', 'new_m_ffn2_w_down', 'new_m_final_norm', 'new_v_meta_tokens', 'new_v_ffn1_norm', 'new_v_ffn1_w_gate', 'new_v_ffn1_w_up', 'new_v_ffn1_w_down', 'new_v_mix_norm', 'new_v_w_in', 'new_v_w_out', 'new_v_lru_conv_w', 'new_v_lru_conv_b', 'new_v_lru_w_a', 'new_v_lru_b_a', 'new_v_lru_w_i', 'new_v_lru_b_i', 'new_v_lru_lambda', 'new_v_lru_norm', 'new_v_gdn_conv_w', 'new_v_gdn_a_log', 'new_v_gdn_dt_bias', 'new_v_gdn_norm', 'new_v_ssd_conv_w', 'new_v_ssd_conv_b', 'new_v_ssd_a_log', 'new_v_ssd_dt_bias', 'new_v_ssd_d', 'new_v_ssd_norm', 'new_v_s5_a_re', 'new_v_s5_a_im', 'new_v_s5_log_dt', 'new_v_s5_b_re', 'new_v_s5_b_im', 'new_v_s5_c_re', 'new_v_s5_c_im', 'new_v_s5_d', 'new_v_s5_w_glu', 'new_v_s5_norm', 'new_v_ffn2_norm', 'new_v_ffn2_w_gate', 'new_v_ffn2_w_up', 'new_v_ffn2_w_down', 'new_v_final_norm']
TWIN_LEAF_KINDS = {'loss': 'loss', 'grad_x': 'grad_x', 'grad_meta_tokens': 'grad_w', 'grad_ffn1_norm': 'grad_w', 'grad_ffn1_w_gate': 'grad_w', 'grad_ffn1_w_up': 'grad_w', 'grad_ffn1_w_down': 'grad_w', 'grad_mix_norm': 'grad_w', 'grad_w_in': 'grad_w', 'grad_w_out': 'grad_w', 'grad_lru_conv_w': 'grad_w', 'grad_lru_conv_b': 'grad_w', 'grad_lru_w_a': 'grad_w', 'grad_lru_b_a': 'grad_w', 'grad_lru_w_i': 'grad_w', 'grad_lru_b_i': 'grad_w', 'grad_lru_lambda': 'grad_w', 'grad_lru_norm': 'grad_w', 'grad_gdn_conv_w': 'grad_w', 'grad_gdn_a_log': 'grad_w', 'grad_gdn_dt_bias': 'grad_w', 'grad_gdn_norm': 'grad_w', 'grad_ssd_conv_w': 'grad_w', 'grad_ssd_conv_b': 'grad_w', 'grad_ssd_a_log': 'grad_w', 'grad_ssd_dt_bias': 'grad_w', 'grad_ssd_d': 'grad_w', 'grad_ssd_norm': 'grad_w', 'grad_s5_a_re': 'grad_w', 'grad_s5_a_im': 'grad_w', 'grad_s5_log_dt': 'grad_w', 'grad_s5_b_re': 'grad_w', 'grad_s5_b_im': 'grad_w', 'grad_s5_c_re': 'grad_w', 'grad_s5_c_im': 'grad_w', 'grad_s5_d': 'grad_w', 'grad_s5_w_glu': 'grad_w', 'grad_s5_norm': 'grad_w', 'grad_ffn2_norm': 'grad_w', 'grad_ffn2_w_gate': 'grad_w', 'grad_ffn2_w_up': 'grad_w', 'grad_ffn2_w_down': 'grad_w', 'grad_final_norm': 'grad_w', 'delta_meta_tokens': 'delta_w', 'delta_ffn1_norm': 'delta_w', 'delta_ffn1_w_gate': 'delta_w', 'delta_ffn1_w_up': 'delta_w', 'delta_ffn1_w_down': 'delta_w', 'delta_mix_norm': 'delta_w', 'delta_w_in': 'delta_w', 'delta_w_out': 'delta_w', 'delta_lru_conv_w': 'delta_w', 'delta_lru_conv_b': 'delta_w', 'delta_lru_w_a': 'delta_w', 'delta_lru_b_a': 'delta_w', 'delta_lru_w_i': 'delta_w', 'delta_lru_b_i': 'delta_w', 'delta_lru_lambda': 'delta_w', 'delta_lru_norm': 'delta_w', 'delta_gdn_conv_w': 'delta_w', 'delta_gdn_a_log': 'delta_w', 'delta_gdn_dt_bias': 'delta_w', 'delta_gdn_norm': 'delta_w', 'delta_ssd_conv_w': 'delta_w', 'delta_ssd_conv_b': 'delta_w', 'delta_ssd_a_log': 'delta_w', 'delta_ssd_dt_bias': 'delta_w', 'delta_ssd_d': 'delta_w', 'delta_ssd_norm': 'delta_w', 'delta_s5_a_re': 'delta_w', 'delta_s5_a_im': 'delta_w', 'delta_s5_log_dt': 'delta_w', 'delta_s5_b_re': 'delta_w', 'delta_s5_b_im': 'delta_w', 'delta_s5_c_re': 'delta_w', 'delta_s5_c_im': 'delta_w', 'delta_s5_d': 'delta_w', 'delta_s5_w_glu': 'delta_w', 'delta_s5_norm': 'delta_w', 'delta_ffn2_norm': 'delta_w', 'delta_ffn2_w_gate': 'delta_w', 'delta_ffn2_w_up': 'delta_w', 'delta_ffn2_w_down': 'delta_w', 'delta_final_norm': 'delta_w', 'new_m_meta_tokens': 'new_m', 'new_m_ffn1_norm': 'new_m', 'new_m_ffn1_w_gate': 'new_m', 'new_m_ffn1_w_up': 'new_m', 'new_m_ffn1_w_down': 'new_m', 'new_m_mix_norm': 'new_m', 'new_m_w_in': 'new_m', 'new_m_w_out': 'new_m', 'new_m_lru_conv_w': 'new_m', 'new_m_lru_conv_b': 'new_m', 'new_m_lru_w_a': 'new_m', 'new_m_lru_b_a': 'new_m', 'new_m_lru_w_i': 'new_m', 'new_m_lru_b_i': 'new_m', 'new_m_lru_lambda': 'new_m', 'new_m_lru_norm': 'new_m', 'new_m_gdn_conv_w': 'new_m', 'new_m_gdn_a_log': 'new_m', 'new_m_gdn_dt_bias': 'new_m', 'new_m_gdn_norm': 'new_m', 'new_m_ssd_conv_w': 'new_m', 'new_m_ssd_conv_b': 'new_m', 'new_m_ssd_a_log': 'new_m', 'new_m_ssd_dt_bias': 'new_m', 'new_m_ssd_d': 'new_m', 'new_m_ssd_norm': 'new_m', 'new_m_s5_a_re': 'new_m', 'new_m_s5_a_im': 'new_m', 'new_m_s5_log_dt': 'new_m', 'new_m_s5_b_re': 'new_m', 'new_m_s5_b_im': 'new_m', 'new_m_s5_c_re': 'new_m', 'new_m_s5_c_im': 'new_m', 'new_m_s5_d': 'new_m', 'new_m_s5_w_glu': 'new_m', 'new_m_s5_norm': 'new_m', 'new_m_ffn2_norm': 'new_m', 'new_m_ffn2_w_gate': 'new_m', 'new_m_ffn2_w_up': 'new_m', 'new_m_ffn2_w_down': 'new_m', 'new_m_final_norm': 'new_m', 'new_v_meta_tokens': 'new_v', 'new_v_ffn1_norm': 'new_v', 'new_v_ffn1_w_gate': 'new_v', 'new_v_ffn1_w_up': 'new_v', 'new_v_ffn1_w_down': 'new_v', 'new_v_mix_norm': 'new_v', 'new_v_w_in': 'new_v', 'new_v_w_out': 'new_v', 'new_v_lru_conv_w': 'new_v', 'new_v_lru_conv_b': 'new_v', 'new_v_lru_w_a': 'new_v', 'new_v_lru_b_a': 'new_v', 'new_v_lru_w_i': 'new_v', 'new_v_lru_b_i': 'new_v', 'new_v_lru_lambda': 'new_v', 'new_v_lru_norm': 'new_v', 'new_v_gdn_conv_w': 'new_v', 'new_v_gdn_a_log': 'new_v', 'new_v_gdn_dt_bias': 'new_v', 'new_v_gdn_norm': 'new_v', 'new_v_ssd_conv_w': 'new_v', 'new_v_ssd_conv_b': 'new_v', 'new_v_ssd_a_log': 'new_v', 'new_v_ssd_dt_bias': 'new_v', 'new_v_ssd_d': 'new_v', 'new_v_ssd_norm': 'new_v', 'new_v_s5_a_re': 'new_v', 'new_v_s5_a_im': 'new_v', 'new_v_s5_log_dt': 'new_v', 'new_v_s5_b_re': 'new_v', 'new_v_s5_b_im': 'new_v', 'new_v_s5_c_re': 'new_v', 'new_v_s5_c_im': 'new_v', 'new_v_s5_d': 'new_v', 'new_v_s5_w_glu': 'new_v', 'new_v_s5_norm': 'new_v', 'new_v_ffn2_norm': 'new_v', 'new_v_ffn2_w_gate': 'new_v', 'new_v_ffn2_w_up': 'new_v', 'new_v_ffn2_w_down': 'new_v', 'new_v_final_norm': 'new_v'}


def _forward(args):
    return _fwd_reference(*[args[k] for k in FWD_PARAMS])


def _output_shape():
    out = _jax.eval_shape(lambda: _forward(_fwd_setup_inputs(0)))
    return out.shape, out.dtype

N_MICROBATCH = 1
ADAM_LR = 0.001
ADAM_B1 = 0.9
ADAM_B2 = 0.999
ADAM_EPS = 1e-08
ADAM_WD = 0.01
ADAM_STEP = 10
PER_EXAMPLE_BATCH_AXIS = {'x': 0, 'loss_target': 0}
SHARED_INPUTS = []
_WEIGHT_DTYPES = {'meta_tokens': _jnp.float32, 'ffn1_norm': _jnp.float32, 'ffn1_w_gate': _jnp.float32, 'ffn1_w_up': _jnp.float32, 'ffn1_w_down': _jnp.float32, 'mix_norm': _jnp.float32, 'w_in': _jnp.float32, 'w_out': _jnp.float32, 'lru_conv_w': _jnp.float32, 'lru_conv_b': _jnp.float32, 'lru_w_a': _jnp.float32, 'lru_b_a': _jnp.float32, 'lru_w_i': _jnp.float32, 'lru_b_i': _jnp.float32, 'lru_lambda': _jnp.float32, 'lru_norm': _jnp.float32, 'gdn_conv_w': _jnp.float32, 'gdn_a_log': _jnp.float32, 'gdn_dt_bias': _jnp.float32, 'gdn_norm': _jnp.float32, 'ssd_conv_w': _jnp.float32, 'ssd_conv_b': _jnp.float32, 'ssd_a_log': _jnp.float32, 'ssd_dt_bias': _jnp.float32, 'ssd_d': _jnp.float32, 'ssd_norm': _jnp.float32, 's5_a_re': _jnp.float32, 's5_a_im': _jnp.float32, 's5_log_dt': _jnp.float32, 's5_b_re': _jnp.float32, 's5_b_im': _jnp.float32, 's5_c_re': _jnp.float32, 's5_c_im': _jnp.float32, 's5_d': _jnp.float32, 's5_w_glu': _jnp.float32, 's5_norm': _jnp.float32, 'ffn2_norm': _jnp.float32, 'ffn2_w_gate': _jnp.float32, 'ffn2_w_up': _jnp.float32, 'ffn2_w_down': _jnp.float32, 'final_norm': _jnp.float32}
MOMENT_SCALE = {'meta_tokens': 8.314291e-03, 'ffn1_norm': 5.568424e-02, 'ffn1_w_gate': 2.379584e-02, 'ffn1_w_up': 2.305968e-02, 'ffn1_w_down': 3.820977e-02, 'mix_norm': 1.193018e-01, 'w_in': 5.188339e-02, 'w_out': 9.178540e-02, 'lru_conv_w': 7.470447e-02, 'lru_conv_b': 6.205788e-01, 'lru_w_a': 2.156919e-02, 'lru_b_a': 1.908446e-02, 'lru_w_i': 4.035868e-02, 'lru_b_i': 2.477537e-02, 'lru_lambda': 3.608615e-02, 'lru_norm': 6.807313e-02, 'gdn_conv_w': 2.902569e-02, 'gdn_a_log': 1.645450e-01, 'gdn_dt_bias': 1.600332e-01, 'gdn_norm': 7.577536e-02, 'ssd_conv_w': 4.842840e-02, 'ssd_conv_b': 6.417144e-02, 'ssd_a_log': 2.361454e-01, 'ssd_dt_bias': 1.515813e-01, 'ssd_d': 7.326793e-01, 'ssd_norm': 6.294451e-02, 's5_a_re': 4.301616e-03, 's5_a_im': 3.651959e-03, 's5_log_dt': 1.827555e+00, 's5_b_re': 2.509224e-03, 's5_b_im': 2.475004e-03, 's5_c_re': 4.868928e-03, 's5_c_im': 4.893849e-03, 's5_d': 8.018782e-02, 's5_w_glu': 1.980188e-02, 's5_norm': 7.674572e-02, 'ffn2_norm': 3.773871e-02, 'ffn2_w_gate': 1.609506e-02, 'ffn2_w_up': 1.569041e-02, 'ffn2_w_down': 2.601417e-02, 'final_norm': 1.612202e+01}


def _to_microbatches(a, axis):
    t = _jnp.moveaxis(a, axis, 0)
    t = t.reshape((N_MICROBATCH, t.shape[0] // N_MICROBATCH) + t.shape[1:])
    return _jnp.moveaxis(t, 1, axis + 1)


def setup_inputs(seed: int = 0) -> dict:
    inp = _fwd_setup_inputs(seed)
    key = _jax.random.fold_in(_jax.random.key(seed), 7919)
    shape, _ = _output_shape()
    out = dict(inp)
    out["loss_target"] = _jax.random.normal(_jax.random.fold_in(key, 0), shape, _jnp.float32)
    for i, name in enumerate(TWIN_WEIGHTS):
        w = inp[name].astype(_jnp.float32)
        if MOMENT_SCALE is None:
            s = _jnp.sqrt(_jnp.mean(_jnp.square(w)) + 1e-30)
        else:
            s = MOMENT_SCALE[name]
        km, kv = _jax.random.split(_jax.random.fold_in(key, i + 1))
        out[name] = w
        out["m_" + name] = s * _jax.random.normal(km, w.shape, _jnp.float32)
        out["v_" + name] = (s * s) * _jax.random.uniform(kv, w.shape, _jnp.float32, 0.5, 1.5)
    if N_MICROBATCH > 1:
        for name, axis in PER_EXAMPLE_BATCH_AXIS.items():
            out[name] = _to_microbatches(out[name], axis)
    return {'x': out['x'], 'meta_tokens': out['meta_tokens'], 'ffn1_norm': out['ffn1_norm'], 'ffn1_w_gate': out['ffn1_w_gate'], 'ffn1_w_up': out['ffn1_w_up'], 'ffn1_w_down': out['ffn1_w_down'], 'mix_norm': out['mix_norm'], 'w_in': out['w_in'], 'w_out': out['w_out'], 'lru_conv_w': out['lru_conv_w'], 'lru_conv_b': out['lru_conv_b'], 'lru_w_a': out['lru_w_a'], 'lru_b_a': out['lru_b_a'], 'lru_w_i': out['lru_w_i'], 'lru_b_i': out['lru_b_i'], 'lru_lambda': out['lru_lambda'], 'lru_norm': out['lru_norm'], 'gdn_conv_w': out['gdn_conv_w'], 'gdn_a_log': out['gdn_a_log'], 'gdn_dt_bias': out['gdn_dt_bias'], 'gdn_norm': out['gdn_norm'], 'ssd_conv_w': out['ssd_conv_w'], 'ssd_conv_b': out['ssd_conv_b'], 'ssd_a_log': out['ssd_a_log'], 'ssd_dt_bias': out['ssd_dt_bias'], 'ssd_d': out['ssd_d'], 'ssd_norm': out['ssd_norm'], 's5_a_re': out['s5_a_re'], 's5_a_im': out['s5_a_im'], 's5_log_dt': out['s5_log_dt'], 's5_b_re': out['s5_b_re'], 's5_b_im': out['s5_b_im'], 's5_c_re': out['s5_c_re'], 's5_c_im': out['s5_c_im'], 's5_d': out['s5_d'], 's5_w_glu': out['s5_w_glu'], 's5_norm': out['s5_norm'], 'ffn2_norm': out['ffn2_norm'], 'ffn2_w_gate': out['ffn2_w_gate'], 'ffn2_w_up': out['ffn2_w_up'], 'ffn2_w_down': out['ffn2_w_down'], 'final_norm': out['final_norm'], 'loss_target': out['loss_target'], 'm_meta_tokens': out['m_meta_tokens'], 'm_ffn1_norm': out['m_ffn1_norm'], 'm_ffn1_w_gate': out['m_ffn1_w_gate'], 'm_ffn1_w_up': out['m_ffn1_w_up'], 'm_ffn1_w_down': out['m_ffn1_w_down'], 'm_mix_norm': out['m_mix_norm'], 'm_w_in': out['m_w_in'], 'm_w_out': out['m_w_out'], 'm_lru_conv_w': out['m_lru_conv_w'], 'm_lru_conv_b': out['m_lru_conv_b'], 'm_lru_w_a': out['m_lru_w_a'], 'm_lru_b_a': out['m_lru_b_a'], 'm_lru_w_i': out['m_lru_w_i'], 'm_lru_b_i': out['m_lru_b_i'], 'm_lru_lambda': out['m_lru_lambda'], 'm_lru_norm': out['m_lru_norm'], 'm_gdn_conv_w': out['m_gdn_conv_w'], 'm_gdn_a_log': out['m_gdn_a_log'], 'm_gdn_dt_bias': out['m_gdn_dt_bias'], 'm_gdn_norm': out['m_gdn_norm'], 'm_ssd_conv_w': out['m_ssd_conv_w'], 'm_ssd_conv_b': out['m_ssd_conv_b'], 'm_ssd_a_log': out['m_ssd_a_log'], 'm_ssd_dt_bias': out['m_ssd_dt_bias'], 'm_ssd_d': out['m_ssd_d'], 'm_ssd_norm': out['m_ssd_norm'], 'm_s5_a_re': out['m_s5_a_re'], 'm_s5_a_im': out['m_s5_a_im'], 'm_s5_log_dt': out['m_s5_log_dt'], 'm_s5_b_re': out['m_s5_b_re'], 'm_s5_b_im': out['m_s5_b_im'], 'm_s5_c_re': out['m_s5_c_re'], 'm_s5_c_im': out['m_s5_c_im'], 'm_s5_d': out['m_s5_d'], 'm_s5_w_glu': out['m_s5_w_glu'], 'm_s5_norm': out['m_s5_norm'], 'm_ffn2_norm': out['m_ffn2_norm'], 'm_ffn2_w_gate': out['m_ffn2_w_gate'], 'm_ffn2_w_up': out['m_ffn2_w_up'], 'm_ffn2_w_down': out['m_ffn2_w_down'], 'm_final_norm': out['m_final_norm'], 'v_meta_tokens': out['v_meta_tokens'], 'v_ffn1_norm': out['v_ffn1_norm'], 'v_ffn1_w_gate': out['v_ffn1_w_gate'], 'v_ffn1_w_up': out['v_ffn1_w_up'], 'v_ffn1_w_down': out['v_ffn1_w_down'], 'v_mix_norm': out['v_mix_norm'], 'v_w_in': out['v_w_in'], 'v_w_out': out['v_w_out'], 'v_lru_conv_w': out['v_lru_conv_w'], 'v_lru_conv_b': out['v_lru_conv_b'], 'v_lru_w_a': out['v_lru_w_a'], 'v_lru_b_a': out['v_lru_b_a'], 'v_lru_w_i': out['v_lru_w_i'], 'v_lru_b_i': out['v_lru_b_i'], 'v_lru_lambda': out['v_lru_lambda'], 'v_lru_norm': out['v_lru_norm'], 'v_gdn_conv_w': out['v_gdn_conv_w'], 'v_gdn_a_log': out['v_gdn_a_log'], 'v_gdn_dt_bias': out['v_gdn_dt_bias'], 'v_gdn_norm': out['v_gdn_norm'], 'v_ssd_conv_w': out['v_ssd_conv_w'], 'v_ssd_conv_b': out['v_ssd_conv_b'], 'v_ssd_a_log': out['v_ssd_a_log'], 'v_ssd_dt_bias': out['v_ssd_dt_bias'], 'v_ssd_d': out['v_ssd_d'], 'v_ssd_norm': out['v_ssd_norm'], 'v_s5_a_re': out['v_s5_a_re'], 'v_s5_a_im': out['v_s5_a_im'], 'v_s5_log_dt': out['v_s5_log_dt'], 'v_s5_b_re': out['v_s5_b_re'], 'v_s5_b_im': out['v_s5_b_im'], 'v_s5_c_re': out['v_s5_c_re'], 'v_s5_c_im': out['v_s5_c_im'], 'v_s5_d': out['v_s5_d'], 'v_s5_w_glu': out['v_s5_w_glu'], 'v_s5_norm': out['v_s5_norm'], 'v_ffn2_norm': out['v_ffn2_norm'], 'v_ffn2_w_gate': out['v_ffn2_w_gate'], 'v_ffn2_w_up': out['v_ffn2_w_up'], 'v_ffn2_w_down': out['v_ffn2_w_down'], 'v_final_norm': out['v_final_norm']}


def _loss(weights, diff, rest, loss_target):
    with _jax.named_scope("forward"):
        args = {**rest, TWIN_DIFF_INPUT: diff, **{k: w.astype(_WEIGHT_DTYPES[k]) for k, w in weights.items()}}
        y = _forward(args)
    with _jax.named_scope("loss_head"):
        err = _jnp.square(y.astype(_jnp.float32) - loss_target)
        return 0.5 * _jnp.sum(_jnp.mean(err, axis=-1)) if err.ndim else 0.5 * err


def _adamw(w, g, m, v):
    m = ADAM_B1 * m + (1.0 - ADAM_B1) * g
    v = ADAM_B2 * v + (1.0 - ADAM_B2) * _jnp.square(g)
    m_hat = m / (1.0 - ADAM_B1 ** ADAM_STEP)
    v_hat = v / (1.0 - ADAM_B2 ** ADAM_STEP)
    delta = -ADAM_LR * (m_hat / (_jnp.sqrt(v_hat) + ADAM_EPS) + ADAM_WD * w)
    return delta, m, v


def reference(x, meta_tokens, ffn1_norm, ffn1_w_gate, ffn1_w_up, ffn1_w_down, mix_norm, w_in, w_out, lru_conv_w, lru_conv_b, lru_w_a, lru_b_a, lru_w_i, lru_b_i, lru_lambda, lru_norm, gdn_conv_w, gdn_a_log, gdn_dt_bias, gdn_norm, ssd_conv_w, ssd_conv_b, ssd_a_log, ssd_dt_bias, ssd_d, ssd_norm, s5_a_re, s5_a_im, s5_log_dt, s5_b_re, s5_b_im, s5_c_re, s5_c_im, s5_d, s5_w_glu, s5_norm, ffn2_norm, ffn2_w_gate, ffn2_w_up, ffn2_w_down, final_norm, loss_target, m_meta_tokens, m_ffn1_norm, m_ffn1_w_gate, m_ffn1_w_up, m_ffn1_w_down, m_mix_norm, m_w_in, m_w_out, m_lru_conv_w, m_lru_conv_b, m_lru_w_a, m_lru_b_a, m_lru_w_i, m_lru_b_i, m_lru_lambda, m_lru_norm, m_gdn_conv_w, m_gdn_a_log, m_gdn_dt_bias, m_gdn_norm, m_ssd_conv_w, m_ssd_conv_b, m_ssd_a_log, m_ssd_dt_bias, m_ssd_d, m_ssd_norm, m_s5_a_re, m_s5_a_im, m_s5_log_dt, m_s5_b_re, m_s5_b_im, m_s5_c_re, m_s5_c_im, m_s5_d, m_s5_w_glu, m_s5_norm, m_ffn2_norm, m_ffn2_w_gate, m_ffn2_w_up, m_ffn2_w_down, m_final_norm, v_meta_tokens, v_ffn1_norm, v_ffn1_w_gate, v_ffn1_w_up, v_ffn1_w_down, v_mix_norm, v_w_in, v_w_out, v_lru_conv_w, v_lru_conv_b, v_lru_w_a, v_lru_b_a, v_lru_w_i, v_lru_b_i, v_lru_lambda, v_lru_norm, v_gdn_conv_w, v_gdn_a_log, v_gdn_dt_bias, v_gdn_norm, v_ssd_conv_w, v_ssd_conv_b, v_ssd_a_log, v_ssd_dt_bias, v_ssd_d, v_ssd_norm, v_s5_a_re, v_s5_a_im, v_s5_log_dt, v_s5_b_re, v_s5_b_im, v_s5_c_re, v_s5_c_im, v_s5_d, v_s5_w_glu, v_s5_norm, v_ffn2_norm, v_ffn2_w_gate, v_ffn2_w_up, v_ffn2_w_down, v_final_norm):
    given = dict(x=x, meta_tokens=meta_tokens, ffn1_norm=ffn1_norm, ffn1_w_gate=ffn1_w_gate, ffn1_w_up=ffn1_w_up, ffn1_w_down=ffn1_w_down, mix_norm=mix_norm, w_in=w_in, w_out=w_out, lru_conv_w=lru_conv_w, lru_conv_b=lru_conv_b, lru_w_a=lru_w_a, lru_b_a=lru_b_a, lru_w_i=lru_w_i, lru_b_i=lru_b_i, lru_lambda=lru_lambda, lru_norm=lru_norm, gdn_conv_w=gdn_conv_w, gdn_a_log=gdn_a_log, gdn_dt_bias=gdn_dt_bias, gdn_norm=gdn_norm, ssd_conv_w=ssd_conv_w, ssd_conv_b=ssd_conv_b, ssd_a_log=ssd_a_log, ssd_dt_bias=ssd_dt_bias, ssd_d=ssd_d, ssd_norm=ssd_norm, s5_a_re=s5_a_re, s5_a_im=s5_a_im, s5_log_dt=s5_log_dt, s5_b_re=s5_b_re, s5_b_im=s5_b_im, s5_c_re=s5_c_re, s5_c_im=s5_c_im, s5_d=s5_d, s5_w_glu=s5_w_glu, s5_norm=s5_norm, ffn2_norm=ffn2_norm, ffn2_w_gate=ffn2_w_gate, ffn2_w_up=ffn2_w_up, ffn2_w_down=ffn2_w_down, final_norm=final_norm, loss_target=loss_target, m_meta_tokens=m_meta_tokens, m_ffn1_norm=m_ffn1_norm, m_ffn1_w_gate=m_ffn1_w_gate, m_ffn1_w_up=m_ffn1_w_up, m_ffn1_w_down=m_ffn1_w_down, m_mix_norm=m_mix_norm, m_w_in=m_w_in, m_w_out=m_w_out, m_lru_conv_w=m_lru_conv_w, m_lru_conv_b=m_lru_conv_b, m_lru_w_a=m_lru_w_a, m_lru_b_a=m_lru_b_a, m_lru_w_i=m_lru_w_i, m_lru_b_i=m_lru_b_i, m_lru_lambda=m_lru_lambda, m_lru_norm=m_lru_norm, m_gdn_conv_w=m_gdn_conv_w, m_gdn_a_log=m_gdn_a_log, m_gdn_dt_bias=m_gdn_dt_bias, m_gdn_norm=m_gdn_norm, m_ssd_conv_w=m_ssd_conv_w, m_ssd_conv_b=m_ssd_conv_b, m_ssd_a_log=m_ssd_a_log, m_ssd_dt_bias=m_ssd_dt_bias, m_ssd_d=m_ssd_d, m_ssd_norm=m_ssd_norm, m_s5_a_re=m_s5_a_re, m_s5_a_im=m_s5_a_im, m_s5_log_dt=m_s5_log_dt, m_s5_b_re=m_s5_b_re, m_s5_b_im=m_s5_b_im, m_s5_c_re=m_s5_c_re, m_s5_c_im=m_s5_c_im, m_s5_d=m_s5_d, m_s5_w_glu=m_s5_w_glu, m_s5_norm=m_s5_norm, m_ffn2_norm=m_ffn2_norm, m_ffn2_w_gate=m_ffn2_w_gate, m_ffn2_w_up=m_ffn2_w_up, m_ffn2_w_down=m_ffn2_w_down, m_final_norm=m_final_norm, v_meta_tokens=v_meta_tokens, v_ffn1_norm=v_ffn1_norm, v_ffn1_w_gate=v_ffn1_w_gate, v_ffn1_w_up=v_ffn1_w_up, v_ffn1_w_down=v_ffn1_w_down, v_mix_norm=v_mix_norm, v_w_in=v_w_in, v_w_out=v_w_out, v_lru_conv_w=v_lru_conv_w, v_lru_conv_b=v_lru_conv_b, v_lru_w_a=v_lru_w_a, v_lru_b_a=v_lru_b_a, v_lru_w_i=v_lru_w_i, v_lru_b_i=v_lru_b_i, v_lru_lambda=v_lru_lambda, v_lru_norm=v_lru_norm, v_gdn_conv_w=v_gdn_conv_w, v_gdn_a_log=v_gdn_a_log, v_gdn_dt_bias=v_gdn_dt_bias, v_gdn_norm=v_gdn_norm, v_ssd_conv_w=v_ssd_conv_w, v_ssd_conv_b=v_ssd_conv_b, v_ssd_a_log=v_ssd_a_log, v_ssd_dt_bias=v_ssd_dt_bias, v_ssd_d=v_ssd_d, v_ssd_norm=v_ssd_norm, v_s5_a_re=v_s5_a_re, v_s5_a_im=v_s5_a_im, v_s5_log_dt=v_s5_log_dt, v_s5_b_re=v_s5_b_re, v_s5_b_im=v_s5_b_im, v_s5_c_re=v_s5_c_re, v_s5_c_im=v_s5_c_im, v_s5_d=v_s5_d, v_s5_w_glu=v_s5_w_glu, v_s5_norm=v_s5_norm, v_ffn2_norm=v_ffn2_norm, v_ffn2_w_gate=v_ffn2_w_gate, v_ffn2_w_up=v_ffn2_w_up, v_ffn2_w_down=v_ffn2_w_down, v_final_norm=v_final_norm)
    weights = {n: given[n] for n in TWIN_WEIGHTS}
    shared = {n: given[n] for n in SHARED_INPUTS}
    per_example = {n: given[n] for n in ['x']}
    grad_fn = _jax.value_and_grad(_loss, argnums=(0, 1))

    def one_microbatch(ex, loss_target):
        ex = dict(ex)
        diff = ex.pop(TWIN_DIFF_INPUT)
        return grad_fn(weights, diff, {**shared, **ex}, loss_target)

    if N_MICROBATCH == 1:
        loss, (grad_w, grad_x) = one_microbatch(per_example, given["loss_target"])
    else:
        def body(carry, xs):
            loss_sum, grad_sum = carry
            l_k, (gw_k, gx_k) = one_microbatch(xs[0], xs[1])
            with _jax.named_scope("update"):
                return (loss_sum + l_k, _jax.tree.map(_jnp.add, grad_sum, gw_k)), gx_k

        init = (_jnp.zeros((), _jnp.float32), _jax.tree.map(_jnp.zeros_like, weights))
        (loss, grad_w), grad_x = _jax.lax.scan(body, init, (per_example, given["loss_target"]))
    with _jax.named_scope("update"):
        delta_w, new_m, new_v = {}, {}, {}
        for n in TWIN_WEIGHTS:
            delta_w[n], new_m[n], new_v[n] = _adamw(weights[n], grad_w[n], given["m_" + n], given["v_" + n])
    return (loss, grad_x, *[grad_w[n] for n in TWIN_WEIGHTS], *[delta_w[n] for n in TWIN_WEIGHTS],
            *[new_m[n] for n in TWIN_WEIGHTS], *[new_v[n] for n in TWIN_WEIGHTS])
```

```python
import functools
import math

import jax
import jax.numpy as jnp
from jax import lax
from jax.experimental import pallas as pl
from jax.experimental.pallas import tpu as pltpu

F32 = jnp.float32
BF16 = jnp.bfloat16

EPS = 1e-6
CHUNK = 64
CONV_K = 4
CONV_TAIL = 8
LRU_C = 8.0
LANE = 128
N_DEV = 8
NEG_BIG = -1e30

ADAM_LR = 0.001
ADAM_B1 = 0.9
ADAM_B2 = 0.999
ADAM_EPS = 1e-08
ADAM_WD = 0.01
ADAM_STEP = 10

VMEM_LIMIT = 56 * 1024 * 1024


def _dg(a, b, dims):
    return lax.dot_general(a.astype(BF16), b.astype(BF16), (dims, ((), ())), preferred_element_type=F32)


@jax.custom_vjp
def bdot(a, b):
    return _dg(a, b, ((1,), (0,)))


@jax.custom_vjp
def bdot_nt(a, b):
    return _dg(a, b, ((1,), (1,)))


@jax.custom_vjp
def bdot_tn(a, b):
    return _dg(a, b, ((0,), (0,)))


bdot.defvjp(lambda a, b: (bdot(a, b), (a, b)),
            lambda r, g: (bdot_nt(g, r[1]).astype(r[0].dtype), bdot_tn(r[0], g).astype(r[1].dtype)))
bdot_nt.defvjp(lambda a, b: (bdot_nt(a, b), (a, b)),
               lambda r, g: (bdot(g, r[1]).astype(r[0].dtype), bdot_tn(g, r[0]).astype(r[1].dtype)))
bdot_tn.defvjp(lambda a, b: (bdot_tn(a, b), (a, b)),
               lambda r, g: (bdot_nt(r[1], g).astype(r[0].dtype), bdot(r[0], g).astype(r[1].dtype)))


def hdot(a, b):
    return jnp.dot(a, b, preferred_element_type=F32, precision=lax.Precision.HIGHEST)


def rms_norm(x, g):
    return x * lax.rsqrt(jnp.mean(x * x, axis=-1, keepdims=True) + EPS) * g


def row_mask(row0, rows, pad):
    r = row0 + lax.broadcasted_iota(jnp.int32, (rows, 1), 0)
    return (r >= pad).astype(F32)


def conv4(tail, u, w):
    rows = u.shape[0]
    xe = jnp.concatenate([tail, u], axis=0)
    y = w[0:1] * xe[CONV_TAIL - 3:CONV_TAIL - 3 + rows]
    for k in range(1, CONV_K):
        y = y + w[k:k + 1] * xe[CONV_TAIL - 3 + k:CONV_TAIL - 3 + k + rows]
    return y


def shift_rows(x, s, fill):
    rows = x.shape[0]
    return jnp.concatenate([jnp.full((s, x.shape[1]), fill, x.dtype), x[:rows - s]], axis=0)


def lin_scan(a, b):
    rows = a.shape[0]
    s = 1
    while s < rows:
        b = a * shift_rows(b, s, 0.0) + b
        a = a * shift_rows(a, s, 1.0)
        s *= 2
    return b


def cscan_const(ar, ai, br, bi):
    rows = br.shape[0]
    s = 1
    while s < rows:
        brs, bis = shift_rows(br, s, 0.0), shift_rows(bi, s, 0.0)
        br, bi = br + ar * brs - ai * bis, bi + ar * bis + ai * brs
        ar, ai = ar * ar - ai * ai, 2.0 * ar * ai
        s *= 2
    return br, bi


def neg_expm1(z):
    t = jnp.tanh(0.5 * z)
    return -2.0 * t / (1.0 - t)


def tri_masks(n):
    r = lax.broadcasted_iota(jnp.int32, (n, n), 0)
    c = lax.broadcasted_iota(jnp.int32, (n, n), 1)
    return r >= c, r > c, (r == c).astype(F32)


def tri_inverse(lmat, eye):
    p = eye - lmat
    m = hdot(lmat, lmat)
    n = lmat.shape[0]
    s = 2
    while s < n:
        p = p + hdot(p, m)
        s *= 2
        if s < n:
            m = hdot(m, m)
    return p


def lru_chunk(pad, row0, params, seqs, carry):
    conv_w, conv_b, w_a, b_a, w_i, b_i, lam, norm_g = params
    u_x, u_gate = seqs
    tail, h0 = carry
    rows = u_x.shape[0]
    m = row_mask(row0, rows, pad)
    xc = conv4(tail, u_x, conv_w) + conv_b
    r = jax.nn.sigmoid(bdot(xc, w_a) + b_a)
    ig = jax.nn.sigmoid(bdot(xc, w_i) + b_i)
    log_a = -LRU_C * r * jax.nn.softplus(-lam)
    a = jnp.exp(log_a)
    b = jnp.sqrt(neg_expm1(2.0 * log_a)) * (ig * xc) * m
    first = (lax.broadcasted_iota(jnp.int32, (rows, 1), 0) == 0).astype(F32)
    b = b + first * (a * h0)
    h = lin_scan(a, b)
    y = jax.nn.gelu(u_gate) * h
    out = rms_norm(y, norm_g) * m
    return (out,), (u_x[rows - CONV_TAIL:], h[rows - 1:])


def gdn_chunk(pad, row0, params, seqs, carry):
    wq, wk, wv, a_log, dt_bias, norm_g = params
    u_q, u_k, u_v, u_z, small = seqs
    tq, tk, tv, state = carry
    rows = u_q.shape[0]
    hd = norm_g.shape[1]
    nh = u_q.shape[1] // hd
    m = row_mask(row0, rows, pad)
    incl, strict, eye = tri_masks(rows)
    tril = incl.astype(F32)
    qc = jax.nn.silu(conv4(tq, u_q, wq))
    kc = jax.nn.silu(conv4(tk, u_k, wk))
    vc = jax.nn.silu(conv4(tv, u_v, wv))
    outs, new_state = [], []
    for h in range(nh):
        sl = slice(h * hd, (h + 1) * hd)
        q = qc[:, sl]
        k = kc[:, sl]
        q = q * lax.rsqrt(jnp.sum(q * q, axis=-1, keepdims=True) + EPS) * (hd ** -0.5) * m
        k = k * lax.rsqrt(jnp.sum(k * k, axis=-1, keepdims=True) + EPS) * m
        v = vc[:, sl] * m
        beta = jax.nn.sigmoid(small[:, h:h + 1]) * m
        g = -jnp.exp(a_log[:, h:h + 1]) * jax.nn.softplus(small[:, nh + h:nh + h + 1] + dt_bias[:, h:h + 1]) * m
        s_h = state[sl]
        gc = hdot(tril, g)
        decay = jnp.exp(jnp.where(incl, gc - gc.T, NEG_BIG))
        k_beta = k * beta
        lmat = jnp.where(strict, bdot_nt(k_beta, k) * decay, 0.0)
        tinv = tri_inverse(lmat, eye)
        eg = jnp.exp(gc)
        u = hdot(tinv, v * beta)
        w = hdot(tinv, k_beta * eg)
        attn = bdot_nt(q, k) * decay
        g_last = gc[rows - 1:]
        v_new = u - bdot(w, s_h)
        o = bdot(q * eg, s_h) + bdot(attn, v_new)
        new_state.append(s_h * jnp.exp(g_last) + bdot_tn(k * jnp.exp(g_last - gc), v_new))
        o = rms_norm(o, norm_g) * jax.nn.silu(u_z[:, sl]) * m
        outs.append(o)
    t0 = rows - CONV_TAIL
    return (jnp.concatenate(outs, axis=1),), (u_q[t0:], u_k[t0:], u_v[t0:], jnp.concatenate(new_state, axis=0))


def ssd_chunk(pad, dt_lane0, n_groups, row0, params, seqs, carry):
    conv_w, conv_b, a_log, dt_bias, d_skip, norm_g = params
    u_z, u_xbc, small = seqs
    tail, state = carry
    rows = u_z.shape[0]
    width = u_z.shape[1]
    nh = a_log.shape[1]
    hd = width // nh
    ns = (u_xbc.shape[1] - width) // (2 * n_groups)
    hpg = nh // n_groups
    m = row_mask(row0, rows, pad)
    incl, _, _ = tri_masks(rows)
    tril = incl.astype(F32)
    xbc = jax.nn.silu(conv4(tail, u_xbc, conv_w) + conv_b)
    xs = xbc[:, :width]
    dt = jax.nn.softplus(small[:, dt_lane0:dt_lane0 + nh] + dt_bias)
    a_neg = -jnp.exp(a_log)
    ys, new_state = [], []
    for g in range(n_groups):
        bm = xbc[:, width + g * ns: width + (g + 1) * ns] * m
        cm = xbc[:, width + (n_groups + g) * ns: width + (n_groups + g + 1) * ns] * m
        cb = bdot_nt(cm, bm)
        for e in range(hpg):
            h = g * hpg + e
            sl = slice(h * hd, (h + 1) * hd)
            xh = xs[:, sl]
            dt_h = dt[:, h:h + 1]
            xdt = xh * dt_h * m
            a_h = dt_h * a_neg[:, h:h + 1] * m
            acum = hdot(tril, a_h)
            lmat = jnp.exp(jnp.where(incl, acum - acum.T, NEG_BIG))
            a_last = acum[rows - 1:]
            s_h = state[sl]
            y = bdot(cb * lmat, xdt) + bdot_nt(cm, s_h) * jnp.exp(acum)
            new_state.append(s_h * jnp.exp(a_last) + bdot_tn(xdt * jnp.exp(a_last - acum), bm))
            ys.append(y + d_skip[:, h:h + 1] * xh)
    y = jnp.concatenate(ys, axis=1) * jax.nn.silu(u_z)
    gw = width // n_groups
    outs = [rms_norm(y[:, g * gw:(g + 1) * gw], norm_g[:, g * gw:(g + 1) * gw]) for g in range(n_groups)]
    out = jnp.concatenate(outs, axis=1) * m
    return (out,), (u_xbc[rows - CONV_TAIL:], jnp.concatenate(new_state, axis=0))


def s5_chunk(pad, row0, params, seqs, carry):
    a_re, a_im, log_dt, b_re, b_im, c_re, c_im, d_skip = params
    (u,) = seqs
    s_re0, s_im0 = carry
    rows = u.shape[0]
    n_state = a_re.shape[1]
    n_grp = log_dt.shape[1]
    per = n_state // n_grp
    expand = (lax.broadcasted_iota(jnp.int32, (n_grp, n_state), 1) // per
              == lax.broadcasted_iota(jnp.int32, (n_grp, n_state), 0)).astype(F32)
    dt = jnp.exp(hdot(log_dt, expand))
    lam_re = jnp.minimum(a_re, -1e-4)
    lam_im = a_im
    mag = jnp.exp(dt * lam_re)
    ab_re = mag * jnp.cos(dt * lam_im)
    ab_im = mag * jnp.sin(dt * lam_im)
    den = lam_re * lam_re + lam_im * lam_im
    f_re = ((ab_re - 1.0) * lam_re + ab_im * lam_im) / den
    f_im = (ab_im * lam_re - (ab_re - 1.0) * lam_im) / den
    bb_re = f_re * b_re - f_im * b_im
    bb_im = f_re * b_im + f_im * b_re
    bu_re = bdot(u, bb_re)
    bu_im = bdot(u, bb_im)
    first = (lax.broadcasted_iota(jnp.int32, (rows, 1), 0) == 0).astype(F32)
    bu_re = bu_re + first * (ab_re * s_re0 - ab_im * s_im0)
    bu_im = bu_im + first * (ab_re * s_im0 + ab_im * s_re0)
    s_re, s_im = cscan_const(ab_re, ab_im, bu_re, bu_im)
    y = bdot(s_re, c_re) - bdot(s_im, c_im) + d_skip * u
    return (y,), (s_re[rows - 1:], s_im[rows - 1:])


def s5_post(pad, row0, params, seqs, carry):
    w_glu, norm_g = params
    (y,) = seqs
    y = jax.nn.gelu(y)
    y = y * jax.nn.sigmoid(bdot(y, w_glu))
    return (rms_norm(y, norm_g),), ()


def mix_out_delta(row0, params, seqs, carry):
    (w_out,) = params
    wd = seqs[0].shape[1]
    acc = bdot(seqs[0], w_out[0:wd])
    for k in range(1, len(seqs)):
        acc = acc + bdot(seqs[k], w_out[k * wd:(k + 1) * wd])
    return (acc,), ()


def blockdiag_expand(w):
    nh, a, b = w.shape
    eye = jnp.eye(nh, dtype=w.dtype)
    return (w[:, :, None, :] * eye[:, None, :, None]).reshape(nh * a, nh * b)


def blockdiag_extract(m, nh):
    a, b = m.shape[0] // nh, m.shape[1] // nh
    m4 = m.reshape(nh, a, nh, b)
    return jnp.stack([m4[h, :, h, :] for h in range(nh)], axis=0)


S5_LANES = LANE


def s5_params_expand(a_re, a_im, log_dt, b_re, b_im, c_re, c_im, d_skip):
    n_grp, n_state = a_re.shape
    ch = b_re.shape[-1]
    gpl = S5_LANES // ch
    nq = n_grp // gpl
    bexp = lambda b: jnp.stack([blockdiag_expand(jnp.swapaxes(b[q * gpl:(q + 1) * gpl], 1, 2)) for q in range(nq)])
    return (a_re.reshape(nq, 1, gpl * n_state), a_im.reshape(nq, 1, gpl * n_state), log_dt.reshape(nq, 1, gpl),
            bexp(b_re), bexp(b_im), bexp(c_re), bexp(c_im), d_skip.reshape(nq, 1, S5_LANES))


def s5_grads_extract(grads, n_grp, n_state, ch):
    da_re, da_im, dlog_dt, db_re, db_im, dc_re, dc_im, dd = grads
    gpl = S5_LANES // ch
    nq = n_grp // gpl
    bext = lambda b: jnp.concatenate([jnp.swapaxes(blockdiag_extract(b[q], gpl), 1, 2) for q in range(nq)], axis=0)
    return (da_re.reshape(n_grp, n_state), da_im.reshape(n_grp, n_state), dlog_dt.reshape(n_grp),
            bext(db_re), bext(db_im), bext(dc_re), bext(dc_im), dd.reshape(n_grp * ch))


def _cparams(**kw):
    return pltpu.CompilerParams(vmem_limit_bytes=VMEM_LIMIT, **kw)


def tiled_call(body_fn, name, *, n_steps, rows, n_groups=1, reverse=False,
               seq_in=(), whole_in=(), step_in=(), seq_out=(), acc_out=(), step_out=(), carry=()):
    def step_of(i):
        return (n_steps - 1 - i) if reverse else i

    def col_of(col, g):
        return col(g) if callable(col) else col

    in_specs, operands = [], []
    for arr, width, col in seq_in:
        in_specs.append(pl.BlockSpec((rows, width), lambda g, i, col=col: (step_of(i), col_of(col, g))))
        operands.append(arr)
    for arr in whole_in:
        if arr.ndim == 2:
            in_specs.append(pl.BlockSpec(arr.shape, lambda g, i: (0, 0)))
        else:
            in_specs.append(pl.BlockSpec((None,) + arr.shape[1:], lambda g, i: (g, 0, 0)))
        operands.append(arr)
    for arr in step_in:
        in_specs.append(pl.BlockSpec((None, None) + arr.shape[2:], lambda g, i: (g, step_of(i), 0, 0)))
        operands.append(arr)
    out_shape, out_specs = [], []
    for total, width, col, dt in seq_out:
        out_shape.append(jax.ShapeDtypeStruct((n_steps * rows, total), dt))
        out_specs.append(pl.BlockSpec((rows, width), lambda g, i, col=col: (step_of(i), col_of(col, g))))
    for r, c in acc_out:
        out_shape.append(jax.ShapeDtypeStruct((n_groups, r, c), F32))
        out_specs.append(pl.BlockSpec((None, r, c), lambda g, i: (g, 0, 0)))
    for r, c in step_out:
        out_shape.append(jax.ShapeDtypeStruct((n_groups, n_steps, r, c), F32))
        out_specs.append(pl.BlockSpec((None, None, r, c), lambda g, i: (g, step_of(i), 0, 0)))
    n_seq, n_whole, n_step = len(seq_in), len(whole_in), len(step_in)
    n_so, n_ao, n_sto = len(seq_out), len(acc_out), len(step_out)

    def body(*refs):
        pos = 0
        seq_refs = refs[pos:pos + n_seq]
        pos += n_seq
        whole_refs = refs[pos:pos + n_whole]
        pos += n_whole
        step_refs = refs[pos:pos + n_step]
        pos += n_step
        so_refs = refs[pos:pos + n_so]
        pos += n_so
        ao_refs = refs[pos:pos + n_ao]
        pos += n_ao
        sto_refs = refs[pos:pos + n_sto]
        pos += n_sto
        carry_refs = refs[pos:]
        i = pl.program_id(1)

        @pl.when(i == 0)
        def _():
            for r in carry_refs:
                r[...] = jnp.zeros(r.shape, r.dtype)
            for r in ao_refs:
                r[...] = jnp.zeros(r.shape, r.dtype)

        row0 = step_of(i) * rows
        seq_o, acc_o, step_o, new_c = body_fn(row0, [r[...] for r in whole_refs], [r[...] for r in seq_refs],
                                              [r[...] for r in step_refs], [r[...] for r in carry_refs])
        for r, val in zip(so_refs, seq_o, strict=True):
            r[...] = val.astype(r.dtype)
        for r, val in zip(ao_refs, acc_o, strict=True):
            r[...] += val
        for r, val in zip(sto_refs, step_o, strict=True):
            r[...] = val
        for r, val in zip(carry_refs, new_c, strict=True):
            r[...] = val

    return pl.pallas_call(
        body, name=name, grid=(n_groups, n_steps), in_specs=in_specs, out_specs=out_specs, out_shape=out_shape,
        scratch_shapes=[pltpu.VMEM((r, c), F32) for r, c in carry],
        compiler_params=_cparams(dimension_semantics=("arbitrary", "arbitrary")),
    )(*operands)


def mixer_fwd(fn, name, *, n_steps, rows, seqs, params, out, carry, n_groups=1):
    def body(row0, whole, seq_vals, steps, carry_vals):
        outs, new_c = fn(row0, tuple(whole), tuple(seq_vals), tuple(carry_vals))
        return list(outs), [], list(carry_vals), list(new_c)

    res = tiled_call(body, name, n_steps=n_steps, rows=rows, n_groups=n_groups, seq_in=seqs, whole_in=params,
                     seq_out=[out], step_out=carry, carry=carry)
    return res[0], list(res[1:])


def mixer_bwd(fn, name, *, n_steps, rows, seqs, params, dout, saved, carry, n_groups=1):
    n_seq = len(seqs)

    def body(row0, whole, seq_vals, steps, dcarry):
        params_f = tuple(p.astype(F32) for p in whole)
        _, vjp = jax.vjp(lambda p, s, c: fn(row0, p, s, c), params_f, tuple(seq_vals[:n_seq]), tuple(steps))
        dp, ds, dc = vjp(((seq_vals[n_seq],), tuple(dcarry)))
        return list(ds), list(dp), [], list(dc)

    seq_out = [(n_groups * w, w, (lambda g: g), F32) if callable(c) else (w, w, 0, F32) for a, w, c in seqs]
    acc_out = [p.shape[-2:] for p in params]
    res = tiled_call(body, name, n_steps=n_steps, rows=rows, n_groups=n_groups, reverse=True,
                     seq_in=list(seqs) + [dout], whole_in=params, step_in=saved,
                     seq_out=seq_out, acc_out=acc_out, carry=carry)
    return list(res[:n_seq]), list(res[n_seq:])


def rms_fwd(h, g, name, rows):
    def body(row0, whole, seqs, steps, carry):
        return [rms_norm(seqs[0], whole[0])], [], [], []
    d = h.shape[1]
    return tiled_call(body, name, n_steps=h.shape[0] // rows, rows=rows, seq_in=[(h, d, 0)], whole_in=[g],
                      seq_out=[(d, d, 0, BF16)])[0]


def rms_bwd_add(h, dxn, dh_out, g, name, rows):
    def body(row0, whole, seqs, steps, carry):
        _, vjp = jax.vjp(rms_norm, seqs[0], whole[0])
        dh, dg = vjp(seqs[1])
        return [seqs[2] + dh], [dg], [], []
    d = h.shape[1]
    dh_in, dg = tiled_call(body, name, n_steps=h.shape[0] // rows, rows=rows,
                           seq_in=[(h, d, 0), (dxn, d, 0), (dh_out, d, 0)], whole_in=[g],
                           seq_out=[(d, d, 0, F32)], acc_out=[(1, d)])
    return dh_in, dg[0]


def mix_out_fwd(h, ys, w_out, name, rows):
    def body(row0, whole, seqs, steps, carry):
        (delta,), _ = mix_out_delta(row0, (whole[0],), tuple(seqs[1:]), ())
        return [seqs[0] + delta], [], [], []
    d, wd = h.shape[1], ys[0].shape[1]
    return tiled_call(body, name, n_steps=h.shape[0] // rows, rows=rows,
                      seq_in=[(h, d, 0)] + [(y, wd, 0) for y in ys], whole_in=[w_out], seq_out=[(d, d, 0, F32)])[0]


def loss_and_grad(h, target, g, name, rows, first_row):
    def body(row0, whole, seqs, steps, carry):
        hh, tt = seqs
        keep = row_mask(row0, hh.shape[0], first_row)

        def f(hv, gv):
            err = rms_norm(hv, gv) - tt
            return 0.5 * jnp.sum(jnp.mean(err * err, axis=-1, keepdims=True) * keep, axis=0, keepdims=True)

        val, vjp = jax.vjp(f, hh, whole[0])
        dh, dg = vjp(jnp.ones((1, 1), F32))
        return [dh], [jnp.broadcast_to(val, (1, LANE)), dg], [], []
    d = h.shape[1]
    dh, loss, dg = tiled_call(body, name, n_steps=h.shape[0] // rows, rows=rows,
                              seq_in=[(h, d, 0), (target, d, 0)], whole_in=[g],
                              seq_out=[(d, d, 0, F32)], acc_out=[(1, LANE), (1, d)])
    return loss[0, 0, 0], dh, dg[0]


def _pick(n, cands):
    for c in cands:
        if n % c == 0:
            return c
    raise ValueError(f"no tile for {n}")


ROW_TILES = (1056, 704, 352, 192, 96, 64)
COL_TILES = (256, 128)
NT_DIMS = (((1,), (1,)), ((), ()))
TN_DIMS = (((0,), (0,)), ((), ()))


def ffn_fwd(h, xn, wg, wu, wd, name):
    t, d = h.shape
    f = wg.shape[1]
    tm = _pick(t, ROW_TILES)
    tn = _pick(f, COL_TILES)
    n_j = f // tn

    def body(h_ref, xn_ref, wg_ref, wu_ref, wd_ref, o_ref, acc_ref):
        j = pl.program_id(1)

        @pl.when(j == 0)
        def _():
            acc_ref[...] = jnp.zeros(acc_ref.shape, F32)

        x = xn_ref[...]
        g = jnp.dot(x, wg_ref[...], preferred_element_type=F32)
        u = jnp.dot(x, wu_ref[...], preferred_element_type=F32)
        a = (jax.nn.silu(g) * u).astype(BF16)
        acc_ref[...] += jnp.dot(a, wd_ref[...], preferred_element_type=F32)

        @pl.when(j == n_j - 1)
        def _():
            o_ref[...] = h_ref[...] + 0.5 * acc_ref[...]

    return pl.pallas_call(
        body, name=name, grid=(t // tm, n_j),
        in_specs=[pl.BlockSpec((tm, d), lambda i, j: (i, 0)), pl.BlockSpec((tm, d), lambda i, j: (i, 0)),
                  pl.BlockSpec((d, tn), lambda i, j: (0, j)), pl.BlockSpec((d, tn), lambda i, j: (0, j)),
                  pl.BlockSpec((tn, d), lambda i, j: (j, 0))],
        out_specs=pl.BlockSpec((tm, d), lambda i, j: (i, 0)),
        out_shape=jax.ShapeDtypeStruct((t, d), F32),
        scratch_shapes=[pltpu.VMEM((tm, d), F32)],
        compiler_params=_cparams(dimension_semantics=("arbitrary", "arbitrary")),
    )(h, xn, wg, wu, wd)


def ffn_bwd(xn, dh, wg, wu, wd, name):
    t, d = dh.shape
    f = wg.shape[1]
    tm = _pick(t, ROW_TILES)
    tn = _pick(f, COL_TILES)

    def body(xn_ref, dh_ref, wg_ref, wu_ref, wd_ref, dxn_ref, dwg_ref, dwu_ref, dwd_ref):
        j, i = pl.program_id(0), pl.program_id(1)
        rows = pl.ds(pl.multiple_of(i * tm, 8), tm)
        x = xn_ref[rows, :]
        dhh = (0.5 * dh_ref[...]).astype(BF16)
        wgv, wuv = wg_ref[...], wu_ref[...]
        g = jnp.dot(x, wgv, preferred_element_type=F32)
        u = jnp.dot(x, wuv, preferred_element_type=F32)
        sg = jax.nn.sigmoid(g)
        s = g * sg
        da = lax.dot_general(dhh, wd_ref[...], NT_DIMS, preferred_element_type=F32)
        dwd = lax.dot_general((s * u).astype(BF16), dhh, TN_DIMS, preferred_element_type=F32)
        dg = (da * u * (sg * (1.0 + g * (1.0 - sg)))).astype(BF16)
        du = (da * s).astype(BF16)
        dwg = lax.dot_general(x, dg, TN_DIMS, preferred_element_type=F32)
        dwu = lax.dot_general(x, du, TN_DIMS, preferred_element_type=F32)
        dx = (lax.dot_general(dg, wgv, NT_DIMS, preferred_element_type=F32)
              + lax.dot_general(du, wuv, NT_DIMS, preferred_element_type=F32))

        @pl.when(i == 0)
        def _():
            dwg_ref[...] = dwg
            dwu_ref[...] = dwu
            dwd_ref[...] = dwd

        @pl.when(i > 0)
        def _():
            dwg_ref[...] += dwg
            dwu_ref[...] += dwu
            dwd_ref[...] += dwd

        @pl.when(j == 0)
        def _():
            dxn_ref[rows, :] = dx

        @pl.when(j > 0)
        def _():
            dxn_ref[rows, :] += dx

    return pl.pallas_call(
        body, name=name, grid=(f // tn, t // tm),
        in_specs=[pl.BlockSpec((t, d), lambda j, i: (0, 0)), pl.BlockSpec((tm, d), lambda j, i: (i, 0)),
                  pl.BlockSpec((d, tn), lambda j, i: (0, j)), pl.BlockSpec((d, tn), lambda j, i: (0, j)),
                  pl.BlockSpec((tn, d), lambda j, i: (j, 0))],
        out_specs=[pl.BlockSpec((t, d), lambda j, i: (0, 0)), pl.BlockSpec((d, tn), lambda j, i: (0, j)),
                   pl.BlockSpec((d, tn), lambda j, i: (0, j)), pl.BlockSpec((tn, d), lambda j, i: (j, 0))],
        out_shape=[jax.ShapeDtypeStruct((t, d), F32), jax.ShapeDtypeStruct((d, f), F32),
                   jax.ShapeDtypeStruct((d, f), F32), jax.ShapeDtypeStruct((f, d), F32)],
        compiler_params=_cparams(dimension_semantics=("arbitrary", "arbitrary")),
    )(xn, dh, wg, wu, wd)


def matmul_cols(xn, w, name):
    t, d = xn.shape
    n = w.shape[1]
    tn = _pick(n, COL_TILES)

    def body(x_ref, w_ref, o_ref):
        o_ref[...] = jnp.dot(x_ref[...], w_ref[...], preferred_element_type=F32)

    return pl.pallas_call(
        body, name=name, grid=(n // tn,),
        in_specs=[pl.BlockSpec((t, d), lambda j: (0, 0)), pl.BlockSpec((d, tn), lambda j: (0, j))],
        out_specs=pl.BlockSpec((t, tn), lambda j: (0, j)),
        out_shape=jax.ShapeDtypeStruct((t, n), F32),
        compiler_params=_cparams(dimension_semantics=("arbitrary",)),
    )(xn, w)


def matmul_cols_bwd(xn, dy, w, name):
    t, d = xn.shape
    n = w.shape[1]
    tn = _pick(n, COL_TILES)

    def body(x_ref, dy_ref, w_ref, dx_ref, dw_ref):
        j = pl.program_id(0)
        dyv = dy_ref[...].astype(BF16)
        dw_ref[...] = lax.dot_general(x_ref[...], dyv, TN_DIMS, preferred_element_type=F32)
        dx = lax.dot_general(dyv, w_ref[...], NT_DIMS, preferred_element_type=F32)

        @pl.when(j == 0)
        def _():
            dx_ref[...] = dx

        @pl.when(j > 0)
        def _():
            dx_ref[...] += dx

    return pl.pallas_call(
        body, name=name, grid=(n // tn,),
        in_specs=[pl.BlockSpec((t, d), lambda j: (0, 0)), pl.BlockSpec((t, tn), lambda j: (0, j)),
                  pl.BlockSpec((d, tn), lambda j: (0, j))],
        out_specs=[pl.BlockSpec((t, d), lambda j: (0, 0)), pl.BlockSpec((d, tn), lambda j: (0, j))],
        out_shape=[jax.ShapeDtypeStruct((t, d), F32), jax.ShapeDtypeStruct((d, n), F32)],
        compiler_params=_cparams(dimension_semantics=("arbitrary",)),
    )(xn, dy, w)


def exchange(x, name, all_to_all):
    blk = x.shape[-2:]

    def body(x_ref, o_ref, send_sems, recv_sems, local_sem):
        mx, my, mc = lax.axis_index("x"), lax.axis_index("y"), lax.axis_index("c")
        me = 4 * mx + 2 * my + mc

        def src(dst):
            return x_ref.at[dst] if all_to_all else x_ref

        local = pltpu.make_async_copy(src(me), o_ref.at[me], local_sem)
        local.start()
        sends, recvs = [], []
        for k in range(1, N_DEV):
            px = 1 - mx if (k >> 2) & 1 else mx
            py = 1 - my if (k >> 1) & 1 else my
            pc = 1 - mc if k & 1 else mc
            peer = 4 * px + 2 * py + pc
            common = dict(send_sem=send_sems.at[k - 1], recv_sem=recv_sems.at[k - 1], device_id=(px, py, pc),
                          device_id_type=pl.DeviceIdType.MESH)
            sends.append(pltpu.make_async_remote_copy(src_ref=src(peer), dst_ref=o_ref.at[me], **common))
            recvs.append(pltpu.make_async_remote_copy(src_ref=src(peer), dst_ref=o_ref.at[peer], **common))
        for cp in sends:
            cp.start()
        for cp in recvs:
            cp.wait_recv()
        for cp in sends:
            cp.wait_send()
        local.wait()

    return pl.pallas_call(
        body, name=name,
        in_specs=[pl.BlockSpec(memory_space=pl.ANY)], out_specs=pl.BlockSpec(memory_space=pl.ANY),
        out_shape=jax.ShapeDtypeStruct((N_DEV,) + blk, x.dtype),
        scratch_shapes=[pltpu.SemaphoreType.DMA((N_DEV - 1,)), pltpu.SemaphoreType.DMA((N_DEV - 1,)),
                        pltpu.SemaphoreType.DMA(())],
    )(x)


PACK_COLS = 1024
PACK_ROWS = 256


def adamw_reduce(parts, w, m, v, name):
    r, c = w.shape
    c1 = 1.0 - ADAM_B1 ** ADAM_STEP
    c2 = 1.0 - ADAM_B2 ** ADAM_STEP

    def body(p_ref, w_ref, m_ref, v_ref, g_ref, d_ref, mo_ref, vo_ref):
        g = p_ref[0].astype(F32)
        for k in range(1, N_DEV):
            g = g + p_ref[k].astype(F32)
        mn = ADAM_B1 * m_ref[...] + (1.0 - ADAM_B1) * g
        vn = ADAM_B2 * v_ref[...] + (1.0 - ADAM_B2) * (g * g)
        g_ref[...] = g
        mo_ref[...] = mn
        vo_ref[...] = vn
        d_ref[...] = -ADAM_LR * ((mn / c1) / (jnp.sqrt(vn / c2) + ADAM_EPS) + ADAM_WD * w_ref[...])

    spec = pl.BlockSpec((PACK_ROWS, c), lambda i: (i, 0))
    return pl.pallas_call(
        body, name=name, grid=(r // PACK_ROWS,),
        in_specs=[pl.BlockSpec((N_DEV, PACK_ROWS, c), lambda i: (0, i, 0)), spec, spec, spec],
        out_specs=[spec] * 4, out_shape=[jax.ShapeDtypeStruct((r, c), F32)] * 4,
        compiler_params=_cparams(dimension_semantics=("arbitrary",)),
    )(parts, w, m, v)


def pack_flat(arrs, dtype):
    flat = jnp.concatenate([a.reshape(-1).astype(dtype) for a in arrs])
    tile = PACK_ROWS * PACK_COLS
    total = -(-flat.shape[0] // tile) * tile
    return jnp.pad(flat, (0, total - flat.shape[0])).reshape(total // PACK_COLS, PACK_COLS)


def unpack_flat(buf, shapes, lead=()):
    flat = buf.reshape(lead + (-1,))
    out, off = [], 0
    for s in shapes:
        n = math.prod(s)
        out.append(flat[..., off:off + n].reshape(lead + tuple(s)))
        off += n
    return out


W_NAMES = ('meta_tokens', 'ffn1_norm', 'ffn1_w_gate', 'ffn1_w_up', 'ffn1_w_down', 'mix_norm', 'w_in', 'w_out',
           'lru_conv_w', 'lru_conv_b', 'lru_w_a', 'lru_b_a', 'lru_w_i', 'lru_b_i', 'lru_lambda', 'lru_norm',
           'gdn_conv_w', 'gdn_a_log', 'gdn_dt_bias', 'gdn_norm', 'ssd_conv_w', 'ssd_conv_b', 'ssd_a_log',
           'ssd_dt_bias', 'ssd_d', 'ssd_norm', 's5_a_re', 's5_a_im', 's5_log_dt', 's5_b_re', 's5_b_im', 's5_c_re',
           's5_c_im', 's5_d', 's5_w_glu', 's5_norm', 'ffn2_norm', 'ffn2_w_gate', 'ffn2_w_up', 'ffn2_w_down',
           'final_norm')
SHARD_AXIS = {'meta_tokens': 1, 'ffn1_w_gate': 2, 'ffn1_w_up': 2, 'ffn1_w_down': 1, 'w_in': 2, 'w_out': 1,
              'lru_conv_w': 2, 'gdn_conv_w': 2, 'ssd_conv_w': 2, 's5_w_glu': 1, 'ffn2_w_gate': 2, 'ffn2_w_up': 2,
              'ffn2_w_down': 1}
BIG_NAMES = ('ffn1_w_gate', 'ffn1_w_up', 'ffn1_w_down', 'w_in', 'w_out', 's5_w_glu', 'ffn2_w_gate', 'ffn2_w_up',
             'ffn2_w_down')
SMALL_NAMES = ('meta_tokens', 'lru_conv_w', 'gdn_conv_w', 'ssd_conv_w')
SHARD_NAMES = tuple(n for n in W_NAMES if n in SHARD_AXIS)
REP_NAMES = tuple(n for n in W_NAMES if n not in SHARD_AXIS)
SSD_GROUPS = 2
S5_CH = 16


def unshard(g, axis):
    return jnp.concatenate([g[p] for p in range(N_DEV)], axis=axis)


def to_shards(a, axis):
    return jnp.stack(jnp.split(a, N_DEV, axis=axis), axis=0)


def kernel(*args):
    n_w = len(W_NAMES)
    x = args[0]
    w = dict(zip(W_NAMES, args[1:1 + n_w]))
    target = args[1 + n_w]
    m_in = dict(zip(W_NAMES, args[2 + n_w:2 + 2 * n_w]))
    v_in = dict(zip(W_NAMES, args[2 + 2 * n_w:2 + 3 * n_w]))

    depth, d = w['ffn1_norm'].shape
    seq = x.shape[1]
    n_meta = w['meta_tokens'].shape[0]
    pad = CHUNK - n_meta
    tp = pad + n_meta + seq
    n_chunks = tp // CHUNK
    wg = d // 2
    xbc_w = w['ssd_conv_w'].shape[-1] * N_DEV
    gdn_hd = w['gdn_norm'].shape[-1]
    gdn_h = wg // gdn_hd
    ssd_h = w['ssd_a_log'].shape[-1]
    lru_h = w['lru_w_a'].shape[1]
    s5_g, s5_n = w['s5_a_re'].shape[1:]
    s5_q = wg // S5_LANES
    row_tile = _pick(tp, (192, 96, 64))

    big = exchange(pack_flat([w[n] for n in BIG_NAMES], BF16), "gather_big", False)
    small = exchange(pack_flat([w[n] for n in SMALL_NAMES], F32), "gather_small", False)
    full = {}
    for n, g in zip(BIG_NAMES, unpack_flat(big, [w[n].shape for n in BIG_NAMES], (N_DEV,))):
        full[n] = unshard(g, SHARD_AXIS[n])
    for n, g in zip(SMALL_NAMES, unpack_flat(small, [w[n].shape for n in SMALL_NAMES], (N_DEV,))):
        full[n] = unshard(g, SHARD_AXIS[n])

    segs = [('a_x', wg), ('a_gate', wg), ('b_q', wg), ('b_k', wg), ('b_v', wg), ('b_z', wg), ('c_xbc', xbc_w),
            ('c_z', wg), ('d_u', wg), ('small_b', LANE), ('small_c', LANE)]
    off, o = {}, 0
    for nme, wd_ in segs:
        assert o % wd_ == 0, (nme, o, wd_)
        off[nme] = o
        o += wd_
    o_beta = 6 * wg
    o_cz = o_beta + 2 * gdn_h
    o_xbc = o_cz + wg
    o_dt = o_xbc + xbc_w
    o_du = o_dt + ssd_h

    def pack_cols(a):
        z = lambda k: jnp.zeros(a.shape[:-1] + (k,), a.dtype)
        return jnp.concatenate([a[..., :o_beta], a[..., o_xbc:o_dt], a[..., o_cz:o_xbc], a[..., o_du:],
                                a[..., o_beta:o_cz], z(LANE - 2 * gdn_h), a[..., o_dt:o_du], z(LANE - ssd_h)], axis=-1)

    def unpack_cols(a):
        sb, sc = off['small_b'], off['small_c']
        return jnp.concatenate([a[..., :o_beta], a[..., sb:sb + 2 * gdn_h], a[..., off['c_z']:off['c_z'] + wg],
                                a[..., off['c_xbc']:off['c_xbc'] + xbc_w], a[..., sc:sc + ssd_h],
                                a[..., off['d_u']:off['d_u'] + wg]], axis=-1)

    w_in_p = pack_cols(full['w_in'])

    def col(name, width):
        return off[name] // width

    def row(a):
        return a.reshape(1, -1)

    def layer_params(l):
        gcw = full['gdn_conv_w'][l]
        lru = [full['lru_conv_w'][l], row(w['lru_conv_b'][l]), blockdiag_expand(w['lru_w_a'][l]), row(w['lru_b_a'][l]),
               blockdiag_expand(w['lru_w_i'][l]), row(w['lru_b_i'][l]), row(w['lru_lambda'][l]), row(w['lru_norm'][l])]
        gdn = [gcw[:, :wg], gcw[:, wg:2 * wg], gcw[:, 2 * wg:], row(w['gdn_a_log'][l]), row(w['gdn_dt_bias'][l]),
               row(w['gdn_norm'][l])]
        ssd = [full['ssd_conv_w'][l], row(w['ssd_conv_b'][l]), row(w['ssd_a_log'][l]), row(w['ssd_dt_bias'][l]),
               row(w['ssd_d'][l]), row(w['ssd_norm'][l])]
        s5 = list(s5_params_expand(*[w[n][l] for n in ('s5_a_re', 's5_a_im', 's5_log_dt', 's5_b_re', 's5_b_im',
                                                          's5_c_re', 's5_c_im', 's5_d')]))
        post = [full['s5_w_glu'][l], row(w['s5_norm'][l])]
        return lru, gdn, ssd, s5, post

    lru_fn = functools.partial(lru_chunk, pad)
    gdn_fn = functools.partial(gdn_chunk, pad)
    ssd_fn = functools.partial(ssd_chunk, pad, 0, SSD_GROUPS)
    s5_fn = functools.partial(s5_chunk, pad)
    post_fn = functools.partial(s5_post, pad)
    n_state_lanes = (S5_LANES // S5_CH) * s5_n

    def mixer_specs(proj):
        lru_seqs = [(proj, wg, col('a_x', wg)), (proj, wg, col('a_gate', wg))]
        gdn_seqs = [(proj, wg, col('b_q', wg)), (proj, wg, col('b_k', wg)), (proj, wg, col('b_v', wg)),
                    (proj, wg, col('b_z', wg)), (proj, LANE, col('small_b', LANE))]
        ssd_seqs = [(proj, wg, col('c_z', wg)), (proj, xbc_w, col('c_xbc', xbc_w)), (proj, LANE, col('small_c', LANE))]
        base = col('d_u', S5_LANES)
        s5_seqs = [(proj, S5_LANES, lambda g: base + g)]
        return lru_seqs, gdn_seqs, ssd_seqs, s5_seqs

    lru_carry = [(CONV_TAIL, wg), (1, wg)]
    gdn_carry = [(CONV_TAIL, wg)] * 3 + [(wg, gdn_hd)]
    ssd_carry = [(CONV_TAIL, xbc_w), (wg, (xbc_w - wg) // (2 * SSD_GROUPS))]
    s5_carry = [(1, n_state_lanes)] * 2
    mk = dict(n_steps=n_chunks, rows=CHUNK)
    rk = dict(n_steps=tp // row_tile, rows=row_tile)
    out_w = (wg, wg, 0, F32)

    h = jnp.concatenate([jnp.zeros((pad, d), F32), full['meta_tokens'], x[0]], axis=0)
    target_p = jnp.concatenate([jnp.zeros((pad + n_meta, d), F32), target[0]], axis=0)
    saved = []
    for l in range(depth):
        lru_p, gdn_p, ssd_p, s5_p, post_p = layer_params(l)
        h0 = h
        xn1 = rms_fwd(h0, row(w['ffn1_norm'][l]), "rms_fwd", row_tile)
        h1 = ffn_fwd(h0, xn1, full['ffn1_w_gate'][l], full['ffn1_w_up'][l], full['ffn1_w_down'][l], "ffn_fwd")
        xn2 = rms_fwd(h1, row(w['mix_norm'][l]), "rms_fwd", row_tile)
        proj = matmul_cols(xn2, w_in_p[l], "mix_in_fwd")
        lru_s, gdn_s, ssd_s, s5_s = mixer_specs(proj)
        ya, lru_c = mixer_fwd(lru_fn, "lru_fwd", seqs=lru_s, params=lru_p, out=out_w, carry=lru_carry, **mk)
        yb, gdn_c = mixer_fwd(gdn_fn, "gdn_fwd", seqs=gdn_s, params=gdn_p, out=out_w, carry=gdn_carry, **mk)
        yc, ssd_c = mixer_fwd(ssd_fn, "ssd_fwd", seqs=ssd_s, params=ssd_p, out=out_w, carry=ssd_carry, **mk)
        y1, s5_c = mixer_fwd(s5_fn, "s5_fwd", seqs=s5_s, params=s5_p, out=(wg, S5_LANES, lambda g: g, F32),
                             carry=s5_carry, n_groups=s5_q, **mk)
        yd, _ = mixer_fwd(post_fn, "s5_post_fwd", seqs=[(y1, wg, 0)], params=post_p, out=out_w, carry=[], **rk)
        h2 = mix_out_fwd(h1, [ya, yb, yc, yd], full['w_out'][l], "mix_out_fwd", row_tile)
        xn3 = rms_fwd(h2, row(w['ffn2_norm'][l]), "rms_fwd", row_tile)
        h3 = ffn_fwd(h2, xn3, full['ffn2_w_gate'][l], full['ffn2_w_up'][l], full['ffn2_w_down'][l], "ffn_fwd")
        saved.append((h0, xn1, h1, xn2, proj, (ya, yb, yc, yd), y1, (lru_c, gdn_c, ssd_c, s5_c), h2, xn3))
        h = h3

    loss_part, dh, d_final = loss_and_grad(h, target_p, row(w['final_norm']), "loss", row_tile, pad + n_meta)
    loss = lax.psum(loss_part, ("x", "y", "c"))

    gw = {n: [None] * depth for n in W_NAMES if n not in ('meta_tokens', 'final_norm')}
    for l in reversed(range(depth)):
        lru_p, gdn_p, ssd_p, s5_p, post_p = layer_params(l)
        h0, xn1, h1, xn2, proj, ys, y1, (lru_c, gdn_c, ssd_c, s5_c), h2, xn3 = saved[l]
        dxn, gw['ffn2_w_gate'][l], gw['ffn2_w_up'][l], gw['ffn2_w_down'][l] = ffn_bwd(
            xn3, dh, full['ffn2_w_gate'][l], full['ffn2_w_up'][l], full['ffn2_w_down'][l], "ffn_bwd")
        dh, dg = rms_bwd_add(h2, dxn, dh, row(w['ffn2_norm'][l]), "rms_bwd", row_tile)
        gw['ffn2_norm'][l] = dg[0]

        dys, (d_wout,) = mixer_bwd(mix_out_delta, "mix_out_bwd", seqs=[(y, wg, 0) for y in ys],
                                   params=[full['w_out'][l]], dout=(dh, d, 0), saved=[], carry=[], **rk)
        gw['w_out'][l] = d_wout[0]
        lru_s, gdn_s, ssd_s, s5_s = mixer_specs(proj)
        (dy1,), d_post = mixer_bwd(post_fn, "s5_post_bwd", seqs=[(y1, wg, 0)], params=post_p, dout=(dys[3], wg, 0),
                                   saved=[], carry=[], **rk)
        gw['s5_w_glu'][l], gw['s5_norm'][l] = d_post[0][0], d_post[1][0, 0]
        (d_du,), d_s5 = mixer_bwd(s5_fn, "s5_bwd", seqs=s5_s, params=s5_p, dout=(dy1, S5_LANES, lambda g: g),
                                  saved=s5_c, carry=s5_carry, n_groups=s5_q, **mk)
        for n, g in zip(('s5_a_re', 's5_a_im', 's5_log_dt', 's5_b_re', 's5_b_im', 's5_c_re', 's5_c_im', 's5_d'),
                        s5_grads_extract(d_s5, s5_g, s5_n, S5_CH)):
            gw[n][l] = g
        (d_cz, d_cxbc, d_sc), d_ssd = mixer_bwd(ssd_fn, "ssd_bwd", seqs=ssd_s, params=ssd_p, dout=(dys[2], wg, 0),
                                                saved=ssd_c, carry=ssd_carry, **mk)
        for n, g in zip(('ssd_conv_w', 'ssd_conv_b', 'ssd_a_log', 'ssd_dt_bias', 'ssd_d', 'ssd_norm'), d_ssd):
            gw[n][l] = g[0] if n == 'ssd_conv_w' else g[0, 0]
        (d_bq, d_bk, d_bv, d_bz, d_sb), d_gdn = mixer_bwd(gdn_fn, "gdn_bwd", seqs=gdn_s, params=gdn_p,
                                                          dout=(dys[1], wg, 0), saved=gdn_c, carry=gdn_carry, **mk)
        gw['gdn_conv_w'][l] = jnp.concatenate([d_gdn[0][0], d_gdn[1][0], d_gdn[2][0]], axis=1)
        gw['gdn_a_log'][l], gw['gdn_dt_bias'][l], gw['gdn_norm'][l] = d_gdn[3][0, 0], d_gdn[4][0, 0], d_gdn[5][0, 0]
        (d_ax, d_ag), d_lru = mixer_bwd(lru_fn, "lru_bwd", seqs=lru_s, params=lru_p, dout=(dys[0], wg, 0),
                                        saved=lru_c, carry=lru_carry, **mk)
        gw['lru_conv_w'][l], gw['lru_conv_b'][l] = d_lru[0][0], d_lru[1][0, 0]
        gw['lru_w_a'][l], gw['lru_b_a'][l] = blockdiag_extract(d_lru[2][0], lru_h), d_lru[3][0, 0]
        gw['lru_w_i'][l], gw['lru_b_i'][l] = blockdiag_extract(d_lru[4][0], lru_h), d_lru[5][0, 0]
        gw['lru_lambda'][l], gw['lru_norm'][l] = d_lru[6][0, 0], d_lru[7][0, 0]

        dproj = jnp.concatenate([d_ax, d_ag, d_bq, d_bk, d_bv, d_bz, d_cxbc, d_cz, d_du, d_sb, d_sc], axis=1)
        dxn, d_win_p = matmul_cols_bwd(xn2, dproj, w_in_p[l], "mix_in_bwd")
        gw['w_in'][l] = unpack_cols(d_win_p)
        dh, dg = rms_bwd_add(h1, dxn, dh, row(w['mix_norm'][l]), "rms_bwd", row_tile)
        gw['mix_norm'][l] = dg[0]

        dxn, gw['ffn1_w_gate'][l], gw['ffn1_w_up'][l], gw['ffn1_w_down'][l] = ffn_bwd(
            xn1, dh, full['ffn1_w_gate'][l], full['ffn1_w_up'][l], full['ffn1_w_down'][l], "ffn_bwd")
        dh, dg = rms_bwd_add(h0, dxn, dh, row(w['ffn1_norm'][l]), "rms_bwd", row_tile)
        gw['ffn1_norm'][l] = dg[0]

    grad_x = dh[pad + n_meta:][None]
    grads = {n: jnp.stack(g, axis=0) for n, g in gw.items()}
    grads['meta_tokens'] = dh[pad:pad + n_meta]
    grads['final_norm'] = d_final[0]

    part_sh = jnp.stack([pack_flat([to_shards(grads[n], SHARD_AXIS[n])[p] for n in SHARD_NAMES], BF16)
                         for p in range(N_DEV)], axis=0)
    recv_sh = exchange(part_sh, "scatter_grads", True)
    recv_rep = exchange(pack_flat([grads[n] for n in REP_NAMES], F32), "gather_rep_grads", False)
    out = {}
    for names, recv, nm in ((SHARD_NAMES, recv_sh, "adamw_sharded"), (REP_NAMES, recv_rep, "adamw_replicated")):
        res = adamw_reduce(recv, pack_flat([w[n] for n in names], F32), pack_flat([m_in[n] for n in names], F32),
                           pack_flat([v_in[n] for n in names], F32), nm)
        shapes = [w[n].shape for n in names]
        for kind, buf in zip(('grad', 'delta', 'new_m', 'new_v'), res):
            for n, a in zip(names, unpack_flat(buf, shapes)):
                out[kind, n] = a
    return (loss, grad_x) + tuple(out[k, n] for k in ('grad', 'delta', 'new_m', 'new_v') for n in W_NAMES)
```

```python
import functools
import math

import jax
import jax.numpy as jnp
from jax import lax
from jax.experimental import pallas as pl
from jax.experimental.pallas import tpu as pltpu

F32 = jnp.float32
BF16 = jnp.bfloat16

EPS = 1e-6
CHUNK = 64
CONV_K = 4
CONV_TAIL = 8
LRU_C = 8.0
LANE = 128
SUBLANE = 8
N_DEV = 8
NEG_BIG = -1e30

ADAM_LR = 0.001
ADAM_B1 = 0.9
ADAM_B2 = 0.999
ADAM_EPS = 1e-08
ADAM_WD = 0.01
ADAM_STEP = 10

VMEM_LIMIT = 56 * 1024 * 1024


def _dg(a, b, dims):
    return lax.dot_general(a.astype(BF16), b.astype(BF16), (dims, ((), ())), preferred_element_type=F32)


@jax.custom_vjp
def bdot(a, b):
    return _dg(a, b, ((1,), (0,)))


@jax.custom_vjp
def bdot_nt(a, b):
    return _dg(a, b, ((1,), (1,)))


@jax.custom_vjp
def bdot_tn(a, b):
    return _dg(a, b, ((0,), (0,)))


bdot.defvjp(lambda a, b: (bdot(a, b), (a, b)),
            lambda r, g: (bdot_nt(g, r[1]).astype(r[0].dtype), bdot_tn(r[0], g).astype(r[1].dtype)))
bdot_nt.defvjp(lambda a, b: (bdot_nt(a, b), (a, b)),
               lambda r, g: (bdot(g, r[1]).astype(r[0].dtype), bdot_tn(g, r[0]).astype(r[1].dtype)))
bdot_tn.defvjp(lambda a, b: (bdot_tn(a, b), (a, b)),
               lambda r, g: (bdot_nt(r[1], g).astype(r[0].dtype), bdot(r[0], g).astype(r[1].dtype)))


def hdot(a, b):
    return jnp.dot(a, b, preferred_element_type=F32, precision=lax.Precision.HIGHEST)


def rms_norm(x, g):
    return x * lax.rsqrt(jnp.mean(x * x, axis=-1, keepdims=True) + EPS) * g


def row_mask(row0, rows, pad):
    r = row0 + lax.broadcasted_iota(jnp.int32, (rows, 1), 0)
    return (r >= pad).astype(F32)


def conv4(tail, u, w):
    rows = u.shape[0]
    xe = jnp.concatenate([tail, u], axis=0)
    y = w[0:1] * xe[CONV_TAIL - 3:CONV_TAIL - 3 + rows]
    for k in range(1, CONV_K):
        y = y + w[k:k + 1] * xe[CONV_TAIL - 3 + k:CONV_TAIL - 3 + k + rows]
    return y


def shift_rows(x, s, fill):
    rows = x.shape[0]
    return jnp.concatenate([jnp.full((s, x.shape[1]), fill, x.dtype), x[:rows - s]], axis=0)


def lin_scan(a, b):
    rows = a.shape[0]
    s = 1
    while s < rows:
        b = a * shift_rows(b, s, 0.0) + b
        a = a * shift_rows(a, s, 1.0)
        s *= 2
    return b


def cscan_const(ar, ai, br, bi):
    rows = br.shape[0]
    s = 1
    while s < rows:
        brs, bis = shift_rows(br, s, 0.0), shift_rows(bi, s, 0.0)
        br, bi = br + ar * brs - ai * bis, bi + ar * bis + ai * brs
        ar, ai = ar * ar - ai * ai, 2.0 * ar * ai
        s *= 2
    return br, bi


def neg_expm1(z):
    t = jnp.tanh(0.5 * z)
    return -2.0 * t / (1.0 - t)


def tri_masks(n):
    r = lax.broadcasted_iota(jnp.int32, (n, n), 0)
    c = lax.broadcasted_iota(jnp.int32, (n, n), 1)
    return r >= c, r > c, (r == c).astype(F32)


def tri_inverse(lmat, eye):
    p = eye - lmat
    m = hdot(lmat, lmat)
    n = lmat.shape[0]
    s = 2
    while s < n:
        p = p + hdot(p, m)
        s *= 2
        if s < n:
            m = hdot(m, m)
    return p


def lru_chunk(pad, row0, params, seqs, carry):
    conv_w, conv_b, w_a, b_a, w_i, b_i, lam, norm_g = params
    u_x, u_gate = seqs
    tail, h0 = carry
    rows = u_x.shape[0]
    m = row_mask(row0, rows, pad)
    xc = conv4(tail, u_x, conv_w) + conv_b
    r = jax.nn.sigmoid(bdot(xc, w_a) + b_a)
    ig = jax.nn.sigmoid(bdot(xc, w_i) + b_i)
    log_a = -LRU_C * r * jax.nn.softplus(-lam)
    a = jnp.exp(log_a)
    b = jnp.sqrt(neg_expm1(2.0 * log_a)) * (ig * xc) * m
    first = (lax.broadcasted_iota(jnp.int32, (rows, 1), 0) == 0).astype(F32)
    b = b + first * (a * h0)
    h = lin_scan(a, b)
    y = jax.nn.gelu(u_gate) * h
    out = rms_norm(y, norm_g) * m
    return (out,), (u_x[rows - CONV_TAIL:], h[rows - 1:])


def gdn_chunk(pad, row0, params, seqs, carry):
    wq, wk, wv, a_log, dt_bias, norm_g = params
    u_q, u_k, u_v, u_z, small = seqs
    tq, tk, tv, state = carry
    rows = u_q.shape[0]
    hd = norm_g.shape[1]
    nh = u_q.shape[1] // hd
    m = row_mask(row0, rows, pad)
    incl, strict, eye = tri_masks(rows)
    tril = incl.astype(F32)
    qc = jax.nn.silu(conv4(tq, u_q, wq))
    kc = jax.nn.silu(conv4(tk, u_k, wk))
    vc = jax.nn.silu(conv4(tv, u_v, wv))
    outs, new_state = [], []
    for h in range(nh):
        sl = slice(h * hd, (h + 1) * hd)
        q = qc[:, sl]
        k = kc[:, sl]
        q = q * lax.rsqrt(jnp.sum(q * q, axis=-1, keepdims=True) + EPS) * (hd ** -0.5) * m
        k = k * lax.rsqrt(jnp.sum(k * k, axis=-1, keepdims=True) + EPS) * m
        v = vc[:, sl] * m
        beta = jax.nn.sigmoid(small[:, h:h + 1]) * m
        g = -jnp.exp(a_log[:, h:h + 1]) * jax.nn.softplus(small[:, nh + h:nh + h + 1] + dt_bias[:, h:h + 1]) * m
        s_h = state[sl]
        gc = hdot(tril, g)
        decay = jnp.exp(jnp.where(incl, gc - gc.T, NEG_BIG))
        k_beta = k * beta
        lmat = jnp.where(strict, bdot_nt(k_beta, k) * decay, 0.0)
        tinv = tri_inverse(lmat, eye)
        eg = jnp.exp(gc)
        u = hdot(tinv, v * beta)
        w = hdot(tinv, k_beta * eg)
        attn = bdot_nt(q, k) * decay
        g_last = gc[rows - 1:]
        v_new = u - bdot(w, s_h)
        o = bdot(q * eg, s_h) + bdot(attn, v_new)
        new_state.append(s_h * jnp.exp(g_last) + bdot_tn(k * jnp.exp(g_last - gc), v_new))
        o = rms_norm(o, norm_g) * jax.nn.silu(u_z[:, sl]) * m
        outs.append(o)
    t0 = rows - CONV_TAIL
    return (jnp.concatenate(outs, axis=1),), (u_q[t0:], u_k[t0:], u_v[t0:], jnp.concatenate(new_state, axis=0))


def ssd_chunk(pad, dt_lane0, n_groups, row0, params, seqs, carry):
    conv_w, conv_b, a_log, dt_bias, d_skip, norm_g = params
    u_z, u_xbc, small = seqs
    tail, state = carry
    rows = u_z.shape[0]
    width = u_z.shape[1]
    nh = a_log.shape[1]
    hd = width // nh
    ns = (u_xbc.shape[1] - width) // (2 * n_groups)
    hpg = nh // n_groups
    m = row_mask(row0, rows, pad)
    incl, _, _ = tri_masks(rows)
    tril = incl.astype(F32)
    xbc = jax.nn.silu(conv4(tail, u_xbc, conv_w) + conv_b)
    xs = xbc[:, :width]
    dt = jax.nn.softplus(small[:, dt_lane0:dt_lane0 + nh] + dt_bias)
    a_neg = -jnp.exp(a_log)
    ys, new_state = [], []
    for g in range(n_groups):
        bm = xbc[:, width + g * ns: width + (g + 1) * ns] * m
        cm = xbc[:, width + (n_groups + g) * ns: width + (n_groups + g + 1) * ns] * m
        cb = bdot_nt(cm, bm)
        for e in range(hpg):
            h = g * hpg + e
            sl = slice(h * hd, (h + 1) * hd)
            xh = xs[:, sl]
            dt_h = dt[:, h:h + 1]
            xdt = xh * dt_h * m
            a_h = dt_h * a_neg[:, h:h + 1] * m
            acum = hdot(tril, a_h)
            lmat = jnp.exp(jnp.where(incl, acum - acum.T, NEG_BIG))
            a_last = acum[rows - 1:]
            s_h = state[sl]
            y = bdot(cb * lmat, xdt) + bdot_nt(cm, s_h) * jnp.exp(acum)
            new_state.append(s_h * jnp.exp(a_last) + bdot_tn(xdt * jnp.exp(a_last - acum), bm))
            ys.append(y + d_skip[:, h:h + 1] * xh)
    y = jnp.concatenate(ys, axis=1) * jax.nn.silu(u_z)
    gw = width // n_groups
    outs = [rms_norm(y[:, g * gw:(g + 1) * gw], norm_g[:, g * gw:(g + 1) * gw]) for g in range(n_groups)]
    out = jnp.concatenate(outs, axis=1) * m
    return (out,), (u_xbc[rows - CONV_TAIL:], jnp.concatenate(new_state, axis=0))


def s5_chunk(pad, row0, params, seqs, carry):
    a_re, a_im, log_dt, b_re, b_im, c_re, c_im, d_skip = params
    (u,) = seqs
    s_re0, s_im0 = carry
    rows = u.shape[0]
    n_state = a_re.shape[1]
    n_grp = log_dt.shape[1]
    per = n_state // n_grp
    expand = (lax.broadcasted_iota(jnp.int32, (n_grp, n_state), 1) // per
              == lax.broadcasted_iota(jnp.int32, (n_grp, n_state), 0)).astype(F32)
    dt = jnp.exp(hdot(log_dt, expand))
    lam_re = jnp.minimum(a_re, -1e-4)
    lam_im = a_im
    mag = jnp.exp(dt * lam_re)
    ab_re = mag * jnp.cos(dt * lam_im)
    ab_im = mag * jnp.sin(dt * lam_im)
    den = lam_re * lam_re + lam_im * lam_im
    f_re = ((ab_re - 1.0) * lam_re + ab_im * lam_im) / den
    f_im = (ab_im * lam_re - (ab_re - 1.0) * lam_im) / den
    bb_re = f_re * b_re - f_im * b_im
    bb_im = f_re * b_im + f_im * b_re
    bu_re = bdot(u, bb_re)
    bu_im = bdot(u, bb_im)
    first = (lax.broadcasted_iota(jnp.int32, (rows, 1), 0) == 0).astype(F32)
    bu_re = bu_re + first * (ab_re * s_re0 - ab_im * s_im0)
    bu_im = bu_im + first * (ab_re * s_im0 + ab_im * s_re0)
    s_re, s_im = cscan_const(ab_re, ab_im, bu_re, bu_im)
    y = bdot(s_re, c_re) - bdot(s_im, c_im) + d_skip * u
    return (y,), (s_re[rows - 1:], s_im[rows - 1:])


def s5_post(pad, row0, params, seqs, carry):
    w_glu, norm_g = params
    (y,) = seqs
    y = jax.nn.gelu(y)
    y = y * jax.nn.sigmoid(bdot(y, w_glu))
    return (rms_norm(y, norm_g),), ()


def mix_out_delta(row0, params, seqs, carry):
    (w_out,) = params
    wd = seqs[0].shape[1]
    acc = bdot(seqs[0], w_out[0:wd])
    for k in range(1, len(seqs)):
        acc = acc + bdot(seqs[k], w_out[k * wd:(k + 1) * wd])
    return (acc,), ()


def multi_chunk(fn, sub):
    def run(row0, params, seqs, carry):
        outs = []
        for j in range(seqs[0].shape[0] // sub):
            (o,), carry = fn(row0 + j * sub, params, tuple(x[j * sub:(j + 1) * sub] for x in seqs), carry)
            outs.append(o)
        return (jnp.concatenate(outs, axis=0),), carry
    return run


def blockdiag_expand(w):
    nh, a, b = w.shape
    eye = jnp.eye(nh, dtype=w.dtype)
    return (w[:, :, None, :] * eye[:, None, :, None]).reshape(nh * a, nh * b)


def blockdiag_extract(m, nh):
    a, b = m.shape[0] // nh, m.shape[1] // nh
    m4 = m.reshape(nh, a, nh, b)
    return jnp.stack([m4[h, :, h, :] for h in range(nh)], axis=0)


S5_LANES = LANE


def s5_params_expand(a_re, a_im, log_dt, b_re, b_im, c_re, c_im, d_skip):
    n_grp, n_state = a_re.shape
    ch = b_re.shape[-1]
    gpl = S5_LANES // ch
    nq = n_grp // gpl
    bexp = lambda b: jnp.stack([blockdiag_expand(jnp.swapaxes(b[q * gpl:(q + 1) * gpl], 1, 2)) for q in range(nq)])
    return (a_re.reshape(nq, 1, gpl * n_state), a_im.reshape(nq, 1, gpl * n_state), log_dt.reshape(nq, 1, gpl),
            bexp(b_re), bexp(b_im), bexp(c_re), bexp(c_im), d_skip.reshape(nq, 1, S5_LANES))


def s5_grads_extract(grads, n_grp, n_state, ch):
    da_re, da_im, dlog_dt, db_re, db_im, dc_re, dc_im, dd = grads
    gpl = S5_LANES // ch
    nq = n_grp // gpl
    bext = lambda b: jnp.concatenate([jnp.swapaxes(blockdiag_extract(b[q], gpl), 1, 2) for q in range(nq)], axis=0)
    return (da_re.reshape(n_grp, n_state), da_im.reshape(n_grp, n_state), dlog_dt.reshape(n_grp),
            bext(db_re), bext(db_im), bext(dc_re), bext(dc_im), dd.reshape(n_grp * ch))


def _cparams(**kw):
    return pltpu.CompilerParams(vmem_limit_bytes=VMEM_LIMIT, **kw)


def tiled_call(body_fn, name, *, n_steps, rows, n_groups=1, reverse=False,
               seq_in=(), whole_in=(), step_in=(), seq_out=(), acc_out=(), step_out=(), carry=()):
    def step_of(i):
        return (n_steps - 1 - i) if reverse else i

    def col_of(col, g):
        return col(g) if callable(col) else col

    in_specs, operands = [], []
    for arr, width, col in seq_in:
        in_specs.append(pl.BlockSpec((rows, width), lambda g, i, col=col: (step_of(i), col_of(col, g))))
        operands.append(arr)
    for arr in whole_in:
        if arr.ndim == 2:
            in_specs.append(pl.BlockSpec(arr.shape, lambda g, i: (0, 0)))
        else:
            in_specs.append(pl.BlockSpec((None,) + arr.shape[1:], lambda g, i: (g, 0, 0)))
        operands.append(arr)
    for arr in step_in:
        in_specs.append(pl.BlockSpec((None, None) + arr.shape[2:], lambda g, i: (g, step_of(i), 0, 0)))
        operands.append(arr)
    out_shape, out_specs = [], []
    for total, width, col, dt in seq_out:
        out_shape.append(jax.ShapeDtypeStruct((n_steps * rows, total), dt))
        out_specs.append(pl.BlockSpec((rows, width), lambda g, i, col=col: (step_of(i), col_of(col, g))))
    for r, c in acc_out:
        out_shape.append(jax.ShapeDtypeStruct((n_groups, r, c), F32))
        out_specs.append(pl.BlockSpec((None, r, c), lambda g, i: (g, 0, 0)))
    for r, c in step_out:
        out_shape.append(jax.ShapeDtypeStruct((n_groups, n_steps, r, c), F32))
        out_specs.append(pl.BlockSpec((None, None, r, c), lambda g, i: (g, step_of(i), 0, 0)))
    n_seq, n_whole, n_step = len(seq_in), len(whole_in), len(step_in)
    n_so, n_ao, n_sto = len(seq_out), len(acc_out), len(step_out)

    def body(*refs):
        pos = 0
        seq_refs = refs[pos:pos + n_seq]
        pos += n_seq
        whole_refs = refs[pos:pos + n_whole]
        pos += n_whole
        step_refs = refs[pos:pos + n_step]
        pos += n_step
        so_refs = refs[pos:pos + n_so]
        pos += n_so
        ao_refs = refs[pos:pos + n_ao]
        pos += n_ao
        sto_refs = refs[pos:pos + n_sto]
        pos += n_sto
        carry_refs = refs[pos:]
        i = pl.program_id(1)

        @pl.when(i == 0)
        def _():
            for r in carry_refs:
                r[...] = jnp.zeros(r.shape, r.dtype)
            for r in ao_refs:
                r[...] = jnp.zeros(r.shape, r.dtype)

        row0 = step_of(i) * rows
        seq_o, acc_o, step_o, new_c = body_fn(row0, [r[...] for r in whole_refs], [r[...] for r in seq_refs],
                                              [r[...] for r in step_refs], [r[...] for r in carry_refs])
        for r, val in zip(so_refs, seq_o, strict=True):
            r[...] = val.astype(r.dtype)
        for r, val in zip(ao_refs, acc_o, strict=True):
            r[...] += val
        for r, val in zip(sto_refs, step_o, strict=True):
            r[...] = val
        for r, val in zip(carry_refs, new_c, strict=True):
            r[...] = val

    return pl.pallas_call(
        body, name=name, grid=(n_groups, n_steps), in_specs=in_specs, out_specs=out_specs, out_shape=out_shape,
        scratch_shapes=[pltpu.VMEM((r, c), F32) for r, c in carry],
        compiler_params=_cparams(dimension_semantics=("arbitrary", "arbitrary")),
    )(*operands)


def mixer_fwd(fn, name, *, n_steps, rows, seqs, params, out, carry, n_groups=1):
    def body(row0, whole, seq_vals, steps, carry_vals):
        outs, new_c = fn(row0, tuple(whole), tuple(seq_vals), tuple(carry_vals))
        return list(outs), [], list(carry_vals), list(new_c)

    res = tiled_call(body, name, n_steps=n_steps, rows=rows, n_groups=n_groups, seq_in=seqs, whole_in=params,
                     seq_out=[out], step_out=carry, carry=carry)
    return res[0], list(res[1:])


def mixer_bwd(fn, name, *, n_steps, rows, seqs, params, dout, saved, carry, n_groups=1):
    n_seq = len(seqs)

    def body(row0, whole, seq_vals, steps, dcarry):
        params_f = tuple(p.astype(F32) for p in whole)
        _, vjp = jax.vjp(lambda p, s, c: fn(row0, p, s, c), params_f, tuple(seq_vals[:n_seq]), tuple(steps))
        dp, ds, dc = vjp(((seq_vals[n_seq],), tuple(dcarry)))
        return list(ds), list(dp), [], list(dc)

    seq_out = [(n_groups * w, w, (lambda g: g), F32) if callable(c) else (w, w, 0, F32) for a, w, c in seqs]
    acc_out = [p.shape[-2:] for p in params]
    res = tiled_call(body, name, n_steps=n_steps, rows=rows, n_groups=n_groups, reverse=True,
                     seq_in=list(seqs) + [dout], whole_in=params, step_in=saved,
                     seq_out=seq_out, acc_out=acc_out, carry=carry)
    return list(res[:n_seq]), list(res[n_seq:])


def rms_fwd(h, g, name, rows):
    def body(row0, whole, seqs, steps, carry):
        return [rms_norm(seqs[0], whole[0])], [], [], []
    d = h.shape[1]
    return tiled_call(body, name, n_steps=h.shape[0] // rows, rows=rows, seq_in=[(h, d, 0)], whole_in=[g],
                      seq_out=[(d, d, 0, BF16)])[0]


def rms_bwd_add(h, dxn, dh_out, g, name, rows):
    def body(row0, whole, seqs, steps, carry):
        _, vjp = jax.vjp(rms_norm, seqs[0], whole[0])
        dh, dg = vjp(seqs[1])
        return [seqs[2] + dh], [dg], [], []
    d = h.shape[1]
    dh_in, dg = tiled_call(body, name, n_steps=h.shape[0] // rows, rows=rows,
                           seq_in=[(h, d, 0), (dxn, d, 0), (dh_out, d, 0)], whole_in=[g],
                           seq_out=[(d, d, 0, F32)], acc_out=[(1, d)])
    return dh_in, dg[0]


def mix_out_fwd(h, ys, w_out, name, rows):
    def body(row0, whole, seqs, steps, carry):
        (delta,), _ = mix_out_delta(row0, (whole[0],), tuple(seqs[1:]), ())
        return [seqs[0] + delta], [], [], []
    d, wd = h.shape[1], ys[0].shape[1]
    return tiled_call(body, name, n_steps=h.shape[0] // rows, rows=rows,
                      seq_in=[(h, d, 0)] + [(y, wd, 0) for y in ys], whole_in=[w_out], seq_out=[(d, d, 0, F32)])[0]


def loss_and_grad(h, target, g, name, rows, first_row):
    def body(row0, whole, seqs, steps, carry):
        hh, tt = seqs
        keep = row_mask(row0, hh.shape[0], first_row)

        def f(hv, gv):
            err = rms_norm(hv, gv) - tt
            return 0.5 * jnp.sum(jnp.mean(err * err, axis=-1, keepdims=True) * keep, axis=0, keepdims=True)

        val, vjp = jax.vjp(f, hh, whole[0])
        dh, dg = vjp(jnp.ones((1, 1), F32))
        return [dh], [jnp.broadcast_to(val, (1, LANE)), dg], [], []
    d = h.shape[1]
    dh, loss, dg = tiled_call(body, name, n_steps=h.shape[0] // rows, rows=rows,
                              seq_in=[(h, d, 0), (target, d, 0)], whole_in=[g],
                              seq_out=[(d, d, 0, F32)], acc_out=[(1, LANE), (1, d)])
    return loss[0, 0, 0], dh, dg[0]


def _pick(n, cands):
    for c in cands:
        if n % c == 0:
            return c
    raise ValueError(f"no tile for {n}")


ROW_TILES = (1056, 704, 352, 192, 96, 64)
COL_TILES = (256, 128)
NT_DIMS = (((1,), (1,)), ((), ()))
TN_DIMS = (((0,), (0,)), ((), ()))


def ffn_fwd(h, xn, wg, wu, wd, name):
    t, d = h.shape
    f = wg.shape[1]
    tm = _pick(t, ROW_TILES)
    tn = _pick(f, COL_TILES)
    n_j = f // tn

    def body(h_ref, xn_ref, wg_ref, wu_ref, wd_ref, o_ref, acc_ref):
        j = pl.program_id(1)

        @pl.when(j == 0)
        def _():
            acc_ref[...] = jnp.zeros(acc_ref.shape, F32)

        x = xn_ref[...]
        g = jnp.dot(x, wg_ref[...], preferred_element_type=F32)
        u = jnp.dot(x, wu_ref[...], preferred_element_type=F32)
        a = (jax.nn.silu(g) * u).astype(BF16)
        acc_ref[...] += jnp.dot(a, wd_ref[...], preferred_element_type=F32)

        @pl.when(j == n_j - 1)
        def _():
            o_ref[...] = h_ref[...] + 0.5 * acc_ref[...]

    return pl.pallas_call(
        body, name=name, grid=(t // tm, n_j),
        in_specs=[pl.BlockSpec((tm, d), lambda i, j: (i, 0)), pl.BlockSpec((tm, d), lambda i, j: (i, 0)),
                  pl.BlockSpec((d, tn), lambda i, j: (0, j)), pl.BlockSpec((d, tn), lambda i, j: (0, j)),
                  pl.BlockSpec((tn, d), lambda i, j: (j, 0))],
        out_specs=pl.BlockSpec((tm, d), lambda i, j: (i, 0)),
        out_shape=jax.ShapeDtypeStruct((t, d), F32),
        scratch_shapes=[pltpu.VMEM((tm, d), F32)],
        compiler_params=_cparams(dimension_semantics=("arbitrary", "arbitrary")),
    )(h, xn, wg, wu, wd)


def ffn_bwd(xn, dh, wg, wu, wd, name):
    t, d = dh.shape
    f = wg.shape[1]
    tm = _pick(t, ROW_TILES)
    tn = _pick(f, COL_TILES)

    def body(xn_ref, dh_ref, wg_ref, wu_ref, wd_ref, dxn_ref, dwg_ref, dwu_ref, dwd_ref):
        j, i = pl.program_id(0), pl.program_id(1)
        rows = pl.ds(pl.multiple_of(i * tm, 8), tm)
        x = xn_ref[rows, :]
        dhh = (0.5 * dh_ref[...]).astype(BF16)
        wgv, wuv = wg_ref[...], wu_ref[...]
        g = jnp.dot(x, wgv, preferred_element_type=F32)
        u = jnp.dot(x, wuv, preferred_element_type=F32)
        sg = jax.nn.sigmoid(g)
        s = g * sg
        da = lax.dot_general(dhh, wd_ref[...], NT_DIMS, preferred_element_type=F32)
        dwd = lax.dot_general((s * u).astype(BF16), dhh, TN_DIMS, preferred_element_type=F32)
        dg = (da * u * (sg * (1.0 + g * (1.0 - sg)))).astype(BF16)
        du = (da * s).astype(BF16)
        dwg = lax.dot_general(x, dg, TN_DIMS, preferred_element_type=F32)
        dwu = lax.dot_general(x, du, TN_DIMS, preferred_element_type=F32)
        dx = (lax.dot_general(dg, wgv, NT_DIMS, preferred_element_type=F32)
              + lax.dot_general(du, wuv, NT_DIMS, preferred_element_type=F32))

        @pl.when(i == 0)
        def _():
            dwg_ref[...] = dwg
            dwu_ref[...] = dwu
            dwd_ref[...] = dwd

        @pl.when(i > 0)
        def _():
            dwg_ref[...] += dwg
            dwu_ref[...] += dwu
            dwd_ref[...] += dwd

        @pl.when(j == 0)
        def _():
            dxn_ref[rows, :] = dx

        @pl.when(j > 0)
        def _():
            dxn_ref[rows, :] += dx

    return pl.pallas_call(
        body, name=name, grid=(f // tn, t // tm),
        in_specs=[pl.BlockSpec((t, d), lambda j, i: (0, 0)), pl.BlockSpec((tm, d), lambda j, i: (i, 0)),
                  pl.BlockSpec((d, tn), lambda j, i: (0, j)), pl.BlockSpec((d, tn), lambda j, i: (0, j)),
                  pl.BlockSpec((tn, d), lambda j, i: (j, 0))],
        out_specs=[pl.BlockSpec((t, d), lambda j, i: (0, 0)), pl.BlockSpec((d, tn), lambda j, i: (0, j)),
                   pl.BlockSpec((d, tn), lambda j, i: (0, j)), pl.BlockSpec((tn, d), lambda j, i: (j, 0))],
        out_shape=[jax.ShapeDtypeStruct((t, d), F32), jax.ShapeDtypeStruct((d, f), F32),
                   jax.ShapeDtypeStruct((d, f), F32), jax.ShapeDtypeStruct((f, d), F32)],
        compiler_params=_cparams(dimension_semantics=("arbitrary", "arbitrary")),
    )(xn, dh, wg, wu, wd)


def matmul_cols(xn, w, name):
    t, d = xn.shape
    n = w.shape[1]
    tn = _pick(n, COL_TILES)

    def body(x_ref, w_ref, o_ref):
        o_ref[...] = jnp.dot(x_ref[...], w_ref[...], preferred_element_type=F32)

    return pl.pallas_call(
        body, name=name, grid=(n // tn,),
        in_specs=[pl.BlockSpec((t, d), lambda j: (0, 0)), pl.BlockSpec((d, tn), lambda j: (0, j))],
        out_specs=pl.BlockSpec((t, tn), lambda j: (0, j)),
        out_shape=jax.ShapeDtypeStruct((t, n), F32),
        compiler_params=_cparams(dimension_semantics=("arbitrary",)),
    )(xn, w)


def matmul_cols_bwd(xn, dy, w, name):
    t, d = xn.shape
    n = w.shape[1]
    tn = _pick(n, COL_TILES)

    def body(x_ref, dy_ref, w_ref, dx_ref, dw_ref):
        j = pl.program_id(0)
        dyv = dy_ref[...].astype(BF16)
        dw_ref[...] = lax.dot_general(x_ref[...], dyv, TN_DIMS, preferred_element_type=F32)
        dx = lax.dot_general(dyv, w_ref[...], NT_DIMS, preferred_element_type=F32)

        @pl.when(j == 0)
        def _():
            dx_ref[...] = dx

        @pl.when(j > 0)
        def _():
            dx_ref[...] += dx

    return pl.pallas_call(
        body, name=name, grid=(n // tn,),
        in_specs=[pl.BlockSpec((t, d), lambda j: (0, 0)), pl.BlockSpec((t, tn), lambda j: (0, j)),
                  pl.BlockSpec((d, tn), lambda j: (0, j))],
        out_specs=[pl.BlockSpec((t, d), lambda j: (0, 0)), pl.BlockSpec((d, tn), lambda j: (0, j))],
        out_shape=[jax.ShapeDtypeStruct((t, d), F32), jax.ShapeDtypeStruct((d, n), F32)],
        compiler_params=_cparams(dimension_semantics=("arbitrary",)),
    )(xn, dy, w)


def _peer(mx, my, mc, k):
    px = 1 - mx if (k >> 2) & 1 else mx
    py = 1 - my if (k >> 1) & 1 else my
    pc = 1 - mc if k & 1 else mc
    return (px, py, pc), 4 * px + 2 * py + pc


def all_to_all(xs, name):
    n = len(xs)

    def body(*refs):
        x_refs, o_refs = refs[:n], refs[n:2 * n]
        send_sems, recv_sems, local_sems = refs[2 * n:]
        mx, my, mc = lax.axis_index("x"), lax.axis_index("y"), lax.axis_index("c")
        me = 4 * mx + 2 * my + mc
        peers = [_peer(mx, my, mc, k) for k in range(1, N_DEV)]
        locals_, sends, recvs = [], [], []
        for a in range(n):
            x_ref, o_ref = x_refs[a], o_refs[a]
            locals_.append(pltpu.make_async_copy(x_ref.at[me], o_ref.at[me], local_sems.at[a]))
            for k, (dev, peer) in enumerate(peers):
                common = dict(send_sem=send_sems.at[a, k], recv_sem=recv_sems.at[a, k], device_id=dev,
                              device_id_type=pl.DeviceIdType.MESH)
                sends.append(pltpu.make_async_remote_copy(src_ref=x_ref.at[peer], dst_ref=o_ref.at[me], **common))
                recvs.append(pltpu.make_async_remote_copy(src_ref=x_ref.at[peer], dst_ref=o_ref.at[peer], **common))
        for cp in locals_ + sends:
            cp.start()
        for cp in recvs:
            cp.wait_recv()
        for cp in sends:
            cp.wait_send()
        for cp in locals_:
            cp.wait()

    return pl.pallas_call(
        body, name=name,
        in_specs=[pl.BlockSpec(memory_space=pl.ANY)] * n, out_specs=[pl.BlockSpec(memory_space=pl.ANY)] * n,
        out_shape=[jax.ShapeDtypeStruct(x.shape, x.dtype) for x in xs],
        scratch_shapes=[pltpu.SemaphoreType.DMA((n, N_DEV - 1)), pltpu.SemaphoreType.DMA((n, N_DEV - 1)),
                        pltpu.SemaphoreType.DMA((n,))],
    )(*xs)


def all_gather(xs, name):
    n = len(xs)
    chip_flips = (4, 2, 6)

    def body(*refs):
        x_refs, o_refs = refs[:n], refs[n:2 * n]
        send_sems, recv_sems, local_sems = refs[2 * n:]
        mx, my, mc = lax.axis_index("x"), lax.axis_index("y"), lax.axis_index("c")
        me = 4 * mx + 2 * my + mc
        sib_dev, sib = _peer(mx, my, mc, 1)

        def copy(a, k, row, to, src=None):
            return pltpu.make_async_remote_copy(
                src_ref=o_refs[a].at[row] if src is None else src, dst_ref=o_refs[a].at[row],
                send_sem=send_sems.at[a, k], recv_sem=recv_sems.at[a, k], device_id=to,
                device_id_type=pl.DeviceIdType.MESH)

        locals_, first, passed = [], [], []
        for a in range(n):
            locals_.append(pltpu.make_async_copy(x_refs[a], o_refs[a].at[me], local_sems.at[a]))
            first.append(copy(a, 0, me, sib_dev, src=x_refs[a]))
            for j, f in enumerate(chip_flips):
                first.append(copy(a, 1 + j, me, _peer(mx, my, mc, f)[0], src=x_refs[a]))
        for cp in locals_ + first:
            cp.start()
        for a in range(n):
            for j, f in enumerate(chip_flips):
                row = _peer(mx, my, mc, f)[1]
                copy(a, 1 + j, row, sib_dev).wait_recv()
                fwd = copy(a, 4 + j, row, sib_dev)
                fwd.start()
                passed.append(fwd)
        for a in range(n):
            copy(a, 0, sib, sib_dev).wait_recv()
            for j, f in enumerate(chip_flips):
                copy(a, 4 + j, _peer(mx, my, mc, f ^ 1)[1], sib_dev).wait_recv()
        for cp in first + passed:
            cp.wait_send()
        for cp in locals_:
            cp.wait()

    return pl.pallas_call(
        body, name=name,
        in_specs=[pl.BlockSpec(memory_space=pl.ANY)] * n, out_specs=[pl.BlockSpec(memory_space=pl.ANY)] * n,
        out_shape=[jax.ShapeDtypeStruct((N_DEV,) + x.shape, x.dtype) for x in xs],
        scratch_shapes=[pltpu.SemaphoreType.DMA((n, N_DEV - 1)), pltpu.SemaphoreType.DMA((n, N_DEV - 1)),
                        pltpu.SemaphoreType.DMA((n,))],
    )(*xs)


PACK_COLS = 1024
PACK_ROWS = 256
PARTS_TILE_BYTES = 4 * 1024 * 1024


def adamw_reduce(parts, w, m, v, name):
    r, c = w.shape
    fits = [t for t in (512, 352, 256, 128, 64, 32, 16, 8) if N_DEV * t * c * parts.dtype.itemsize <= PARTS_TILE_BYTES]
    tr = r if r < 2 * SUBLANE else _pick(r, fits)
    c1 = 1.0 - ADAM_B1 ** ADAM_STEP
    c2 = 1.0 - ADAM_B2 ** ADAM_STEP

    def body(p_ref, w_ref, m_ref, v_ref, g_ref, d_ref, mo_ref, vo_ref):
        g = p_ref[0].astype(F32)
        for k in range(1, N_DEV):
            g = g + p_ref[k].astype(F32)
        mn = ADAM_B1 * m_ref[...] + (1.0 - ADAM_B1) * g
        vn = ADAM_B2 * v_ref[...] + (1.0 - ADAM_B2) * (g * g)
        g_ref[...] = g
        mo_ref[...] = mn
        vo_ref[...] = vn
        d_ref[...] = -ADAM_LR * ((mn / c1) / (jnp.sqrt(vn / c2) + ADAM_EPS) + ADAM_WD * w_ref[...])

    spec = pl.BlockSpec((tr, c), lambda i: (i, 0))
    return pl.pallas_call(
        body, name=name, grid=(r // tr,),
        in_specs=[pl.BlockSpec((N_DEV, tr, c), lambda i: (0, i, 0)), spec, spec, spec],
        out_specs=[spec] * 4, out_shape=[jax.ShapeDtypeStruct((r, c), F32)] * 4,
        compiler_params=_cparams(dimension_semantics=("arbitrary",)),
    )(parts, w, m, v)


def pack_flat(arrs, dtype):
    flat = jnp.concatenate([a.reshape(-1).astype(dtype) for a in arrs])
    tile = PACK_ROWS * PACK_COLS
    total = -(-flat.shape[0] // tile) * tile
    return jnp.pad(flat, (0, total - flat.shape[0])).reshape(total // PACK_COLS, PACK_COLS)


def unpack_flat(buf, shapes, lead=()):
    flat = buf.reshape(lead + (-1,))
    out, off = [], 0
    for s in shapes:
        n = math.prod(s)
        out.append(flat[..., off:off + n].reshape(lead + tuple(s)))
        off += n
    return out


W_NAMES = ('meta_tokens', 'ffn1_norm', 'ffn1_w_gate', 'ffn1_w_up', 'ffn1_w_down', 'mix_norm', 'w_in', 'w_out',
           'lru_conv_w', 'lru_conv_b', 'lru_w_a', 'lru_b_a', 'lru_w_i', 'lru_b_i', 'lru_lambda', 'lru_norm',
           'gdn_conv_w', 'gdn_a_log', 'gdn_dt_bias', 'gdn_norm', 'ssd_conv_w', 'ssd_conv_b', 'ssd_a_log',
           'ssd_dt_bias', 'ssd_d', 'ssd_norm', 's5_a_re', 's5_a_im', 's5_log_dt', 's5_b_re', 's5_b_im', 's5_c_re',
           's5_c_im', 's5_d', 's5_w_glu', 's5_norm', 'ffn2_norm', 'ffn2_w_gate', 'ffn2_w_up', 'ffn2_w_down',
           'final_norm')
SHARD_AXIS = {'meta_tokens': 1, 'ffn1_w_gate': 2, 'ffn1_w_up': 2, 'ffn1_w_down': 1, 'w_in': 2, 'w_out': 1,
              'lru_conv_w': 2, 'gdn_conv_w': 2, 'ssd_conv_w': 2, 's5_w_glu': 1, 'ffn2_w_gate': 2, 'ffn2_w_up': 2,
              'ffn2_w_down': 1}
BIG_NAMES = ('ffn1_w_gate', 'ffn1_w_up', 'ffn1_w_down', 'w_in', 'w_out', 's5_w_glu', 'ffn2_w_gate', 'ffn2_w_up',
             'ffn2_w_down')
SHARD_NAMES = tuple(n for n in W_NAMES if n in SHARD_AXIS)
REP_NAMES = tuple(n for n in W_NAMES if n not in SHARD_AXIS)
SSD_GROUPS = 2
S5_CH = 16


def unshard(g, axis):
    return jnp.concatenate([g[p] for p in range(N_DEV)], axis=axis)


def kernel(*args):
    n_w = len(W_NAMES)
    x = args[0]
    w = dict(zip(W_NAMES, args[1:1 + n_w]))
    target = args[1 + n_w]
    m_in = dict(zip(W_NAMES, args[2 + n_w:2 + 2 * n_w]))
    v_in = dict(zip(W_NAMES, args[2 + 2 * n_w:2 + 3 * n_w]))

    depth, d = w['ffn1_norm'].shape
    seq = x.shape[1]
    n_meta = w['meta_tokens'].shape[0]
    pad = CHUNK - n_meta
    tp = pad + n_meta + seq
    wg = d // 2
    xbc_w = w['ssd_conv_w'].shape[-1] * N_DEV
    gdn_hd = w['gdn_norm'].shape[-1]
    gdn_h = wg // gdn_hd
    ssd_h = w['ssd_a_log'].shape[-1]
    lru_h = w['lru_w_a'].shape[1]
    s5_g, s5_n = w['s5_a_re'].shape[1:]
    s5_q = wg // S5_LANES
    row_tile = _pick(tp, (192, 96, 64))

    gathered = all_gather([w[n].astype(BF16 if n in BIG_NAMES else F32) for n in SHARD_NAMES], "gather_weights")
    full = {n: unshard(g, SHARD_AXIS[n]) for n, g in zip(SHARD_NAMES, gathered)}

    segs = [('a_x', wg), ('a_gate', wg), ('b_q', wg), ('b_k', wg), ('b_v', wg), ('b_z', wg), ('c_xbc', xbc_w),
            ('c_z', wg), ('d_u', wg), ('small_b', LANE), ('small_c', LANE)]
    off, o = {}, 0
    for nme, wd_ in segs:
        assert o % wd_ == 0, (nme, o, wd_)
        off[nme] = o
        o += wd_
    o_beta = 6 * wg
    o_cz = o_beta + 2 * gdn_h
    o_xbc = o_cz + wg
    o_dt = o_xbc + xbc_w
    o_du = o_dt + ssd_h

    def pack_cols(a):
        z = lambda k: jnp.zeros(a.shape[:-1] + (k,), a.dtype)
        return jnp.concatenate([a[..., :o_beta], a[..., o_xbc:o_dt], a[..., o_cz:o_xbc], a[..., o_du:],
                                a[..., o_beta:o_cz], z(LANE - 2 * gdn_h), a[..., o_dt:o_du], z(LANE - ssd_h)], axis=-1)

    def unpack_cols(a):
        sb, sc = off['small_b'], off['small_c']
        return jnp.concatenate([a[..., :o_beta], a[..., sb:sb + 2 * gdn_h], a[..., off['c_z']:off['c_z'] + wg],
                                a[..., off['c_xbc']:off['c_xbc'] + xbc_w], a[..., sc:sc + ssd_h],
                                a[..., off['d_u']:off['d_u'] + wg]], axis=-1)

    w_in_p = pack_cols(full['w_in'])

    def col(name, width):
        return off[name] // width

    def row(a):
        return a.reshape(1, -1)

    def layer_params(l):
        gcw = full['gdn_conv_w'][l]
        lru = [full['lru_conv_w'][l], row(w['lru_conv_b'][l]), blockdiag_expand(w['lru_w_a'][l]), row(w['lru_b_a'][l]),
               blockdiag_expand(w['lru_w_i'][l]), row(w['lru_b_i'][l]), row(w['lru_lambda'][l]), row(w['lru_norm'][l])]
        gdn = [gcw[:, :wg], gcw[:, wg:2 * wg], gcw[:, 2 * wg:], row(w['gdn_a_log'][l]), row(w['gdn_dt_bias'][l]),
               row(w['gdn_norm'][l])]
        ssd = [full['ssd_conv_w'][l], row(w['ssd_conv_b'][l]), row(w['ssd_a_log'][l]), row(w['ssd_dt_bias'][l]),
               row(w['ssd_d'][l]), row(w['ssd_norm'][l])]
        s5 = list(s5_params_expand(*[w[n][l] for n in ('s5_a_re', 's5_a_im', 's5_log_dt', 's5_b_re', 's5_b_im',
                                                          's5_c_re', 's5_c_im', 's5_d')]))
        post = [full['s5_w_glu'][l], row(w['s5_norm'][l])]
        return lru, gdn, ssd, s5, post

    lru_fn = functools.partial(lru_chunk, pad)
    gdn_fn = multi_chunk(functools.partial(gdn_chunk, pad), CHUNK)
    ssd_fn = multi_chunk(functools.partial(ssd_chunk, pad, 0, SSD_GROUPS), CHUNK)
    s5_fn = functools.partial(s5_chunk, pad)
    post_fn = functools.partial(s5_post, pad)
    n_state_lanes = (S5_LANES // S5_CH) * s5_n

    def mixer_specs(proj):
        lru_seqs = [(proj, wg, col('a_x', wg)), (proj, wg, col('a_gate', wg))]
        gdn_seqs = [(proj, wg, col('b_q', wg)), (proj, wg, col('b_k', wg)), (proj, wg, col('b_v', wg)),
                    (proj, wg, col('b_z', wg)), (proj, LANE, col('small_b', LANE))]
        ssd_seqs = [(proj, wg, col('c_z', wg)), (proj, xbc_w, col('c_xbc', xbc_w)), (proj, LANE, col('small_c', LANE))]
        base = col('d_u', S5_LANES)
        s5_seqs = [(proj, S5_LANES, lambda g: base + g)]
        return lru_seqs, gdn_seqs, ssd_seqs, s5_seqs

    lru_carry = [(CONV_TAIL, wg), (1, wg)]
    gdn_carry = [(CONV_TAIL, wg)] * 3 + [(wg, gdn_hd)]
    ssd_carry = [(CONV_TAIL, xbc_w), (wg, (xbc_w - wg) // (2 * SSD_GROUPS))]
    s5_carry = [(1, n_state_lanes)] * 2
    rk = dict(n_steps=tp // row_tile, rows=row_tile)
    mk = rk
    out_w = (wg, wg, 0, F32)

    h = jnp.concatenate([jnp.zeros((pad, d), F32), full['meta_tokens'], x[0]], axis=0)
    target_p = jnp.concatenate([jnp.zeros((pad + n_meta, d), F32), target[0]], axis=0)
    saved = []
    for l in range(depth):
        lru_p, gdn_p, ssd_p, s5_p, post_p = layer_params(l)
        h0 = h
        xn1 = rms_fwd(h0, row(w['ffn1_norm'][l]), "rms_fwd", row_tile)
        h1 = ffn_fwd(h0, xn1, full['ffn1_w_gate'][l], full['ffn1_w_up'][l], full['ffn1_w_down'][l], "ffn_fwd")
        xn2 = rms_fwd(h1, row(w['mix_norm'][l]), "rms_fwd", row_tile)
        proj = matmul_cols(xn2, w_in_p[l], "mix_in_fwd")
        lru_s, gdn_s, ssd_s, s5_s = mixer_specs(proj)
        ya, lru_c = mixer_fwd(lru_fn, "lru_fwd", seqs=lru_s, params=lru_p, out=out_w, carry=lru_carry, **mk)
        yb, gdn_c = mixer_fwd(gdn_fn, "gdn_fwd", seqs=gdn_s, params=gdn_p, out=out_w, carry=gdn_carry, **mk)
        yc, ssd_c = mixer_fwd(ssd_fn, "ssd_fwd", seqs=ssd_s, params=ssd_p, out=out_w, carry=ssd_carry, **mk)
        y1, s5_c = mixer_fwd(s5_fn, "s5_fwd", seqs=s5_s, params=s5_p, out=(wg, S5_LANES, lambda g: g, F32),
                             carry=s5_carry, n_groups=s5_q, **mk)
        yd, _ = mixer_fwd(post_fn, "s5_post_fwd", seqs=[(y1, wg, 0)], params=post_p, out=out_w, carry=[], **rk)
        h2 = mix_out_fwd(h1, [ya, yb, yc, yd], full['w_out'][l], "mix_out_fwd", row_tile)
        xn3 = rms_fwd(h2, row(w['ffn2_norm'][l]), "rms_fwd", row_tile)
        h3 = ffn_fwd(h2, xn3, full['ffn2_w_gate'][l], full['ffn2_w_up'][l], full['ffn2_w_down'][l], "ffn_fwd")
        saved.append((h0, xn1, h1, xn2, proj, (ya, yb, yc, yd), y1, (lru_c, gdn_c, ssd_c, s5_c), h2, xn3))
        h = h3

    loss_part, dh, d_final = loss_and_grad(h, target_p, row(w['final_norm']), "loss", row_tile, pad + n_meta)
    loss = lax.psum(loss_part, ("x", "y", "c"))

    gw = {n: [None] * depth for n in W_NAMES if n not in ('meta_tokens', 'final_norm')}
    for l in reversed(range(depth)):
        lru_p, gdn_p, ssd_p, s5_p, post_p = layer_params(l)
        h0, xn1, h1, xn2, proj, ys, y1, (lru_c, gdn_c, ssd_c, s5_c), h2, xn3 = saved[l]
        dxn, gw['ffn2_w_gate'][l], gw['ffn2_w_up'][l], gw['ffn2_w_down'][l] = ffn_bwd(
            xn3, dh, full['ffn2_w_gate'][l], full['ffn2_w_up'][l], full['ffn2_w_down'][l], "ffn_bwd")
        dh, dg = rms_bwd_add(h2, dxn, dh, row(w['ffn2_norm'][l]), "rms_bwd", row_tile)
        gw['ffn2_norm'][l] = dg[0]

        dys, (d_wout,) = mixer_bwd(mix_out_delta, "mix_out_bwd", seqs=[(y, wg, 0) for y in ys],
                                   params=[full['w_out'][l]], dout=(dh, d, 0), saved=[], carry=[], **rk)
        gw['w_out'][l] = d_wout[0]
        lru_s, gdn_s, ssd_s, s5_s = mixer_specs(proj)
        (dy1,), d_post = mixer_bwd(post_fn, "s5_post_bwd", seqs=[(y1, wg, 0)], params=post_p, dout=(dys[3], wg, 0),
                                   saved=[], carry=[], **rk)
        gw['s5_w_glu'][l], gw['s5_norm'][l] = d_post[0][0], d_post[1][0, 0]
        (d_du,), d_s5 = mixer_bwd(s5_fn, "s5_bwd", seqs=s5_s, params=s5_p, dout=(dy1, S5_LANES, lambda g: g),
                                  saved=s5_c, carry=s5_carry, n_groups=s5_q, **mk)
        for n, g in zip(('s5_a_re', 's5_a_im', 's5_log_dt', 's5_b_re', 's5_b_im', 's5_c_re', 's5_c_im', 's5_d'),
                        s5_grads_extract(d_s5, s5_g, s5_n, S5_CH)):
            gw[n][l] = g
        (d_cz, d_cxbc, d_sc), d_ssd = mixer_bwd(ssd_fn, "ssd_bwd", seqs=ssd_s, params=ssd_p, dout=(dys[2], wg, 0),
                                                saved=ssd_c, carry=ssd_carry, **mk)
        for n, g in zip(('ssd_conv_w', 'ssd_conv_b', 'ssd_a_log', 'ssd_dt_bias', 'ssd_d', 'ssd_norm'), d_ssd):
            gw[n][l] = g[0] if n == 'ssd_conv_w' else g[0, 0]
        (d_bq, d_bk, d_bv, d_bz, d_sb), d_gdn = mixer_bwd(gdn_fn, "gdn_bwd", seqs=gdn_s, params=gdn_p,
                                                          dout=(dys[1], wg, 0), saved=gdn_c, carry=gdn_carry, **mk)
        gw['gdn_conv_w'][l] = jnp.concatenate([d_gdn[0][0], d_gdn[1][0], d_gdn[2][0]], axis=1)
        gw['gdn_a_log'][l], gw['gdn_dt_bias'][l], gw['gdn_norm'][l] = d_gdn[3][0, 0], d_gdn[4][0, 0], d_gdn[5][0, 0]
        (d_ax, d_ag), d_lru = mixer_bwd(lru_fn, "lru_bwd", seqs=lru_s, params=lru_p, dout=(dys[0], wg, 0),
                                        saved=lru_c, carry=lru_carry, **mk)
        gw['lru_conv_w'][l], gw['lru_conv_b'][l] = d_lru[0][0], d_lru[1][0, 0]
        gw['lru_w_a'][l], gw['lru_b_a'][l] = blockdiag_extract(d_lru[2][0], lru_h), d_lru[3][0, 0]
        gw['lru_w_i'][l], gw['lru_b_i'][l] = blockdiag_extract(d_lru[4][0], lru_h), d_lru[5][0, 0]
        gw['lru_lambda'][l], gw['lru_norm'][l] = d_lru[6][0, 0], d_lru[7][0, 0]

        dproj = jnp.concatenate([d_ax, d_ag, d_bq, d_bk, d_bv, d_bz, d_cxbc, d_cz, d_du, d_sb, d_sc], axis=1)
        dxn, d_win_p = matmul_cols_bwd(xn2, dproj, w_in_p[l], "mix_in_bwd")
        gw['w_in'][l] = unpack_cols(d_win_p)
        dh, dg = rms_bwd_add(h1, dxn, dh, row(w['mix_norm'][l]), "rms_bwd", row_tile)
        gw['mix_norm'][l] = dg[0]

        dxn, gw['ffn1_w_gate'][l], gw['ffn1_w_up'][l], gw['ffn1_w_down'][l] = ffn_bwd(
            xn1, dh, full['ffn1_w_gate'][l], full['ffn1_w_up'][l], full['ffn1_w_down'][l], "ffn_bwd")
        dh, dg = rms_bwd_add(h0, dxn, dh, row(w['ffn1_norm'][l]), "rms_bwd", row_tile)
        gw['ffn1_norm'][l] = dg[0]

    grad_x = dh[pad + n_meta:][None]
    grads = {n: jnp.stack(g, axis=0) for n, g in gw.items()}
    grads['meta_tokens'] = dh[pad:pad + n_meta]
    grads['final_norm'] = d_final[0]

    def shards_of(a, axis):
        sh = a.shape
        return jnp.moveaxis(a.reshape(sh[:axis] + (N_DEV, sh[axis] // N_DEV) + sh[axis + 1:]), axis, 0)

    recv_sh = all_to_all([shards_of(grads[n], SHARD_AXIS[n]).astype(BF16) for n in SHARD_NAMES], "scatter_grads")
    (recv_rep,) = all_gather([pack_flat([grads[n] for n in REP_NAMES], F32)], "gather_rep_grads")
    out = {}
    for n, recv in zip(SHARD_NAMES, recv_sh):
        c = w[n].shape[-1]
        res = adamw_reduce(recv.reshape(N_DEV, -1, c), w[n].reshape(-1, c), m_in[n].reshape(-1, c),
                           v_in[n].reshape(-1, c), "adamw_" + n)
        for kind, buf in zip(('grad', 'delta', 'new_m', 'new_v'), res):
            out[kind, n] = buf.reshape(w[n].shape)
    res = adamw_reduce(recv_rep, pack_flat([w[n] for n in REP_NAMES], F32), pack_flat([m_in[n] for n in REP_NAMES], F32),
                       pack_flat([v_in[n] for n in REP_NAMES], F32), "adamw_replicated")
    shapes = [w[n].shape for n in REP_NAMES]
    for kind, buf in zip(('grad', 'delta', 'new_m', 'new_v'), res):
        for n, a in zip(REP_NAMES, unpack_flat(buf, shapes)):
            out[kind, n] = a
    return (loss, grad_x) + tuple(out[k, n] for k in ('grad', 'delta', 'new_m', 'new_v') for n in W_NAMES)
```

```python
import functools
import math

import jax
import jax.numpy as jnp
from jax import lax
from jax.experimental import pallas as pl
from jax.experimental.pallas import tpu as pltpu

F32 = jnp.float32
BF16 = jnp.bfloat16

EPS = 1e-6
CHUNK = 64
CONV_K = 4
CONV_TAIL = 8
LRU_C = 8.0
LANE = 128
SUBLANE = 8
N_DEV = 8
NEG_BIG = -1e30

ADAM_LR = 0.001
ADAM_B1 = 0.9
ADAM_B2 = 0.999
ADAM_EPS = 1e-08
ADAM_WD = 0.01
ADAM_STEP = 10

VMEM_LIMIT = 56 * 1024 * 1024


def _dg(a, b, dims):
    return lax.dot_general(a.astype(BF16), b.astype(BF16), (dims, ((), ())), preferred_element_type=F32)


@jax.custom_vjp
def bdot(a, b):
    return _dg(a, b, ((1,), (0,)))


@jax.custom_vjp
def bdot_nt(a, b):
    return _dg(a, b, ((1,), (1,)))


@jax.custom_vjp
def bdot_tn(a, b):
    return _dg(a, b, ((0,), (0,)))


bdot.defvjp(lambda a, b: (bdot(a, b), (a, b)),
            lambda r, g: (bdot_nt(g, r[1]).astype(r[0].dtype), bdot_tn(r[0], g).astype(r[1].dtype)))
bdot_nt.defvjp(lambda a, b: (bdot_nt(a, b), (a, b)),
               lambda r, g: (bdot(g, r[1]).astype(r[0].dtype), bdot_tn(g, r[0]).astype(r[1].dtype)))
bdot_tn.defvjp(lambda a, b: (bdot_tn(a, b), (a, b)),
               lambda r, g: (bdot_nt(r[1], g).astype(r[0].dtype), bdot(r[0], g).astype(r[1].dtype)))


def hdot(a, b):
    return jnp.dot(a, b, preferred_element_type=F32, precision=lax.Precision.HIGHEST)


def rms_norm(x, g):
    return x * lax.rsqrt(jnp.mean(x * x, axis=-1, keepdims=True) + EPS) * g


def row_mask(row0, rows, pad):
    r = row0 + lax.broadcasted_iota(jnp.int32, (rows, 1), 0)
    return (r >= pad).astype(F32)


def conv4(tail, u, w):
    rows = u.shape[0]
    xe = jnp.concatenate([tail, u], axis=0)
    y = w[0:1] * xe[CONV_TAIL - 3:CONV_TAIL - 3 + rows]
    for k in range(1, CONV_K):
        y = y + w[k:k + 1] * xe[CONV_TAIL - 3 + k:CONV_TAIL - 3 + k + rows]
    return y


def shift_rows(x, s, fill):
    rows = x.shape[0]
    return jnp.concatenate([jnp.full((s, x.shape[1]), fill, x.dtype), x[:rows - s]], axis=0)


def lin_scan(a, b):
    rows = a.shape[0]
    s = 1
    while s < rows:
        b = a * shift_rows(b, s, 0.0) + b
        a = a * shift_rows(a, s, 1.0)
        s *= 2
    return b


def cscan_const(ar, ai, br, bi):
    rows = br.shape[0]
    s = 1
    while s < rows:
        brs, bis = shift_rows(br, s, 0.0), shift_rows(bi, s, 0.0)
        br, bi = br + ar * brs - ai * bis, bi + ar * bis + ai * brs
        ar, ai = ar * ar - ai * ai, 2.0 * ar * ai
        s *= 2
    return br, bi


def neg_expm1(z):
    t = jnp.tanh(0.5 * z)
    return -2.0 * t / (1.0 - t)


def tri_masks(n):
    r = lax.broadcasted_iota(jnp.int32, (n, n), 0)
    c = lax.broadcasted_iota(jnp.int32, (n, n), 1)
    return r >= c, r > c, (r == c).astype(F32)


def lru_chunk(pad, row0, params, seqs, carry):
    conv_w, conv_b, w_a, b_a, w_i, b_i, lam, norm_g = params
    u_x, u_gate = seqs
    tail, h0 = carry
    rows = u_x.shape[0]
    m = row_mask(row0, rows, pad)
    xc = conv4(tail, u_x, conv_w) + conv_b
    r = jax.nn.sigmoid(bdot(xc, w_a) + b_a)
    ig = jax.nn.sigmoid(bdot(xc, w_i) + b_i)
    log_a = -LRU_C * r * jax.nn.softplus(-lam)
    a = jnp.exp(log_a)
    b = jnp.sqrt(neg_expm1(2.0 * log_a)) * (ig * xc) * m
    first = (lax.broadcasted_iota(jnp.int32, (rows, 1), 0) == 0).astype(F32)
    b = b + first * (a * h0)
    h = lin_scan(a, b)
    y = jax.nn.gelu(u_gate) * h
    out = rms_norm(y, norm_g) * m
    return (out,), (u_x[rows - CONV_TAIL:], h[rows - 1:])


def gdn_multi(pad, sub, row0, params, seqs, carry):
    wq, wk, wv, a_log, dt_bias, norm_g = params
    u_q, u_k, u_v, u_z, small = seqs
    tq, tk, tv, state = carry
    rows = u_q.shape[0]
    hd = norm_g.shape[1]
    nh = u_q.shape[1] // hd
    nc = rows // sub
    m = row_mask(row0, rows, pad)
    incl, strict, eye = tri_masks(sub)
    tril = incl.astype(F32)
    qc = jax.nn.silu(conv4(tq, u_q, wq))
    kc = jax.nn.silu(conv4(tk, u_k, wk))
    vc = jax.nn.silu(conv4(tv, u_v, wv))
    beta = jax.nn.sigmoid(small[:, :nh]) * m
    g = -jnp.exp(a_log) * jax.nn.softplus(small[:, nh:2 * nh] + dt_bias) * m
    gate = jax.nn.silu(u_z)
    heads = [slice(h * hd, (h + 1) * hd) for h in range(nh)]
    q_h = [qc[:, sl] for sl in heads]
    k_h = [kc[:, sl] for sl in heads]
    q_h = [q * lax.rsqrt(jnp.sum(q * q, axis=-1, keepdims=True) + EPS) * (hd ** -0.5) * m for q in q_h]
    k_h = [k * lax.rsqrt(jnp.sum(k * k, axis=-1, keepdims=True) + EPS) * m for k in k_h]
    v_h = [vc[:, sl] * m for sl in heads]
    pairs = [(c, h) for c in range(nc) for h in range(nh)]
    cs = lambda x, c: x[c * sub:(c + 1) * sub]
    q = {(c, h): cs(q_h[h], c) for c, h in pairs}
    k = {(c, h): cs(k_h[h], c) for c, h in pairs}
    v = {(c, h): cs(v_h[h], c) for c, h in pairs}
    bt = {(c, h): cs(beta, c)[:, h:h + 1] for c, h in pairs}
    gcs = [hdot(tril, cs(g, c)) for c in range(nc)]
    gc = {(c, h): gcs[c][:, h:h + 1] for c, h in pairs}
    decay = {p: jnp.exp(jnp.where(incl, gc[p] - gc[p].T, NEG_BIG)) for p in pairs}
    kb = {p: k[p] * bt[p] for p in pairs}
    kk = {p: bdot_nt(kb[p], k[p]) for p in pairs}
    lmat = {p: jnp.where(strict, kk[p] * decay[p], 0.0) for p in pairs}
    pm = {p: eye - lmat[p] for p in pairs}
    mm = {p: hdot(lmat[p], lmat[p]) for p in pairs}
    s = 2
    while s < sub:
        pm = {p: pm[p] + hdot(pm[p], mm[p]) for p in pairs}
        s *= 2
        if s < sub:
            mm = {p: hdot(mm[p], mm[p]) for p in pairs}
    eg = {p: jnp.exp(gc[p]) for p in pairs}
    u = {p: hdot(pm[p], v[p] * bt[p]) for p in pairs}
    w = {p: hdot(pm[p], kb[p] * eg[p]) for p in pairs}
    attn = {p: bdot_nt(q[p], k[p]) * decay[p] for p in pairs}
    qd = {p: q[p] * eg[p] for p in pairs}
    g_last = {p: gc[p][sub - 1:] for p in pairs}
    kd = {p: k[p] * jnp.exp(g_last[p] - gc[p]) for p in pairs}
    last = {p: jnp.exp(g_last[p]) for p in pairs}
    s_h = [state[sl] for sl in heads]
    o = {}
    for c in range(nc):
        ws = [bdot(w[c, h], s_h[h]) for h in range(nh)]
        qs = [bdot(qd[c, h], s_h[h]) for h in range(nh)]
        v_new = [u[c, h] - ws[h] for h in range(nh)]
        av = [bdot(attn[c, h], v_new[h]) for h in range(nh)]
        kv = [bdot_tn(kd[c, h], v_new[h]) for h in range(nh)]
        for h in range(nh):
            o[c, h] = qs[h] + av[h]
        s_h = [s_h[h] * last[c, h] + kv[h] for h in range(nh)]
    out = jnp.concatenate([jnp.concatenate([rms_norm(o[c, h], norm_g) for h in range(nh)], axis=1)
                           for c in range(nc)], axis=0) * gate * m
    t0 = rows - CONV_TAIL
    return (out,), (u_q[t0:], u_k[t0:], u_v[t0:], jnp.concatenate(s_h, axis=0))


def ssd_multi(pad, dt_lane0, n_groups, sub, row0, params, seqs, carry):
    conv_w, conv_b, a_log, dt_bias, d_skip, norm_g = params
    u_z, u_xbc, small = seqs
    tail, state = carry
    rows = u_z.shape[0]
    width = u_z.shape[1]
    nh = a_log.shape[1]
    hd = width // nh
    ns = (u_xbc.shape[1] - width) // (2 * n_groups)
    hpg = nh // n_groups
    nc = rows // sub
    m = row_mask(row0, rows, pad)
    incl, _, _ = tri_masks(sub)
    tril = incl.astype(F32)
    xbc = jax.nn.silu(conv4(tail, u_xbc, conv_w) + conv_b)
    xs = xbc[:, :width]
    dt = jax.nn.softplus(small[:, dt_lane0:dt_lane0 + nh] + dt_bias)
    a_all = dt * (-jnp.exp(a_log)) * m
    cs = lambda x, c: x[c * sub:(c + 1) * sub]
    heads = [slice(h * hd, (h + 1) * hd) for h in range(nh)]
    pairs = [(c, h) for c in range(nc) for h in range(nh)]
    grp = lambda h: h // hpg
    bm = {(c, g): cs(xbc[:, width + g * ns: width + (g + 1) * ns] * m, c) for c in range(nc) for g in range(n_groups)}
    cm = {(c, g): cs(xbc[:, width + (n_groups + g) * ns: width + (n_groups + g + 1) * ns] * m, c)
          for c in range(nc) for g in range(n_groups)}
    xh = {(c, h): cs(xs[:, heads[h]], c) for c, h in pairs}
    xdt = {(c, h): xh[c, h] * cs(dt[:, h:h + 1] * m, c) for c, h in pairs}
    acums = [hdot(tril, cs(a_all, c)) for c in range(nc)]
    acum = {(c, h): acums[c][:, h:h + 1] for c, h in pairs}
    a_last = {p: acum[p][sub - 1:] for p in pairs}
    lmat = {p: jnp.exp(jnp.where(incl, acum[p] - acum[p].T, NEG_BIG)) for p in pairs}
    cb = {cg: bdot_nt(cm[cg], bm[cg]) for cg in bm}
    y_diag = {(c, h): bdot(cb[c, grp(h)] * lmat[c, h], xdt[c, h]) for c, h in pairs}
    st = {(c, h): bdot_tn(xdt[c, h] * jnp.exp(a_last[c, h] - acum[c, h]), bm[c, grp(h)]) for c, h in pairs}
    e_in = {p: jnp.exp(acum[p]) for p in pairs}
    e_out = {p: jnp.exp(a_last[p]) for p in pairs}
    s_h = [state[sl] for sl in heads]
    y = {}
    for c in range(nc):
        off = [bdot_nt(cm[c, grp(h)], s_h[h]) for h in range(nh)]
        for h in range(nh):
            y[c, h] = y_diag[c, h] + off[h] * e_in[c, h] + d_skip[:, h:h + 1] * xh[c, h]
        s_h = [s_h[h] * e_out[c, h] + st[c, h] for h in range(nh)]
    yy = jnp.concatenate([jnp.concatenate([y[c, h] for h in range(nh)], axis=1) for c in range(nc)], axis=0)
    yy = yy * jax.nn.silu(u_z)
    gw = width // n_groups
    outs = [rms_norm(yy[:, g * gw:(g + 1) * gw], norm_g[:, g * gw:(g + 1) * gw]) for g in range(n_groups)]
    out = jnp.concatenate(outs, axis=1) * m
    return (out,), (u_xbc[rows - CONV_TAIL:], jnp.concatenate(s_h, axis=0))


def s5_chunk(pad, row0, params, seqs, carry):
    a_re, a_im, log_dt, b_re, b_im, c_re, c_im, d_skip = params
    (u,) = seqs
    s_re0, s_im0 = carry
    rows = u.shape[0]
    n_state = a_re.shape[1]
    n_grp = log_dt.shape[1]
    per = n_state // n_grp
    expand = (lax.broadcasted_iota(jnp.int32, (n_grp, n_state), 1) // per
              == lax.broadcasted_iota(jnp.int32, (n_grp, n_state), 0)).astype(F32)
    dt = jnp.exp(hdot(log_dt, expand))
    lam_re = jnp.minimum(a_re, -1e-4)
    lam_im = a_im
    mag = jnp.exp(dt * lam_re)
    ab_re = mag * jnp.cos(dt * lam_im)
    ab_im = mag * jnp.sin(dt * lam_im)
    den = lam_re * lam_re + lam_im * lam_im
    f_re = ((ab_re - 1.0) * lam_re + ab_im * lam_im) / den
    f_im = (ab_im * lam_re - (ab_re - 1.0) * lam_im) / den
    bb_re = f_re * b_re - f_im * b_im
    bb_im = f_re * b_im + f_im * b_re
    bu_re = bdot(u, bb_re)
    bu_im = bdot(u, bb_im)
    first = (lax.broadcasted_iota(jnp.int32, (rows, 1), 0) == 0).astype(F32)
    bu_re = bu_re + first * (ab_re * s_re0 - ab_im * s_im0)
    bu_im = bu_im + first * (ab_re * s_im0 + ab_im * s_re0)
    s_re, s_im = cscan_const(ab_re, ab_im, bu_re, bu_im)
    y = bdot(s_re, c_re) - bdot(s_im, c_im) + d_skip * u
    return (y,), (s_re[rows - 1:], s_im[rows - 1:])


def s5_post(pad, row0, params, seqs, carry):
    w_glu, norm_g = params
    (y,) = seqs
    y = jax.nn.gelu(y)
    y = y * jax.nn.sigmoid(bdot(y, w_glu))
    return (rms_norm(y, norm_g),), ()


def mix_out_delta(row0, params, seqs, carry):
    (w_out,) = params
    wd = seqs[0].shape[1]
    acc = bdot(seqs[0], w_out[0:wd])
    for k in range(1, len(seqs)):
        acc = acc + bdot(seqs[k], w_out[k * wd:(k + 1) * wd])
    return (acc,), ()


def blockdiag_expand(w):
    nh, a, b = w.shape
    eye = jnp.eye(nh, dtype=w.dtype)
    return (w[:, :, None, :] * eye[:, None, :, None]).reshape(nh * a, nh * b)


def blockdiag_extract(m, nh):
    a, b = m.shape[0] // nh, m.shape[1] // nh
    on_diag = jnp.eye(nh, dtype=bool)[:, None, :, None]
    return jnp.sum(jnp.where(on_diag, m.reshape(nh, a, nh, b), 0.0), axis=2)


S5_LANES = LANE


def s5_params_expand(a_re, a_im, log_dt, b_re, b_im, c_re, c_im, d_skip):
    n_grp, n_state = a_re.shape
    ch = b_re.shape[-1]
    gpl = S5_LANES // ch
    nq = n_grp // gpl
    eye = jnp.eye(gpl, dtype=F32)[None, :, None, :, None]

    def bexp(b):
        bt = jnp.swapaxes(b, 1, 2).reshape(nq, gpl, b.shape[2], 1, b.shape[1])
        return (bt * eye).reshape(nq, gpl * b.shape[2], gpl * b.shape[1])

    return (a_re.reshape(nq, 1, gpl * n_state), a_im.reshape(nq, 1, gpl * n_state), log_dt.reshape(nq, 1, gpl),
            bexp(b_re), bexp(b_im), bexp(c_re), bexp(c_im), d_skip.reshape(nq, 1, S5_LANES))


def s5_grads_extract(grads, n_grp, n_state, ch):
    da_re, da_im, dlog_dt, db_re, db_im, dc_re, dc_im, dd = grads
    gpl = S5_LANES // ch
    nq = n_grp // gpl
    on_diag = jnp.eye(gpl, dtype=bool)[None, :, None, :, None]

    def bext(b):
        r, c = b.shape[1] // gpl, b.shape[2] // gpl
        d = jnp.sum(jnp.where(on_diag, b.reshape(nq, gpl, r, gpl, c), 0.0), axis=3)
        return jnp.swapaxes(d.reshape(n_grp, r, c), 1, 2)

    return (da_re.reshape(n_grp, n_state), da_im.reshape(n_grp, n_state), dlog_dt.reshape(n_grp),
            bext(db_re), bext(db_im), bext(dc_re), bext(dc_im), dd.reshape(n_grp * ch))


def _cparams(**kw):
    return pltpu.CompilerParams(vmem_limit_bytes=VMEM_LIMIT, **kw)


def tiled_call(body_fn, name, *, n_steps, rows, n_groups=1, reverse=False,
               seq_in=(), whole_in=(), step_in=(), seq_out=(), acc_out=(), step_out=(), carry=()):
    def step_of(i):
        return (n_steps - 1 - i) if reverse else i

    def col_of(col, g):
        return col(g) if callable(col) else col

    in_specs, operands = [], []
    for arr, width, col in seq_in:
        in_specs.append(pl.BlockSpec((rows, width), lambda g, i, col=col: (step_of(i), col_of(col, g))))
        operands.append(arr)
    for arr in whole_in:
        if arr.ndim == 2:
            in_specs.append(pl.BlockSpec(arr.shape, lambda g, i: (0, 0)))
        else:
            in_specs.append(pl.BlockSpec((None,) + arr.shape[1:], lambda g, i: (g, 0, 0)))
        operands.append(arr)
    for arr in step_in:
        in_specs.append(pl.BlockSpec((None, None) + arr.shape[2:], lambda g, i: (g, step_of(i), 0, 0)))
        operands.append(arr)
    out_shape, out_specs = [], []
    for total, width, col, dt in seq_out:
        out_shape.append(jax.ShapeDtypeStruct((n_steps * rows, total), dt))
        out_specs.append(pl.BlockSpec((rows, width), lambda g, i, col=col: (step_of(i), col_of(col, g))))
    for r, c in acc_out:
        out_shape.append(jax.ShapeDtypeStruct((n_groups, r, c), F32))
        out_specs.append(pl.BlockSpec((None, r, c), lambda g, i: (g, 0, 0)))
    for r, c in step_out:
        out_shape.append(jax.ShapeDtypeStruct((n_groups, n_steps, r, c), F32))
        out_specs.append(pl.BlockSpec((None, None, r, c), lambda g, i: (g, step_of(i), 0, 0)))
    n_seq, n_whole, n_step = len(seq_in), len(whole_in), len(step_in)
    n_so, n_ao, n_sto = len(seq_out), len(acc_out), len(step_out)

    def body(*refs):
        pos = 0
        seq_refs = refs[pos:pos + n_seq]
        pos += n_seq
        whole_refs = refs[pos:pos + n_whole]
        pos += n_whole
        step_refs = refs[pos:pos + n_step]
        pos += n_step
        so_refs = refs[pos:pos + n_so]
        pos += n_so
        ao_refs = refs[pos:pos + n_ao]
        pos += n_ao
        sto_refs = refs[pos:pos + n_sto]
        pos += n_sto
        carry_refs = refs[pos:]
        i = pl.program_id(1)

        @pl.when(i == 0)
        def _():
            for r in carry_refs:
                r[...] = jnp.zeros(r.shape, r.dtype)
            for r in ao_refs:
                r[...] = jnp.zeros(r.shape, r.dtype)

        row0 = step_of(i) * rows
        seq_o, acc_o, step_o, new_c = body_fn(row0, [r[...] for r in whole_refs], [r[...] for r in seq_refs],
                                              [r[...] for r in step_refs], [r[...] for r in carry_refs])
        for r, val in zip(so_refs, seq_o, strict=True):
            r[...] = val.astype(r.dtype)
        for r, val in zip(ao_refs, acc_o, strict=True):
            r[...] += val
        for r, val in zip(sto_refs, step_o, strict=True):
            r[...] = val
        for r, val in zip(carry_refs, new_c, strict=True):
            r[...] = val

    return pl.pallas_call(
        body, name=name, grid=(n_groups, n_steps), in_specs=in_specs, out_specs=out_specs, out_shape=out_shape,
        scratch_shapes=[pltpu.VMEM((r, c), F32) for r, c in carry],
        compiler_params=_cparams(dimension_semantics=("arbitrary", "arbitrary")),
    )(*operands)


def mixer_fwd(fn, name, *, n_steps, rows, seqs, params, out, carry, n_groups=1):
    def body(row0, whole, seq_vals, steps, carry_vals):
        outs, new_c = fn(row0, tuple(whole), tuple(seq_vals), tuple(carry_vals))
        return list(outs), [], list(carry_vals), list(new_c)

    res = tiled_call(body, name, n_steps=n_steps, rows=rows, n_groups=n_groups, seq_in=seqs, whole_in=params,
                     seq_out=[out], step_out=carry, carry=carry)
    return res[0], list(res[1:])


def mixer_bwd(fn, name, *, n_steps, rows, seqs, params, dout, saved, carry, n_groups=1):
    n_seq = len(seqs)

    def body(row0, whole, seq_vals, steps, dcarry):
        params_f = tuple(p.astype(F32) for p in whole)
        _, vjp = jax.vjp(lambda p, s, c: fn(row0, p, s, c), params_f, tuple(seq_vals[:n_seq]), tuple(steps))
        dp, ds, dc = vjp(((seq_vals[n_seq],), tuple(dcarry)))
        return list(ds), list(dp), [], list(dc)

    seq_out = [(n_groups * w, w, (lambda g: g), F32) if callable(c) else (w, w, 0, F32) for a, w, c in seqs]
    acc_out = [p.shape[-2:] for p in params]
    res = tiled_call(body, name, n_steps=n_steps, rows=rows, n_groups=n_groups, reverse=True,
                     seq_in=list(seqs) + [dout], whole_in=params, step_in=saved,
                     seq_out=seq_out, acc_out=acc_out, carry=carry)
    return list(res[:n_seq]), list(res[n_seq:])


def rms_fwd(h, g, name, rows):
    def body(row0, whole, seqs, steps, carry):
        return [rms_norm(seqs[0], whole[0])], [], [], []
    d = h.shape[1]
    return tiled_call(body, name, n_steps=h.shape[0] // rows, rows=rows, seq_in=[(h, d, 0)], whole_in=[g],
                      seq_out=[(d, d, 0, BF16)])[0]


def rms_bwd_add(h, dxn, dh_out, g, name, rows):
    def body(row0, whole, seqs, steps, carry):
        _, vjp = jax.vjp(rms_norm, seqs[0], whole[0])
        dh, dg = vjp(seqs[1])
        return [seqs[2] + dh], [dg], [], []
    d = h.shape[1]
    dh_in, dg = tiled_call(body, name, n_steps=h.shape[0] // rows, rows=rows,
                           seq_in=[(h, d, 0), (dxn, d, 0), (dh_out, d, 0)], whole_in=[g],
                           seq_out=[(d, d, 0, F32)], acc_out=[(1, d)])
    return dh_in, dg[0]


def mix_out_fwd(h, ys, w_out, name, rows):
    def body(row0, whole, seqs, steps, carry):
        (delta,), _ = mix_out_delta(row0, (whole[0],), tuple(seqs[1:]), ())
        return [seqs[0] + delta], [], [], []
    d, wd = h.shape[1], ys[0].shape[1]
    return tiled_call(body, name, n_steps=h.shape[0] // rows, rows=rows,
                      seq_in=[(h, d, 0)] + [(y, wd, 0) for y in ys], whole_in=[w_out], seq_out=[(d, d, 0, F32)])[0]


def loss_and_grad(h, target, g, name, rows, first_row):
    def body(row0, whole, seqs, steps, carry):
        hh, tt = seqs
        keep = row_mask(row0, hh.shape[0], first_row)

        def f(hv, gv):
            err = rms_norm(hv, gv) - tt
            return 0.5 * jnp.sum(jnp.mean(err * err, axis=-1, keepdims=True) * keep, axis=0, keepdims=True)

        val, vjp = jax.vjp(f, hh, whole[0])
        dh, dg = vjp(jnp.ones((1, 1), F32))
        return [dh], [jnp.broadcast_to(val, (1, LANE)), dg], [], []
    d = h.shape[1]
    dh, loss, dg = tiled_call(body, name, n_steps=h.shape[0] // rows, rows=rows,
                              seq_in=[(h, d, 0), (target, d, 0)], whole_in=[g],
                              seq_out=[(d, d, 0, F32)], acc_out=[(1, LANE), (1, d)])
    return loss[0, 0, 0], dh, dg[0]


def _pick(n, cands):
    for c in cands:
        if n % c == 0:
            return c
    raise ValueError(f"no tile for {n}")


ROW_TILES = (1056, 704, 352, 192, 96, 64)
COL_TILES = (256, 128)
NT_DIMS = (((1,), (1,)), ((), ()))
TN_DIMS = (((0,), (0,)), ((), ()))


def ffn_fwd(h, xn, wg, wu, wd, name):
    t, d = h.shape
    f = wg.shape[1]
    tm = _pick(t, ROW_TILES)
    tn = _pick(f, COL_TILES)
    n_j = f // tn

    def body(h_ref, xn_ref, wg_ref, wu_ref, wd_ref, o_ref, acc_ref):
        j = pl.program_id(1)

        @pl.when(j == 0)
        def _():
            acc_ref[...] = jnp.zeros(acc_ref.shape, F32)

        x = xn_ref[...]
        g = jnp.dot(x, wg_ref[...], preferred_element_type=F32)
        u = jnp.dot(x, wu_ref[...], preferred_element_type=F32)
        a = (jax.nn.silu(g) * u).astype(BF16)
        acc_ref[...] += jnp.dot(a, wd_ref[...], preferred_element_type=F32)

        @pl.when(j == n_j - 1)
        def _():
            o_ref[...] = h_ref[...] + 0.5 * acc_ref[...]

    return pl.pallas_call(
        body, name=name, grid=(t // tm, n_j),
        in_specs=[pl.BlockSpec((tm, d), lambda i, j: (i, 0)), pl.BlockSpec((tm, d), lambda i, j: (i, 0)),
                  pl.BlockSpec((d, tn), lambda i, j: (0, j)), pl.BlockSpec((d, tn), lambda i, j: (0, j)),
                  pl.BlockSpec((tn, d), lambda i, j: (j, 0))],
        out_specs=pl.BlockSpec((tm, d), lambda i, j: (i, 0)),
        out_shape=jax.ShapeDtypeStruct((t, d), F32),
        scratch_shapes=[pltpu.VMEM((tm, d), F32)],
        compiler_params=_cparams(dimension_semantics=("arbitrary", "arbitrary")),
    )(h, xn, wg, wu, wd)


def ffn_bwd(xn, dh, wg, wu, wd, name):
    t, d = dh.shape
    f = wg.shape[1]
    tm = _pick(t, ROW_TILES)
    tn = _pick(f, COL_TILES)

    def body(xn_ref, dh_ref, wg_ref, wu_ref, wd_ref, dxn_ref, dwg_ref, dwu_ref, dwd_ref):
        j, i = pl.program_id(0), pl.program_id(1)
        rows = pl.ds(pl.multiple_of(i * tm, 8), tm)
        x = xn_ref[rows, :]
        dhh = (0.5 * dh_ref[...]).astype(BF16)
        wgv, wuv = wg_ref[...], wu_ref[...]
        g = jnp.dot(x, wgv, preferred_element_type=F32)
        u = jnp.dot(x, wuv, preferred_element_type=F32)
        sg = jax.nn.sigmoid(g)
        s = g * sg
        da = lax.dot_general(dhh, wd_ref[...], NT_DIMS, preferred_element_type=F32)
        dwd = lax.dot_general((s * u).astype(BF16), dhh, TN_DIMS, preferred_element_type=F32)
        dg = (da * u * (sg * (1.0 + g * (1.0 - sg)))).astype(BF16)
        du = (da * s).astype(BF16)
        dwg = lax.dot_general(x, dg, TN_DIMS, preferred_element_type=F32)
        dwu = lax.dot_general(x, du, TN_DIMS, preferred_element_type=F32)
        dx = (lax.dot_general(dg, wgv, NT_DIMS, preferred_element_type=F32)
              + lax.dot_general(du, wuv, NT_DIMS, preferred_element_type=F32))

        @pl.when(i == 0)
        def _():
            dwg_ref[...] = dwg
            dwu_ref[...] = dwu
            dwd_ref[...] = dwd

        @pl.when(i > 0)
        def _():
            dwg_ref[...] += dwg
            dwu_ref[...] += dwu
            dwd_ref[...] += dwd

        @pl.when(j == 0)
        def _():
            dxn_ref[rows, :] = dx

        @pl.when(j > 0)
        def _():
            dxn_ref[rows, :] += dx

    return pl.pallas_call(
        body, name=name, grid=(f // tn, t // tm),
        in_specs=[pl.BlockSpec((t, d), lambda j, i: (0, 0)), pl.BlockSpec((tm, d), lambda j, i: (i, 0)),
                  pl.BlockSpec((d, tn), lambda j, i: (0, j)), pl.BlockSpec((d, tn), lambda j, i: (0, j)),
                  pl.BlockSpec((tn, d), lambda j, i: (j, 0))],
        out_specs=[pl.BlockSpec((t, d), lambda j, i: (0, 0)), pl.BlockSpec((d, tn), lambda j, i: (0, j)),
                   pl.BlockSpec((d, tn), lambda j, i: (0, j)), pl.BlockSpec((tn, d), lambda j, i: (j, 0))],
        out_shape=[jax.ShapeDtypeStruct((t, d), F32), jax.ShapeDtypeStruct((d, f), F32),
                   jax.ShapeDtypeStruct((d, f), F32), jax.ShapeDtypeStruct((f, d), F32)],
        compiler_params=_cparams(dimension_semantics=("arbitrary", "arbitrary")),
    )(xn, dh, wg, wu, wd)


def matmul_cols(xn, w, name):
    t, d = xn.shape
    n = w.shape[1]
    tn = _pick(n, COL_TILES)

    def body(x_ref, w_ref, o_ref):
        o_ref[...] = jnp.dot(x_ref[...], w_ref[...], preferred_element_type=F32)

    return pl.pallas_call(
        body, name=name, grid=(n // tn,),
        in_specs=[pl.BlockSpec((t, d), lambda j: (0, 0)), pl.BlockSpec((d, tn), lambda j: (0, j))],
        out_specs=pl.BlockSpec((t, tn), lambda j: (0, j)),
        out_shape=jax.ShapeDtypeStruct((t, n), F32),
        compiler_params=_cparams(dimension_semantics=("arbitrary",)),
    )(xn, w)


def matmul_cols_bwd(xn, dy, w, name):
    t, d = xn.shape
    n = w.shape[1]
    tn = _pick(n, COL_TILES)

    def body(x_ref, dy_ref, w_ref, dx_ref, dw_ref):
        j = pl.program_id(0)
        dyv = dy_ref[...].astype(BF16)
        dw_ref[...] = lax.dot_general(x_ref[...], dyv, TN_DIMS, preferred_element_type=F32)
        dx = lax.dot_general(dyv, w_ref[...], NT_DIMS, preferred_element_type=F32)

        @pl.when(j == 0)
        def _():
            dx_ref[...] = dx

        @pl.when(j > 0)
        def _():
            dx_ref[...] += dx

    return pl.pallas_call(
        body, name=name, grid=(n // tn,),
        in_specs=[pl.BlockSpec((t, d), lambda j: (0, 0)), pl.BlockSpec((t, tn), lambda j: (0, j)),
                  pl.BlockSpec((d, tn), lambda j: (0, j))],
        out_specs=[pl.BlockSpec((t, d), lambda j: (0, 0)), pl.BlockSpec((d, tn), lambda j: (0, j))],
        out_shape=[jax.ShapeDtypeStruct((t, d), F32), jax.ShapeDtypeStruct((d, n), F32)],
        compiler_params=_cparams(dimension_semantics=("arbitrary",)),
    )(xn, dy, w)


def _peer(mx, my, mc, k):
    px = 1 - mx if (k >> 2) & 1 else mx
    py = 1 - my if (k >> 1) & 1 else my
    pc = 1 - mc if k & 1 else mc
    return (px, py, pc), 4 * px + 2 * py + pc


def all_to_all(xs, name):
    n = len(xs)

    def body(*refs):
        x_refs, o_refs = refs[:n], refs[n:2 * n]
        send_sems, recv_sems, local_sems = refs[2 * n:]
        mx, my, mc = lax.axis_index("x"), lax.axis_index("y"), lax.axis_index("c")
        me = 4 * mx + 2 * my + mc
        peers = [_peer(mx, my, mc, k) for k in range(1, N_DEV)]
        locals_, sends, recvs = [], [], []
        for a in range(n):
            x_ref, o_ref = x_refs[a], o_refs[a]
            locals_.append(pltpu.make_async_copy(x_ref.at[me], o_ref.at[me], local_sems.at[a]))
            for k, (dev, peer) in enumerate(peers):
                common = dict(send_sem=send_sems.at[a, k], recv_sem=recv_sems.at[a, k], device_id=dev,
                              device_id_type=pl.DeviceIdType.MESH)
                sends.append(pltpu.make_async_remote_copy(src_ref=x_ref.at[peer], dst_ref=o_ref.at[me], **common))
                recvs.append(pltpu.make_async_remote_copy(src_ref=x_ref.at[peer], dst_ref=o_ref.at[peer], **common))
        for cp in locals_ + sends:
            cp.start()
        for cp in recvs:
            cp.wait_recv()
        for cp in sends:
            cp.wait_send()
        for cp in locals_:
            cp.wait()

    return pl.pallas_call(
        body, name=name,
        in_specs=[pl.BlockSpec(memory_space=pl.ANY)] * n, out_specs=[pl.BlockSpec(memory_space=pl.ANY)] * n,
        out_shape=[jax.ShapeDtypeStruct(x.shape, x.dtype) for x in xs],
        scratch_shapes=[pltpu.SemaphoreType.DMA((n, N_DEV - 1)), pltpu.SemaphoreType.DMA((n, N_DEV - 1)),
                        pltpu.SemaphoreType.DMA((n,))],
    )(*xs)


def all_gather(xs, name):
    n = len(xs)
    chip_flips = (4, 2, 6)

    def body(*refs):
        x_refs, o_refs = refs[:n], refs[n:2 * n]
        send_sems, recv_sems, local_sems = refs[2 * n:]
        mx, my, mc = lax.axis_index("x"), lax.axis_index("y"), lax.axis_index("c")
        me = 4 * mx + 2 * my + mc
        sib_dev, sib = _peer(mx, my, mc, 1)

        def copy(a, k, row, to, src=None):
            return pltpu.make_async_remote_copy(
                src_ref=o_refs[a].at[row] if src is None else src, dst_ref=o_refs[a].at[row],
                send_sem=send_sems.at[a, k], recv_sem=recv_sems.at[a, k], device_id=to,
                device_id_type=pl.DeviceIdType.MESH)

        locals_, first, passed = [], [], []
        for a in range(n):
            locals_.append(pltpu.make_async_copy(x_refs[a], o_refs[a].at[me], local_sems.at[a]))
            first.append(copy(a, 0, me, sib_dev, src=x_refs[a]))
            for j, f in enumerate(chip_flips):
                first.append(copy(a, 1 + j, me, _peer(mx, my, mc, f)[0], src=x_refs[a]))
        for cp in locals_ + first:
            cp.start()
        for a in range(n):
            for j, f in enumerate(chip_flips):
                row = _peer(mx, my, mc, f)[1]
                copy(a, 1 + j, row, sib_dev).wait_recv()
                fwd = copy(a, 4 + j, row, sib_dev)
                fwd.start()
                passed.append(fwd)
        for a in range(n):
            copy(a, 0, sib, sib_dev).wait_recv()
            for j, f in enumerate(chip_flips):
                copy(a, 4 + j, _peer(mx, my, mc, f ^ 1)[1], sib_dev).wait_recv()
        for cp in first + passed:
            cp.wait_send()
        for cp in locals_:
            cp.wait()

    return pl.pallas_call(
        body, name=name,
        in_specs=[pl.BlockSpec(memory_space=pl.ANY)] * n, out_specs=[pl.BlockSpec(memory_space=pl.ANY)] * n,
        out_shape=[jax.ShapeDtypeStruct((N_DEV,) + x.shape, x.dtype) for x in xs],
        scratch_shapes=[pltpu.SemaphoreType.DMA((n, N_DEV - 1)), pltpu.SemaphoreType.DMA((n, N_DEV - 1)),
                        pltpu.SemaphoreType.DMA((n,))],
    )(*xs)


PACK_COLS = 1024
PACK_ROWS = 256
PARTS_TILE_BYTES = 4 * 1024 * 1024


def adamw_reduce(parts, w, m, v, name):
    r, c = w.shape
    fits = [t for t in (512, 352, 256, 128, 64, 32, 16, 8) if N_DEV * t * c * parts.dtype.itemsize <= PARTS_TILE_BYTES]
    tr = r if r < 2 * SUBLANE else _pick(r, fits)
    c1 = 1.0 - ADAM_B1 ** ADAM_STEP
    c2 = 1.0 - ADAM_B2 ** ADAM_STEP

    def body(p_ref, w_ref, m_ref, v_ref, g_ref, d_ref, mo_ref, vo_ref):
        g = p_ref[0].astype(F32)
        for k in range(1, N_DEV):
            g = g + p_ref[k].astype(F32)
        mn = ADAM_B1 * m_ref[...] + (1.0 - ADAM_B1) * g
        vn = ADAM_B2 * v_ref[...] + (1.0 - ADAM_B2) * (g * g)
        g_ref[...] = g
        mo_ref[...] = mn
        vo_ref[...] = vn
        d_ref[...] = -ADAM_LR * ((mn / c1) / (jnp.sqrt(vn / c2) + ADAM_EPS) + ADAM_WD * w_ref[...])

    spec = pl.BlockSpec((tr, c), lambda i: (i, 0))
    return pl.pallas_call(
        body, name=name, grid=(r // tr,),
        in_specs=[pl.BlockSpec((N_DEV, tr, c), lambda i: (0, i, 0)), spec, spec, spec],
        out_specs=[spec] * 4, out_shape=[jax.ShapeDtypeStruct((r, c), F32)] * 4,
        compiler_params=_cparams(dimension_semantics=("arbitrary",)),
    )(parts, w, m, v)


def pack_flat(arrs, dtype):
    parts = []
    for a in arrs:
        flat = a.reshape(-1).astype(dtype)
        k = -(-flat.shape[0] // PACK_COLS)
        parts.append(jnp.pad(flat, (0, k * PACK_COLS - flat.shape[0])).reshape(k, PACK_COLS))
    buf = jnp.concatenate(parts, axis=0)
    return jnp.pad(buf, ((0, -buf.shape[0] % PACK_ROWS), (0, 0)))


def unpack_flat(buf, shapes):
    out, r0 = [], 0
    for s in shapes:
        n = math.prod(s)
        k = -(-n // PACK_COLS)
        out.append(buf[r0:r0 + k].reshape(-1)[:n].reshape(tuple(s)))
        r0 += k
    return out


W_NAMES = ('meta_tokens', 'ffn1_norm', 'ffn1_w_gate', 'ffn1_w_up', 'ffn1_w_down', 'mix_norm', 'w_in', 'w_out',
           'lru_conv_w', 'lru_conv_b', 'lru_w_a', 'lru_b_a', 'lru_w_i', 'lru_b_i', 'lru_lambda', 'lru_norm',
           'gdn_conv_w', 'gdn_a_log', 'gdn_dt_bias', 'gdn_norm', 'ssd_conv_w', 'ssd_conv_b', 'ssd_a_log',
           'ssd_dt_bias', 'ssd_d', 'ssd_norm', 's5_a_re', 's5_a_im', 's5_log_dt', 's5_b_re', 's5_b_im', 's5_c_re',
           's5_c_im', 's5_d', 's5_w_glu', 's5_norm', 'ffn2_norm', 'ffn2_w_gate', 'ffn2_w_up', 'ffn2_w_down',
           'final_norm')
SHARD_AXIS = {'meta_tokens': 1, 'ffn1_w_gate': 2, 'ffn1_w_up': 2, 'ffn1_w_down': 1, 'w_in': 2, 'w_out': 1,
              'lru_conv_w': 2, 'gdn_conv_w': 2, 'ssd_conv_w': 2, 's5_w_glu': 1, 'ffn2_w_gate': 2, 'ffn2_w_up': 2,
              'ffn2_w_down': 1}
BIG_NAMES = ('ffn1_w_gate', 'ffn1_w_up', 'ffn1_w_down', 'w_in', 'w_out', 's5_w_glu', 'ffn2_w_gate', 'ffn2_w_up',
             'ffn2_w_down')
SHARD_NAMES = tuple(n for n in W_NAMES if n in SHARD_AXIS)
REP_NAMES = tuple(n for n in W_NAMES if n not in SHARD_AXIS)
SSD_GROUPS = 2
S5_CH = 16


def unshard(g, axis):
    return jnp.concatenate([g[p] for p in range(N_DEV)], axis=axis)


def kernel(*args):
    n_w = len(W_NAMES)
    x = args[0]
    w = dict(zip(W_NAMES, args[1:1 + n_w]))
    target = args[1 + n_w]
    m_in = dict(zip(W_NAMES, args[2 + n_w:2 + 2 * n_w]))
    v_in = dict(zip(W_NAMES, args[2 + 2 * n_w:2 + 3 * n_w]))

    depth, d = w['ffn1_norm'].shape
    seq = x.shape[1]
    n_meta = w['meta_tokens'].shape[0]
    pad = CHUNK - n_meta
    tp = pad + n_meta + seq
    wg = d // 2
    xbc_w = w['ssd_conv_w'].shape[-1] * N_DEV
    gdn_hd = w['gdn_norm'].shape[-1]
    gdn_h = wg // gdn_hd
    ssd_h = w['ssd_a_log'].shape[-1]
    lru_h = w['lru_w_a'].shape[1]
    s5_g, s5_n = w['s5_a_re'].shape[1:]
    s5_q = wg // S5_LANES
    row_tile = _pick(tp, (192, 96, 64))

    gathered = all_gather([w[n].astype(BF16 if n in BIG_NAMES else F32) for n in SHARD_NAMES], "gather_weights")
    full = {n: unshard(g, SHARD_AXIS[n]) for n, g in zip(SHARD_NAMES, gathered)}

    segs = [('a_x', wg), ('a_gate', wg), ('b_q', wg), ('b_k', wg), ('b_v', wg), ('b_z', wg), ('c_xbc', xbc_w),
            ('c_z', wg), ('d_u', wg), ('small_b', LANE), ('small_c', LANE)]
    off, o = {}, 0
    for nme, wd_ in segs:
        assert o % wd_ == 0, (nme, o, wd_)
        off[nme] = o
        o += wd_
    o_beta = 6 * wg
    o_cz = o_beta + 2 * gdn_h
    o_xbc = o_cz + wg
    o_dt = o_xbc + xbc_w
    o_du = o_dt + ssd_h

    def pack_cols(a):
        z = lambda k: jnp.zeros(a.shape[:-1] + (k,), a.dtype)
        return jnp.concatenate([a[..., :o_beta], a[..., o_xbc:o_dt], a[..., o_cz:o_xbc], a[..., o_du:],
                                a[..., o_beta:o_cz], z(LANE - 2 * gdn_h), a[..., o_dt:o_du], z(LANE - ssd_h)], axis=-1)

    def unpack_cols(a):
        sb, sc = off['small_b'], off['small_c']
        return jnp.concatenate([a[..., :o_beta], a[..., sb:sb + 2 * gdn_h], a[..., off['c_z']:off['c_z'] + wg],
                                a[..., off['c_xbc']:off['c_xbc'] + xbc_w], a[..., sc:sc + ssd_h],
                                a[..., off['d_u']:off['d_u'] + wg]], axis=-1)

    w_in_p = pack_cols(full['w_in'])

    def col(name, width):
        return off[name] // width

    def row(a):
        return a.reshape(1, -1)

    def layer_params(l):
        gcw = full['gdn_conv_w'][l]
        lru = [full['lru_conv_w'][l], row(w['lru_conv_b'][l]), blockdiag_expand(w['lru_w_a'][l]), row(w['lru_b_a'][l]),
               blockdiag_expand(w['lru_w_i'][l]), row(w['lru_b_i'][l]), row(w['lru_lambda'][l]), row(w['lru_norm'][l])]
        gdn = [gcw[:, :wg], gcw[:, wg:2 * wg], gcw[:, 2 * wg:], row(w['gdn_a_log'][l]), row(w['gdn_dt_bias'][l]),
               row(w['gdn_norm'][l])]
        ssd = [full['ssd_conv_w'][l], row(w['ssd_conv_b'][l]), row(w['ssd_a_log'][l]), row(w['ssd_dt_bias'][l]),
               row(w['ssd_d'][l]), row(w['ssd_norm'][l])]
        s5 = list(s5_params_expand(*[w[n][l] for n in ('s5_a_re', 's5_a_im', 's5_log_dt', 's5_b_re', 's5_b_im',
                                                          's5_c_re', 's5_c_im', 's5_d')]))
        post = [full['s5_w_glu'][l], row(w['s5_norm'][l])]
        return lru, gdn, ssd, s5, post

    lru_fn = functools.partial(lru_chunk, pad)
    gdn_fn = functools.partial(gdn_multi, pad, CHUNK)
    ssd_fn = functools.partial(ssd_multi, pad, 0, SSD_GROUPS, CHUNK)
    s5_fn = functools.partial(s5_chunk, pad)
    post_fn = functools.partial(s5_post, pad)
    n_state_lanes = (S5_LANES // S5_CH) * s5_n

    def mixer_specs(proj):
        lru_seqs = [(proj, wg, col('a_x', wg)), (proj, wg, col('a_gate', wg))]
        gdn_seqs = [(proj, wg, col('b_q', wg)), (proj, wg, col('b_k', wg)), (proj, wg, col('b_v', wg)),
                    (proj, wg, col('b_z', wg)), (proj, LANE, col('small_b', LANE))]
        ssd_seqs = [(proj, wg, col('c_z', wg)), (proj, xbc_w, col('c_xbc', xbc_w)), (proj, LANE, col('small_c', LANE))]
        base = col('d_u', S5_LANES)
        s5_seqs = [(proj, S5_LANES, lambda g: base + g)]
        return lru_seqs, gdn_seqs, ssd_seqs, s5_seqs

    lru_carry = [(CONV_TAIL, wg), (1, wg)]
    gdn_carry = [(CONV_TAIL, wg)] * 3 + [(wg, gdn_hd)]
    ssd_carry = [(CONV_TAIL, xbc_w), (wg, (xbc_w - wg) // (2 * SSD_GROUPS))]
    s5_carry = [(1, n_state_lanes)] * 2
    rk = dict(n_steps=tp // row_tile, rows=row_tile)
    mk = rk
    out_w = (wg, wg, 0, F32)

    h = jnp.concatenate([jnp.zeros((pad, d), F32), full['meta_tokens'], x[0]], axis=0)
    target_p = jnp.concatenate([jnp.zeros((pad + n_meta, d), F32), target[0]], axis=0)
    saved = []
    for l in range(depth):
        lru_p, gdn_p, ssd_p, s5_p, post_p = layer_params(l)
        h0 = h
        xn1 = rms_fwd(h0, row(w['ffn1_norm'][l]), "rms_fwd", row_tile)
        h1 = ffn_fwd(h0, xn1, full['ffn1_w_gate'][l], full['ffn1_w_up'][l], full['ffn1_w_down'][l], "ffn_fwd")
        xn2 = rms_fwd(h1, row(w['mix_norm'][l]), "rms_fwd", row_tile)
        proj = matmul_cols(xn2, w_in_p[l], "mix_in_fwd")
        lru_s, gdn_s, ssd_s, s5_s = mixer_specs(proj)
        ya, lru_c = mixer_fwd(lru_fn, "lru_fwd", seqs=lru_s, params=lru_p, out=out_w, carry=lru_carry, **mk)
        yb, gdn_c = mixer_fwd(gdn_fn, "gdn_fwd", seqs=gdn_s, params=gdn_p, out=out_w, carry=gdn_carry, **mk)
        yc, ssd_c = mixer_fwd(ssd_fn, "ssd_fwd", seqs=ssd_s, params=ssd_p, out=out_w, carry=ssd_carry, **mk)
        y1, s5_c = mixer_fwd(s5_fn, "s5_fwd", seqs=s5_s, params=s5_p, out=(wg, S5_LANES, lambda g: g, F32),
                             carry=s5_carry, n_groups=s5_q, **mk)
        yd, _ = mixer_fwd(post_fn, "s5_post_fwd", seqs=[(y1, wg, 0)], params=post_p, out=out_w, carry=[], **rk)
        h2 = mix_out_fwd(h1, [ya, yb, yc, yd], full['w_out'][l], "mix_out_fwd", row_tile)
        xn3 = rms_fwd(h2, row(w['ffn2_norm'][l]), "rms_fwd", row_tile)
        h3 = ffn_fwd(h2, xn3, full['ffn2_w_gate'][l], full['ffn2_w_up'][l], full['ffn2_w_down'][l], "ffn_fwd")
        saved.append((h0, xn1, h1, xn2, proj, (ya, yb, yc, yd), y1, (lru_c, gdn_c, ssd_c, s5_c), h2, xn3))
        h = h3

    loss_part, dh, d_final = loss_and_grad(h, target_p, row(w['final_norm']), "loss", row_tile, pad + n_meta)
    loss = lax.psum(loss_part, ("x", "y", "c"))

    gw = {n: [None] * depth for n in W_NAMES if n not in ('meta_tokens', 'final_norm')}
    for l in reversed(range(depth)):
        lru_p, gdn_p, ssd_p, s5_p, post_p = layer_params(l)
        h0, xn1, h1, xn2, proj, ys, y1, (lru_c, gdn_c, ssd_c, s5_c), h2, xn3 = saved[l]
        dxn, gw['ffn2_w_gate'][l], gw['ffn2_w_up'][l], gw['ffn2_w_down'][l] = ffn_bwd(
            xn3, dh, full['ffn2_w_gate'][l], full['ffn2_w_up'][l], full['ffn2_w_down'][l], "ffn_bwd")
        dh, dg = rms_bwd_add(h2, dxn, dh, row(w['ffn2_norm'][l]), "rms_bwd", row_tile)
        gw['ffn2_norm'][l] = dg[0]

        dys, (d_wout,) = mixer_bwd(mix_out_delta, "mix_out_bwd", seqs=[(y, wg, 0) for y in ys],
                                   params=[full['w_out'][l]], dout=(dh, d, 0), saved=[], carry=[], **rk)
        gw['w_out'][l] = d_wout[0]
        lru_s, gdn_s, ssd_s, s5_s = mixer_specs(proj)
        (dy1,), d_post = mixer_bwd(post_fn, "s5_post_bwd", seqs=[(y1, wg, 0)], params=post_p, dout=(dys[3], wg, 0),
                                   saved=[], carry=[], **rk)
        gw['s5_w_glu'][l], gw['s5_norm'][l] = d_post[0][0], d_post[1][0, 0]
        (d_du,), d_s5 = mixer_bwd(s5_fn, "s5_bwd", seqs=s5_s, params=s5_p, dout=(dy1, S5_LANES, lambda g: g),
                                  saved=s5_c, carry=s5_carry, n_groups=s5_q, **mk)
        for n, g in zip(('s5_a_re', 's5_a_im', 's5_log_dt', 's5_b_re', 's5_b_im', 's5_c_re', 's5_c_im', 's5_d'),
                        s5_grads_extract(d_s5, s5_g, s5_n, S5_CH)):
            gw[n][l] = g
        (d_cz, d_cxbc, d_sc), d_ssd = mixer_bwd(ssd_fn, "ssd_bwd", seqs=ssd_s, params=ssd_p, dout=(dys[2], wg, 0),
                                                saved=ssd_c, carry=ssd_carry, **mk)
        for n, g in zip(('ssd_conv_w', 'ssd_conv_b', 'ssd_a_log', 'ssd_dt_bias', 'ssd_d', 'ssd_norm'), d_ssd):
            gw[n][l] = g[0] if n == 'ssd_conv_w' else g[0, 0]
        (d_bq, d_bk, d_bv, d_bz, d_sb), d_gdn = mixer_bwd(gdn_fn, "gdn_bwd", seqs=gdn_s, params=gdn_p,
                                                          dout=(dys[1], wg, 0), saved=gdn_c, carry=gdn_carry, **mk)
        gw['gdn_conv_w'][l] = jnp.concatenate([d_gdn[0][0], d_gdn[1][0], d_gdn[2][0]], axis=1)
        gw['gdn_a_log'][l], gw['gdn_dt_bias'][l], gw['gdn_norm'][l] = d_gdn[3][0, 0], d_gdn[4][0, 0], d_gdn[5][0, 0]
        (d_ax, d_ag), d_lru = mixer_bwd(lru_fn, "lru_bwd", seqs=lru_s, params=lru_p, dout=(dys[0], wg, 0),
                                        saved=lru_c, carry=lru_carry, **mk)
        gw['lru_conv_w'][l], gw['lru_conv_b'][l] = d_lru[0][0], d_lru[1][0, 0]
        gw['lru_w_a'][l], gw['lru_b_a'][l] = blockdiag_extract(d_lru[2][0], lru_h), d_lru[3][0, 0]
        gw['lru_w_i'][l], gw['lru_b_i'][l] = blockdiag_extract(d_lru[4][0], lru_h), d_lru[5][0, 0]
        gw['lru_lambda'][l], gw['lru_norm'][l] = d_lru[6][0, 0], d_lru[7][0, 0]

        dproj = jnp.concatenate([d_ax, d_ag, d_bq, d_bk, d_bv, d_bz, d_cxbc, d_cz, d_du, d_sb, d_sc], axis=1)
        dxn, d_win_p = matmul_cols_bwd(xn2, dproj, w_in_p[l], "mix_in_bwd")
        gw['w_in'][l] = unpack_cols(d_win_p)
        dh, dg = rms_bwd_add(h1, dxn, dh, row(w['mix_norm'][l]), "rms_bwd", row_tile)
        gw['mix_norm'][l] = dg[0]

        dxn, gw['ffn1_w_gate'][l], gw['ffn1_w_up'][l], gw['ffn1_w_down'][l] = ffn_bwd(
            xn1, dh, full['ffn1_w_gate'][l], full['ffn1_w_up'][l], full['ffn1_w_down'][l], "ffn_bwd")
        dh, dg = rms_bwd_add(h0, dxn, dh, row(w['ffn1_norm'][l]), "rms_bwd", row_tile)
        gw['ffn1_norm'][l] = dg[0]

    grad_x = dh[pad + n_meta:][None]
    grads = {n: jnp.stack(g, axis=0) for n, g in gw.items()}
    grads['meta_tokens'] = dh[pad:pad + n_meta]
    grads['final_norm'] = d_final[0]

    def shards_of(a, axis):
        sh = a.shape
        return jnp.moveaxis(a.reshape(sh[:axis] + (N_DEV, sh[axis] // N_DEV) + sh[axis + 1:]), axis, 0)

    recv_sh = all_to_all([shards_of(grads[n], SHARD_AXIS[n]).astype(BF16) for n in SHARD_NAMES], "scatter_grads")
    (recv_rep,) = all_gather([pack_flat([grads[n] for n in REP_NAMES], F32)], "gather_rep_grads")
    out = {}
    for n, recv in zip(SHARD_NAMES, recv_sh):
        c = w[n].shape[-1]
        res = adamw_reduce(recv.reshape(N_DEV, -1, c), w[n].reshape(-1, c), m_in[n].reshape(-1, c),
                           v_in[n].reshape(-1, c), "adamw_" + n)
        for kind, buf in zip(('grad', 'delta', 'new_m', 'new_v'), res):
            out[kind, n] = buf.reshape(w[n].shape)
    res = adamw_reduce(recv_rep, pack_flat([w[n] for n in REP_NAMES], F32), pack_flat([m_in[n] for n in REP_NAMES], F32),
                       pack_flat([v_in[n] for n in REP_NAMES], F32), "adamw_replicated")
    shapes = [w[n].shape for n in REP_NAMES]
    for kind, buf in zip(('grad', 'delta', 'new_m', 'new_v'), res):
        for n, a in zip(REP_NAMES, unpack_flat(buf, shapes)):
            out[kind, n] = a
    return (loss, grad_x) + tuple(out[k, n] for k in ('grad', 'delta', 'new_m', 'new_v') for n in W_NAMES)
```

```python
import functools
import math

import jax
import jax.numpy as jnp
from jax import lax
from jax.experimental import pallas as pl
from jax.experimental.pallas import tpu as pltpu

F32 = jnp.float32
BF16 = jnp.bfloat16

EPS = 1e-6
CHUNK = 64
CONV_K = 4
CONV_TAIL = 8
LRU_C = 8.0
LANE = 128
SUBLANE = 8
N_DEV = 8
NEG_BIG = -1e30

ADAM_LR = 0.001
ADAM_B1 = 0.9
ADAM_B2 = 0.999
ADAM_EPS = 1e-08
ADAM_WD = 0.01
ADAM_STEP = 10

VMEM_LIMIT = 56 * 1024 * 1024


def _dg(a, b, dims):
    return lax.dot_general(a.astype(BF16), b.astype(BF16), (dims, ((), ())), preferred_element_type=F32)


@jax.custom_vjp
def bdot(a, b):
    return _dg(a, b, ((1,), (0,)))


@jax.custom_vjp
def bdot_nt(a, b):
    return _dg(a, b, ((1,), (1,)))


@jax.custom_vjp
def bdot_tn(a, b):
    return _dg(a, b, ((0,), (0,)))


bdot.defvjp(lambda a, b: (bdot(a, b), (a, b)),
            lambda r, g: (bdot_nt(g, r[1]).astype(r[0].dtype), bdot_tn(r[0], g).astype(r[1].dtype)))
bdot_nt.defvjp(lambda a, b: (bdot_nt(a, b), (a, b)),
               lambda r, g: (bdot(g, r[1]).astype(r[0].dtype), bdot_tn(g, r[0]).astype(r[1].dtype)))
bdot_tn.defvjp(lambda a, b: (bdot_tn(a, b), (a, b)),
               lambda r, g: (bdot_nt(r[1], g).astype(r[0].dtype), bdot(r[0], g).astype(r[1].dtype)))


def hdot(a, b):
    return jnp.dot(a, b, preferred_element_type=F32, precision=lax.Precision.HIGHEST)


def rms_norm(x, g):
    return x * lax.rsqrt(jnp.mean(x * x, axis=-1, keepdims=True) + EPS) * g


def row_mask(row0, rows, pad):
    r = row0 + lax.broadcasted_iota(jnp.int32, (rows, 1), 0)
    return (r >= pad).astype(F32)


def conv4(tail, u, w):
    rows = u.shape[0]
    xe = jnp.concatenate([tail, u], axis=0)
    y = w[0:1] * xe[CONV_TAIL - 3:CONV_TAIL - 3 + rows]
    for k in range(1, CONV_K):
        y = y + w[k:k + 1] * xe[CONV_TAIL - 3 + k:CONV_TAIL - 3 + k + rows]
    return y


def shift_rows(x, s, fill):
    rows = x.shape[0]
    return jnp.concatenate([jnp.full((s, x.shape[1]), fill, x.dtype), x[:rows - s]], axis=0)


def lin_scan(a, b):
    rows = a.shape[0]
    s = 1
    while s < rows:
        b = a * shift_rows(b, s, 0.0) + b
        a = a * shift_rows(a, s, 1.0)
        s *= 2
    return b


def cscan_const(ar, ai, br, bi):
    rows = br.shape[0]
    s = 1
    while s < rows:
        brs, bis = shift_rows(br, s, 0.0), shift_rows(bi, s, 0.0)
        br, bi = br + ar * brs - ai * bis, bi + ar * bis + ai * brs
        ar, ai = ar * ar - ai * ai, 2.0 * ar * ai
        s *= 2
    return br, bi


def neg_expm1(z):
    t = jnp.tanh(0.5 * z)
    return -2.0 * t / (1.0 - t)


def tri_masks(n):
    r = lax.broadcasted_iota(jnp.int32, (n, n), 0)
    c = lax.broadcasted_iota(jnp.int32, (n, n), 1)
    return r >= c, r > c, (r == c).astype(F32)


def lru_chunk(pad, row0, params, seqs, carry):
    conv_w, conv_b, w_a, b_a, w_i, b_i, lam, norm_g = params
    u_x, u_gate = seqs
    tail, h0 = carry
    rows = u_x.shape[0]
    m = row_mask(row0, rows, pad)
    xc = conv4(tail, u_x, conv_w) + conv_b
    r = jax.nn.sigmoid(bdot(xc, w_a) + b_a)
    ig = jax.nn.sigmoid(bdot(xc, w_i) + b_i)
    log_a = -LRU_C * r * jax.nn.softplus(-lam)
    a = jnp.exp(log_a)
    b = jnp.sqrt(neg_expm1(2.0 * log_a)) * (ig * xc) * m
    first = (lax.broadcasted_iota(jnp.int32, (rows, 1), 0) == 0).astype(F32)
    b = b + first * (a * h0)
    h = lin_scan(a, b)
    y = jax.nn.gelu(u_gate) * h
    out = rms_norm(y, norm_g) * m
    return (out,), (u_x[rows - CONV_TAIL:], h[rows - 1:])


def gdn_multi(pad, sub, row0, params, seqs, carry):
    wq, wk, wv, a_log, dt_bias, norm_g = params
    u_q, u_k, u_v, u_z, small = seqs
    tq, tk, tv, state = carry
    rows = u_q.shape[0]
    hd = norm_g.shape[1]
    nh = u_q.shape[1] // hd
    nc = rows // sub
    m = row_mask(row0, rows, pad)
    incl, strict, eye = tri_masks(sub)
    tril = incl.astype(F32)
    qc = jax.nn.silu(conv4(tq, u_q, wq))
    kc = jax.nn.silu(conv4(tk, u_k, wk))
    vc = jax.nn.silu(conv4(tv, u_v, wv))
    beta = jax.nn.sigmoid(small[:, :nh]) * m
    g = -jnp.exp(a_log) * jax.nn.softplus(small[:, nh:2 * nh] + dt_bias) * m
    gate = jax.nn.silu(u_z)
    heads = [slice(h * hd, (h + 1) * hd) for h in range(nh)]
    q_h = [qc[:, sl] for sl in heads]
    k_h = [kc[:, sl] for sl in heads]
    q_h = [q * lax.rsqrt(jnp.sum(q * q, axis=-1, keepdims=True) + EPS) * (hd ** -0.5) * m for q in q_h]
    k_h = [k * lax.rsqrt(jnp.sum(k * k, axis=-1, keepdims=True) + EPS) * m for k in k_h]
    v_h = [vc[:, sl] * m for sl in heads]
    pairs = [(c, h) for c in range(nc) for h in range(nh)]
    cs = lambda x, c: x[c * sub:(c + 1) * sub]
    q = {(c, h): cs(q_h[h], c) for c, h in pairs}
    k = {(c, h): cs(k_h[h], c) for c, h in pairs}
    v = {(c, h): cs(v_h[h], c) for c, h in pairs}
    bt = {(c, h): cs(beta, c)[:, h:h + 1] for c, h in pairs}
    gcs = [hdot(tril, cs(g, c)) for c in range(nc)]
    gc = {(c, h): gcs[c][:, h:h + 1] for c, h in pairs}
    decay = {p: jnp.exp(jnp.where(incl, gc[p] - gc[p].T, NEG_BIG)) for p in pairs}
    kb = {p: k[p] * bt[p] for p in pairs}
    kk = {p: bdot_nt(kb[p], k[p]) for p in pairs}
    lmat = {p: jnp.where(strict, kk[p] * decay[p], 0.0) for p in pairs}
    pm = {p: eye - lmat[p] for p in pairs}
    mm = {p: hdot(lmat[p], lmat[p]) for p in pairs}
    s = 2
    while s < sub:
        pm = {p: pm[p] + hdot(pm[p], mm[p]) for p in pairs}
        s *= 2
        if s < sub:
            mm = {p: hdot(mm[p], mm[p]) for p in pairs}
    eg = {p: jnp.exp(gc[p]) for p in pairs}
    u = {p: hdot(pm[p], v[p] * bt[p]) for p in pairs}
    w = {p: hdot(pm[p], kb[p] * eg[p]) for p in pairs}
    attn = {p: bdot_nt(q[p], k[p]) * decay[p] for p in pairs}
    qd = {p: q[p] * eg[p] for p in pairs}
    g_last = {p: gc[p][sub - 1:] for p in pairs}
    kd = {p: k[p] * jnp.exp(g_last[p] - gc[p]) for p in pairs}
    last = {p: jnp.exp(g_last[p]) for p in pairs}
    s_h = [state[sl] for sl in heads]
    o = {}
    for c in range(nc):
        ws = [bdot(w[c, h], s_h[h]) for h in range(nh)]
        qs = [bdot(qd[c, h], s_h[h]) for h in range(nh)]
        v_new = [u[c, h] - ws[h] for h in range(nh)]
        av = [bdot(attn[c, h], v_new[h]) for h in range(nh)]
        kv = [bdot_tn(kd[c, h], v_new[h]) for h in range(nh)]
        for h in range(nh):
            o[c, h] = qs[h] + av[h]
        s_h = [s_h[h] * last[c, h] + kv[h] for h in range(nh)]
    out = jnp.concatenate([jnp.concatenate([rms_norm(o[c, h], norm_g) for h in range(nh)], axis=1)
                           for c in range(nc)], axis=0) * gate * m
    t0 = rows - CONV_TAIL
    return (out,), (u_q[t0:], u_k[t0:], u_v[t0:], jnp.concatenate(s_h, axis=0))


def ssd_multi(pad, dt_lane0, n_groups, sub, row0, params, seqs, carry):
    conv_w, conv_b, a_log, dt_bias, d_skip, norm_g = params
    u_z, u_xbc, small = seqs
    tail, state = carry
    rows = u_z.shape[0]
    width = u_z.shape[1]
    nh = a_log.shape[1]
    hd = width // nh
    ns = (u_xbc.shape[1] - width) // (2 * n_groups)
    hpg = nh // n_groups
    nc = rows // sub
    m = row_mask(row0, rows, pad)
    incl, _, _ = tri_masks(sub)
    tril = incl.astype(F32)
    xbc = jax.nn.silu(conv4(tail, u_xbc, conv_w) + conv_b)
    xs = xbc[:, :width]
    dt = jax.nn.softplus(small[:, dt_lane0:dt_lane0 + nh] + dt_bias)
    a_all = dt * (-jnp.exp(a_log)) * m
    cs = lambda x, c: x[c * sub:(c + 1) * sub]
    heads = [slice(h * hd, (h + 1) * hd) for h in range(nh)]
    pairs = [(c, h) for c in range(nc) for h in range(nh)]
    grp = lambda h: h // hpg
    bm = {(c, g): cs(xbc[:, width + g * ns: width + (g + 1) * ns] * m, c) for c in range(nc) for g in range(n_groups)}
    cm = {(c, g): cs(xbc[:, width + (n_groups + g) * ns: width + (n_groups + g + 1) * ns] * m, c)
          for c in range(nc) for g in range(n_groups)}
    xh = {(c, h): cs(xs[:, heads[h]], c) for c, h in pairs}
    xdt = {(c, h): xh[c, h] * cs(dt[:, h:h + 1] * m, c) for c, h in pairs}
    acums = [hdot(tril, cs(a_all, c)) for c in range(nc)]
    acum = {(c, h): acums[c][:, h:h + 1] for c, h in pairs}
    a_last = {p: acum[p][sub - 1:] for p in pairs}
    lmat = {p: jnp.exp(jnp.where(incl, acum[p] - acum[p].T, NEG_BIG)) for p in pairs}
    cb = {cg: bdot_nt(cm[cg], bm[cg]) for cg in bm}
    y_diag = {(c, h): bdot(cb[c, grp(h)] * lmat[c, h], xdt[c, h]) for c, h in pairs}
    st = {(c, h): bdot_tn(xdt[c, h] * jnp.exp(a_last[c, h] - acum[c, h]), bm[c, grp(h)]) for c, h in pairs}
    e_in = {p: jnp.exp(acum[p]) for p in pairs}
    e_out = {p: jnp.exp(a_last[p]) for p in pairs}
    s_h = [state[sl] for sl in heads]
    y = {}
    for c in range(nc):
        off = [bdot_nt(cm[c, grp(h)], s_h[h]) for h in range(nh)]
        for h in range(nh):
            y[c, h] = y_diag[c, h] + off[h] * e_in[c, h] + d_skip[:, h:h + 1] * xh[c, h]
        s_h = [s_h[h] * e_out[c, h] + st[c, h] for h in range(nh)]
    yy = jnp.concatenate([jnp.concatenate([y[c, h] for h in range(nh)], axis=1) for c in range(nc)], axis=0)
    yy = yy * jax.nn.silu(u_z)
    gw = width // n_groups
    outs = [rms_norm(yy[:, g * gw:(g + 1) * gw], norm_g[:, g * gw:(g + 1) * gw]) for g in range(n_groups)]
    out = jnp.concatenate(outs, axis=1) * m
    return (out,), (u_xbc[rows - CONV_TAIL:], jnp.concatenate(s_h, axis=0))


def s5_chunk(pad, row0, params, seqs, carry):
    a_re, a_im, log_dt, b_re, b_im, c_re, c_im, d_skip = params
    (u,) = seqs
    s_re0, s_im0 = carry
    rows = u.shape[0]
    n_state = a_re.shape[1]
    n_grp = log_dt.shape[1]
    per = n_state // n_grp
    expand = (lax.broadcasted_iota(jnp.int32, (n_grp, n_state), 1) // per
              == lax.broadcasted_iota(jnp.int32, (n_grp, n_state), 0)).astype(F32)
    dt = jnp.exp(hdot(log_dt, expand))
    lam_re = jnp.minimum(a_re, -1e-4)
    lam_im = a_im
    mag = jnp.exp(dt * lam_re)
    ab_re = mag * jnp.cos(dt * lam_im)
    ab_im = mag * jnp.sin(dt * lam_im)
    den = lam_re * lam_re + lam_im * lam_im
    f_re = ((ab_re - 1.0) * lam_re + ab_im * lam_im) / den
    f_im = (ab_im * lam_re - (ab_re - 1.0) * lam_im) / den
    bb_re = f_re * b_re - f_im * b_im
    bb_im = f_re * b_im + f_im * b_re
    bu_re = bdot(u, bb_re)
    bu_im = bdot(u, bb_im)
    first = (lax.broadcasted_iota(jnp.int32, (rows, 1), 0) == 0).astype(F32)
    bu_re = bu_re + first * (ab_re * s_re0 - ab_im * s_im0)
    bu_im = bu_im + first * (ab_re * s_im0 + ab_im * s_re0)
    s_re, s_im = cscan_const(ab_re, ab_im, bu_re, bu_im)
    y = bdot(s_re, c_re) - bdot(s_im, c_im) + d_skip * u
    return (y,), (s_re[rows - 1:], s_im[rows - 1:])


def s5_post(pad, row0, params, seqs, carry):
    w_glu, norm_g = params
    (y,) = seqs
    y = jax.nn.gelu(y)
    y = y * jax.nn.sigmoid(bdot(y, w_glu))
    return (rms_norm(y, norm_g),), ()


def mix_out_delta(row0, params, seqs, carry):
    (w_out,) = params
    wd = seqs[0].shape[1]
    acc = bdot(seqs[0], w_out[0:wd])
    for k in range(1, len(seqs)):
        acc = acc + bdot(seqs[k], w_out[k * wd:(k + 1) * wd])
    return (acc,), ()


def blockdiag_expand(w):
    nh, a, b = w.shape
    eye = jnp.eye(nh, dtype=w.dtype)
    return (w[:, :, None, :] * eye[:, None, :, None]).reshape(nh * a, nh * b)


def blockdiag_extract(m, nh):
    a, b = m.shape[0] // nh, m.shape[1] // nh
    on_diag = jnp.eye(nh, dtype=bool)[:, None, :, None]
    return jnp.sum(jnp.where(on_diag, m.reshape(nh, a, nh, b), 0.0), axis=2)


S5_LANES = LANE


def s5_params_expand(a_re, a_im, log_dt, b_re, b_im, c_re, c_im, d_skip):
    n_grp, n_state = a_re.shape
    ch = b_re.shape[-1]
    gpl = S5_LANES // ch
    nq = n_grp // gpl
    eye = jnp.eye(gpl, dtype=F32)[None, :, None, :, None]

    def bexp(b):
        bt = jnp.swapaxes(b, 1, 2).reshape(nq, gpl, b.shape[2], 1, b.shape[1])
        return (bt * eye).reshape(nq, gpl * b.shape[2], gpl * b.shape[1])

    return (a_re.reshape(nq, 1, gpl * n_state), a_im.reshape(nq, 1, gpl * n_state), log_dt.reshape(nq, 1, gpl),
            bexp(b_re), bexp(b_im), bexp(c_re), bexp(c_im), d_skip.reshape(nq, 1, S5_LANES))


def s5_grads_extract(grads, n_grp, n_state, ch):
    da_re, da_im, dlog_dt, db_re, db_im, dc_re, dc_im, dd = grads
    gpl = S5_LANES // ch
    nq = n_grp // gpl
    on_diag = jnp.eye(gpl, dtype=bool)[None, :, None, :, None]

    def bext(b):
        r, c = b.shape[1] // gpl, b.shape[2] // gpl
        d = jnp.sum(jnp.where(on_diag, b.reshape(nq, gpl, r, gpl, c), 0.0), axis=3)
        return jnp.swapaxes(d.reshape(n_grp, r, c), 1, 2)

    return (da_re.reshape(n_grp, n_state), da_im.reshape(n_grp, n_state), dlog_dt.reshape(n_grp),
            bext(db_re), bext(db_im), bext(dc_re), bext(dc_im), dd.reshape(n_grp * ch))


def _cparams(**kw):
    return pltpu.CompilerParams(vmem_limit_bytes=VMEM_LIMIT, **kw)


def tiled_call(body_fn, name, *, n_steps, rows, n_groups=1, reverse=False,
               seq_in=(), whole_in=(), step_in=(), seq_out=(), acc_out=(), step_out=(), carry=(), a2a=()):
    def step_of(i):
        return (n_steps - 1 - i) if reverse else i

    def col_of(col, g):
        return col(g) if callable(col) else col

    in_specs, operands = [], []
    for arr, width, col in seq_in:
        in_specs.append(pl.BlockSpec((rows, width), lambda g, i, col=col: (step_of(i), col_of(col, g))))
        operands.append(arr)
    for arr in whole_in:
        if arr.ndim == 2:
            in_specs.append(pl.BlockSpec(arr.shape, lambda g, i: (0, 0)))
        else:
            in_specs.append(pl.BlockSpec((None,) + arr.shape[1:], lambda g, i: (g, 0, 0)))
        operands.append(arr)
    for arr in step_in:
        in_specs.append(pl.BlockSpec((None, None) + arr.shape[2:], lambda g, i: (g, step_of(i), 0, 0)))
        operands.append(arr)
    out_shape, out_specs = [], []
    for total, width, col, dt in seq_out:
        out_shape.append(jax.ShapeDtypeStruct((n_steps * rows, total), dt))
        out_specs.append(pl.BlockSpec((rows, width), lambda g, i, col=col: (step_of(i), col_of(col, g))))
    for r, c in acc_out:
        out_shape.append(jax.ShapeDtypeStruct((n_groups, r, c), F32))
        out_specs.append(pl.BlockSpec((None, r, c), lambda g, i: (g, 0, 0)))
    for r, c in step_out:
        out_shape.append(jax.ShapeDtypeStruct((n_groups, n_steps, r, c), F32))
        out_specs.append(pl.BlockSpec((None, None, r, c), lambda g, i: (g, step_of(i), 0, 0)))
    n_seq, n_whole, n_step = len(seq_in), len(whole_in), len(step_in)
    n_so, n_ao, n_sto = len(seq_out), len(acc_out), len(step_out)
    n_x = len(a2a)
    hbm = pl.BlockSpec(memory_space=pl.ANY)
    in_specs += [hbm] * n_x
    operands += list(a2a)
    out_specs += [hbm] * n_x
    out_shape += [jax.ShapeDtypeStruct(x.shape, x.dtype) for x in a2a]
    scratch = [pltpu.VMEM((r, c), F32) for r, c in carry]
    if n_x:
        scratch += [pltpu.SemaphoreType.DMA((n_x, N_DEV - 1)), pltpu.SemaphoreType.DMA((n_x, N_DEV - 1)),
                    pltpu.SemaphoreType.DMA((n_x,))]

    def body(*refs):
        pos = 0
        seq_refs = refs[pos:pos + n_seq]
        pos += n_seq
        whole_refs = refs[pos:pos + n_whole]
        pos += n_whole
        step_refs = refs[pos:pos + n_step]
        pos += n_step + n_x
        so_refs = refs[pos:pos + n_so]
        pos += n_so
        ao_refs = refs[pos:pos + n_ao]
        pos += n_ao
        sto_refs = refs[pos:pos + n_sto]
        pos += n_sto
        xo_refs = refs[pos:pos + n_x]
        pos += n_x
        carry_refs = refs[pos:pos + len(carry)]
        pos += len(carry)
        x_refs = refs[n_seq + n_whole + n_step:n_seq + n_whole + n_step + n_x]
        g, i = pl.program_id(0), pl.program_id(1)
        if n_x:
            locals_, sends, recvs = a2a_copies(x_refs, xo_refs, *refs[pos:])

            @pl.when((g == 0) & (i == 0))
            def _():
                for cp in locals_ + sends:
                    cp.start()

        @pl.when(i == 0)
        def _():
            for r in carry_refs:
                r[...] = jnp.zeros(r.shape, r.dtype)
            for r in ao_refs:
                r[...] = jnp.zeros(r.shape, r.dtype)

        row0 = step_of(i) * rows
        seq_o, acc_o, step_o, new_c = body_fn(row0, [r[...] for r in whole_refs], [r[...] for r in seq_refs],
                                              [r[...] for r in step_refs], [r[...] for r in carry_refs])
        for r, val in zip(so_refs, seq_o, strict=True):
            r[...] = val.astype(r.dtype)
        for r, val in zip(ao_refs, acc_o, strict=True):
            r[...] += val
        for r, val in zip(sto_refs, step_o, strict=True):
            r[...] = val
        for r, val in zip(carry_refs, new_c, strict=True):
            r[...] = val
        if n_x:
            @pl.when((g == n_groups - 1) & (i == n_steps - 1))
            def _():
                for cp in recvs:
                    cp.wait_recv()
                for cp in sends:
                    cp.wait_send()
                for cp in locals_:
                    cp.wait()

    return pl.pallas_call(
        body, name=name, grid=(n_groups, n_steps), in_specs=in_specs, out_specs=out_specs, out_shape=out_shape,
        scratch_shapes=scratch,
        compiler_params=_cparams(dimension_semantics=("arbitrary", "arbitrary")),
    )(*operands)


def mixer_fwd(fn, name, *, n_steps, rows, seqs, params, out, carry, n_groups=1):
    def body(row0, whole, seq_vals, steps, carry_vals):
        outs, new_c = fn(row0, tuple(whole), tuple(seq_vals), tuple(carry_vals))
        return list(outs), [], list(carry_vals), list(new_c)

    res = tiled_call(body, name, n_steps=n_steps, rows=rows, n_groups=n_groups, seq_in=seqs, whole_in=params,
                     seq_out=[out], step_out=carry, carry=carry)
    return res[0], list(res[1:])


def mixer_bwd(fn, name, *, n_steps, rows, seqs, params, dout, saved, carry, n_groups=1, a2a=()):
    n_seq = len(seqs)

    def body(row0, whole, seq_vals, steps, dcarry):
        params_f = tuple(p.astype(F32) for p in whole)
        _, vjp = jax.vjp(lambda p, s, c: fn(row0, p, s, c), params_f, tuple(seq_vals[:n_seq]), tuple(steps))
        dp, ds, dc = vjp(((seq_vals[n_seq],), tuple(dcarry)))
        return list(ds), list(dp), [], list(dc)

    seq_out = [(n_groups * w, w, (lambda g: g), F32) if callable(c) else (w, w, 0, F32) for a, w, c in seqs]
    acc_out = [p.shape[-2:] for p in params]
    res = tiled_call(body, name, n_steps=n_steps, rows=rows, n_groups=n_groups, reverse=True,
                     seq_in=list(seqs) + [dout], whole_in=params, step_in=saved,
                     seq_out=seq_out, acc_out=acc_out, carry=carry, a2a=a2a)
    n_p = len(params)
    if a2a:
        return list(res[:n_seq]), list(res[n_seq:n_seq + n_p]), list(res[n_seq + n_p:])
    return list(res[:n_seq]), list(res[n_seq:])


def rms_fwd(h, g, name, rows):
    def body(row0, whole, seqs, steps, carry):
        return [rms_norm(seqs[0], whole[0])], [], [], []
    d = h.shape[1]
    return tiled_call(body, name, n_steps=h.shape[0] // rows, rows=rows, seq_in=[(h, d, 0)], whole_in=[g],
                      seq_out=[(d, d, 0, BF16)])[0]


def rms_bwd_add(h, dxn, dh_out, g, name, rows):
    def body(row0, whole, seqs, steps, carry):
        _, vjp = jax.vjp(rms_norm, seqs[0], whole[0])
        dh, dg = vjp(seqs[1])
        return [seqs[2] + dh], [dg], [], []
    d = h.shape[1]
    dh_in, dg = tiled_call(body, name, n_steps=h.shape[0] // rows, rows=rows,
                           seq_in=[(h, d, 0), (dxn, d, 0), (dh_out, d, 0)], whole_in=[g],
                           seq_out=[(d, d, 0, F32)], acc_out=[(1, d)])
    return dh_in, dg[0]


def mix_out_fwd(h, ys, w_out, name, rows):
    def body(row0, whole, seqs, steps, carry):
        (delta,), _ = mix_out_delta(row0, (whole[0],), tuple(seqs[1:]), ())
        return [seqs[0] + delta], [], [], []
    d, wd = h.shape[1], ys[0].shape[1]
    return tiled_call(body, name, n_steps=h.shape[0] // rows, rows=rows,
                      seq_in=[(h, d, 0)] + [(y, wd, 0) for y in ys], whole_in=[w_out], seq_out=[(d, d, 0, F32)])[0]


def loss_and_grad(h, target, g, name, rows, first_row):
    def body(row0, whole, seqs, steps, carry):
        hh, tt = seqs
        keep = row_mask(row0, hh.shape[0], first_row)

        def f(hv, gv):
            err = rms_norm(hv, gv) - tt
            return 0.5 * jnp.sum(jnp.mean(err * err, axis=-1, keepdims=True) * keep, axis=0, keepdims=True)

        val, vjp = jax.vjp(f, hh, whole[0])
        dh, dg = vjp(jnp.ones((1, 1), F32))
        return [dh], [jnp.broadcast_to(val, (1, LANE)), dg], [], []
    d = h.shape[1]
    dh, loss, dg = tiled_call(body, name, n_steps=h.shape[0] // rows, rows=rows,
                              seq_in=[(h, d, 0), (target, d, 0)], whole_in=[g],
                              seq_out=[(d, d, 0, F32)], acc_out=[(1, LANE), (1, d)])
    return loss[0, 0, 0], dh, dg[0]


def _pick(n, cands):
    for c in cands:
        if n % c == 0:
            return c
    raise ValueError(f"no tile for {n}")


ROW_TILES = (1056, 704, 352, 192, 96, 64)
COL_TILES = (256, 128)
NT_DIMS = (((1,), (1,)), ((), ()))
TN_DIMS = (((0,), (0,)), ((), ()))


def ffn_fwd(h, xn, wg, wu, wd, name):
    t, d = h.shape
    f = wg.shape[1]
    tm = _pick(t, ROW_TILES)
    tn = _pick(f, COL_TILES)
    n_j = f // tn

    def body(h_ref, xn_ref, wg_ref, wu_ref, wd_ref, o_ref, acc_ref):
        j = pl.program_id(1)

        @pl.when(j == 0)
        def _():
            acc_ref[...] = jnp.zeros(acc_ref.shape, F32)

        x = xn_ref[...]
        g = jnp.dot(x, wg_ref[...], preferred_element_type=F32)
        u = jnp.dot(x, wu_ref[...], preferred_element_type=F32)
        a = (jax.nn.silu(g) * u).astype(BF16)
        acc_ref[...] += jnp.dot(a, wd_ref[...], preferred_element_type=F32)

        @pl.when(j == n_j - 1)
        def _():
            o_ref[...] = h_ref[...] + 0.5 * acc_ref[...]

    return pl.pallas_call(
        body, name=name, grid=(t // tm, n_j),
        in_specs=[pl.BlockSpec((tm, d), lambda i, j: (i, 0)), pl.BlockSpec((tm, d), lambda i, j: (i, 0)),
                  pl.BlockSpec((d, tn), lambda i, j: (0, j)), pl.BlockSpec((d, tn), lambda i, j: (0, j)),
                  pl.BlockSpec((tn, d), lambda i, j: (j, 0))],
        out_specs=pl.BlockSpec((tm, d), lambda i, j: (i, 0)),
        out_shape=jax.ShapeDtypeStruct((t, d), F32),
        scratch_shapes=[pltpu.VMEM((tm, d), F32)],
        compiler_params=_cparams(dimension_semantics=("arbitrary", "arbitrary")),
    )(h, xn, wg, wu, wd)


def ffn_bwd(xn, dh, wg, wu, wd, name):
    t, d = dh.shape
    f = wg.shape[1]
    tm = _pick(t, ROW_TILES)
    tn = _pick(f, COL_TILES)

    def body(xn_ref, dh_ref, wg_ref, wu_ref, wd_ref, dxn_ref, dwg_ref, dwu_ref, dwd_ref):
        j, i = pl.program_id(0), pl.program_id(1)
        rows = pl.ds(pl.multiple_of(i * tm, 8), tm)
        x = xn_ref[rows, :]
        dhh = (0.5 * dh_ref[...]).astype(BF16)
        wgv, wuv = wg_ref[...], wu_ref[...]
        g = jnp.dot(x, wgv, preferred_element_type=F32)
        u = jnp.dot(x, wuv, preferred_element_type=F32)
        sg = jax.nn.sigmoid(g)
        s = g * sg
        da = lax.dot_general(dhh, wd_ref[...], NT_DIMS, preferred_element_type=F32)
        dwd = lax.dot_general((s * u).astype(BF16), dhh, TN_DIMS, preferred_element_type=F32)
        dg = (da * u * (sg * (1.0 + g * (1.0 - sg)))).astype(BF16)
        du = (da * s).astype(BF16)
        dwg = lax.dot_general(x, dg, TN_DIMS, preferred_element_type=F32)
        dwu = lax.dot_general(x, du, TN_DIMS, preferred_element_type=F32)
        dx = (lax.dot_general(dg, wgv, NT_DIMS, preferred_element_type=F32)
              + lax.dot_general(du, wuv, NT_DIMS, preferred_element_type=F32))

        @pl.when(i == 0)
        def _():
            dwg_ref[...] = dwg
            dwu_ref[...] = dwu
            dwd_ref[...] = dwd

        @pl.when(i > 0)
        def _():
            dwg_ref[...] += dwg
            dwu_ref[...] += dwu
            dwd_ref[...] += dwd

        @pl.when(j == 0)
        def _():
            dxn_ref[rows, :] = dx

        @pl.when(j > 0)
        def _():
            dxn_ref[rows, :] += dx

    return pl.pallas_call(
        body, name=name, grid=(f // tn, t // tm),
        in_specs=[pl.BlockSpec((t, d), lambda j, i: (0, 0)), pl.BlockSpec((tm, d), lambda j, i: (i, 0)),
                  pl.BlockSpec((d, tn), lambda j, i: (0, j)), pl.BlockSpec((d, tn), lambda j, i: (0, j)),
                  pl.BlockSpec((tn, d), lambda j, i: (j, 0))],
        out_specs=[pl.BlockSpec((t, d), lambda j, i: (0, 0)), pl.BlockSpec((d, tn), lambda j, i: (0, j)),
                   pl.BlockSpec((d, tn), lambda j, i: (0, j)), pl.BlockSpec((tn, d), lambda j, i: (j, 0))],
        out_shape=[jax.ShapeDtypeStruct((t, d), F32), jax.ShapeDtypeStruct((d, f), F32),
                   jax.ShapeDtypeStruct((d, f), F32), jax.ShapeDtypeStruct((f, d), F32)],
        compiler_params=_cparams(dimension_semantics=("arbitrary", "arbitrary")),
    )(xn, dh, wg, wu, wd)


def matmul_cols(xn, w, name):
    t, d = xn.shape
    n = w.shape[1]
    tn = _pick(n, COL_TILES)

    def body(x_ref, w_ref, o_ref):
        o_ref[...] = jnp.dot(x_ref[...], w_ref[...], preferred_element_type=F32)

    return pl.pallas_call(
        body, name=name, grid=(n // tn,),
        in_specs=[pl.BlockSpec((t, d), lambda j: (0, 0)), pl.BlockSpec((d, tn), lambda j: (0, j))],
        out_specs=pl.BlockSpec((t, tn), lambda j: (0, j)),
        out_shape=jax.ShapeDtypeStruct((t, n), F32),
        compiler_params=_cparams(dimension_semantics=("arbitrary",)),
    )(xn, w)


def matmul_cols_bwd(xn, dy, w, name):
    t, d = xn.shape
    n = w.shape[1]
    tn = _pick(n, COL_TILES)

    def body(x_ref, dy_ref, w_ref, dx_ref, dw_ref):
        j = pl.program_id(0)
        dyv = dy_ref[...].astype(BF16)
        dw_ref[...] = lax.dot_general(x_ref[...], dyv, TN_DIMS, preferred_element_type=F32)
        dx = lax.dot_general(dyv, w_ref[...], NT_DIMS, preferred_element_type=F32)

        @pl.when(j == 0)
        def _():
            dx_ref[...] = dx

        @pl.when(j > 0)
        def _():
            dx_ref[...] += dx

    return pl.pallas_call(
        body, name=name, grid=(n // tn,),
        in_specs=[pl.BlockSpec((t, d), lambda j: (0, 0)), pl.BlockSpec((t, tn), lambda j: (0, j)),
                  pl.BlockSpec((d, tn), lambda j: (0, j))],
        out_specs=[pl.BlockSpec((t, d), lambda j: (0, 0)), pl.BlockSpec((d, tn), lambda j: (0, j))],
        out_shape=[jax.ShapeDtypeStruct((t, d), F32), jax.ShapeDtypeStruct((d, n), F32)],
        compiler_params=_cparams(dimension_semantics=("arbitrary",)),
    )(xn, dy, w)


def _peer(mx, my, mc, k):
    px = 1 - mx if (k >> 2) & 1 else mx
    py = 1 - my if (k >> 1) & 1 else my
    pc = 1 - mc if k & 1 else mc
    return (px, py, pc), 4 * px + 2 * py + pc


def a2a_copies(x_refs, o_refs, send_sems, recv_sems, local_sems):
    mx, my, mc = lax.axis_index("x"), lax.axis_index("y"), lax.axis_index("c")
    me = 4 * mx + 2 * my + mc
    peers = [_peer(mx, my, mc, k) for k in range(1, N_DEV)]
    locals_, sends, recvs = [], [], []
    for a, (x_ref, o_ref) in enumerate(zip(x_refs, o_refs, strict=True)):
        locals_.append(pltpu.make_async_copy(x_ref.at[me], o_ref.at[me], local_sems.at[a]))
        for k, (dev, peer) in enumerate(peers):
            common = dict(send_sem=send_sems.at[a, k], recv_sem=recv_sems.at[a, k], device_id=dev,
                          device_id_type=pl.DeviceIdType.MESH)
            sends.append(pltpu.make_async_remote_copy(src_ref=x_ref.at[peer], dst_ref=o_ref.at[me], **common))
            recvs.append(pltpu.make_async_remote_copy(src_ref=x_ref.at[peer], dst_ref=o_ref.at[peer], **common))
    return locals_, sends, recvs


def all_to_all(xs, name):
    n = len(xs)

    def body(*refs):
        locals_, sends, recvs = a2a_copies(refs[:n], refs[n:2 * n], *refs[2 * n:])
        for cp in locals_ + sends:
            cp.start()
        for cp in recvs:
            cp.wait_recv()
        for cp in sends:
            cp.wait_send()
        for cp in locals_:
            cp.wait()

    return pl.pallas_call(
        body, name=name,
        in_specs=[pl.BlockSpec(memory_space=pl.ANY)] * n, out_specs=[pl.BlockSpec(memory_space=pl.ANY)] * n,
        out_shape=[jax.ShapeDtypeStruct(x.shape, x.dtype) for x in xs],
        scratch_shapes=[pltpu.SemaphoreType.DMA((n, N_DEV - 1)), pltpu.SemaphoreType.DMA((n, N_DEV - 1)),
                        pltpu.SemaphoreType.DMA((n,))],
    )(*xs)


def all_gather(xs, name):
    n = len(xs)
    chip_flips = (4, 2, 6)

    def body(*refs):
        x_refs, o_refs = refs[:n], refs[n:2 * n]
        send_sems, recv_sems, local_sems = refs[2 * n:]
        mx, my, mc = lax.axis_index("x"), lax.axis_index("y"), lax.axis_index("c")
        me = 4 * mx + 2 * my + mc
        sib_dev, sib = _peer(mx, my, mc, 1)

        def copy(a, k, row, to, src=None):
            return pltpu.make_async_remote_copy(
                src_ref=o_refs[a].at[row] if src is None else src, dst_ref=o_refs[a].at[row],
                send_sem=send_sems.at[a, k], recv_sem=recv_sems.at[a, k], device_id=to,
                device_id_type=pl.DeviceIdType.MESH)

        locals_, first, passed = [], [], []
        for a in range(n):
            locals_.append(pltpu.make_async_copy(x_refs[a], o_refs[a].at[me], local_sems.at[a]))
            first.append(copy(a, 0, me, sib_dev, src=x_refs[a]))
            for j, f in enumerate(chip_flips):
                first.append(copy(a, 1 + j, me, _peer(mx, my, mc, f)[0], src=x_refs[a]))
        for cp in locals_ + first:
            cp.start()
        for a in range(n):
            for j, f in enumerate(chip_flips):
                row = _peer(mx, my, mc, f)[1]
                copy(a, 1 + j, row, sib_dev).wait_recv()
                fwd = copy(a, 4 + j, row, sib_dev)
                fwd.start()
                passed.append(fwd)
        for a in range(n):
            copy(a, 0, sib, sib_dev).wait_recv()
            for j, f in enumerate(chip_flips):
                copy(a, 4 + j, _peer(mx, my, mc, f ^ 1)[1], sib_dev).wait_recv()
        for cp in first + passed:
            cp.wait_send()
        for cp in locals_:
            cp.wait()

    return pl.pallas_call(
        body, name=name,
        in_specs=[pl.BlockSpec(memory_space=pl.ANY)] * n, out_specs=[pl.BlockSpec(memory_space=pl.ANY)] * n,
        out_shape=[jax.ShapeDtypeStruct((N_DEV,) + x.shape, x.dtype) for x in xs],
        scratch_shapes=[pltpu.SemaphoreType.DMA((n, N_DEV - 1)), pltpu.SemaphoreType.DMA((n, N_DEV - 1)),
                        pltpu.SemaphoreType.DMA((n,))],
    )(*xs)


PACK_COLS = 1024
PACK_ROWS = 256
PARTS_TILE_BYTES = 4 * 1024 * 1024


def adamw_reduce(parts, w, m, v, name):
    r, c = w.shape
    fits = [t for t in (512, 352, 256, 128, 64, 32, 16, 8) if N_DEV * t * c * parts.dtype.itemsize <= PARTS_TILE_BYTES]
    tr = r if r < 2 * SUBLANE else _pick(r, fits)
    c1 = 1.0 - ADAM_B1 ** ADAM_STEP
    c2 = 1.0 - ADAM_B2 ** ADAM_STEP

    def body(p_ref, w_ref, m_ref, v_ref, g_ref, d_ref, mo_ref, vo_ref):
        g = p_ref[0].astype(F32)
        for k in range(1, N_DEV):
            g = g + p_ref[k].astype(F32)
        mn = ADAM_B1 * m_ref[...] + (1.0 - ADAM_B1) * g
        vn = ADAM_B2 * v_ref[...] + (1.0 - ADAM_B2) * (g * g)
        g_ref[...] = g
        mo_ref[...] = mn
        vo_ref[...] = vn
        d_ref[...] = -ADAM_LR * ((mn / c1) / (jnp.sqrt(vn / c2) + ADAM_EPS) + ADAM_WD * w_ref[...])

    spec = pl.BlockSpec((tr, c), lambda i: (i, 0))
    return pl.pallas_call(
        body, name=name, grid=(r // tr,),
        in_specs=[pl.BlockSpec((N_DEV, tr, c), lambda i: (0, i, 0)), spec, spec, spec],
        out_specs=[spec] * 4, out_shape=[jax.ShapeDtypeStruct((r, c), F32)] * 4,
        compiler_params=_cparams(dimension_semantics=("arbitrary",)),
    )(parts, w, m, v)


def pack_flat(arrs, dtype):
    parts = []
    for a in arrs:
        flat = a.reshape(-1).astype(dtype)
        k = -(-flat.shape[0] // PACK_COLS)
        parts.append(jnp.pad(flat, (0, k * PACK_COLS - flat.shape[0])).reshape(k, PACK_COLS))
    buf = jnp.concatenate(parts, axis=0)
    return jnp.pad(buf, ((0, -buf.shape[0] % PACK_ROWS), (0, 0)))


def unpack_flat(buf, shapes):
    out, r0 = [], 0
    for s in shapes:
        n = math.prod(s)
        k = -(-n // PACK_COLS)
        out.append(buf[r0:r0 + k].reshape(-1)[:n].reshape(tuple(s)))
        r0 += k
    return out


W_NAMES = ('meta_tokens', 'ffn1_norm', 'ffn1_w_gate', 'ffn1_w_up', 'ffn1_w_down', 'mix_norm', 'w_in', 'w_out',
           'lru_conv_w', 'lru_conv_b', 'lru_w_a', 'lru_b_a', 'lru_w_i', 'lru_b_i', 'lru_lambda', 'lru_norm',
           'gdn_conv_w', 'gdn_a_log', 'gdn_dt_bias', 'gdn_norm', 'ssd_conv_w', 'ssd_conv_b', 'ssd_a_log',
           'ssd_dt_bias', 'ssd_d', 'ssd_norm', 's5_a_re', 's5_a_im', 's5_log_dt', 's5_b_re', 's5_b_im', 's5_c_re',
           's5_c_im', 's5_d', 's5_w_glu', 's5_norm', 'ffn2_norm', 'ffn2_w_gate', 'ffn2_w_up', 'ffn2_w_down',
           'final_norm')
SHARD_AXIS = {'meta_tokens': 1, 'ffn1_w_gate': 2, 'ffn1_w_up': 2, 'ffn1_w_down': 1, 'w_in': 2, 'w_out': 1,
              'lru_conv_w': 2, 'gdn_conv_w': 2, 'ssd_conv_w': 2, 's5_w_glu': 1, 'ffn2_w_gate': 2, 'ffn2_w_up': 2,
              'ffn2_w_down': 1}
BIG_NAMES = ('ffn1_w_gate', 'ffn1_w_up', 'ffn1_w_down', 'w_in', 'w_out', 's5_w_glu', 'ffn2_w_gate', 'ffn2_w_up',
             'ffn2_w_down')
SHARD_NAMES = tuple(n for n in W_NAMES if n in SHARD_AXIS)
REP_NAMES = tuple(n for n in W_NAMES if n not in SHARD_AXIS)
SSD_GROUPS = 2
S5_CH = 16


def unshard(g, axis):
    return jnp.concatenate([g[p] for p in range(N_DEV)], axis=axis)


def kernel(*args):
    n_w = len(W_NAMES)
    x = args[0]
    w = dict(zip(W_NAMES, args[1:1 + n_w]))
    target = args[1 + n_w]
    m_in = dict(zip(W_NAMES, args[2 + n_w:2 + 2 * n_w]))
    v_in = dict(zip(W_NAMES, args[2 + 2 * n_w:2 + 3 * n_w]))

    depth, d = w['ffn1_norm'].shape
    seq = x.shape[1]
    n_meta = w['meta_tokens'].shape[0]
    pad = CHUNK - n_meta
    tp = pad + n_meta + seq
    wg = d // 2
    xbc_w = w['ssd_conv_w'].shape[-1] * N_DEV
    gdn_hd = w['gdn_norm'].shape[-1]
    gdn_h = wg // gdn_hd
    ssd_h = w['ssd_a_log'].shape[-1]
    lru_h = w['lru_w_a'].shape[1]
    s5_g, s5_n = w['s5_a_re'].shape[1:]
    s5_q = wg // S5_LANES
    row_tile = _pick(tp, (192, 96, 64))

    gathered = all_gather([w[n].astype(BF16 if n in BIG_NAMES else F32) for n in SHARD_NAMES], "gather_weights")
    full = {n: unshard(g, SHARD_AXIS[n]) for n, g in zip(SHARD_NAMES, gathered)}

    segs = [('a_x', wg), ('a_gate', wg), ('b_q', wg), ('b_k', wg), ('b_v', wg), ('b_z', wg), ('c_xbc', xbc_w),
            ('c_z', wg), ('d_u', wg), ('small_b', LANE), ('small_c', LANE)]
    off, o = {}, 0
    for nme, wd_ in segs:
        assert o % wd_ == 0, (nme, o, wd_)
        off[nme] = o
        o += wd_
    o_beta = 6 * wg
    o_cz = o_beta + 2 * gdn_h
    o_xbc = o_cz + wg
    o_dt = o_xbc + xbc_w
    o_du = o_dt + ssd_h

    def pack_cols(a):
        z = lambda k: jnp.zeros(a.shape[:-1] + (k,), a.dtype)
        return jnp.concatenate([a[..., :o_beta], a[..., o_xbc:o_dt], a[..., o_cz:o_xbc], a[..., o_du:],
                                a[..., o_beta:o_cz], z(LANE - 2 * gdn_h), a[..., o_dt:o_du], z(LANE - ssd_h)], axis=-1)

    def unpack_cols(a):
        sb, sc = off['small_b'], off['small_c']
        return jnp.concatenate([a[..., :o_beta], a[..., sb:sb + 2 * gdn_h], a[..., off['c_z']:off['c_z'] + wg],
                                a[..., off['c_xbc']:off['c_xbc'] + xbc_w], a[..., sc:sc + ssd_h],
                                a[..., off['d_u']:off['d_u'] + wg]], axis=-1)

    w_in_p = pack_cols(full['w_in'])

    def col(name, width):
        return off[name] // width

    def row(a):
        return a.reshape(1, -1)

    def layer_params(l):
        gcw = full['gdn_conv_w'][l]
        lru = [full['lru_conv_w'][l], row(w['lru_conv_b'][l]), blockdiag_expand(w['lru_w_a'][l]), row(w['lru_b_a'][l]),
               blockdiag_expand(w['lru_w_i'][l]), row(w['lru_b_i'][l]), row(w['lru_lambda'][l]), row(w['lru_norm'][l])]
        gdn = [gcw[:, :wg], gcw[:, wg:2 * wg], gcw[:, 2 * wg:], row(w['gdn_a_log'][l]), row(w['gdn_dt_bias'][l]),
               row(w['gdn_norm'][l])]
        ssd = [full['ssd_conv_w'][l], row(w['ssd_conv_b'][l]), row(w['ssd_a_log'][l]), row(w['ssd_dt_bias'][l]),
               row(w['ssd_d'][l]), row(w['ssd_norm'][l])]
        s5 = list(s5_params_expand(*[w[n][l] for n in ('s5_a_re', 's5_a_im', 's5_log_dt', 's5_b_re', 's5_b_im',
                                                          's5_c_re', 's5_c_im', 's5_d')]))
        post = [full['s5_w_glu'][l], row(w['s5_norm'][l])]
        return lru, gdn, ssd, s5, post

    lru_fn = functools.partial(lru_chunk, pad)
    gdn_fn = functools.partial(gdn_multi, pad, CHUNK)
    ssd_fn = functools.partial(ssd_multi, pad, 0, SSD_GROUPS, CHUNK)
    s5_fn = functools.partial(s5_chunk, pad)
    post_fn = functools.partial(s5_post, pad)
    n_state_lanes = (S5_LANES // S5_CH) * s5_n

    def mixer_specs(proj):
        lru_seqs = [(proj, wg, col('a_x', wg)), (proj, wg, col('a_gate', wg))]
        gdn_seqs = [(proj, wg, col('b_q', wg)), (proj, wg, col('b_k', wg)), (proj, wg, col('b_v', wg)),
                    (proj, wg, col('b_z', wg)), (proj, LANE, col('small_b', LANE))]
        ssd_seqs = [(proj, wg, col('c_z', wg)), (proj, xbc_w, col('c_xbc', xbc_w)), (proj, LANE, col('small_c', LANE))]
        base = col('d_u', S5_LANES)
        s5_seqs = [(proj, S5_LANES, lambda g: base + g)]
        return lru_seqs, gdn_seqs, ssd_seqs, s5_seqs

    lru_carry = [(CONV_TAIL, wg), (1, wg)]
    gdn_carry = [(CONV_TAIL, wg)] * 3 + [(wg, gdn_hd)]
    ssd_carry = [(CONV_TAIL, xbc_w), (wg, (xbc_w - wg) // (2 * SSD_GROUPS))]
    s5_carry = [(1, n_state_lanes)] * 2
    rk = dict(n_steps=tp // row_tile, rows=row_tile)
    mk = rk
    out_w = (wg, wg, 0, F32)

    h = jnp.concatenate([jnp.zeros((pad, d), F32), full['meta_tokens'], x[0]], axis=0)
    target_p = jnp.concatenate([jnp.zeros((pad + n_meta, d), F32), target[0]], axis=0)
    saved = []
    for l in range(depth):
        lru_p, gdn_p, ssd_p, s5_p, post_p = layer_params(l)
        h0 = h
        xn1 = rms_fwd(h0, row(w['ffn1_norm'][l]), "rms_fwd", row_tile)
        h1 = ffn_fwd(h0, xn1, full['ffn1_w_gate'][l], full['ffn1_w_up'][l], full['ffn1_w_down'][l], "ffn_fwd")
        xn2 = rms_fwd(h1, row(w['mix_norm'][l]), "rms_fwd", row_tile)
        proj = matmul_cols(xn2, w_in_p[l], "mix_in_fwd")
        lru_s, gdn_s, ssd_s, s5_s = mixer_specs(proj)
        ya, lru_c = mixer_fwd(lru_fn, "lru_fwd", seqs=lru_s, params=lru_p, out=out_w, carry=lru_carry, **mk)
        yb, gdn_c = mixer_fwd(gdn_fn, "gdn_fwd", seqs=gdn_s, params=gdn_p, out=out_w, carry=gdn_carry, **mk)
        yc, ssd_c = mixer_fwd(ssd_fn, "ssd_fwd", seqs=ssd_s, params=ssd_p, out=out_w, carry=ssd_carry, **mk)
        y1, s5_c = mixer_fwd(s5_fn, "s5_fwd", seqs=s5_s, params=s5_p, out=(wg, S5_LANES, lambda g: g, F32),
                             carry=s5_carry, n_groups=s5_q, **mk)
        yd, _ = mixer_fwd(post_fn, "s5_post_fwd", seqs=[(y1, wg, 0)], params=post_p, out=out_w, carry=[], **rk)
        h2 = mix_out_fwd(h1, [ya, yb, yc, yd], full['w_out'][l], "mix_out_fwd", row_tile)
        xn3 = rms_fwd(h2, row(w['ffn2_norm'][l]), "rms_fwd", row_tile)
        h3 = ffn_fwd(h2, xn3, full['ffn2_w_gate'][l], full['ffn2_w_up'][l], full['ffn2_w_down'][l], "ffn_fwd")
        saved.append((h0, xn1, h1, xn2, proj, (ya, yb, yc, yd), y1, (lru_c, gdn_c, ssd_c, s5_c), h2, xn3))
        h = h3

    loss_part, dh, d_final = loss_and_grad(h, target_p, row(w['final_norm']), "loss", row_tile, pad + n_meta)
    loss = lax.psum(loss_part, ("x", "y", "c"))

    def shards_of(a, axis):
        sh = a.shape
        return jnp.moveaxis(a.reshape(sh[:axis] + (N_DEV, sh[axis] // N_DEV) + sh[axis + 1:]), axis, 0)

    def to_send(n, g):
        return shards_of(g, SHARD_AXIS[n] - 1).astype(BF16)

    received = {n: [None] * depth for n in SHARD_NAMES if n != 'meta_tokens'}
    pending = []

    def hosted(names_layers):
        keys = [k for k in pending if (k[0], k[1]) in names_layers]
        for k in keys:
            pending.remove(k)
        return [(k[0], k[1]) for k in keys], [k[2] for k in keys]

    def store(keys, arrays):
        for (n, l), a in zip(keys, arrays, strict=True):
            received[n][l] = a

    gw = {n: [None] * depth for n in W_NAMES if n not in ('meta_tokens', 'final_norm')}
    for l in reversed(range(depth)):
        lru_p, gdn_p, ssd_p, s5_p, post_p = layer_params(l)
        h0, xn1, h1, xn2, proj, ys, y1, (lru_c, gdn_c, ssd_c, s5_c), h2, xn3 = saved[l]
        dxn, gw['ffn2_w_gate'][l], gw['ffn2_w_up'][l], gw['ffn2_w_down'][l] = ffn_bwd(
            xn3, dh, full['ffn2_w_gate'][l], full['ffn2_w_up'][l], full['ffn2_w_down'][l], "ffn_bwd")
        pending += [(n, l, to_send(n, gw[n][l])) for n in ('ffn2_w_gate', 'ffn2_w_up', 'ffn2_w_down')]
        dh, dg = rms_bwd_add(h2, dxn, dh, row(w['ffn2_norm'][l]), "rms_bwd", row_tile)
        gw['ffn2_norm'][l] = dg[0]

        dys, (d_wout,) = mixer_bwd(mix_out_delta, "mix_out_bwd", seqs=[(y, wg, 0) for y in ys],
                                   params=[full['w_out'][l]], dout=(dh, d, 0), saved=[], carry=[], **rk)
        gw['w_out'][l] = d_wout[0]
        pending.append(('w_out', l, to_send('w_out', gw['w_out'][l])))
        lru_s, gdn_s, ssd_s, s5_s = mixer_specs(proj)
        (dy1,), d_post = mixer_bwd(post_fn, "s5_post_bwd", seqs=[(y1, wg, 0)], params=post_p, dout=(dys[3], wg, 0),
                                   saved=[], carry=[], **rk)
        gw['s5_w_glu'][l], gw['s5_norm'][l] = d_post[0][0], d_post[1][0, 0]
        pending.append(('s5_w_glu', l, to_send('s5_w_glu', gw['s5_w_glu'][l])))
        keys, arrs = hosted({('ffn1_w_gate', l + 1), ('ffn1_w_up', l + 1), ('ffn1_w_down', l + 1)})
        (d_du,), d_s5, *got = mixer_bwd(s5_fn, "s5_bwd", seqs=s5_s, params=s5_p, dout=(dy1, S5_LANES, lambda g: g),
                                        saved=s5_c, carry=s5_carry, n_groups=s5_q, a2a=arrs, **mk)
        store(keys, got[0] if got else [])
        for n, g in zip(('s5_a_re', 's5_a_im', 's5_log_dt', 's5_b_re', 's5_b_im', 's5_c_re', 's5_c_im', 's5_d'),
                        s5_grads_extract(d_s5, s5_g, s5_n, S5_CH)):
            gw[n][l] = g
        keys, arrs = hosted({('w_in', l + 1), ('w_out', l), ('s5_w_glu', l)})
        (d_cz, d_cxbc, d_sc), d_ssd, *got = mixer_bwd(ssd_fn, "ssd_bwd", seqs=ssd_s, params=ssd_p,
                                                      dout=(dys[2], wg, 0), saved=ssd_c, carry=ssd_carry, a2a=arrs, **mk)
        store(keys, got[0] if got else [])
        for n, g in zip(('ssd_conv_w', 'ssd_conv_b', 'ssd_a_log', 'ssd_dt_bias', 'ssd_d', 'ssd_norm'), d_ssd):
            gw[n][l] = g[0] if n == 'ssd_conv_w' else g[0, 0]
        keys, arrs = hosted({('ffn2_w_gate', l), ('ffn2_w_up', l), ('ffn2_w_down', l)})
        (d_bq, d_bk, d_bv, d_bz, d_sb), d_gdn, *got = mixer_bwd(gdn_fn, "gdn_bwd", seqs=gdn_s, params=gdn_p,
                                                                dout=(dys[1], wg, 0), saved=gdn_c, carry=gdn_carry,
                                                                a2a=arrs, **mk)
        store(keys, got[0] if got else [])
        gw['gdn_conv_w'][l] = jnp.concatenate([d_gdn[0][0], d_gdn[1][0], d_gdn[2][0]], axis=1)
        gw['gdn_a_log'][l], gw['gdn_dt_bias'][l], gw['gdn_norm'][l] = d_gdn[3][0, 0], d_gdn[4][0, 0], d_gdn[5][0, 0]
        (d_ax, d_ag), d_lru = mixer_bwd(lru_fn, "lru_bwd", seqs=lru_s, params=lru_p, dout=(dys[0], wg, 0),
                                        saved=lru_c, carry=lru_carry, **mk)
        gw['lru_conv_w'][l], gw['lru_conv_b'][l] = d_lru[0][0], d_lru[1][0, 0]
        gw['lru_w_a'][l], gw['lru_b_a'][l] = blockdiag_extract(d_lru[2][0], lru_h), d_lru[3][0, 0]
        gw['lru_w_i'][l], gw['lru_b_i'][l] = blockdiag_extract(d_lru[4][0], lru_h), d_lru[5][0, 0]
        gw['lru_lambda'][l], gw['lru_norm'][l] = d_lru[6][0, 0], d_lru[7][0, 0]

        dproj = jnp.concatenate([d_ax, d_ag, d_bq, d_bk, d_bv, d_bz, d_cxbc, d_cz, d_du, d_sb, d_sc], axis=1)
        dxn, d_win_p = matmul_cols_bwd(xn2, dproj, w_in_p[l], "mix_in_bwd")
        gw['w_in'][l] = unpack_cols(d_win_p)
        pending.append(('w_in', l, to_send('w_in', gw['w_in'][l])))
        dh, dg = rms_bwd_add(h1, dxn, dh, row(w['mix_norm'][l]), "rms_bwd", row_tile)
        gw['mix_norm'][l] = dg[0]

        dxn, gw['ffn1_w_gate'][l], gw['ffn1_w_up'][l], gw['ffn1_w_down'][l] = ffn_bwd(
            xn1, dh, full['ffn1_w_gate'][l], full['ffn1_w_up'][l], full['ffn1_w_down'][l], "ffn_bwd")
        dh, dg = rms_bwd_add(h0, dxn, dh, row(w['ffn1_norm'][l]), "rms_bwd", row_tile)
        gw['ffn1_norm'][l] = dg[0]
        pending += [(n, l, to_send(n, gw[n][l])) for n in ('ffn1_w_gate', 'ffn1_w_up', 'ffn1_w_down')]

    grad_x = dh[pad + n_meta:][None]
    grads = {n: jnp.stack(g, axis=0) for n, g in gw.items()}
    grads['meta_tokens'] = dh[pad:pad + n_meta]
    grads['final_norm'] = d_final[0]

    for n in ('lru_conv_w', 'gdn_conv_w', 'ssd_conv_w'):
        pending += [(n, l, to_send(n, gw[n][l])) for l in range(depth)]
    last = [k[2] for k in pending] + [shards_of(grads['meta_tokens'], SHARD_AXIS['meta_tokens']).astype(BF16)]
    *got, recv_meta = all_to_all(last, "scatter_grads")
    store([(k[0], k[1]) for k in pending], got)
    recv_sh = [recv_meta if n == 'meta_tokens' else jnp.stack(received[n], axis=1) for n in SHARD_NAMES]
    (recv_rep,) = all_gather([pack_flat([grads[n] for n in REP_NAMES], F32)], "gather_rep_grads")
    out = {}
    for n, recv in zip(SHARD_NAMES, recv_sh):
        c = w[n].shape[-1]
        res = adamw_reduce(recv.reshape(N_DEV, -1, c), w[n].reshape(-1, c), m_in[n].reshape(-1, c),
                           v_in[n].reshape(-1, c), "adamw_" + n)
        for kind, buf in zip(('grad', 'delta', 'new_m', 'new_v'), res):
            out[kind, n] = buf.reshape(w[n].shape)
    res = adamw_reduce(recv_rep, pack_flat([w[n] for n in REP_NAMES], F32), pack_flat([m_in[n] for n in REP_NAMES], F32),
                       pack_flat([v_in[n] for n in REP_NAMES], F32), "adamw_replicated")
    shapes = [w[n].shape for n in REP_NAMES]
    for kind, buf in zip(('grad', 'delta', 'new_m', 'new_v'), res):
        for n, a in zip(REP_NAMES, unpack_flat(buf, shapes)):
            out[kind, n] = a
    return (loss, grad_x) + tuple(out[k, n] for k in ('grad', 'delta', 'new_m', 'new_v') for n in W_NAMES)
```

```python
import functools
import math

import jax
import jax.numpy as jnp
from jax import lax
from jax.experimental import pallas as pl
from jax.experimental.pallas import tpu as pltpu

F32 = jnp.float32
BF16 = jnp.bfloat16

EPS = 1e-6
CHUNK = 64
CONV_K = 4
CONV_TAIL = 8
LRU_C = 8.0
LANE = 128
SUBLANE = 8
N_DEV = 8
NEG_BIG = -1e30

ADAM_LR = 0.001
ADAM_B1 = 0.9
ADAM_B2 = 0.999
ADAM_EPS = 1e-08
ADAM_WD = 0.01
ADAM_STEP = 10

VMEM_LIMIT = 56 * 1024 * 1024


def _dg(a, b, dims):
    return lax.dot_general(a.astype(BF16), b.astype(BF16), (dims, ((), ())), preferred_element_type=F32)


@jax.custom_vjp
def bdot(a, b):
    return _dg(a, b, ((1,), (0,)))


@jax.custom_vjp
def bdot_nt(a, b):
    return _dg(a, b, ((1,), (1,)))


@jax.custom_vjp
def bdot_tn(a, b):
    return _dg(a, b, ((0,), (0,)))


bdot.defvjp(lambda a, b: (bdot(a, b), (a, b)),
            lambda r, g: (bdot_nt(g, r[1]).astype(r[0].dtype), bdot_tn(r[0], g).astype(r[1].dtype)))
bdot_nt.defvjp(lambda a, b: (bdot_nt(a, b), (a, b)),
               lambda r, g: (bdot(g, r[1]).astype(r[0].dtype), bdot_tn(g, r[0]).astype(r[1].dtype)))
bdot_tn.defvjp(lambda a, b: (bdot_tn(a, b), (a, b)),
               lambda r, g: (bdot_nt(r[1], g).astype(r[0].dtype), bdot(r[0], g).astype(r[1].dtype)))


def _split_bf16(a):
    hi = a.astype(BF16)
    return hi, (a - hi.astype(F32)).astype(BF16)


def _dot3(a, b, dims):
    (ah, al), (bh, bl) = _split_bf16(a), _split_bf16(b)
    d = lambda x, y: lax.dot_general(x, y, (dims, ((), ())), preferred_element_type=F32)
    return d(ah, bh) + (d(ah, bl) + d(al, bh))


@jax.custom_vjp
def hdot(a, b):
    return _dot3(a, b, ((1,), (0,)))


@jax.custom_vjp
def hdot_tn(a, b):
    return _dot3(a, b, ((0,), (0,)))


hdot.defvjp(lambda a, b: (hdot(a, b), (a, b)),
            lambda r, g: (_dot3(g, r[1], ((1,), (1,))), _dot3(r[0], g, ((0,), (0,)))))
hdot_tn.defvjp(lambda a, b: (hdot_tn(a, b), (a, b)),
               lambda r, g: (_dot3(r[1], g, ((1,), (1,))), _dot3(r[0], g, ((1,), (0,)))))


def rms_norm(x, g):
    return x * lax.rsqrt(jnp.mean(x * x, axis=-1, keepdims=True) + EPS) * g


def row_mask(row0, rows, pad):
    r = row0 + lax.broadcasted_iota(jnp.int32, (rows, 1), 0)
    return (r >= pad).astype(F32)


def conv4(tail, u, w):
    rows = u.shape[0]
    xe = jnp.concatenate([tail, u], axis=0)
    y = w[0:1] * xe[CONV_TAIL - 3:CONV_TAIL - 3 + rows]
    for k in range(1, CONV_K):
        y = y + w[k:k + 1] * xe[CONV_TAIL - 3 + k:CONV_TAIL - 3 + k + rows]
    return y


def shift_rows(x, s, fill):
    rows = x.shape[0]
    return jnp.concatenate([jnp.full((s, x.shape[1]), fill, x.dtype), x[:rows - s]], axis=0)


def lin_scan(a, b):
    rows = a.shape[0]
    s = 1
    while s < rows:
        b = a * shift_rows(b, s, 0.0) + b
        a = a * shift_rows(a, s, 1.0)
        s *= 2
    return b


def cscan_const(ar, ai, br, bi):
    rows = br.shape[0]
    s = 1
    while s < rows:
        brs, bis = shift_rows(br, s, 0.0), shift_rows(bi, s, 0.0)
        br, bi = br + ar * brs - ai * bis, bi + ar * bis + ai * brs
        ar, ai = ar * ar - ai * ai, 2.0 * ar * ai
        s *= 2
    return br, bi


def neg_expm1(z):
    t = jnp.tanh(0.5 * z)
    return -2.0 * t / (1.0 - t)


def tri_masks(n):
    r = lax.broadcasted_iota(jnp.int32, (n, n), 0)
    c = lax.broadcasted_iota(jnp.int32, (n, n), 1)
    return r >= c, r > c, (r == c).astype(F32)


def lru_chunk(pad, row0, params, seqs, carry):
    conv_w, conv_b, w_a, b_a, w_i, b_i, lam, norm_g = params
    u_x, u_gate = seqs
    tail, h0 = carry
    rows = u_x.shape[0]
    m = row_mask(row0, rows, pad)
    xc = conv4(tail, u_x, conv_w) + conv_b
    r = jax.nn.sigmoid(bdot(xc, w_a) + b_a)
    ig = jax.nn.sigmoid(bdot(xc, w_i) + b_i)
    log_a = -LRU_C * r * jax.nn.softplus(-lam)
    a = jnp.exp(log_a)
    b = jnp.sqrt(neg_expm1(2.0 * log_a)) * (ig * xc) * m
    first = (lax.broadcasted_iota(jnp.int32, (rows, 1), 0) == 0).astype(F32)
    b = b + first * (a * h0)
    h = lin_scan(a, b)
    y = jax.nn.gelu(u_gate) * h
    out = rms_norm(y, norm_g) * m
    return (out,), (u_x[rows - CONV_TAIL:], h[rows - 1:])


def gdn_multi(pad, sub, row0, params, seqs, carry):
    wq, wk, wv, a_log, dt_bias, norm_g = params
    u_q, u_k, u_v, u_z, small = seqs
    tq, tk, tv, state = carry
    rows = u_q.shape[0]
    hd = norm_g.shape[1]
    nh = u_q.shape[1] // hd
    nc = rows // sub
    m = row_mask(row0, rows, pad)
    incl, strict, eye = tri_masks(sub)
    tril = incl.astype(F32)
    triu = (lax.broadcasted_iota(jnp.int32, (sub, sub), 0) <= lax.broadcasted_iota(jnp.int32, (sub, sub), 1)).astype(F32)
    qc = jax.nn.silu(conv4(tq, u_q, wq))
    kc = jax.nn.silu(conv4(tk, u_k, wk))
    vc = jax.nn.silu(conv4(tv, u_v, wv))
    beta = jax.nn.sigmoid(small[:, :nh]) * m
    g = -jnp.exp(a_log) * jax.nn.softplus(small[:, nh:2 * nh] + dt_bias) * m
    gate = jax.nn.silu(u_z)
    heads = [slice(h * hd, (h + 1) * hd) for h in range(nh)]
    q_h = [qc[:, sl] for sl in heads]
    k_h = [kc[:, sl] for sl in heads]
    q_h = [q * lax.rsqrt(jnp.sum(q * q, axis=-1, keepdims=True) + EPS) * (hd ** -0.5) * m for q in q_h]
    k_h = [k * lax.rsqrt(jnp.sum(k * k, axis=-1, keepdims=True) + EPS) * m for k in k_h]
    v_h = [vc[:, sl] * m for sl in heads]
    pairs = [(c, h) for c in range(nc) for h in range(nh)]
    cs = lambda x, c: x[c * sub:(c + 1) * sub]
    q = {(c, h): cs(q_h[h], c) for c, h in pairs}
    k = {(c, h): cs(k_h[h], c) for c, h in pairs}
    v = {(c, h): cs(v_h[h], c) for c, h in pairs}
    bt = {(c, h): cs(beta, c)[:, h:h + 1] for c, h in pairs}
    gcs = [hdot(tril, cs(g, c)) for c in range(nc)]
    gts = [hdot_tn(cs(g, c), triu) for c in range(nc)]
    gc = {(c, h): gcs[c][:, h:h + 1] for c, h in pairs}
    decay = {(c, h): jnp.exp(jnp.where(incl, gc[c, h] - gts[c][h:h + 1], NEG_BIG)) for c, h in pairs}
    kb = {p: k[p] * bt[p] for p in pairs}
    kk = {p: bdot_nt(kb[p], k[p]) for p in pairs}
    lmat = {p: jnp.where(strict, kk[p] * decay[p], 0.0) for p in pairs}
    pm = {p: eye - lmat[p] for p in pairs}
    mm = {p: hdot(lmat[p], lmat[p]) for p in pairs}
    s = 2
    while s < sub:
        pm = {p: pm[p] + hdot(pm[p], mm[p]) for p in pairs}
        s *= 2
        if s < sub:
            mm = {p: hdot(mm[p], mm[p]) for p in pairs}
    eg = {p: jnp.exp(gc[p]) for p in pairs}
    u = {p: hdot(pm[p], v[p] * bt[p]) for p in pairs}
    w = {p: hdot(pm[p], kb[p] * eg[p]) for p in pairs}
    attn = {p: bdot_nt(q[p], k[p]) * decay[p] for p in pairs}
    qd = {p: q[p] * eg[p] for p in pairs}
    g_last = {p: gc[p][sub - 1:] for p in pairs}
    kd = {p: k[p] * jnp.exp(g_last[p] - gc[p]) for p in pairs}
    last = {p: jnp.exp(g_last[p]) for p in pairs}
    s_h = [state[sl] for sl in heads]
    o = {}
    for c in range(nc):
        ws = [bdot(w[c, h], s_h[h]) for h in range(nh)]
        qs = [bdot(qd[c, h], s_h[h]) for h in range(nh)]
        v_new = [u[c, h] - ws[h] for h in range(nh)]
        av = [bdot(attn[c, h], v_new[h]) for h in range(nh)]
        kv = [bdot_tn(kd[c, h], v_new[h]) for h in range(nh)]
        for h in range(nh):
            o[c, h] = qs[h] + av[h]
        s_h = [s_h[h] * last[c, h] + kv[h] for h in range(nh)]
    out = jnp.concatenate([jnp.concatenate([rms_norm(o[c, h], norm_g) for h in range(nh)], axis=1)
                           for c in range(nc)], axis=0) * gate * m
    t0 = rows - CONV_TAIL
    return (out,), (u_q[t0:], u_k[t0:], u_v[t0:], jnp.concatenate(s_h, axis=0))


def ssd_multi(pad, dt_lane0, n_groups, sub, row0, params, seqs, carry):
    conv_w, conv_b, a_log, dt_bias, d_skip, norm_g = params
    u_z, u_xbc, small = seqs
    tail, state = carry
    rows = u_z.shape[0]
    width = u_z.shape[1]
    nh = a_log.shape[1]
    hd = width // nh
    ns = (u_xbc.shape[1] - width) // (2 * n_groups)
    hpg = nh // n_groups
    nc = rows // sub
    m = row_mask(row0, rows, pad)
    incl, _, _ = tri_masks(sub)
    tril = incl.astype(F32)
    triu = (lax.broadcasted_iota(jnp.int32, (sub, sub), 0) <= lax.broadcasted_iota(jnp.int32, (sub, sub), 1)).astype(F32)
    xbc = jax.nn.silu(conv4(tail, u_xbc, conv_w) + conv_b)
    xs = xbc[:, :width]
    dt = jax.nn.softplus(small[:, dt_lane0:dt_lane0 + nh] + dt_bias)
    a_all = dt * (-jnp.exp(a_log)) * m
    cs = lambda x, c: x[c * sub:(c + 1) * sub]
    heads = [slice(h * hd, (h + 1) * hd) for h in range(nh)]
    pairs = [(c, h) for c in range(nc) for h in range(nh)]
    grp = lambda h: h // hpg
    bm = {(c, g): cs(xbc[:, width + g * ns: width + (g + 1) * ns] * m, c) for c in range(nc) for g in range(n_groups)}
    cm = {(c, g): cs(xbc[:, width + (n_groups + g) * ns: width + (n_groups + g + 1) * ns] * m, c)
          for c in range(nc) for g in range(n_groups)}
    xh = {(c, h): cs(xs[:, heads[h]], c) for c, h in pairs}
    xdt = {(c, h): xh[c, h] * cs(dt[:, h:h + 1] * m, c) for c, h in pairs}
    acums = [hdot(tril, cs(a_all, c)) for c in range(nc)]
    acts = [hdot_tn(cs(a_all, c), triu) for c in range(nc)]
    acum = {(c, h): acums[c][:, h:h + 1] for c, h in pairs}
    a_last = {p: acum[p][sub - 1:] for p in pairs}
    lmat = {(c, h): jnp.exp(jnp.where(incl, acum[c, h] - acts[c][h:h + 1], NEG_BIG)) for c, h in pairs}
    cb = {cg: bdot_nt(cm[cg], bm[cg]) for cg in bm}
    y_diag = {(c, h): bdot(cb[c, grp(h)] * lmat[c, h], xdt[c, h]) for c, h in pairs}
    st = {(c, h): bdot_tn(xdt[c, h] * jnp.exp(a_last[c, h] - acum[c, h]), bm[c, grp(h)]) for c, h in pairs}
    e_in = {p: jnp.exp(acum[p]) for p in pairs}
    e_out = {p: jnp.exp(a_last[p]) for p in pairs}
    s_h = [state[sl] for sl in heads]
    y = {}
    for c in range(nc):
        off = [bdot_nt(cm[c, grp(h)], s_h[h]) for h in range(nh)]
        for h in range(nh):
            y[c, h] = y_diag[c, h] + off[h] * e_in[c, h] + d_skip[:, h:h + 1] * xh[c, h]
        s_h = [s_h[h] * e_out[c, h] + st[c, h] for h in range(nh)]
    yy = jnp.concatenate([jnp.concatenate([y[c, h] for h in range(nh)], axis=1) for c in range(nc)], axis=0)
    yy = yy * jax.nn.silu(u_z)
    gw = width // n_groups
    outs = [rms_norm(yy[:, g * gw:(g + 1) * gw], norm_g[:, g * gw:(g + 1) * gw]) for g in range(n_groups)]
    out = jnp.concatenate(outs, axis=1) * m
    return (out,), (u_xbc[rows - CONV_TAIL:], jnp.concatenate(s_h, axis=0))


def s5_chunk(pad, row0, params, seqs, carry):
    a_re, a_im, log_dt, b_re, b_im, c_re, c_im, d_skip = params
    (u,) = seqs
    s_re0, s_im0 = carry
    rows = u.shape[0]
    n_state = a_re.shape[1]
    n_grp = log_dt.shape[1]
    per = n_state // n_grp
    expand = (lax.broadcasted_iota(jnp.int32, (n_grp, n_state), 1) // per
              == lax.broadcasted_iota(jnp.int32, (n_grp, n_state), 0)).astype(F32)
    dt = jnp.exp(hdot(log_dt, expand))
    lam_re = jnp.minimum(a_re, -1e-4)
    lam_im = a_im
    mag = jnp.exp(dt * lam_re)
    ab_re = mag * jnp.cos(dt * lam_im)
    ab_im = mag * jnp.sin(dt * lam_im)
    den = lam_re * lam_re + lam_im * lam_im
    f_re = ((ab_re - 1.0) * lam_re + ab_im * lam_im) / den
    f_im = (ab_im * lam_re - (ab_re - 1.0) * lam_im) / den
    bb_re = f_re * b_re - f_im * b_im
    bb_im = f_re * b_im + f_im * b_re
    bu_re = bdot(u, bb_re)
    bu_im = bdot(u, bb_im)
    first = (lax.broadcasted_iota(jnp.int32, (rows, 1), 0) == 0).astype(F32)
    bu_re = bu_re + first * (ab_re * s_re0 - ab_im * s_im0)
    bu_im = bu_im + first * (ab_re * s_im0 + ab_im * s_re0)
    s_re, s_im = cscan_const(ab_re, ab_im, bu_re, bu_im)
    y = bdot(s_re, c_re) - bdot(s_im, c_im) + d_skip * u
    return (y,), (s_re[rows - 1:], s_im[rows - 1:])


def s5_post(pad, row0, params, seqs, carry):
    w_glu, norm_g = params
    (y,) = seqs
    y = jax.nn.gelu(y)
    y = y * jax.nn.sigmoid(bdot(y, w_glu))
    return (rms_norm(y, norm_g),), ()


def mix_out_delta(row0, params, seqs, carry):
    (w_out,) = params
    wd = seqs[0].shape[1]
    acc = bdot(seqs[0], w_out[0:wd])
    for k in range(1, len(seqs)):
        acc = acc + bdot(seqs[k], w_out[k * wd:(k + 1) * wd])
    return (acc,), ()


def blockdiag_expand(w):
    nh, a, b = w.shape
    eye = jnp.eye(nh, dtype=w.dtype)
    return (w[:, :, None, :] * eye[:, None, :, None]).reshape(nh * a, nh * b)


def blockdiag_extract(m, nh):
    a, b = m.shape[0] // nh, m.shape[1] // nh
    on_diag = jnp.eye(nh, dtype=bool)[:, None, :, None]
    return jnp.sum(jnp.where(on_diag, m.reshape(nh, a, nh, b), 0.0), axis=2)


S5_LANES = LANE


def s5_params_expand(a_re, a_im, log_dt, b_re, b_im, c_re, c_im, d_skip):
    n_grp, n_state = a_re.shape
    ch = b_re.shape[-1]
    gpl = S5_LANES // ch
    nq = n_grp // gpl
    eye = jnp.eye(gpl, dtype=F32)[None, :, None, :, None]

    def bexp(b):
        bt = jnp.swapaxes(b, 1, 2).reshape(nq, gpl, b.shape[2], 1, b.shape[1])
        return (bt * eye).reshape(nq, gpl * b.shape[2], gpl * b.shape[1])

    return (a_re.reshape(nq, 1, gpl * n_state), a_im.reshape(nq, 1, gpl * n_state), log_dt.reshape(nq, 1, gpl),
            bexp(b_re), bexp(b_im), bexp(c_re), bexp(c_im), d_skip.reshape(nq, 1, S5_LANES))


def s5_grads_extract(grads, n_grp, n_state, ch):
    da_re, da_im, dlog_dt, db_re, db_im, dc_re, dc_im, dd = grads
    gpl = S5_LANES // ch
    nq = n_grp // gpl
    on_diag = jnp.eye(gpl, dtype=bool)[None, :, None, :, None]

    def bext(b):
        r, c = b.shape[1] // gpl, b.shape[2] // gpl
        d = jnp.sum(jnp.where(on_diag, b.reshape(nq, gpl, r, gpl, c), 0.0), axis=3)
        return jnp.swapaxes(d.reshape(n_grp, r, c), 1, 2)

    return (da_re.reshape(n_grp, n_state), da_im.reshape(n_grp, n_state), dlog_dt.reshape(n_grp),
            bext(db_re), bext(db_im), bext(dc_re), bext(dc_im), dd.reshape(n_grp * ch))


def _cparams(**kw):
    return pltpu.CompilerParams(vmem_limit_bytes=VMEM_LIMIT, **kw)


def tiled_call(body_fn, name, *, n_steps, rows, n_groups=1, reverse=False,
               seq_in=(), whole_in=(), step_in=(), seq_out=(), acc_out=(), step_out=(), carry=(), a2a=()):
    def step_of(i):
        return (n_steps - 1 - i) if reverse else i

    def col_of(col, g):
        return col(g) if callable(col) else col

    in_specs, operands = [], []
    for arr, width, col in seq_in:
        in_specs.append(pl.BlockSpec((rows, width), lambda g, i, col=col: (step_of(i), col_of(col, g))))
        operands.append(arr)
    for arr in whole_in:
        if arr.ndim == 2:
            in_specs.append(pl.BlockSpec(arr.shape, lambda g, i: (0, 0)))
        else:
            in_specs.append(pl.BlockSpec((None,) + arr.shape[1:], lambda g, i: (g, 0, 0)))
        operands.append(arr)
    for arr in step_in:
        in_specs.append(pl.BlockSpec((None, None) + arr.shape[2:], lambda g, i: (g, step_of(i), 0, 0)))
        operands.append(arr)
    out_shape, out_specs = [], []
    for total, width, col, dt in seq_out:
        out_shape.append(jax.ShapeDtypeStruct((n_steps * rows, total), dt))
        out_specs.append(pl.BlockSpec((rows, width), lambda g, i, col=col: (step_of(i), col_of(col, g))))
    for r, c in acc_out:
        out_shape.append(jax.ShapeDtypeStruct((n_groups, r, c), F32))
        out_specs.append(pl.BlockSpec((None, r, c), lambda g, i: (g, 0, 0)))
    for r, c in step_out:
        out_shape.append(jax.ShapeDtypeStruct((n_groups, n_steps, r, c), F32))
        out_specs.append(pl.BlockSpec((None, None, r, c), lambda g, i: (g, step_of(i), 0, 0)))
    n_seq, n_whole, n_step = len(seq_in), len(whole_in), len(step_in)
    n_so, n_ao, n_sto = len(seq_out), len(acc_out), len(step_out)
    n_x = len(a2a)
    hbm = pl.BlockSpec(memory_space=pl.ANY)
    in_specs += [hbm] * n_x
    operands += list(a2a)
    out_specs += [hbm] * n_x
    out_shape += [jax.ShapeDtypeStruct(x.shape, x.dtype) for x in a2a]
    scratch = [pltpu.VMEM((r, c), F32) for r, c in carry]
    if n_x:
        scratch += [pltpu.SemaphoreType.DMA((n_x, N_DEV - 1)), pltpu.SemaphoreType.DMA((n_x, N_DEV - 1)),
                    pltpu.SemaphoreType.DMA((n_x,))]

    def body(*refs):
        pos = 0
        seq_refs = refs[pos:pos + n_seq]
        pos += n_seq
        whole_refs = refs[pos:pos + n_whole]
        pos += n_whole
        step_refs = refs[pos:pos + n_step]
        pos += n_step + n_x
        so_refs = refs[pos:pos + n_so]
        pos += n_so
        ao_refs = refs[pos:pos + n_ao]
        pos += n_ao
        sto_refs = refs[pos:pos + n_sto]
        pos += n_sto
        xo_refs = refs[pos:pos + n_x]
        pos += n_x
        carry_refs = refs[pos:pos + len(carry)]
        pos += len(carry)
        x_refs = refs[n_seq + n_whole + n_step:n_seq + n_whole + n_step + n_x]
        g, i = pl.program_id(0), pl.program_id(1)
        if n_x:
            locals_, sends, recvs = a2a_copies(x_refs, xo_refs, *refs[pos:])

            @pl.when((g == 0) & (i == 0))
            def _():
                for cp in locals_ + sends:
                    cp.start()

        @pl.when(i == 0)
        def _():
            for r in carry_refs:
                r[...] = jnp.zeros(r.shape, r.dtype)
            for r in ao_refs:
                r[...] = jnp.zeros(r.shape, r.dtype)

        row0 = step_of(i) * rows
        seq_o, acc_o, step_o, new_c = body_fn(row0, [r[...] for r in whole_refs], [r[...] for r in seq_refs],
                                              [r[...] for r in step_refs], [r[...] for r in carry_refs])
        for r, val in zip(so_refs, seq_o, strict=True):
            r[...] = val.astype(r.dtype)
        for r, val in zip(ao_refs, acc_o, strict=True):
            r[...] += val
        for r, val in zip(sto_refs, step_o, strict=True):
            r[...] = val
        for r, val in zip(carry_refs, new_c, strict=True):
            r[...] = val
        if n_x:
            @pl.when((g == n_groups - 1) & (i == n_steps - 1))
            def _():
                for cp in recvs:
                    cp.wait_recv()
                for cp in sends:
                    cp.wait_send()
                for cp in locals_:
                    cp.wait()

    return pl.pallas_call(
        body, name=name, grid=(n_groups, n_steps), in_specs=in_specs, out_specs=out_specs, out_shape=out_shape,
        scratch_shapes=scratch,
        compiler_params=_cparams(dimension_semantics=("arbitrary", "arbitrary")),
    )(*operands)


def mixer_fwd(fn, name, *, n_steps, rows, seqs, params, out, carry, n_groups=1):
    def body(row0, whole, seq_vals, steps, carry_vals):
        outs, new_c = fn(row0, tuple(whole), tuple(seq_vals), tuple(carry_vals))
        return list(outs), [], list(carry_vals), list(new_c)

    res = tiled_call(body, name, n_steps=n_steps, rows=rows, n_groups=n_groups, seq_in=seqs, whole_in=params,
                     seq_out=[out], step_out=carry, carry=carry)
    return res[0], list(res[1:])


def mixer_bwd(fn, name, *, n_steps, rows, seqs, params, dout, saved, carry, n_groups=1, a2a=()):
    n_seq = len(seqs)

    def body(row0, whole, seq_vals, steps, dcarry):
        params_f = tuple(p.astype(F32) for p in whole)
        _, vjp = jax.vjp(lambda p, s, c: fn(row0, p, s, c), params_f, tuple(seq_vals[:n_seq]), tuple(steps))
        dp, ds, dc = vjp(((seq_vals[n_seq],), tuple(dcarry)))
        return list(ds), list(dp), [], list(dc)

    seq_out = [(n_groups * w, w, (lambda g: g), F32) if callable(c) else (w, w, 0, F32) for a, w, c in seqs]
    acc_out = [p.shape[-2:] for p in params]
    res = tiled_call(body, name, n_steps=n_steps, rows=rows, n_groups=n_groups, reverse=True,
                     seq_in=list(seqs) + [dout], whole_in=params, step_in=saved,
                     seq_out=seq_out, acc_out=acc_out, carry=carry, a2a=a2a)
    n_p = len(params)
    if a2a:
        return list(res[:n_seq]), list(res[n_seq:n_seq + n_p]), list(res[n_seq + n_p:])
    return list(res[:n_seq]), list(res[n_seq:])


def rms_fwd(h, g, name, rows):
    def body(row0, whole, seqs, steps, carry):
        return [rms_norm(seqs[0], whole[0])], [], [], []
    d = h.shape[1]
    return tiled_call(body, name, n_steps=h.shape[0] // rows, rows=rows, seq_in=[(h, d, 0)], whole_in=[g],
                      seq_out=[(d, d, 0, BF16)])[0]


def rms_bwd_add(h, dxn, dh_out, g, name, rows):
    def body(row0, whole, seqs, steps, carry):
        _, vjp = jax.vjp(rms_norm, seqs[0], whole[0])
        dh, dg = vjp(seqs[1])
        return [seqs[2] + dh], [dg], [], []
    d = h.shape[1]
    dh_in, dg = tiled_call(body, name, n_steps=h.shape[0] // rows, rows=rows,
                           seq_in=[(h, d, 0), (dxn, d, 0), (dh_out, d, 0)], whole_in=[g],
                           seq_out=[(d, d, 0, F32)], acc_out=[(1, d)])
    return dh_in, dg[0]


def mix_out_fwd(h, ys, w_out, name, rows):
    def body(row0, whole, seqs, steps, carry):
        (delta,), _ = mix_out_delta(row0, (whole[0],), tuple(seqs[1:]), ())
        return [seqs[0] + delta], [], [], []
    d, wd = h.shape[1], ys[0].shape[1]
    return tiled_call(body, name, n_steps=h.shape[0] // rows, rows=rows,
                      seq_in=[(h, d, 0)] + [(y, wd, 0) for y in ys], whole_in=[w_out], seq_out=[(d, d, 0, F32)])[0]


def loss_and_grad(h, target, g, name, rows, first_row):
    def body(row0, whole, seqs, steps, carry):
        hh, tt = seqs
        keep = row_mask(row0, hh.shape[0], first_row)

        def f(hv, gv):
            err = rms_norm(hv, gv) - tt
            return 0.5 * jnp.sum(jnp.mean(err * err, axis=-1, keepdims=True) * keep, axis=0, keepdims=True)

        val, vjp = jax.vjp(f, hh, whole[0])
        dh, dg = vjp(jnp.ones((1, 1), F32))
        return [dh], [jnp.broadcast_to(val, (1, LANE)), dg], [], []
    d = h.shape[1]
    dh, loss, dg = tiled_call(body, name, n_steps=h.shape[0] // rows, rows=rows,
                              seq_in=[(h, d, 0), (target, d, 0)], whole_in=[g],
                              seq_out=[(d, d, 0, F32)], acc_out=[(1, LANE), (1, d)])
    return loss[0, 0, 0], dh, dg[0]


def _pick(n, cands):
    for c in cands:
        if n % c == 0:
            return c
    raise ValueError(f"no tile for {n}")


ROW_TILES = (1056, 704, 352, 192, 96, 64)
COL_TILES = (256, 128)
NT_DIMS = (((1,), (1,)), ((), ()))
TN_DIMS = (((0,), (0,)), ((), ()))


def ffn_fwd(h, xn, wg, wu, wd, name):
    t, d = h.shape
    f = wg.shape[1]
    tm = _pick(t, ROW_TILES)
    tn = _pick(f, COL_TILES)
    n_j = f // tn

    def body(h_ref, xn_ref, wg_ref, wu_ref, wd_ref, o_ref, acc_ref):
        j = pl.program_id(1)

        @pl.when(j == 0)
        def _():
            acc_ref[...] = jnp.zeros(acc_ref.shape, F32)

        x = xn_ref[...]
        g = jnp.dot(x, wg_ref[...], preferred_element_type=F32)
        u = jnp.dot(x, wu_ref[...], preferred_element_type=F32)
        a = (jax.nn.silu(g) * u).astype(BF16)
        acc_ref[...] += jnp.dot(a, wd_ref[...], preferred_element_type=F32)

        @pl.when(j == n_j - 1)
        def _():
            o_ref[...] = h_ref[...] + 0.5 * acc_ref[...]

    return pl.pallas_call(
        body, name=name, grid=(t // tm, n_j),
        in_specs=[pl.BlockSpec((tm, d), lambda i, j: (i, 0)), pl.BlockSpec((tm, d), lambda i, j: (i, 0)),
                  pl.BlockSpec((d, tn), lambda i, j: (0, j)), pl.BlockSpec((d, tn), lambda i, j: (0, j)),
                  pl.BlockSpec((tn, d), lambda i, j: (j, 0))],
        out_specs=pl.BlockSpec((tm, d), lambda i, j: (i, 0)),
        out_shape=jax.ShapeDtypeStruct((t, d), F32),
        scratch_shapes=[pltpu.VMEM((tm, d), F32)],
        compiler_params=_cparams(dimension_semantics=("arbitrary", "arbitrary")),
    )(h, xn, wg, wu, wd)


def ffn_bwd(xn, dh, wg, wu, wd, name):
    t, d = dh.shape
    f = wg.shape[1]
    tm = _pick(t, ROW_TILES)
    tn = _pick(f, COL_TILES)

    def body(xn_ref, dh_ref, wg_ref, wu_ref, wd_ref, dxn_ref, dwg_ref, dwu_ref, dwd_ref):
        j, i = pl.program_id(0), pl.program_id(1)
        rows = pl.ds(pl.multiple_of(i * tm, 8), tm)
        x = xn_ref[rows, :]
        dhh = (0.5 * dh_ref[...]).astype(BF16)
        wgv, wuv = wg_ref[...], wu_ref[...]
        g = jnp.dot(x, wgv, preferred_element_type=F32)
        u = jnp.dot(x, wuv, preferred_element_type=F32)
        sg = jax.nn.sigmoid(g)
        s = g * sg
        da = lax.dot_general(dhh, wd_ref[...], NT_DIMS, preferred_element_type=F32)
        dwd = lax.dot_general((s * u).astype(BF16), dhh, TN_DIMS, preferred_element_type=F32)
        dg = (da * u * (sg * (1.0 + g * (1.0 - sg)))).astype(BF16)
        du = (da * s).astype(BF16)
        dwg = lax.dot_general(x, dg, TN_DIMS, preferred_element_type=F32)
        dwu = lax.dot_general(x, du, TN_DIMS, preferred_element_type=F32)
        dx = (lax.dot_general(dg, wgv, NT_DIMS, preferred_element_type=F32)
              + lax.dot_general(du, wuv, NT_DIMS, preferred_element_type=F32))

        @pl.when(i == 0)
        def _():
            dwg_ref[...] = dwg
            dwu_ref[...] = dwu
            dwd_ref[...] = dwd

        @pl.when(i > 0)
        def _():
            dwg_ref[...] += dwg
            dwu_ref[...] += dwu
            dwd_ref[...] += dwd

        @pl.when(j == 0)
        def _():
            dxn_ref[rows, :] = dx

        @pl.when(j > 0)
        def _():
            dxn_ref[rows, :] += dx

    return pl.pallas_call(
        body, name=name, grid=(f // tn, t // tm),
        in_specs=[pl.BlockSpec((t, d), lambda j, i: (0, 0)), pl.BlockSpec((tm, d), lambda j, i: (i, 0)),
                  pl.BlockSpec((d, tn), lambda j, i: (0, j)), pl.BlockSpec((d, tn), lambda j, i: (0, j)),
                  pl.BlockSpec((tn, d), lambda j, i: (j, 0))],
        out_specs=[pl.BlockSpec((t, d), lambda j, i: (0, 0)), pl.BlockSpec((d, tn), lambda j, i: (0, j)),
                   pl.BlockSpec((d, tn), lambda j, i: (0, j)), pl.BlockSpec((tn, d), lambda j, i: (j, 0))],
        out_shape=[jax.ShapeDtypeStruct((t, d), F32), jax.ShapeDtypeStruct((d, f), F32),
                   jax.ShapeDtypeStruct((d, f), F32), jax.ShapeDtypeStruct((f, d), F32)],
        compiler_params=_cparams(dimension_semantics=("arbitrary", "arbitrary")),
    )(xn, dh, wg, wu, wd)


def matmul_cols(xn, w, name):
    t, d = xn.shape
    n = w.shape[1]
    tn = _pick(n, COL_TILES)

    def body(x_ref, w_ref, o_ref):
        o_ref[...] = jnp.dot(x_ref[...], w_ref[...], preferred_element_type=F32)

    return pl.pallas_call(
        body, name=name, grid=(n // tn,),
        in_specs=[pl.BlockSpec((t, d), lambda j: (0, 0)), pl.BlockSpec((d, tn), lambda j: (0, j))],
        out_specs=pl.BlockSpec((t, tn), lambda j: (0, j)),
        out_shape=jax.ShapeDtypeStruct((t, n), F32),
        compiler_params=_cparams(dimension_semantics=("arbitrary",)),
    )(xn, w)


def matmul_cols_bwd(xn, dy, w, name):
    t, d = xn.shape
    n = w.shape[1]
    tn = _pick(n, COL_TILES)

    def body(x_ref, dy_ref, w_ref, dx_ref, dw_ref):
        j = pl.program_id(0)
        dyv = dy_ref[...].astype(BF16)
        dw_ref[...] = lax.dot_general(x_ref[...], dyv, TN_DIMS, preferred_element_type=F32)
        dx = lax.dot_general(dyv, w_ref[...], NT_DIMS, preferred_element_type=F32)

        @pl.when(j == 0)
        def _():
            dx_ref[...] = dx

        @pl.when(j > 0)
        def _():
            dx_ref[...] += dx

    return pl.pallas_call(
        body, name=name, grid=(n // tn,),
        in_specs=[pl.BlockSpec((t, d), lambda j: (0, 0)), pl.BlockSpec((t, tn), lambda j: (0, j)),
                  pl.BlockSpec((d, tn), lambda j: (0, j))],
        out_specs=[pl.BlockSpec((t, d), lambda j: (0, 0)), pl.BlockSpec((d, tn), lambda j: (0, j))],
        out_shape=[jax.ShapeDtypeStruct((t, d), F32), jax.ShapeDtypeStruct((d, n), F32)],
        compiler_params=_cparams(dimension_semantics=("arbitrary",)),
    )(xn, dy, w)


def _peer(mx, my, mc, k):
    px = 1 - mx if (k >> 2) & 1 else mx
    py = 1 - my if (k >> 1) & 1 else my
    pc = 1 - mc if k & 1 else mc
    return (px, py, pc), 4 * px + 2 * py + pc


def a2a_copies(x_refs, o_refs, send_sems, recv_sems, local_sems):
    mx, my, mc = lax.axis_index("x"), lax.axis_index("y"), lax.axis_index("c")
    me = 4 * mx + 2 * my + mc
    peers = [_peer(mx, my, mc, k) for k in range(1, N_DEV)]
    locals_, sends, recvs = [], [], []
    for a, (x_ref, o_ref) in enumerate(zip(x_refs, o_refs, strict=True)):
        locals_.append(pltpu.make_async_copy(x_ref.at[me], o_ref.at[me], local_sems.at[a]))
        for k, (dev, peer) in enumerate(peers):
            common = dict(send_sem=send_sems.at[a, k], recv_sem=recv_sems.at[a, k], device_id=dev,
                          device_id_type=pl.DeviceIdType.MESH)
            sends.append(pltpu.make_async_remote_copy(src_ref=x_ref.at[peer], dst_ref=o_ref.at[me], **common))
            recvs.append(pltpu.make_async_remote_copy(src_ref=x_ref.at[peer], dst_ref=o_ref.at[peer], **common))
    return locals_, sends, recvs


def all_to_all(xs, name):
    n = len(xs)

    def body(*refs):
        locals_, sends, recvs = a2a_copies(refs[:n], refs[n:2 * n], *refs[2 * n:])
        for cp in locals_ + sends:
            cp.start()
        for cp in recvs:
            cp.wait_recv()
        for cp in sends:
            cp.wait_send()
        for cp in locals_:
            cp.wait()

    return pl.pallas_call(
        body, name=name,
        in_specs=[pl.BlockSpec(memory_space=pl.ANY)] * n, out_specs=[pl.BlockSpec(memory_space=pl.ANY)] * n,
        out_shape=[jax.ShapeDtypeStruct(x.shape, x.dtype) for x in xs],
        scratch_shapes=[pltpu.SemaphoreType.DMA((n, N_DEV - 1)), pltpu.SemaphoreType.DMA((n, N_DEV - 1)),
                        pltpu.SemaphoreType.DMA((n,))],
    )(*xs)


def all_gather(xs, name):
    n = len(xs)
    chip_flips = (4, 2, 6)

    def body(*refs):
        x_refs, o_refs = refs[:n], refs[n:2 * n]
        send_sems, recv_sems, local_sems = refs[2 * n:]
        mx, my, mc = lax.axis_index("x"), lax.axis_index("y"), lax.axis_index("c")
        me = 4 * mx + 2 * my + mc
        sib_dev, sib = _peer(mx, my, mc, 1)

        def copy(a, k, row, to, src=None):
            return pltpu.make_async_remote_copy(
                src_ref=o_refs[a].at[row] if src is None else src, dst_ref=o_refs[a].at[row],
                send_sem=send_sems.at[a, k], recv_sem=recv_sems.at[a, k], device_id=to,
                device_id_type=pl.DeviceIdType.MESH)

        locals_, first, passed = [], [], []
        for a in range(n):
            locals_.append(pltpu.make_async_copy(x_refs[a], o_refs[a].at[me], local_sems.at[a]))
            first.append(copy(a, 0, me, sib_dev, src=x_refs[a]))
            for j, f in enumerate(chip_flips):
                first.append(copy(a, 1 + j, me, _peer(mx, my, mc, f)[0], src=x_refs[a]))
        for cp in locals_ + first:
            cp.start()
        for a in range(n):
            for j, f in enumerate(chip_flips):
                row = _peer(mx, my, mc, f)[1]
                copy(a, 1 + j, row, sib_dev).wait_recv()
                fwd = copy(a, 4 + j, row, sib_dev)
                fwd.start()
                passed.append(fwd)
        for a in range(n):
            copy(a, 0, sib, sib_dev).wait_recv()
            for j, f in enumerate(chip_flips):
                copy(a, 4 + j, _peer(mx, my, mc, f ^ 1)[1], sib_dev).wait_recv()
        for cp in first + passed:
            cp.wait_send()
        for cp in locals_:
            cp.wait()

    return pl.pallas_call(
        body, name=name,
        in_specs=[pl.BlockSpec(memory_space=pl.ANY)] * n, out_specs=[pl.BlockSpec(memory_space=pl.ANY)] * n,
        out_shape=[jax.ShapeDtypeStruct((N_DEV,) + x.shape, x.dtype) for x in xs],
        scratch_shapes=[pltpu.SemaphoreType.DMA((n, N_DEV - 1)), pltpu.SemaphoreType.DMA((n, N_DEV - 1)),
                        pltpu.SemaphoreType.DMA((n,))],
    )(*xs)


PACK_COLS = 1024
PACK_ROWS = 256
PARTS_TILE_BYTES = 4 * 1024 * 1024


def adamw_reduce(parts, w, m, v, name):
    r, c = w.shape
    fits = [t for t in (512, 352, 256, 128, 64, 32, 16, 8) if N_DEV * t * c * parts.dtype.itemsize <= PARTS_TILE_BYTES]
    tr = r if r < 2 * SUBLANE else _pick(r, fits)
    c1 = 1.0 - ADAM_B1 ** ADAM_STEP
    c2 = 1.0 - ADAM_B2 ** ADAM_STEP

    def body(p_ref, w_ref, m_ref, v_ref, g_ref, d_ref, mo_ref, vo_ref):
        g = p_ref[0].astype(F32)
        for k in range(1, N_DEV):
            g = g + p_ref[k].astype(F32)
        mn = ADAM_B1 * m_ref[...] + (1.0 - ADAM_B1) * g
        vn = ADAM_B2 * v_ref[...] + (1.0 - ADAM_B2) * (g * g)
        g_ref[...] = g
        mo_ref[...] = mn
        vo_ref[...] = vn
        d_ref[...] = -ADAM_LR * ((mn / c1) / (jnp.sqrt(vn / c2) + ADAM_EPS) + ADAM_WD * w_ref[...])

    spec = pl.BlockSpec((tr, c), lambda i: (i, 0))
    return pl.pallas_call(
        body, name=name, grid=(r // tr,),
        in_specs=[pl.BlockSpec((N_DEV, tr, c), lambda i: (0, i, 0)), spec, spec, spec],
        out_specs=[spec] * 4, out_shape=[jax.ShapeDtypeStruct((r, c), F32)] * 4,
        compiler_params=_cparams(dimension_semantics=("arbitrary",)),
    )(parts, w, m, v)


def pack_flat(arrs, dtype):
    parts = []
    for a in arrs:
        flat = a.reshape(-1).astype(dtype)
        k = -(-flat.shape[0] // PACK_COLS)
        parts.append(jnp.pad(flat, (0, k * PACK_COLS - flat.shape[0])).reshape(k, PACK_COLS))
    buf = jnp.concatenate(parts, axis=0)
    return jnp.pad(buf, ((0, -buf.shape[0] % PACK_ROWS), (0, 0)))


def unpack_flat(buf, shapes):
    out, r0 = [], 0
    for s in shapes:
        n = math.prod(s)
        k = -(-n // PACK_COLS)
        out.append(buf[r0:r0 + k].reshape(-1)[:n].reshape(tuple(s)))
        r0 += k
    return out


W_NAMES = ('meta_tokens', 'ffn1_norm', 'ffn1_w_gate', 'ffn1_w_up', 'ffn1_w_down', 'mix_norm', 'w_in', 'w_out',
           'lru_conv_w', 'lru_conv_b', 'lru_w_a', 'lru_b_a', 'lru_w_i', 'lru_b_i', 'lru_lambda', 'lru_norm',
           'gdn_conv_w', 'gdn_a_log', 'gdn_dt_bias', 'gdn_norm', 'ssd_conv_w', 'ssd_conv_b', 'ssd_a_log',
           'ssd_dt_bias', 'ssd_d', 'ssd_norm', 's5_a_re', 's5_a_im', 's5_log_dt', 's5_b_re', 's5_b_im', 's5_c_re',
           's5_c_im', 's5_d', 's5_w_glu', 's5_norm', 'ffn2_norm', 'ffn2_w_gate', 'ffn2_w_up', 'ffn2_w_down',
           'final_norm')
SHARD_AXIS = {'meta_tokens': 1, 'ffn1_w_gate': 2, 'ffn1_w_up': 2, 'ffn1_w_down': 1, 'w_in': 2, 'w_out': 1,
              'lru_conv_w': 2, 'gdn_conv_w': 2, 'ssd_conv_w': 2, 's5_w_glu': 1, 'ffn2_w_gate': 2, 'ffn2_w_up': 2,
              'ffn2_w_down': 1}
BIG_NAMES = ('ffn1_w_gate', 'ffn1_w_up', 'ffn1_w_down', 'w_in', 'w_out', 's5_w_glu', 'ffn2_w_gate', 'ffn2_w_up',
             'ffn2_w_down')
SHARD_NAMES = tuple(n for n in W_NAMES if n in SHARD_AXIS)
REP_NAMES = tuple(n for n in W_NAMES if n not in SHARD_AXIS)
SSD_GROUPS = 2
S5_CH = 16


def unshard(g, axis):
    return jnp.concatenate([g[p] for p in range(N_DEV)], axis=axis)


def kernel(*args):
    n_w = len(W_NAMES)
    x = args[0]
    w = dict(zip(W_NAMES, args[1:1 + n_w]))
    target = args[1 + n_w]
    m_in = dict(zip(W_NAMES, args[2 + n_w:2 + 2 * n_w]))
    v_in = dict(zip(W_NAMES, args[2 + 2 * n_w:2 + 3 * n_w]))

    depth, d = w['ffn1_norm'].shape
    seq = x.shape[1]
    n_meta = w['meta_tokens'].shape[0]
    pad = CHUNK - n_meta
    tp = pad + n_meta + seq
    wg = d // 2
    xbc_w = w['ssd_conv_w'].shape[-1] * N_DEV
    gdn_hd = w['gdn_norm'].shape[-1]
    gdn_h = wg // gdn_hd
    ssd_h = w['ssd_a_log'].shape[-1]
    lru_h = w['lru_w_a'].shape[1]
    s5_g, s5_n = w['s5_a_re'].shape[1:]
    s5_q = wg // S5_LANES
    row_tile = _pick(tp, (192, 96, 64))

    gathered = all_gather([w[n].astype(BF16 if n in BIG_NAMES else F32) for n in SHARD_NAMES], "gather_weights")
    full = {n: unshard(g, SHARD_AXIS[n]) for n, g in zip(SHARD_NAMES, gathered)}

    segs = [('a_x', wg), ('a_gate', wg), ('b_q', wg), ('b_k', wg), ('b_v', wg), ('b_z', wg), ('c_xbc', xbc_w),
            ('c_z', wg), ('d_u', wg), ('small_b', LANE), ('small_c', LANE)]
    off, o = {}, 0
    for nme, wd_ in segs:
        assert o % wd_ == 0, (nme, o, wd_)
        off[nme] = o
        o += wd_
    o_beta = 6 * wg
    o_cz = o_beta + 2 * gdn_h
    o_xbc = o_cz + wg
    o_dt = o_xbc + xbc_w
    o_du = o_dt + ssd_h

    def pack_cols(a):
        z = lambda k: jnp.zeros(a.shape[:-1] + (k,), a.dtype)
        return jnp.concatenate([a[..., :o_beta], a[..., o_xbc:o_dt], a[..., o_cz:o_xbc], a[..., o_du:],
                                a[..., o_beta:o_cz], z(LANE - 2 * gdn_h), a[..., o_dt:o_du], z(LANE - ssd_h)], axis=-1)

    def unpack_cols(a):
        sb, sc = off['small_b'], off['small_c']
        return jnp.concatenate([a[..., :o_beta], a[..., sb:sb + 2 * gdn_h], a[..., off['c_z']:off['c_z'] + wg],
                                a[..., off['c_xbc']:off['c_xbc'] + xbc_w], a[..., sc:sc + ssd_h],
                                a[..., off['d_u']:off['d_u'] + wg]], axis=-1)

    w_in_p = pack_cols(full['w_in'])

    def col(name, width):
        return off[name] // width

    def row(a):
        return a.reshape(1, -1)

    def layer_params(l):
        gcw = full['gdn_conv_w'][l]
        lru = [full['lru_conv_w'][l], row(w['lru_conv_b'][l]), blockdiag_expand(w['lru_w_a'][l]), row(w['lru_b_a'][l]),
               blockdiag_expand(w['lru_w_i'][l]), row(w['lru_b_i'][l]), row(w['lru_lambda'][l]), row(w['lru_norm'][l])]
        gdn = [gcw[:, :wg], gcw[:, wg:2 * wg], gcw[:, 2 * wg:], row(w['gdn_a_log'][l]), row(w['gdn_dt_bias'][l]),
               row(w['gdn_norm'][l])]
        ssd = [full['ssd_conv_w'][l], row(w['ssd_conv_b'][l]), row(w['ssd_a_log'][l]), row(w['ssd_dt_bias'][l]),
               row(w['ssd_d'][l]), row(w['ssd_norm'][l])]
        s5 = list(s5_params_expand(*[w[n][l] for n in ('s5_a_re', 's5_a_im', 's5_log_dt', 's5_b_re', 's5_b_im',
                                                          's5_c_re', 's5_c_im', 's5_d')]))
        post = [full['s5_w_glu'][l], row(w['s5_norm'][l])]
        return lru, gdn, ssd, s5, post

    lru_fn = functools.partial(lru_chunk, pad)
    gdn_fn = functools.partial(gdn_multi, pad, CHUNK)
    ssd_fn = functools.partial(ssd_multi, pad, 0, SSD_GROUPS, CHUNK)
    s5_fn = functools.partial(s5_chunk, pad)
    post_fn = functools.partial(s5_post, pad)
    n_state_lanes = (S5_LANES // S5_CH) * s5_n

    def mixer_specs(proj):
        lru_seqs = [(proj, wg, col('a_x', wg)), (proj, wg, col('a_gate', wg))]
        gdn_seqs = [(proj, wg, col('b_q', wg)), (proj, wg, col('b_k', wg)), (proj, wg, col('b_v', wg)),
                    (proj, wg, col('b_z', wg)), (proj, LANE, col('small_b', LANE))]
        ssd_seqs = [(proj, wg, col('c_z', wg)), (proj, xbc_w, col('c_xbc', xbc_w)), (proj, LANE, col('small_c', LANE))]
        base = col('d_u', S5_LANES)
        s5_seqs = [(proj, S5_LANES, lambda g: base + g)]
        return lru_seqs, gdn_seqs, ssd_seqs, s5_seqs

    lru_carry = [(CONV_TAIL, wg), (1, wg)]
    gdn_carry = [(CONV_TAIL, wg)] * 3 + [(wg, gdn_hd)]
    ssd_carry = [(CONV_TAIL, xbc_w), (wg, (xbc_w - wg) // (2 * SSD_GROUPS))]
    s5_carry = [(1, n_state_lanes)] * 2
    rk = dict(n_steps=tp // row_tile, rows=row_tile)
    mk = rk
    out_w = (wg, wg, 0, F32)

    h = jnp.concatenate([jnp.zeros((pad, d), F32), full['meta_tokens'], x[0]], axis=0)
    target_p = jnp.concatenate([jnp.zeros((pad + n_meta, d), F32), target[0]], axis=0)
    saved = []
    for l in range(depth):
        lru_p, gdn_p, ssd_p, s5_p, post_p = layer_params(l)
        h0 = h
        xn1 = rms_fwd(h0, row(w['ffn1_norm'][l]), "rms_fwd", row_tile)
        h1 = ffn_fwd(h0, xn1, full['ffn1_w_gate'][l], full['ffn1_w_up'][l], full['ffn1_w_down'][l], "ffn_fwd")
        xn2 = rms_fwd(h1, row(w['mix_norm'][l]), "rms_fwd", row_tile)
        proj = matmul_cols(xn2, w_in_p[l], "mix_in_fwd")
        lru_s, gdn_s, ssd_s, s5_s = mixer_specs(proj)
        ya, lru_c = mixer_fwd(lru_fn, "lru_fwd", seqs=lru_s, params=lru_p, out=out_w, carry=lru_carry, **mk)
        yb, gdn_c = mixer_fwd(gdn_fn, "gdn_fwd", seqs=gdn_s, params=gdn_p, out=out_w, carry=gdn_carry, **mk)
        yc, ssd_c = mixer_fwd(ssd_fn, "ssd_fwd", seqs=ssd_s, params=ssd_p, out=out_w, carry=ssd_carry, **mk)
        y1, s5_c = mixer_fwd(s5_fn, "s5_fwd", seqs=s5_s, params=s5_p, out=(wg, S5_LANES, lambda g: g, F32),
                             carry=s5_carry, n_groups=s5_q, **mk)
        yd, _ = mixer_fwd(post_fn, "s5_post_fwd", seqs=[(y1, wg, 0)], params=post_p, out=out_w, carry=[], **rk)
        h2 = mix_out_fwd(h1, [ya, yb, yc, yd], full['w_out'][l], "mix_out_fwd", row_tile)
        xn3 = rms_fwd(h2, row(w['ffn2_norm'][l]), "rms_fwd", row_tile)
        h3 = ffn_fwd(h2, xn3, full['ffn2_w_gate'][l], full['ffn2_w_up'][l], full['ffn2_w_down'][l], "ffn_fwd")
        saved.append((h0, xn1, h1, xn2, proj, (ya, yb, yc, yd), y1, (lru_c, gdn_c, ssd_c, s5_c), h2, xn3))
        h = h3

    loss_part, dh, d_final = loss_and_grad(h, target_p, row(w['final_norm']), "loss", row_tile, pad + n_meta)
    loss = lax.psum(loss_part, ("x", "y", "c"))

    def shards_of(a, axis):
        sh = a.shape
        return jnp.moveaxis(a.reshape(sh[:axis] + (N_DEV, sh[axis] // N_DEV) + sh[axis + 1:]), axis, 0)

    def to_send(n, g):
        return shards_of(g, SHARD_AXIS[n] - 1).astype(BF16)

    received = {n: [None] * depth for n in SHARD_NAMES if n != 'meta_tokens'}
    pending = []

    def hosted(names_layers):
        keys = [k for k in pending if (k[0], k[1]) in names_layers]
        for k in keys:
            pending.remove(k)
        return [(k[0], k[1]) for k in keys], [k[2] for k in keys]

    def store(keys, arrays):
        for (n, l), a in zip(keys, arrays, strict=True):
            received[n][l] = a

    gw = {n: [None] * depth for n in W_NAMES if n not in ('meta_tokens', 'final_norm')}
    for l in reversed(range(depth)):
        lru_p, gdn_p, ssd_p, s5_p, post_p = layer_params(l)
        h0, xn1, h1, xn2, proj, ys, y1, (lru_c, gdn_c, ssd_c, s5_c), h2, xn3 = saved[l]
        dxn, gw['ffn2_w_gate'][l], gw['ffn2_w_up'][l], gw['ffn2_w_down'][l] = ffn_bwd(
            xn3, dh, full['ffn2_w_gate'][l], full['ffn2_w_up'][l], full['ffn2_w_down'][l], "ffn_bwd")
        pending += [(n, l, to_send(n, gw[n][l])) for n in ('ffn2_w_gate', 'ffn2_w_up', 'ffn2_w_down')]
        dh, dg = rms_bwd_add(h2, dxn, dh, row(w['ffn2_norm'][l]), "rms_bwd", row_tile)
        gw['ffn2_norm'][l] = dg[0]

        dys, (d_wout,) = mixer_bwd(mix_out_delta, "mix_out_bwd", seqs=[(y, wg, 0) for y in ys],
                                   params=[full['w_out'][l]], dout=(dh, d, 0), saved=[], carry=[], **rk)
        gw['w_out'][l] = d_wout[0]
        pending.append(('w_out', l, to_send('w_out', gw['w_out'][l])))
        lru_s, gdn_s, ssd_s, s5_s = mixer_specs(proj)
        (dy1,), d_post = mixer_bwd(post_fn, "s5_post_bwd", seqs=[(y1, wg, 0)], params=post_p, dout=(dys[3], wg, 0),
                                   saved=[], carry=[], **rk)
        gw['s5_w_glu'][l], gw['s5_norm'][l] = d_post[0][0], d_post[1][0, 0]
        pending.append(('s5_w_glu', l, to_send('s5_w_glu', gw['s5_w_glu'][l])))
        keys, arrs = hosted({('ffn1_w_gate', l + 1), ('ffn1_w_up', l + 1), ('ffn1_w_down', l + 1)})
        (d_du,), d_s5, *got = mixer_bwd(s5_fn, "s5_bwd", seqs=s5_s, params=s5_p, dout=(dy1, S5_LANES, lambda g: g),
                                        saved=s5_c, carry=s5_carry, n_groups=s5_q, a2a=arrs, **mk)
        store(keys, got[0] if got else [])
        for n, g in zip(('s5_a_re', 's5_a_im', 's5_log_dt', 's5_b_re', 's5_b_im', 's5_c_re', 's5_c_im', 's5_d'),
                        s5_grads_extract(d_s5, s5_g, s5_n, S5_CH)):
            gw[n][l] = g
        keys, arrs = hosted({('w_in', l + 1), ('w_out', l), ('s5_w_glu', l)})
        (d_cz, d_cxbc, d_sc), d_ssd, *got = mixer_bwd(ssd_fn, "ssd_bwd", seqs=ssd_s, params=ssd_p,
                                                      dout=(dys[2], wg, 0), saved=ssd_c, carry=ssd_carry, a2a=arrs, **mk)
        store(keys, got[0] if got else [])
        for n, g in zip(('ssd_conv_w', 'ssd_conv_b', 'ssd_a_log', 'ssd_dt_bias', 'ssd_d', 'ssd_norm'), d_ssd):
            gw[n][l] = g[0] if n == 'ssd_conv_w' else g[0, 0]
        keys, arrs = hosted({('ffn2_w_gate', l), ('ffn2_w_up', l), ('ffn2_w_down', l)})
        (d_bq, d_bk, d_bv, d_bz, d_sb), d_gdn, *got = mixer_bwd(gdn_fn, "gdn_bwd", seqs=gdn_s, params=gdn_p,
                                                                dout=(dys[1], wg, 0), saved=gdn_c, carry=gdn_carry,
                                                                a2a=arrs, **mk)
        store(keys, got[0] if got else [])
        gw['gdn_conv_w'][l] = jnp.concatenate([d_gdn[0][0], d_gdn[1][0], d_gdn[2][0]], axis=1)
        gw['gdn_a_log'][l], gw['gdn_dt_bias'][l], gw['gdn_norm'][l] = d_gdn[3][0, 0], d_gdn[4][0, 0], d_gdn[5][0, 0]
        (d_ax, d_ag), d_lru = mixer_bwd(lru_fn, "lru_bwd", seqs=lru_s, params=lru_p, dout=(dys[0], wg, 0),
                                        saved=lru_c, carry=lru_carry, **mk)
        gw['lru_conv_w'][l], gw['lru_conv_b'][l] = d_lru[0][0], d_lru[1][0, 0]
        gw['lru_w_a'][l], gw['lru_b_a'][l] = blockdiag_extract(d_lru[2][0], lru_h), d_lru[3][0, 0]
        gw['lru_w_i'][l], gw['lru_b_i'][l] = blockdiag_extract(d_lru[4][0], lru_h), d_lru[5][0, 0]
        gw['lru_lambda'][l], gw['lru_norm'][l] = d_lru[6][0, 0], d_lru[7][0, 0]

        dproj = jnp.concatenate([d_ax, d_ag, d_bq, d_bk, d_bv, d_bz, d_cxbc, d_cz, d_du, d_sb, d_sc], axis=1)
        dxn, d_win_p = matmul_cols_bwd(xn2, dproj, w_in_p[l], "mix_in_bwd")
        gw['w_in'][l] = unpack_cols(d_win_p)
        pending.append(('w_in', l, to_send('w_in', gw['w_in'][l])))
        dh, dg = rms_bwd_add(h1, dxn, dh, row(w['mix_norm'][l]), "rms_bwd", row_tile)
        gw['mix_norm'][l] = dg[0]

        dxn, gw['ffn1_w_gate'][l], gw['ffn1_w_up'][l], gw['ffn1_w_down'][l] = ffn_bwd(
            xn1, dh, full['ffn1_w_gate'][l], full['ffn1_w_up'][l], full['ffn1_w_down'][l], "ffn_bwd")
        dh, dg = rms_bwd_add(h0, dxn, dh, row(w['ffn1_norm'][l]), "rms_bwd", row_tile)
        gw['ffn1_norm'][l] = dg[0]
        pending += [(n, l, to_send(n, gw[n][l])) for n in ('ffn1_w_gate', 'ffn1_w_up', 'ffn1_w_down')]

    grad_x = dh[pad + n_meta:][None]
    grads = {n: jnp.stack(g, axis=0) for n, g in gw.items()}
    grads['meta_tokens'] = dh[pad:pad + n_meta]
    grads['final_norm'] = d_final[0]

    for n in ('lru_conv_w', 'gdn_conv_w', 'ssd_conv_w'):
        pending += [(n, l, to_send(n, gw[n][l])) for l in range(depth)]
    last = [k[2] for k in pending] + [shards_of(grads['meta_tokens'], SHARD_AXIS['meta_tokens']).astype(BF16)]
    *got, recv_meta = all_to_all(last, "scatter_grads")
    store([(k[0], k[1]) for k in pending], got)
    recv_sh = [recv_meta if n == 'meta_tokens' else jnp.stack(received[n], axis=1) for n in SHARD_NAMES]
    (recv_rep,) = all_gather([pack_flat([grads[n] for n in REP_NAMES], F32)], "gather_rep_grads")
    out = {}
    for n, recv in zip(SHARD_NAMES, recv_sh):
        c = w[n].shape[-1]
        res = adamw_reduce(recv.reshape(N_DEV, -1, c), w[n].reshape(-1, c), m_in[n].reshape(-1, c),
                           v_in[n].reshape(-1, c), "adamw_" + n)
        for kind, buf in zip(('grad', 'delta', 'new_m', 'new_v'), res):
            out[kind, n] = buf.reshape(w[n].shape)
    res = adamw_reduce(recv_rep, pack_flat([w[n] for n in REP_NAMES], F32), pack_flat([m_in[n] for n in REP_NAMES], F32),
                       pack_flat([v_in[n] for n in REP_NAMES], F32), "adamw_replicated")
    shapes = [w[n].shape for n in REP_NAMES]
    for kind, buf in zip(('grad', 'delta', 'new_m', 'new_v'), res):
        for n, a in zip(REP_NAMES, unpack_flat(buf, shapes)):
            out[kind, n] = a
    return (loss, grad_x) + tuple(out[k, n] for k in ('grad', 'delta', 'new_m', 'new_v') for n in W_NAMES)
```

```python
import functools
import math

import jax
import jax.numpy as jnp
from jax import lax
from jax.experimental import pallas as pl
from jax.experimental.pallas import tpu as pltpu

F32 = jnp.float32
BF16 = jnp.bfloat16

EPS = 1e-6
CHUNK = 64
CONV_K = 4
CONV_TAIL = 8
LRU_C = 8.0
LANE = 128
SUBLANE = 8
N_DEV = 8
NEG_BIG = -1e30

ADAM_LR = 0.001
ADAM_B1 = 0.9
ADAM_B2 = 0.999
ADAM_EPS = 1e-08
ADAM_WD = 0.01
ADAM_STEP = 10

VMEM_LIMIT = 56 * 1024 * 1024


def _dg(a, b, dims):
    return lax.dot_general(a.astype(BF16), b.astype(BF16), (dims, ((), ())), preferred_element_type=F32)


@jax.custom_vjp
def bdot(a, b):
    return _dg(a, b, ((1,), (0,)))


@jax.custom_vjp
def bdot_nt(a, b):
    return _dg(a, b, ((1,), (1,)))


@jax.custom_vjp
def bdot_tn(a, b):
    return _dg(a, b, ((0,), (0,)))


bdot.defvjp(lambda a, b: (bdot(a, b), (a, b)),
            lambda r, g: (bdot_nt(g, r[1]).astype(r[0].dtype), bdot_tn(r[0], g).astype(r[1].dtype)))
bdot_nt.defvjp(lambda a, b: (bdot_nt(a, b), (a, b)),
               lambda r, g: (bdot(g, r[1]).astype(r[0].dtype), bdot_tn(g, r[0]).astype(r[1].dtype)))
bdot_tn.defvjp(lambda a, b: (bdot_tn(a, b), (a, b)),
               lambda r, g: (bdot_nt(r[1], g).astype(r[0].dtype), bdot(r[0], g).astype(r[1].dtype)))


def _split_bf16(a):
    hi = a.astype(BF16)
    return hi, (a - hi.astype(F32)).astype(BF16)


def _dot3(a, b, dims):
    (ah, al), (bh, bl) = _split_bf16(a), _split_bf16(b)
    d = lambda x, y: lax.dot_general(x, y, (dims, ((), ())), preferred_element_type=F32)
    return d(ah, bh) + (d(ah, bl) + d(al, bh))


@jax.custom_vjp
def hdot(a, b):
    return _dot3(a, b, ((1,), (0,)))


@jax.custom_vjp
def hdot_tn(a, b):
    return _dot3(a, b, ((0,), (0,)))


hdot.defvjp(lambda a, b: (hdot(a, b), (a, b)),
            lambda r, g: (_dot3(g, r[1], ((1,), (1,))), _dot3(r[0], g, ((0,), (0,)))))
hdot_tn.defvjp(lambda a, b: (hdot_tn(a, b), (a, b)),
               lambda r, g: (_dot3(r[1], g, ((1,), (1,))), _dot3(r[0], g, ((1,), (0,)))))


def rms_norm(x, g):
    return x * lax.rsqrt(jnp.mean(x * x, axis=-1, keepdims=True) + EPS) * g


def row_mask(row0, rows, pad):
    r = row0 + lax.broadcasted_iota(jnp.int32, (rows, 1), 0)
    return (r >= pad).astype(F32)


def conv4(tail, u, w):
    rows = u.shape[0]
    xe = jnp.concatenate([tail, u], axis=0)
    y = w[0:1] * xe[CONV_TAIL - 3:CONV_TAIL - 3 + rows]
    for k in range(1, CONV_K):
        y = y + w[k:k + 1] * xe[CONV_TAIL - 3 + k:CONV_TAIL - 3 + k + rows]
    return y


def shift_rows(x, s, fill):
    rows = x.shape[0]
    return jnp.concatenate([jnp.full((s, x.shape[1]), fill, x.dtype), x[:rows - s]], axis=0)


def lin_scan(a, b):
    rows = a.shape[0]
    s = 1
    while s < rows:
        b = a * shift_rows(b, s, 0.0) + b
        a = a * shift_rows(a, s, 1.0)
        s *= 2
    return b


def cscan_const(ar, ai, br, bi):
    rows = br.shape[0]
    s = 1
    while s < rows:
        brs, bis = shift_rows(br, s, 0.0), shift_rows(bi, s, 0.0)
        br, bi = br + ar * brs - ai * bis, bi + ar * bis + ai * brs
        ar, ai = ar * ar - ai * ai, 2.0 * ar * ai
        s *= 2
    return br, bi


def neg_expm1(z):
    t = jnp.tanh(0.5 * z)
    return -2.0 * t / (1.0 - t)


def tri_masks(n):
    r = lax.broadcasted_iota(jnp.int32, (n, n), 0)
    c = lax.broadcasted_iota(jnp.int32, (n, n), 1)
    return r >= c, r > c, (r == c).astype(F32)


def lru_chunk(pad, row0, params, seqs, carry):
    conv_w, conv_b, w_a, b_a, w_i, b_i, lam, norm_g = params
    u_x, u_gate = seqs
    tail, h0 = carry
    rows = u_x.shape[0]
    m = row_mask(row0, rows, pad)
    xc = conv4(tail, u_x, conv_w) + conv_b
    r = jax.nn.sigmoid(bdot(xc, w_a) + b_a)
    ig = jax.nn.sigmoid(bdot(xc, w_i) + b_i)
    log_a = -LRU_C * r * jax.nn.softplus(-lam)
    a = jnp.exp(log_a)
    b = jnp.sqrt(neg_expm1(2.0 * log_a)) * (ig * xc) * m
    first = (lax.broadcasted_iota(jnp.int32, (rows, 1), 0) == 0).astype(F32)
    b = b + first * (a * h0)
    h = lin_scan(a, b)
    y = jax.nn.gelu(u_gate) * h
    out = rms_norm(y, norm_g) * m
    return (out,), (u_x[rows - CONV_TAIL:], h[rows - 1:])


def gdn_multi(pad, sub, row0, params, seqs, carry):
    wq, wk, wv, a_log, dt_bias, norm_g = params
    u_q, u_k, u_v, u_z, small = seqs
    tq, tk, tv, state = carry
    rows = u_q.shape[0]
    hd = norm_g.shape[1]
    nh = u_q.shape[1] // hd
    nc = rows // sub
    m = row_mask(row0, rows, pad)
    incl, strict, eye = tri_masks(sub)
    tril = incl.astype(F32)
    triu = (lax.broadcasted_iota(jnp.int32, (sub, sub), 0) <= lax.broadcasted_iota(jnp.int32, (sub, sub), 1)).astype(F32)
    qc = jax.nn.silu(conv4(tq, u_q, wq))
    kc = jax.nn.silu(conv4(tk, u_k, wk))
    vc = jax.nn.silu(conv4(tv, u_v, wv))
    beta = jax.nn.sigmoid(small[:, :nh]) * m
    g = -jnp.exp(a_log) * jax.nn.softplus(small[:, nh:2 * nh] + dt_bias) * m
    gate = jax.nn.silu(u_z)
    heads = [slice(h * hd, (h + 1) * hd) for h in range(nh)]
    q_h = [qc[:, sl] for sl in heads]
    k_h = [kc[:, sl] for sl in heads]
    q_h = [q * lax.rsqrt(jnp.sum(q * q, axis=-1, keepdims=True) + EPS) * (hd ** -0.5) * m for q in q_h]
    k_h = [k * lax.rsqrt(jnp.sum(k * k, axis=-1, keepdims=True) + EPS) * m for k in k_h]
    v_h = [vc[:, sl] * m for sl in heads]
    pairs = [(c, h) for c in range(nc) for h in range(nh)]
    cs = lambda x, c: x[c * sub:(c + 1) * sub]
    q = {(c, h): cs(q_h[h], c) for c, h in pairs}
    k = {(c, h): cs(k_h[h], c) for c, h in pairs}
    v = {(c, h): cs(v_h[h], c) for c, h in pairs}
    bt = {(c, h): cs(beta, c)[:, h:h + 1] for c, h in pairs}
    gcs = [hdot(tril, cs(g, c)) for c in range(nc)]
    gts = [hdot_tn(cs(g, c), triu) for c in range(nc)]
    gc = {(c, h): gcs[c][:, h:h + 1] for c, h in pairs}
    decay = {(c, h): jnp.exp(jnp.where(incl, gc[c, h] - gts[c][h:h + 1], NEG_BIG)) for c, h in pairs}
    kb = {p: k[p] * bt[p] for p in pairs}
    kk = {p: bdot_nt(kb[p], k[p]) for p in pairs}
    lmat = {p: jnp.where(strict, kk[p] * decay[p], 0.0) for p in pairs}
    pm = {p: eye - lmat[p] for p in pairs}
    mm = {p: hdot(lmat[p], lmat[p]) for p in pairs}
    s = 2
    while s < sub:
        pm = {p: pm[p] + hdot(pm[p], mm[p]) for p in pairs}
        s *= 2
        if s < sub:
            mm = {p: hdot(mm[p], mm[p]) for p in pairs}
    eg = {p: jnp.exp(gc[p]) for p in pairs}
    u = {p: hdot(pm[p], v[p] * bt[p]) for p in pairs}
    w = {p: hdot(pm[p], kb[p] * eg[p]) for p in pairs}
    attn = {p: bdot_nt(q[p], k[p]) * decay[p] for p in pairs}
    qd = {p: q[p] * eg[p] for p in pairs}
    g_last = {p: gc[p][sub - 1:] for p in pairs}
    kd = {p: k[p] * jnp.exp(g_last[p] - gc[p]) for p in pairs}
    last = {p: jnp.exp(g_last[p]) for p in pairs}
    s_h = [state[sl] for sl in heads]
    o = {}
    for c in range(nc):
        ws = [bdot(w[c, h], s_h[h]) for h in range(nh)]
        qs = [bdot(qd[c, h], s_h[h]) for h in range(nh)]
        v_new = [u[c, h] - ws[h] for h in range(nh)]
        av = [bdot(attn[c, h], v_new[h]) for h in range(nh)]
        kv = [bdot_tn(kd[c, h], v_new[h]) for h in range(nh)]
        for h in range(nh):
            o[c, h] = qs[h] + av[h]
        s_h = [s_h[h] * last[c, h] + kv[h] for h in range(nh)]
    out = jnp.concatenate([jnp.concatenate([rms_norm(o[c, h], norm_g) for h in range(nh)], axis=1)
                           for c in range(nc)], axis=0) * gate * m
    t0 = rows - CONV_TAIL
    return (out,), (u_q[t0:], u_k[t0:], u_v[t0:], jnp.concatenate(s_h, axis=0))


def ssd_multi(pad, dt_lane0, n_groups, sub, row0, params, seqs, carry):
    conv_w, conv_b, a_log, dt_bias, d_skip, norm_g = params
    u_z, u_xbc, small = seqs
    tail, state = carry
    rows = u_z.shape[0]
    width = u_z.shape[1]
    nh = a_log.shape[1]
    hd = width // nh
    ns = (u_xbc.shape[1] - width) // (2 * n_groups)
    hpg = nh // n_groups
    nc = rows // sub
    m = row_mask(row0, rows, pad)
    incl, _, _ = tri_masks(sub)
    tril = incl.astype(F32)
    triu = (lax.broadcasted_iota(jnp.int32, (sub, sub), 0) <= lax.broadcasted_iota(jnp.int32, (sub, sub), 1)).astype(F32)
    xbc = jax.nn.silu(conv4(tail, u_xbc, conv_w) + conv_b)
    xs = xbc[:, :width]
    dt = jax.nn.softplus(small[:, dt_lane0:dt_lane0 + nh] + dt_bias)
    a_all = dt * (-jnp.exp(a_log)) * m
    cs = lambda x, c: x[c * sub:(c + 1) * sub]
    heads = [slice(h * hd, (h + 1) * hd) for h in range(nh)]
    pairs = [(c, h) for c in range(nc) for h in range(nh)]
    grp = lambda h: h // hpg
    bm = {(c, g): cs(xbc[:, width + g * ns: width + (g + 1) * ns] * m, c) for c in range(nc) for g in range(n_groups)}
    cm = {(c, g): cs(xbc[:, width + (n_groups + g) * ns: width + (n_groups + g + 1) * ns] * m, c)
          for c in range(nc) for g in range(n_groups)}
    xh = {(c, h): cs(xs[:, heads[h]], c) for c, h in pairs}
    xdt = {(c, h): xh[c, h] * cs(dt[:, h:h + 1] * m, c) for c, h in pairs}
    acums = [hdot(tril, cs(a_all, c)) for c in range(nc)]
    acts = [hdot_tn(cs(a_all, c), triu) for c in range(nc)]
    acum = {(c, h): acums[c][:, h:h + 1] for c, h in pairs}
    a_last = {p: acum[p][sub - 1:] for p in pairs}
    lmat = {(c, h): jnp.exp(jnp.where(incl, acum[c, h] - acts[c][h:h + 1], NEG_BIG)) for c, h in pairs}
    cb = {cg: bdot_nt(cm[cg], bm[cg]) for cg in bm}
    y_diag = {(c, h): bdot(cb[c, grp(h)] * lmat[c, h], xdt[c, h]) for c, h in pairs}
    st = {(c, h): bdot_tn(xdt[c, h] * jnp.exp(a_last[c, h] - acum[c, h]), bm[c, grp(h)]) for c, h in pairs}
    e_in = {p: jnp.exp(acum[p]) for p in pairs}
    e_out = {p: jnp.exp(a_last[p]) for p in pairs}
    s_h = [state[sl] for sl in heads]
    y = {}
    for c in range(nc):
        off = [bdot_nt(cm[c, grp(h)], s_h[h]) for h in range(nh)]
        for h in range(nh):
            y[c, h] = y_diag[c, h] + off[h] * e_in[c, h] + d_skip[:, h:h + 1] * xh[c, h]
        s_h = [s_h[h] * e_out[c, h] + st[c, h] for h in range(nh)]
    yy = jnp.concatenate([jnp.concatenate([y[c, h] for h in range(nh)], axis=1) for c in range(nc)], axis=0)
    yy = yy * jax.nn.silu(u_z)
    gw = width // n_groups
    outs = [rms_norm(yy[:, g * gw:(g + 1) * gw], norm_g[:, g * gw:(g + 1) * gw]) for g in range(n_groups)]
    out = jnp.concatenate(outs, axis=1) * m
    return (out,), (u_xbc[rows - CONV_TAIL:], jnp.concatenate(s_h, axis=0))


def s5_chunk(pad, row0, params, seqs, carry):
    a_re, a_im, log_dt, b_re, b_im, c_re, c_im, d_skip = params
    (u,) = seqs
    s_re0, s_im0 = carry
    rows = u.shape[0]
    n_state = a_re.shape[1]
    n_grp = log_dt.shape[1]
    per = n_state // n_grp
    expand = (lax.broadcasted_iota(jnp.int32, (n_grp, n_state), 1) // per
              == lax.broadcasted_iota(jnp.int32, (n_grp, n_state), 0)).astype(F32)
    dt = jnp.exp(hdot(log_dt, expand))
    lam_re = jnp.minimum(a_re, -1e-4)
    lam_im = a_im
    mag = jnp.exp(dt * lam_re)
    ab_re = mag * jnp.cos(dt * lam_im)
    ab_im = mag * jnp.sin(dt * lam_im)
    den = lam_re * lam_re + lam_im * lam_im
    f_re = ((ab_re - 1.0) * lam_re + ab_im * lam_im) / den
    f_im = (ab_im * lam_re - (ab_re - 1.0) * lam_im) / den
    bb_re = f_re * b_re - f_im * b_im
    bb_im = f_re * b_im + f_im * b_re
    bu_re = bdot(u, bb_re)
    bu_im = bdot(u, bb_im)
    first = (lax.broadcasted_iota(jnp.int32, (rows, 1), 0) == 0).astype(F32)
    bu_re = bu_re + first * (ab_re * s_re0 - ab_im * s_im0)
    bu_im = bu_im + first * (ab_re * s_im0 + ab_im * s_re0)
    s_re, s_im = cscan_const(ab_re, ab_im, bu_re, bu_im)
    y = bdot(s_re, c_re) - bdot(s_im, c_im) + d_skip * u
    return (y,), (s_re[rows - 1:], s_im[rows - 1:])


def s5_post(pad, row0, params, seqs, carry):
    w_glu, norm_g = params
    (y,) = seqs
    y = jax.nn.gelu(y)
    y = y * jax.nn.sigmoid(bdot(y, w_glu))
    return (rms_norm(y, norm_g),), ()


def mix_out_delta(row0, params, seqs, carry):
    (w_out,) = params
    wd = seqs[0].shape[1]
    acc = bdot(seqs[0], w_out[0:wd])
    for k in range(1, len(seqs)):
        acc = acc + bdot(seqs[k], w_out[k * wd:(k + 1) * wd])
    return (acc,), ()


def blockdiag_expand(w):
    nh, a, b = w.shape
    eye = jnp.eye(nh, dtype=w.dtype)
    return (w[:, :, None, :] * eye[:, None, :, None]).reshape(nh * a, nh * b)


def blockdiag_extract(m, nh):
    a, b = m.shape[0] // nh, m.shape[1] // nh
    on_diag = jnp.eye(nh, dtype=bool)[:, None, :, None]
    return jnp.sum(jnp.where(on_diag, m.reshape(nh, a, nh, b), 0.0), axis=2)


S5_LANES = LANE


def s5_params_expand(a_re, a_im, log_dt, b_re, b_im, c_re, c_im, d_skip):
    n_grp, n_state = a_re.shape
    ch = b_re.shape[-1]
    gpl = S5_LANES // ch
    nq = n_grp // gpl
    eye = jnp.eye(gpl, dtype=F32)[None, :, None, :, None]

    def bexp(b):
        bt = jnp.swapaxes(b, 1, 2).reshape(nq, gpl, b.shape[2], 1, b.shape[1])
        return (bt * eye).reshape(nq, gpl * b.shape[2], gpl * b.shape[1])

    return (a_re.reshape(nq, 1, gpl * n_state), a_im.reshape(nq, 1, gpl * n_state), log_dt.reshape(nq, 1, gpl),
            bexp(b_re), bexp(b_im), bexp(c_re), bexp(c_im), d_skip.reshape(nq, 1, S5_LANES))


def s5_grads_extract(grads, n_grp, n_state, ch):
    da_re, da_im, dlog_dt, db_re, db_im, dc_re, dc_im, dd = grads
    gpl = S5_LANES // ch
    nq = n_grp // gpl
    on_diag = jnp.eye(gpl, dtype=bool)[None, :, None, :, None]

    def bext(b):
        r, c = b.shape[1] // gpl, b.shape[2] // gpl
        d = jnp.sum(jnp.where(on_diag, b.reshape(nq, gpl, r, gpl, c), 0.0), axis=3)
        return jnp.swapaxes(d.reshape(n_grp, r, c), 1, 2)

    return (da_re.reshape(n_grp, n_state), da_im.reshape(n_grp, n_state), dlog_dt.reshape(n_grp),
            bext(db_re), bext(db_im), bext(dc_re), bext(dc_im), dd.reshape(n_grp * ch))


def _cparams(**kw):
    return pltpu.CompilerParams(vmem_limit_bytes=VMEM_LIMIT, **kw)


def tiled_call(body_fn, name, *, n_steps, rows, n_groups=1, reverse=False,
               seq_in=(), whole_in=(), step_in=(), seq_out=(), acc_out=(), step_out=(), carry=(), a2a=(), ag=()):
    def step_of(i):
        return (n_steps - 1 - i) if reverse else i

    def col_of(col, g):
        return col(g) if callable(col) else col

    in_specs, operands = [], []
    for arr, width, col in seq_in:
        in_specs.append(pl.BlockSpec((rows, width), lambda g, i, col=col: (step_of(i), col_of(col, g))))
        operands.append(arr)
    for arr in whole_in:
        if arr.ndim == 2:
            in_specs.append(pl.BlockSpec(arr.shape, lambda g, i: (0, 0)))
        else:
            in_specs.append(pl.BlockSpec((None,) + arr.shape[1:], lambda g, i: (g, 0, 0)))
        operands.append(arr)
    for arr in step_in:
        in_specs.append(pl.BlockSpec((None, None) + arr.shape[2:], lambda g, i: (g, step_of(i), 0, 0)))
        operands.append(arr)
    out_shape, out_specs = [], []
    for total, width, col, dt in seq_out:
        out_shape.append(jax.ShapeDtypeStruct((n_steps * rows, total), dt))
        out_specs.append(pl.BlockSpec((rows, width), lambda g, i, col=col: (step_of(i), col_of(col, g))))
    for r, c in acc_out:
        out_shape.append(jax.ShapeDtypeStruct((n_groups, r, c), F32))
        out_specs.append(pl.BlockSpec((None, r, c), lambda g, i: (g, 0, 0)))
    for r, c in step_out:
        out_shape.append(jax.ShapeDtypeStruct((n_groups, n_steps, r, c), F32))
        out_specs.append(pl.BlockSpec((None, None, r, c), lambda g, i: (g, step_of(i), 0, 0)))
    n_seq, n_whole, n_step = len(seq_in), len(whole_in), len(step_in)
    n_so, n_ao, n_sto = len(seq_out), len(acc_out), len(step_out)
    assert not (a2a and ag)
    hosted = list(a2a) + list(ag)
    n_x = len(hosted)
    n_sem = AG_FIRST_COPIES if ag else N_DEV - 1
    hbm = pl.BlockSpec(memory_space=pl.ANY)
    in_specs += [hbm] * n_x
    operands += hosted
    out_specs += [hbm] * n_x
    out_shape += [jax.ShapeDtypeStruct((N_DEV,) + x.shape if ag else x.shape, x.dtype) for x in hosted]
    scratch = [pltpu.VMEM((r, c), F32) for r, c in carry]
    if n_x:
        scratch += [pltpu.SemaphoreType.DMA((n_x, n_sem)), pltpu.SemaphoreType.DMA((n_x, n_sem)),
                    pltpu.SemaphoreType.DMA((n_x,))]

    def body(*refs):
        pos = 0
        seq_refs = refs[pos:pos + n_seq]
        pos += n_seq
        whole_refs = refs[pos:pos + n_whole]
        pos += n_whole
        step_refs = refs[pos:pos + n_step]
        pos += n_step + n_x
        so_refs = refs[pos:pos + n_so]
        pos += n_so
        ao_refs = refs[pos:pos + n_ao]
        pos += n_ao
        sto_refs = refs[pos:pos + n_sto]
        pos += n_sto
        xo_refs = refs[pos:pos + n_x]
        pos += n_x
        carry_refs = refs[pos:pos + len(carry)]
        pos += len(carry)
        x_refs = refs[n_seq + n_whole + n_step:n_seq + n_whole + n_step + n_x]
        g, i = pl.program_id(0), pl.program_id(1)
        if n_x:
            locals_, sends, recvs = (ag_first_copies if ag else a2a_copies)(x_refs, xo_refs, *refs[pos:])

            @pl.when((g == 0) & (i == 0))
            def _():
                for cp in locals_ + sends:
                    cp.start()

        @pl.when(i == 0)
        def _():
            for r in carry_refs:
                r[...] = jnp.zeros(r.shape, r.dtype)
            for r in ao_refs:
                r[...] = jnp.zeros(r.shape, r.dtype)

        row0 = step_of(i) * rows
        seq_o, acc_o, step_o, new_c = body_fn(row0, [r[...] for r in whole_refs], [r[...] for r in seq_refs],
                                              [r[...] for r in step_refs], [r[...] for r in carry_refs])
        for r, val in zip(so_refs, seq_o, strict=True):
            r[...] = val.astype(r.dtype)
        for r, val in zip(ao_refs, acc_o, strict=True):
            r[...] += val
        for r, val in zip(sto_refs, step_o, strict=True):
            r[...] = val
        for r, val in zip(carry_refs, new_c, strict=True):
            r[...] = val
        if n_x:
            @pl.when((g == n_groups - 1) & (i == n_steps - 1))
            def _():
                for cp in recvs:
                    cp.wait_recv()
                for cp in sends:
                    cp.wait_send()
                for cp in locals_:
                    cp.wait()

    return pl.pallas_call(
        body, name=name, grid=(n_groups, n_steps), in_specs=in_specs, out_specs=out_specs, out_shape=out_shape,
        scratch_shapes=scratch,
        compiler_params=_cparams(dimension_semantics=("arbitrary", "arbitrary")),
    )(*operands)


def mixer_fwd(fn, name, *, n_steps, rows, seqs, params, out, carry, n_groups=1, ag=()):
    def body(row0, whole, seq_vals, steps, carry_vals):
        outs, new_c = fn(row0, tuple(whole), tuple(seq_vals), tuple(carry_vals))
        return list(outs), [], list(carry_vals), list(new_c)

    res = tiled_call(body, name, n_steps=n_steps, rows=rows, n_groups=n_groups, seq_in=seqs, whole_in=params,
                     seq_out=[out], step_out=carry, carry=carry, ag=ag)
    return res[0], list(res[1:1 + len(carry)]), list(res[1 + len(carry):])


def mixer_bwd(fn, name, *, n_steps, rows, seqs, params, dout, saved, carry, n_groups=1, a2a=()):
    n_seq = len(seqs)

    def body(row0, whole, seq_vals, steps, dcarry):
        params_f = tuple(p.astype(F32) for p in whole)
        _, vjp = jax.vjp(lambda p, s, c: fn(row0, p, s, c), params_f, tuple(seq_vals[:n_seq]), tuple(steps))
        dp, ds, dc = vjp(((seq_vals[n_seq],), tuple(dcarry)))
        return list(ds), list(dp), [], list(dc)

    seq_out = [(n_groups * w, w, (lambda g: g), F32) if callable(c) else (w, w, 0, F32) for a, w, c in seqs]
    acc_out = [p.shape[-2:] for p in params]
    res = tiled_call(body, name, n_steps=n_steps, rows=rows, n_groups=n_groups, reverse=True,
                     seq_in=list(seqs) + [dout], whole_in=params, step_in=saved,
                     seq_out=seq_out, acc_out=acc_out, carry=carry, a2a=a2a)
    n_p = len(params)
    if a2a:
        return list(res[:n_seq]), list(res[n_seq:n_seq + n_p]), list(res[n_seq + n_p:])
    return list(res[:n_seq]), list(res[n_seq:])


def rms_fwd(h, g, name, rows):
    def body(row0, whole, seqs, steps, carry):
        return [rms_norm(seqs[0], whole[0])], [], [], []
    d = h.shape[1]
    return tiled_call(body, name, n_steps=h.shape[0] // rows, rows=rows, seq_in=[(h, d, 0)], whole_in=[g],
                      seq_out=[(d, d, 0, BF16)])[0]


def rms_bwd_add(h, dxn, dh_out, g, name, rows):
    def body(row0, whole, seqs, steps, carry):
        _, vjp = jax.vjp(rms_norm, seqs[0], whole[0])
        dh, dg = vjp(seqs[1])
        return [seqs[2] + dh], [dg], [], []
    d = h.shape[1]
    dh_in, dg = tiled_call(body, name, n_steps=h.shape[0] // rows, rows=rows,
                           seq_in=[(h, d, 0), (dxn, d, 0), (dh_out, d, 0)], whole_in=[g],
                           seq_out=[(d, d, 0, F32)], acc_out=[(1, d)])
    return dh_in, dg[0]


def mix_out_fwd(h, ys, w_out, name, rows, ag=()):
    def body(row0, whole, seqs, steps, carry):
        (delta,), _ = mix_out_delta(row0, (whole[0],), tuple(seqs[1:]), ())
        return [seqs[0] + delta], [], [], []
    d, wd = h.shape[1], ys[0].shape[1]
    res = tiled_call(body, name, n_steps=h.shape[0] // rows, rows=rows,
                     seq_in=[(h, d, 0)] + [(y, wd, 0) for y in ys], whole_in=[w_out], seq_out=[(d, d, 0, F32)], ag=ag)
    return res[0], list(res[1:])


def loss_and_grad(h, target, g, name, rows, first_row):
    def body(row0, whole, seqs, steps, carry):
        hh, tt = seqs
        keep = row_mask(row0, hh.shape[0], first_row)

        def f(hv, gv):
            err = rms_norm(hv, gv) - tt
            return 0.5 * jnp.sum(jnp.mean(err * err, axis=-1, keepdims=True) * keep, axis=0, keepdims=True)

        val, vjp = jax.vjp(f, hh, whole[0])
        dh, dg = vjp(jnp.ones((1, 1), F32))
        return [dh], [jnp.broadcast_to(val, (1, LANE)), dg], [], []
    d = h.shape[1]
    dh, loss, dg = tiled_call(body, name, n_steps=h.shape[0] // rows, rows=rows,
                              seq_in=[(h, d, 0), (target, d, 0)], whole_in=[g],
                              seq_out=[(d, d, 0, F32)], acc_out=[(1, LANE), (1, d)])
    return loss[0, 0, 0], dh, dg[0]


def _pick(n, cands):
    for c in cands:
        if n % c == 0:
            return c
    raise ValueError(f"no tile for {n}")


ROW_TILES = (1056, 704, 352, 192, 96, 64)
COL_TILES = (256, 128)
NT_DIMS = (((1,), (1,)), ((), ()))
TN_DIMS = (((0,), (0,)), ((), ()))


def ffn_fwd(h, xn, wg, wu, wd, name):
    t, d = h.shape
    f = wg.shape[1]
    tm = _pick(t, ROW_TILES)
    tn = _pick(f, COL_TILES)
    n_j = f // tn

    def body(h_ref, xn_ref, wg_ref, wu_ref, wd_ref, o_ref, acc_ref):
        j = pl.program_id(1)

        @pl.when(j == 0)
        def _():
            acc_ref[...] = jnp.zeros(acc_ref.shape, F32)

        x = xn_ref[...]
        g = jnp.dot(x, wg_ref[...], preferred_element_type=F32)
        u = jnp.dot(x, wu_ref[...], preferred_element_type=F32)
        a = (jax.nn.silu(g) * u).astype(BF16)
        acc_ref[...] += jnp.dot(a, wd_ref[...], preferred_element_type=F32)

        @pl.when(j == n_j - 1)
        def _():
            o_ref[...] = h_ref[...] + 0.5 * acc_ref[...]

    return pl.pallas_call(
        body, name=name, grid=(t // tm, n_j),
        in_specs=[pl.BlockSpec((tm, d), lambda i, j: (i, 0)), pl.BlockSpec((tm, d), lambda i, j: (i, 0)),
                  pl.BlockSpec((d, tn), lambda i, j: (0, j)), pl.BlockSpec((d, tn), lambda i, j: (0, j)),
                  pl.BlockSpec((tn, d), lambda i, j: (j, 0))],
        out_specs=pl.BlockSpec((tm, d), lambda i, j: (i, 0)),
        out_shape=jax.ShapeDtypeStruct((t, d), F32),
        scratch_shapes=[pltpu.VMEM((tm, d), F32)],
        compiler_params=_cparams(dimension_semantics=("arbitrary", "arbitrary")),
    )(h, xn, wg, wu, wd)


def ffn_bwd(xn, dh, wg, wu, wd, name):
    t, d = dh.shape
    f = wg.shape[1]
    tm = _pick(t, ROW_TILES)
    tn = _pick(f, COL_TILES)

    def body(xn_ref, dh_ref, wg_ref, wu_ref, wd_ref, dxn_ref, dwg_ref, dwu_ref, dwd_ref):
        j, i = pl.program_id(0), pl.program_id(1)
        rows = pl.ds(pl.multiple_of(i * tm, 8), tm)
        x = xn_ref[rows, :]
        dhh = (0.5 * dh_ref[...]).astype(BF16)
        wgv, wuv = wg_ref[...], wu_ref[...]
        g = jnp.dot(x, wgv, preferred_element_type=F32)
        u = jnp.dot(x, wuv, preferred_element_type=F32)
        sg = jax.nn.sigmoid(g)
        s = g * sg
        da = lax.dot_general(dhh, wd_ref[...], NT_DIMS, preferred_element_type=F32)
        dwd = lax.dot_general((s * u).astype(BF16), dhh, TN_DIMS, preferred_element_type=F32)
        dg = (da * u * (sg * (1.0 + g * (1.0 - sg)))).astype(BF16)
        du = (da * s).astype(BF16)
        dwg = lax.dot_general(x, dg, TN_DIMS, preferred_element_type=F32)
        dwu = lax.dot_general(x, du, TN_DIMS, preferred_element_type=F32)
        dx = (lax.dot_general(dg, wgv, NT_DIMS, preferred_element_type=F32)
              + lax.dot_general(du, wuv, NT_DIMS, preferred_element_type=F32))

        @pl.when(i == 0)
        def _():
            dwg_ref[...] = dwg
            dwu_ref[...] = dwu
            dwd_ref[...] = dwd

        @pl.when(i > 0)
        def _():
            dwg_ref[...] += dwg
            dwu_ref[...] += dwu
            dwd_ref[...] += dwd

        @pl.when(j == 0)
        def _():
            dxn_ref[rows, :] = dx

        @pl.when(j > 0)
        def _():
            dxn_ref[rows, :] += dx

    return pl.pallas_call(
        body, name=name, grid=(f // tn, t // tm),
        in_specs=[pl.BlockSpec((t, d), lambda j, i: (0, 0)), pl.BlockSpec((tm, d), lambda j, i: (i, 0)),
                  pl.BlockSpec((d, tn), lambda j, i: (0, j)), pl.BlockSpec((d, tn), lambda j, i: (0, j)),
                  pl.BlockSpec((tn, d), lambda j, i: (j, 0))],
        out_specs=[pl.BlockSpec((t, d), lambda j, i: (0, 0)), pl.BlockSpec((d, tn), lambda j, i: (0, j)),
                   pl.BlockSpec((d, tn), lambda j, i: (0, j)), pl.BlockSpec((tn, d), lambda j, i: (j, 0))],
        out_shape=[jax.ShapeDtypeStruct((t, d), F32), jax.ShapeDtypeStruct((d, f), F32),
                   jax.ShapeDtypeStruct((d, f), F32), jax.ShapeDtypeStruct((f, d), F32)],
        compiler_params=_cparams(dimension_semantics=("arbitrary", "arbitrary")),
    )(xn, dh, wg, wu, wd)


def matmul_cols(xn, w, name):
    t, d = xn.shape
    n = w.shape[1]
    tn = _pick(n, COL_TILES)

    def body(x_ref, w_ref, o_ref):
        o_ref[...] = jnp.dot(x_ref[...], w_ref[...], preferred_element_type=F32)

    return pl.pallas_call(
        body, name=name, grid=(n // tn,),
        in_specs=[pl.BlockSpec((t, d), lambda j: (0, 0)), pl.BlockSpec((d, tn), lambda j: (0, j))],
        out_specs=pl.BlockSpec((t, tn), lambda j: (0, j)),
        out_shape=jax.ShapeDtypeStruct((t, n), F32),
        compiler_params=_cparams(dimension_semantics=("arbitrary",)),
    )(xn, w)


def matmul_cols_bwd(xn, dy, w, name):
    t, d = xn.shape
    n = w.shape[1]
    tn = _pick(n, COL_TILES)

    def body(x_ref, dy_ref, w_ref, dx_ref, dw_ref):
        j = pl.program_id(0)
        dyv = dy_ref[...].astype(BF16)
        dw_ref[...] = lax.dot_general(x_ref[...], dyv, TN_DIMS, preferred_element_type=F32)
        dx = lax.dot_general(dyv, w_ref[...], NT_DIMS, preferred_element_type=F32)

        @pl.when(j == 0)
        def _():
            dx_ref[...] = dx

        @pl.when(j > 0)
        def _():
            dx_ref[...] += dx

    return pl.pallas_call(
        body, name=name, grid=(n // tn,),
        in_specs=[pl.BlockSpec((t, d), lambda j: (0, 0)), pl.BlockSpec((t, tn), lambda j: (0, j)),
                  pl.BlockSpec((d, tn), lambda j: (0, j))],
        out_specs=[pl.BlockSpec((t, d), lambda j: (0, 0)), pl.BlockSpec((d, tn), lambda j: (0, j))],
        out_shape=[jax.ShapeDtypeStruct((t, d), F32), jax.ShapeDtypeStruct((d, n), F32)],
        compiler_params=_cparams(dimension_semantics=("arbitrary",)),
    )(xn, dy, w)


def _peer(mx, my, mc, k):
    px = 1 - mx if (k >> 2) & 1 else mx
    py = 1 - my if (k >> 1) & 1 else my
    pc = 1 - mc if k & 1 else mc
    return (px, py, pc), 4 * px + 2 * py + pc


def a2a_copies(x_refs, o_refs, send_sems, recv_sems, local_sems):
    mx, my, mc = lax.axis_index("x"), lax.axis_index("y"), lax.axis_index("c")
    me = 4 * mx + 2 * my + mc
    peers = [_peer(mx, my, mc, k) for k in range(1, N_DEV)]
    locals_, sends, recvs = [], [], []
    for a, (x_ref, o_ref) in enumerate(zip(x_refs, o_refs, strict=True)):
        locals_.append(pltpu.make_async_copy(x_ref.at[me], o_ref.at[me], local_sems.at[a]))
        for k, (dev, peer) in enumerate(peers):
            common = dict(send_sem=send_sems.at[a, k], recv_sem=recv_sems.at[a, k], device_id=dev,
                          device_id_type=pl.DeviceIdType.MESH)
            sends.append(pltpu.make_async_remote_copy(src_ref=x_ref.at[peer], dst_ref=o_ref.at[me], **common))
            recvs.append(pltpu.make_async_remote_copy(src_ref=x_ref.at[peer], dst_ref=o_ref.at[peer], **common))
    return locals_, sends, recvs


def all_gather(xs, name, with_a2a=()):
    n, n2 = len(xs), len(with_a2a)
    chip_flips = (4, 2, 6)

    def body(*refs):
        x_refs, t_refs = refs[:n], refs[n:n + n2]
        o_refs, r_refs = refs[n + n2:2 * n + n2], refs[2 * n + n2:2 * n + 2 * n2]
        send_sems, recv_sems, local_sems = refs[2 * n + 2 * n2:2 * n + 2 * n2 + 3]
        mx, my, mc = lax.axis_index("x"), lax.axis_index("y"), lax.axis_index("c")
        me = 4 * mx + 2 * my + mc
        sib_dev, sib = _peer(mx, my, mc, 1)

        def copy(a, k, row, to, src=None):
            return pltpu.make_async_remote_copy(
                src_ref=o_refs[a].at[row] if src is None else src, dst_ref=o_refs[a].at[row],
                send_sem=send_sems.at[a, k], recv_sem=recv_sems.at[a, k], device_id=to,
                device_id_type=pl.DeviceIdType.MESH)

        locals_, first, passed = [], [], []
        t_recvs = []
        if n2:
            t_locals, t_sends, t_recvs = a2a_copies(t_refs, r_refs, *refs[2 * n + 2 * n2 + 3:])
            locals_ += t_locals
            first += t_sends
        for a in range(n):
            locals_.append(pltpu.make_async_copy(x_refs[a], o_refs[a].at[me], local_sems.at[a]))
            first.append(copy(a, 0, me, sib_dev, src=x_refs[a]))
            for j, f in enumerate(chip_flips):
                first.append(copy(a, 1 + j, me, _peer(mx, my, mc, f)[0], src=x_refs[a]))
        for cp in locals_ + first:
            cp.start()
        for a in range(n):
            for j, f in enumerate(chip_flips):
                row = _peer(mx, my, mc, f)[1]
                copy(a, 1 + j, row, sib_dev).wait_recv()
                fwd = copy(a, 4 + j, row, sib_dev)
                fwd.start()
                passed.append(fwd)
        for a in range(n):
            copy(a, 0, sib, sib_dev).wait_recv()
            for j, f in enumerate(chip_flips):
                copy(a, 4 + j, _peer(mx, my, mc, f ^ 1)[1], sib_dev).wait_recv()
        for cp in t_recvs:
            cp.wait_recv()
        for cp in first + passed:
            cp.wait_send()
        for cp in locals_:
            cp.wait()

    hbm = pl.BlockSpec(memory_space=pl.ANY)
    scratch = [pltpu.SemaphoreType.DMA((n, N_DEV - 1)), pltpu.SemaphoreType.DMA((n, N_DEV - 1)),
               pltpu.SemaphoreType.DMA((n,))]
    if n2:
        scratch += [pltpu.SemaphoreType.DMA((n2, N_DEV - 1)), pltpu.SemaphoreType.DMA((n2, N_DEV - 1)),
                    pltpu.SemaphoreType.DMA((n2,))]
    return pl.pallas_call(
        body, name=name, in_specs=[hbm] * (n + n2), out_specs=[hbm] * (n + n2),
        out_shape=[jax.ShapeDtypeStruct((N_DEV,) + x.shape, x.dtype) for x in xs]
        + [jax.ShapeDtypeStruct(x.shape, x.dtype) for x in with_a2a],
        scratch_shapes=scratch,
    )(*xs, *with_a2a)


AG_FIRST_COPIES = 4


def ag_first_copies(x_refs, o_refs, send_sems, recv_sems, local_sems):
    mx, my, mc = lax.axis_index("x"), lax.axis_index("y"), lax.axis_index("c")
    me = 4 * mx + 2 * my + mc
    targets = [_peer(mx, my, mc, f) for f in (1, 4, 2, 6)]
    locals_, sends, recvs = [], [], []
    for a, (x_ref, o_ref) in enumerate(zip(x_refs, o_refs, strict=True)):
        locals_.append(pltpu.make_async_copy(x_ref, o_ref.at[me], local_sems.at[a]))
        for k, (dev, row) in enumerate(targets):
            common = dict(send_sem=send_sems.at[a, k], recv_sem=recv_sems.at[a, k], device_id=dev,
                          device_id_type=pl.DeviceIdType.MESH)
            sends.append(pltpu.make_async_remote_copy(src_ref=x_ref, dst_ref=o_ref.at[me], **common))
            recvs.append(pltpu.make_async_remote_copy(src_ref=x_ref, dst_ref=o_ref.at[row], **common))
    return locals_, sends, recvs


def ag_second_level(bufs, name):
    n = len(bufs)
    chip_flips = (4, 2, 6)

    def body(*refs):
        x_refs, o_refs = refs[:n], refs[n:2 * n]
        send_sems, recv_sems, local_sems = refs[2 * n:]
        mx, my, mc = lax.axis_index("x"), lax.axis_index("y"), lax.axis_index("c")
        me = 4 * mx + 2 * my + mc
        sib_dev, sib = _peer(mx, my, mc, 1)
        have = [me, sib] + [_peer(mx, my, mc, f)[1] for f in chip_flips]
        locals_, sends, recvs = [], [], []
        for a in range(n):
            for i, row in enumerate(have):
                locals_.append(pltpu.make_async_copy(x_refs[a].at[row], o_refs[a].at[row], local_sems.at[a, i]))
            for j, f in enumerate(chip_flips):
                common = dict(send_sem=send_sems.at[a, j], recv_sem=recv_sems.at[a, j], device_id=sib_dev,
                              device_id_type=pl.DeviceIdType.MESH)
                row, sib_row = _peer(mx, my, mc, f)[1], _peer(mx, my, mc, f ^ 1)[1]
                sends.append(pltpu.make_async_remote_copy(src_ref=x_refs[a].at[row], dst_ref=o_refs[a].at[row], **common))
                recvs.append(pltpu.make_async_remote_copy(src_ref=x_refs[a].at[row], dst_ref=o_refs[a].at[sib_row],
                                                          **common))
        for cp in locals_ + sends:
            cp.start()
        for cp in recvs:
            cp.wait_recv()
        for cp in sends:
            cp.wait_send()
        for cp in locals_:
            cp.wait()

    return pl.pallas_call(
        body, name=name,
        in_specs=[pl.BlockSpec(memory_space=pl.ANY)] * n, out_specs=[pl.BlockSpec(memory_space=pl.ANY)] * n,
        out_shape=[jax.ShapeDtypeStruct(x.shape, x.dtype) for x in bufs],
        scratch_shapes=[pltpu.SemaphoreType.DMA((n, len(chip_flips))), pltpu.SemaphoreType.DMA((n, len(chip_flips))),
                        pltpu.SemaphoreType.DMA((n, 2 + len(chip_flips)))],
    )(*bufs)


PACK_COLS = 1024
PACK_ROWS = 256
PARTS_TILE_BYTES = 4 * 1024 * 1024


def adamw_reduce(parts, w, m, v, name):
    r, c = w.shape
    fits = [t for t in (512, 352, 256, 128, 64, 32, 16, 8) if N_DEV * t * c * parts.dtype.itemsize <= PARTS_TILE_BYTES]
    tr = r if r < 2 * SUBLANE else _pick(r, fits)
    c1 = 1.0 - ADAM_B1 ** ADAM_STEP
    c2 = 1.0 - ADAM_B2 ** ADAM_STEP

    def body(p_ref, w_ref, m_ref, v_ref, g_ref, d_ref, mo_ref, vo_ref):
        g = p_ref[0].astype(F32)
        for k in range(1, N_DEV):
            g = g + p_ref[k].astype(F32)
        mn = ADAM_B1 * m_ref[...] + (1.0 - ADAM_B1) * g
        vn = ADAM_B2 * v_ref[...] + (1.0 - ADAM_B2) * (g * g)
        g_ref[...] = g
        mo_ref[...] = mn
        vo_ref[...] = vn
        d_ref[...] = -ADAM_LR * ((mn / c1) / (jnp.sqrt(vn / c2) + ADAM_EPS) + ADAM_WD * w_ref[...])

    spec = pl.BlockSpec((tr, c), lambda i: (i, 0))
    return pl.pallas_call(
        body, name=name, grid=(r // tr,),
        in_specs=[pl.BlockSpec((N_DEV, tr, c), lambda i: (0, i, 0)), spec, spec, spec],
        out_specs=[spec] * 4, out_shape=[jax.ShapeDtypeStruct((r, c), F32)] * 4,
        compiler_params=_cparams(dimension_semantics=("arbitrary",)),
    )(parts, w, m, v)


def pack_flat(arrs, dtype):
    parts = []
    for a in arrs:
        flat = a.reshape(-1).astype(dtype)
        k = -(-flat.shape[0] // PACK_COLS)
        parts.append(jnp.pad(flat, (0, k * PACK_COLS - flat.shape[0])).reshape(k, PACK_COLS))
    buf = jnp.concatenate(parts, axis=0)
    return jnp.pad(buf, ((0, -buf.shape[0] % PACK_ROWS), (0, 0)))


def unpack_flat(buf, shapes):
    out, r0 = [], 0
    for s in shapes:
        n = math.prod(s)
        k = -(-n // PACK_COLS)
        out.append(buf[r0:r0 + k].reshape(-1)[:n].reshape(tuple(s)))
        r0 += k
    return out


W_NAMES = ('meta_tokens', 'ffn1_norm', 'ffn1_w_gate', 'ffn1_w_up', 'ffn1_w_down', 'mix_norm', 'w_in', 'w_out',
           'lru_conv_w', 'lru_conv_b', 'lru_w_a', 'lru_b_a', 'lru_w_i', 'lru_b_i', 'lru_lambda', 'lru_norm',
           'gdn_conv_w', 'gdn_a_log', 'gdn_dt_bias', 'gdn_norm', 'ssd_conv_w', 'ssd_conv_b', 'ssd_a_log',
           'ssd_dt_bias', 'ssd_d', 'ssd_norm', 's5_a_re', 's5_a_im', 's5_log_dt', 's5_b_re', 's5_b_im', 's5_c_re',
           's5_c_im', 's5_d', 's5_w_glu', 's5_norm', 'ffn2_norm', 'ffn2_w_gate', 'ffn2_w_up', 'ffn2_w_down',
           'final_norm')
SHARD_AXIS = {'meta_tokens': 1, 'ffn1_w_gate': 2, 'ffn1_w_up': 2, 'ffn1_w_down': 1, 'w_in': 2, 'w_out': 1,
              'lru_conv_w': 2, 'gdn_conv_w': 2, 'ssd_conv_w': 2, 's5_w_glu': 1, 'ffn2_w_gate': 2, 'ffn2_w_up': 2,
              'ffn2_w_down': 1}
BIG_NAMES = ('ffn1_w_gate', 'ffn1_w_up', 'ffn1_w_down', 'w_in', 'w_out', 's5_w_glu', 'ffn2_w_gate', 'ffn2_w_up',
             'ffn2_w_down')
SHARD_NAMES = tuple(n for n in W_NAMES if n in SHARD_AXIS)
REP_NAMES = tuple(n for n in W_NAMES if n not in SHARD_AXIS)
SSD_GROUPS = 2
S5_CH = 16


def unshard(g, axis):
    return jnp.concatenate([g[p] for p in range(N_DEV)], axis=axis)


def kernel(*args):
    n_w = len(W_NAMES)
    x = args[0]
    w = dict(zip(W_NAMES, args[1:1 + n_w]))
    target = args[1 + n_w]
    m_in = dict(zip(W_NAMES, args[2 + n_w:2 + 2 * n_w]))
    v_in = dict(zip(W_NAMES, args[2 + 2 * n_w:2 + 3 * n_w]))

    depth, d = w['ffn1_norm'].shape
    seq = x.shape[1]
    n_meta = w['meta_tokens'].shape[0]
    pad = CHUNK - n_meta
    tp = pad + n_meta + seq
    wg = d // 2
    xbc_w = w['ssd_conv_w'].shape[-1] * N_DEV
    gdn_hd = w['gdn_norm'].shape[-1]
    gdn_h = wg // gdn_hd
    ssd_h = w['ssd_a_log'].shape[-1]
    lru_h = w['lru_w_a'].shape[1]
    s5_g, s5_n = w['s5_a_re'].shape[1:]
    s5_q = wg // S5_LANES
    row_tile = _pick(tp, (192, 96, 64))

    LAYER_NAMES = tuple(n for n in SHARD_NAMES if n != 'meta_tokens')

    def local_of(n, l):
        return w[n][l].astype(BF16 if n in BIG_NAMES else F32)

    def unshard_layer(gathered):
        return {n: unshard(g, SHARD_AXIS[n] - 1) for n, g in gathered.items()}

    first = all_gather([local_of(n, 0) for n in LAYER_NAMES] + [w['meta_tokens']], "gather_weights")
    meta_full = unshard(first[-1], SHARD_AXIS['meta_tokens'])
    full = [unshard_layer(dict(zip(LAYER_NAMES, first[:-1])))] + [None] * (depth - 1)

    segs = [('a_x', wg), ('a_gate', wg), ('b_q', wg), ('b_k', wg), ('b_v', wg), ('b_z', wg), ('c_xbc', xbc_w),
            ('c_z', wg), ('d_u', wg), ('small_b', LANE), ('small_c', LANE)]
    off, o = {}, 0
    for nme, wd_ in segs:
        assert o % wd_ == 0, (nme, o, wd_)
        off[nme] = o
        o += wd_
    o_beta = 6 * wg
    o_cz = o_beta + 2 * gdn_h
    o_xbc = o_cz + wg
    o_dt = o_xbc + xbc_w
    o_du = o_dt + ssd_h

    def pack_cols(a):
        z = lambda k: jnp.zeros(a.shape[:-1] + (k,), a.dtype)
        return jnp.concatenate([a[..., :o_beta], a[..., o_xbc:o_dt], a[..., o_cz:o_xbc], a[..., o_du:],
                                a[..., o_beta:o_cz], z(LANE - 2 * gdn_h), a[..., o_dt:o_du], z(LANE - ssd_h)], axis=-1)

    def unpack_cols(a):
        sb, sc = off['small_b'], off['small_c']
        return jnp.concatenate([a[..., :o_beta], a[..., sb:sb + 2 * gdn_h], a[..., off['c_z']:off['c_z'] + wg],
                                a[..., off['c_xbc']:off['c_xbc'] + xbc_w], a[..., sc:sc + ssd_h],
                                a[..., off['d_u']:off['d_u'] + wg]], axis=-1)

    w_in_p = [None] * depth
    w_in_p[0] = pack_cols(full[0]['w_in'])

    def col(name, width):
        return off[name] // width

    def row(a):
        return a.reshape(1, -1)

    def layer_params(l):
        gcw = full[l]['gdn_conv_w']
        lru = [full[l]['lru_conv_w'], row(w['lru_conv_b'][l]), blockdiag_expand(w['lru_w_a'][l]), row(w['lru_b_a'][l]),
               blockdiag_expand(w['lru_w_i'][l]), row(w['lru_b_i'][l]), row(w['lru_lambda'][l]), row(w['lru_norm'][l])]
        gdn = [gcw[:, :wg], gcw[:, wg:2 * wg], gcw[:, 2 * wg:], row(w['gdn_a_log'][l]), row(w['gdn_dt_bias'][l]),
               row(w['gdn_norm'][l])]
        ssd = [full[l]['ssd_conv_w'], row(w['ssd_conv_b'][l]), row(w['ssd_a_log'][l]), row(w['ssd_dt_bias'][l]),
               row(w['ssd_d'][l]), row(w['ssd_norm'][l])]
        s5 = list(s5_params_expand(*[w[n][l] for n in ('s5_a_re', 's5_a_im', 's5_log_dt', 's5_b_re', 's5_b_im',
                                                          's5_c_re', 's5_c_im', 's5_d')]))
        post = [full[l]['s5_w_glu'], row(w['s5_norm'][l])]
        return lru, gdn, ssd, s5, post

    lru_fn = functools.partial(lru_chunk, pad)
    gdn_fn = functools.partial(gdn_multi, pad, CHUNK)
    ssd_fn = functools.partial(ssd_multi, pad, 0, SSD_GROUPS, CHUNK)
    s5_fn = functools.partial(s5_chunk, pad)
    post_fn = functools.partial(s5_post, pad)
    n_state_lanes = (S5_LANES // S5_CH) * s5_n

    def mixer_specs(proj):
        lru_seqs = [(proj, wg, col('a_x', wg)), (proj, wg, col('a_gate', wg))]
        gdn_seqs = [(proj, wg, col('b_q', wg)), (proj, wg, col('b_k', wg)), (proj, wg, col('b_v', wg)),
                    (proj, wg, col('b_z', wg)), (proj, LANE, col('small_b', LANE))]
        ssd_seqs = [(proj, wg, col('c_z', wg)), (proj, xbc_w, col('c_xbc', xbc_w)), (proj, LANE, col('small_c', LANE))]
        base = col('d_u', S5_LANES)
        s5_seqs = [(proj, S5_LANES, lambda g: base + g)]
        return lru_seqs, gdn_seqs, ssd_seqs, s5_seqs

    lru_carry = [(CONV_TAIL, wg), (1, wg)]
    gdn_carry = [(CONV_TAIL, wg)] * 3 + [(wg, gdn_hd)]
    ssd_carry = [(CONV_TAIL, xbc_w), (wg, (xbc_w - wg) // (2 * SSD_GROUPS))]
    s5_carry = [(1, n_state_lanes)] * 2
    rk = dict(n_steps=tp // row_tile, rows=row_tile)
    mk = rk
    out_w = (wg, wg, 0, F32)

    h = jnp.concatenate([jnp.zeros((pad, d), F32), meta_full, x[0]], axis=0)
    target_p = jnp.concatenate([jnp.zeros((pad + n_meta, d), F32), target[0]], axis=0)
    saved = []
    for l in range(depth):
        lru_p, gdn_p, ssd_p, s5_p, post_p = layer_params(l)
        h0 = h
        xn1 = rms_fwd(h0, row(w['ffn1_norm'][l]), "rms_fwd", row_tile)
        nxt = l + 1 < depth
        take = lambda *names: [local_of(n, l + 1) for n in names] if nxt else []
        fw = full[l]
        h1 = ffn_fwd(h0, xn1, fw['ffn1_w_gate'], fw['ffn1_w_up'], fw['ffn1_w_down'], "ffn_fwd")
        xn2 = rms_fwd(h1, row(w['mix_norm'][l]), "rms_fwd", row_tile)
        proj = matmul_cols(xn2, w_in_p[l], "mix_in_fwd")
        lru_s, gdn_s, ssd_s, s5_s = mixer_specs(proj)
        got = {}
        names = ('ffn2_w_gate',)
        ya, lru_c, bufs = mixer_fwd(lru_fn, "lru_fwd", seqs=lru_s, params=lru_p, out=out_w, carry=lru_carry,
                                    ag=take(*names), **mk)
        got.update(zip(names, bufs))
        names = ('w_in',)
        yb, gdn_c, bufs = mixer_fwd(gdn_fn, "gdn_fwd", seqs=gdn_s, params=gdn_p, out=out_w, carry=gdn_carry,
                                    ag=take(*names), **mk)
        got.update(zip(names, bufs))
        names = ('w_out', 's5_w_glu', 'lru_conv_w', 'gdn_conv_w', 'ssd_conv_w')
        yc, ssd_c, bufs = mixer_fwd(ssd_fn, "ssd_fwd", seqs=ssd_s, params=ssd_p, out=out_w, carry=ssd_carry,
                                    ag=take(*names), **mk)
        got.update(zip(names, bufs))
        names = ('ffn1_w_gate', 'ffn1_w_up', 'ffn1_w_down')
        y1, s5_c, bufs = mixer_fwd(s5_fn, "s5_fwd", seqs=s5_s, params=s5_p, out=(wg, S5_LANES, lambda g: g, F32),
                                   carry=s5_carry, n_groups=s5_q, ag=take(*names), **mk)
        got.update(zip(names, bufs))
        names = ('ffn2_w_down',)
        yd, _, bufs = mixer_fwd(post_fn, "s5_post_fwd", seqs=[(y1, wg, 0)], params=post_p, out=out_w, carry=[],
                                ag=take(*names), **rk)
        got.update(zip(names, bufs))
        names = ('ffn2_w_up',)
        h2, bufs = mix_out_fwd(h1, [ya, yb, yc, yd], fw['w_out'], "mix_out_fwd", row_tile, ag=take(*names))
        got.update(zip(names, bufs))
        if nxt:
            full[l + 1] = unshard_layer(dict(zip(LAYER_NAMES, ag_second_level([got[n] for n in LAYER_NAMES],
                                                                                "gather_pass_on"))))
            w_in_p[l + 1] = pack_cols(full[l + 1]['w_in'])
        xn3 = rms_fwd(h2, row(w['ffn2_norm'][l]), "rms_fwd", row_tile)
        h3 = ffn_fwd(h2, xn3, fw['ffn2_w_gate'], fw['ffn2_w_up'], fw['ffn2_w_down'], "ffn_fwd")
        saved.append((h0, xn1, h1, xn2, proj, (ya, yb, yc, yd), y1, (lru_c, gdn_c, ssd_c, s5_c), h2, xn3))
        h = h3

    loss_part, dh, d_final = loss_and_grad(h, target_p, row(w['final_norm']), "loss", row_tile, pad + n_meta)
    loss = lax.psum(loss_part, ("x", "y", "c"))

    def shards_of(a, axis):
        sh = a.shape
        return jnp.moveaxis(a.reshape(sh[:axis] + (N_DEV, sh[axis] // N_DEV) + sh[axis + 1:]), axis, 0)

    def to_send(n, g):
        return shards_of(g, SHARD_AXIS[n] - 1).astype(BF16)

    received = {n: [None] * depth for n in SHARD_NAMES if n != 'meta_tokens'}
    pending = []

    def hosted(names_layers):
        keys = [k for k in pending if (k[0], k[1]) in names_layers]
        for k in keys:
            pending.remove(k)
        return [(k[0], k[1]) for k in keys], [k[2] for k in keys]

    def store(keys, arrays):
        for (n, l), a in zip(keys, arrays, strict=True):
            received[n][l] = a

    gw = {n: [None] * depth for n in W_NAMES if n not in ('meta_tokens', 'final_norm')}
    for l in reversed(range(depth)):
        lru_p, gdn_p, ssd_p, s5_p, post_p = layer_params(l)
        h0, xn1, h1, xn2, proj, ys, y1, (lru_c, gdn_c, ssd_c, s5_c), h2, xn3 = saved[l]
        dxn, gw['ffn2_w_gate'][l], gw['ffn2_w_up'][l], gw['ffn2_w_down'][l] = ffn_bwd(
            xn3, dh, full[l]['ffn2_w_gate'], full[l]['ffn2_w_up'], full[l]['ffn2_w_down'], "ffn_bwd")
        pending += [(n, l, to_send(n, gw[n][l])) for n in ('ffn2_w_gate', 'ffn2_w_up', 'ffn2_w_down')]
        dh, dg = rms_bwd_add(h2, dxn, dh, row(w['ffn2_norm'][l]), "rms_bwd", row_tile)
        gw['ffn2_norm'][l] = dg[0]

        dys, (d_wout,) = mixer_bwd(mix_out_delta, "mix_out_bwd", seqs=[(y, wg, 0) for y in ys],
                                   params=[full[l]['w_out']], dout=(dh, d, 0), saved=[], carry=[], **rk)
        gw['w_out'][l] = d_wout[0]
        pending.append(('w_out', l, to_send('w_out', gw['w_out'][l])))
        lru_s, gdn_s, ssd_s, s5_s = mixer_specs(proj)
        (dy1,), d_post = mixer_bwd(post_fn, "s5_post_bwd", seqs=[(y1, wg, 0)], params=post_p, dout=(dys[3], wg, 0),
                                   saved=[], carry=[], **rk)
        gw['s5_w_glu'][l], gw['s5_norm'][l] = d_post[0][0], d_post[1][0, 0]
        pending.append(('s5_w_glu', l, to_send('s5_w_glu', gw['s5_w_glu'][l])))
        keys, arrs = hosted({('ffn1_w_gate', l + 1), ('ffn1_w_up', l + 1), ('ffn1_w_down', l + 1)})
        (d_du,), d_s5, *got = mixer_bwd(s5_fn, "s5_bwd", seqs=s5_s, params=s5_p, dout=(dy1, S5_LANES, lambda g: g),
                                        saved=s5_c, carry=s5_carry, n_groups=s5_q, a2a=arrs, **mk)
        store(keys, got[0] if got else [])
        for n, g in zip(('s5_a_re', 's5_a_im', 's5_log_dt', 's5_b_re', 's5_b_im', 's5_c_re', 's5_c_im', 's5_d'),
                        s5_grads_extract(d_s5, s5_g, s5_n, S5_CH)):
            gw[n][l] = g
        keys, arrs = hosted({('w_in', l + 1), ('w_out', l), ('s5_w_glu', l)})
        (d_cz, d_cxbc, d_sc), d_ssd, *got = mixer_bwd(ssd_fn, "ssd_bwd", seqs=ssd_s, params=ssd_p,
                                                      dout=(dys[2], wg, 0), saved=ssd_c, carry=ssd_carry, a2a=arrs, **mk)
        store(keys, got[0] if got else [])
        for n, g in zip(('ssd_conv_w', 'ssd_conv_b', 'ssd_a_log', 'ssd_dt_bias', 'ssd_d', 'ssd_norm'), d_ssd):
            gw[n][l] = g[0] if n == 'ssd_conv_w' else g[0, 0]
        keys, arrs = hosted({('ffn2_w_gate', l), ('ffn2_w_up', l), ('ffn2_w_down', l)})
        (d_bq, d_bk, d_bv, d_bz, d_sb), d_gdn, *got = mixer_bwd(gdn_fn, "gdn_bwd", seqs=gdn_s, params=gdn_p,
                                                                dout=(dys[1], wg, 0), saved=gdn_c, carry=gdn_carry,
                                                                a2a=arrs, **mk)
        store(keys, got[0] if got else [])
        gw['gdn_conv_w'][l] = jnp.concatenate([d_gdn[0][0], d_gdn[1][0], d_gdn[2][0]], axis=1)
        gw['gdn_a_log'][l], gw['gdn_dt_bias'][l], gw['gdn_norm'][l] = d_gdn[3][0, 0], d_gdn[4][0, 0], d_gdn[5][0, 0]
        (d_ax, d_ag), d_lru = mixer_bwd(lru_fn, "lru_bwd", seqs=lru_s, params=lru_p, dout=(dys[0], wg, 0),
                                        saved=lru_c, carry=lru_carry, **mk)
        gw['lru_conv_w'][l], gw['lru_conv_b'][l] = d_lru[0][0], d_lru[1][0, 0]
        gw['lru_w_a'][l], gw['lru_b_a'][l] = blockdiag_extract(d_lru[2][0], lru_h), d_lru[3][0, 0]
        gw['lru_w_i'][l], gw['lru_b_i'][l] = blockdiag_extract(d_lru[4][0], lru_h), d_lru[5][0, 0]
        gw['lru_lambda'][l], gw['lru_norm'][l] = d_lru[6][0, 0], d_lru[7][0, 0]

        dproj = jnp.concatenate([d_ax, d_ag, d_bq, d_bk, d_bv, d_bz, d_cxbc, d_cz, d_du, d_sb, d_sc], axis=1)
        dxn, d_win_p = matmul_cols_bwd(xn2, dproj, w_in_p[l], "mix_in_bwd")
        gw['w_in'][l] = unpack_cols(d_win_p)
        pending.append(('w_in', l, to_send('w_in', gw['w_in'][l])))
        dh, dg = rms_bwd_add(h1, dxn, dh, row(w['mix_norm'][l]), "rms_bwd", row_tile)
        gw['mix_norm'][l] = dg[0]

        dxn, gw['ffn1_w_gate'][l], gw['ffn1_w_up'][l], gw['ffn1_w_down'][l] = ffn_bwd(
            xn1, dh, full[l]['ffn1_w_gate'], full[l]['ffn1_w_up'], full[l]['ffn1_w_down'], "ffn_bwd")
        dh, dg = rms_bwd_add(h0, dxn, dh, row(w['ffn1_norm'][l]), "rms_bwd", row_tile)
        gw['ffn1_norm'][l] = dg[0]
        pending += [(n, l, to_send(n, gw[n][l])) for n in ('ffn1_w_gate', 'ffn1_w_up', 'ffn1_w_down')]

    grad_x = dh[pad + n_meta:][None]
    grads = {n: jnp.stack(g, axis=0) for n, g in gw.items()}
    grads['meta_tokens'] = dh[pad:pad + n_meta]
    grads['final_norm'] = d_final[0]

    for n in ('lru_conv_w', 'gdn_conv_w', 'ssd_conv_w'):
        pending += [(n, l, to_send(n, gw[n][l])) for l in range(depth)]
    last = [k[2] for k in pending] + [shards_of(grads['meta_tokens'], SHARD_AXIS['meta_tokens']).astype(BF16)]
    recv_rep, *got, recv_meta = all_gather([pack_flat([grads[n] for n in REP_NAMES], F32)], "last_exchange",
                                           with_a2a=last)
    store([(k[0], k[1]) for k in pending], got)
    recv_sh = [recv_meta if n == 'meta_tokens' else jnp.stack(received[n], axis=1) for n in SHARD_NAMES]
    out = {}
    for n, recv in zip(SHARD_NAMES, recv_sh):
        c = w[n].shape[-1]
        res = adamw_reduce(recv.reshape(N_DEV, -1, c), w[n].reshape(-1, c), m_in[n].reshape(-1, c),
                           v_in[n].reshape(-1, c), "adamw_" + n)
        for kind, buf in zip(('grad', 'delta', 'new_m', 'new_v'), res):
            out[kind, n] = buf.reshape(w[n].shape)
    res = adamw_reduce(recv_rep, pack_flat([w[n] for n in REP_NAMES], F32), pack_flat([m_in[n] for n in REP_NAMES], F32),
                       pack_flat([v_in[n] for n in REP_NAMES], F32), "adamw_replicated")
    shapes = [w[n].shape for n in REP_NAMES]
    for kind, buf in zip(('grad', 'delta', 'new_m', 'new_v'), res):
        for n, a in zip(REP_NAMES, unpack_flat(buf, shapes)):
            out[kind, n] = a
    return (loss, grad_x) + tuple(out[k, n] for k in ('grad', 'delta', 'new_m', 'new_v') for n in W_NAMES)
```

```python
import functools
import math

import jax
import jax.numpy as jnp
from jax import lax
from jax.experimental import pallas as pl
from jax.experimental.pallas import tpu as pltpu

F32 = jnp.float32
BF16 = jnp.bfloat16

EPS = 1e-6
CHUNK = 64
CONV_K = 4
CONV_TAIL = 8
LRU_C = 8.0
LANE = 128
SUBLANE = 8
N_DEV = 8
NEG_BIG = -1e30

ADAM_LR = 0.001
ADAM_B1 = 0.9
ADAM_B2 = 0.999
ADAM_EPS = 1e-08
ADAM_WD = 0.01
ADAM_STEP = 10

VMEM_LIMIT = 56 * 1024 * 1024


def _dg(a, b, dims):
    return lax.dot_general(a.astype(BF16), b.astype(BF16), (dims, ((), ())), preferred_element_type=F32)


@jax.custom_vjp
def bdot(a, b):
    return _dg(a, b, ((1,), (0,)))


@jax.custom_vjp
def bdot_nt(a, b):
    return _dg(a, b, ((1,), (1,)))


@jax.custom_vjp
def bdot_tn(a, b):
    return _dg(a, b, ((0,), (0,)))


bdot.defvjp(lambda a, b: (bdot(a, b), (a, b)),
            lambda r, g: (bdot_nt(g, r[1]).astype(r[0].dtype), bdot_tn(r[0], g).astype(r[1].dtype)))
bdot_nt.defvjp(lambda a, b: (bdot_nt(a, b), (a, b)),
               lambda r, g: (bdot(g, r[1]).astype(r[0].dtype), bdot_tn(g, r[0]).astype(r[1].dtype)))
bdot_tn.defvjp(lambda a, b: (bdot_tn(a, b), (a, b)),
               lambda r, g: (bdot_nt(r[1], g).astype(r[0].dtype), bdot(r[0], g).astype(r[1].dtype)))


def _split_bf16(a):
    hi = a.astype(BF16)
    return hi, (a - hi.astype(F32)).astype(BF16)


def _dot3(a, b, dims):
    (ah, al), (bh, bl) = _split_bf16(a), _split_bf16(b)
    d = lambda x, y: lax.dot_general(x, y, (dims, ((), ())), preferred_element_type=F32)
    return d(ah, bh) + (d(ah, bl) + d(al, bh))


@jax.custom_vjp
def hdot(a, b):
    return _dot3(a, b, ((1,), (0,)))


@jax.custom_vjp
def hdot_tn(a, b):
    return _dot3(a, b, ((0,), (0,)))


hdot.defvjp(lambda a, b: (hdot(a, b), (a, b)),
            lambda r, g: (_dot3(g, r[1], ((1,), (1,))), _dot3(r[0], g, ((0,), (0,)))))
hdot_tn.defvjp(lambda a, b: (hdot_tn(a, b), (a, b)),
               lambda r, g: (_dot3(r[1], g, ((1,), (1,))), _dot3(r[0], g, ((1,), (0,)))))


def rms_norm(x, g):
    return x * lax.rsqrt(jnp.mean(x * x, axis=-1, keepdims=True) + EPS) * g


def row_mask(row0, rows, pad):
    r = row0 + lax.broadcasted_iota(jnp.int32, (rows, 1), 0)
    return (r >= pad).astype(F32)


def conv4(tail, u, w):
    rows = u.shape[0]
    xe = jnp.concatenate([tail, u], axis=0)
    y = w[0:1] * xe[CONV_TAIL - 3:CONV_TAIL - 3 + rows]
    for k in range(1, CONV_K):
        y = y + w[k:k + 1] * xe[CONV_TAIL - 3 + k:CONV_TAIL - 3 + k + rows]
    return y


def shift_rows(x, s, fill):
    rows = x.shape[0]
    return jnp.concatenate([jnp.full((s, x.shape[1]), fill, x.dtype), x[:rows - s]], axis=0)


def lin_scan(a, b):
    rows = a.shape[0]
    s = 1
    while s < rows:
        b = a * shift_rows(b, s, 0.0) + b
        a = a * shift_rows(a, s, 1.0)
        s *= 2
    return b


def cscan_const(ar, ai, br, bi):
    rows = br.shape[0]
    s = 1
    while s < rows:
        brs, bis = shift_rows(br, s, 0.0), shift_rows(bi, s, 0.0)
        br, bi = br + ar * brs - ai * bis, bi + ar * bis + ai * brs
        ar, ai = ar * ar - ai * ai, 2.0 * ar * ai
        s *= 2
    return br, bi


def neg_expm1(z):
    t = jnp.tanh(0.5 * z)
    return -2.0 * t / (1.0 - t)


def tri_masks(n):
    r = lax.broadcasted_iota(jnp.int32, (n, n), 0)
    c = lax.broadcasted_iota(jnp.int32, (n, n), 1)
    return r >= c, r > c, (r == c).astype(F32)


def lru_chunk(pad, row0, params, seqs, carry):
    conv_w, conv_b, w_a, b_a, w_i, b_i, lam, norm_g = params
    u_x, u_gate = seqs
    tail, h0 = carry
    rows = u_x.shape[0]
    m = row_mask(row0, rows, pad)
    xc = conv4(tail, u_x, conv_w) + conv_b
    r = jax.nn.sigmoid(bdot(xc, w_a) + b_a)
    ig = jax.nn.sigmoid(bdot(xc, w_i) + b_i)
    log_a = -LRU_C * r * jax.nn.softplus(-lam)
    a = jnp.exp(log_a)
    b = jnp.sqrt(neg_expm1(2.0 * log_a)) * (ig * xc) * m
    first = (lax.broadcasted_iota(jnp.int32, (rows, 1), 0) == 0).astype(F32)
    b = b + first * (a * h0)
    h = lin_scan(a, b)
    y = jax.nn.gelu(u_gate) * h
    out = rms_norm(y, norm_g) * m
    return (out,), (u_x[rows - CONV_TAIL:], h[rows - 1:])


def gdn_multi(pad, sub, row0, params, seqs, carry):
    wq, wk, wv, a_log, dt_bias, norm_g = params
    u_q, u_k, u_v, u_z, small = seqs
    tq, tk, tv, state = carry
    rows = u_q.shape[0]
    hd = norm_g.shape[1]
    nh = u_q.shape[1] // hd
    nc = rows // sub
    m = row_mask(row0, rows, pad)
    incl, strict, eye = tri_masks(sub)
    tril = incl.astype(F32)
    triu = (lax.broadcasted_iota(jnp.int32, (sub, sub), 0) <= lax.broadcasted_iota(jnp.int32, (sub, sub), 1)).astype(F32)
    qc = jax.nn.silu(conv4(tq, u_q, wq))
    kc = jax.nn.silu(conv4(tk, u_k, wk))
    vc = jax.nn.silu(conv4(tv, u_v, wv))
    beta = jax.nn.sigmoid(small[:, :nh]) * m
    g = -jnp.exp(a_log) * jax.nn.softplus(small[:, nh:2 * nh] + dt_bias) * m
    gate = jax.nn.silu(u_z)
    heads = [slice(h * hd, (h + 1) * hd) for h in range(nh)]
    q_h = [qc[:, sl] for sl in heads]
    k_h = [kc[:, sl] for sl in heads]
    q_h = [q * lax.rsqrt(jnp.sum(q * q, axis=-1, keepdims=True) + EPS) * (hd ** -0.5) * m for q in q_h]
    k_h = [k * lax.rsqrt(jnp.sum(k * k, axis=-1, keepdims=True) + EPS) * m for k in k_h]
    v_h = [vc[:, sl] * m for sl in heads]
    pairs = [(c, h) for c in range(nc) for h in range(nh)]
    cs = lambda x, c: x[c * sub:(c + 1) * sub]
    q = {(c, h): cs(q_h[h], c) for c, h in pairs}
    k = {(c, h): cs(k_h[h], c) for c, h in pairs}
    v = {(c, h): cs(v_h[h], c) for c, h in pairs}
    bt = {(c, h): cs(beta, c)[:, h:h + 1] for c, h in pairs}
    gcs = [hdot(tril, cs(g, c)) for c in range(nc)]
    gts = [hdot_tn(cs(g, c), triu) for c in range(nc)]
    gc = {(c, h): gcs[c][:, h:h + 1] for c, h in pairs}
    decay = {(c, h): jnp.exp(jnp.where(incl, gc[c, h] - gts[c][h:h + 1], NEG_BIG)) for c, h in pairs}
    kb = {p: k[p] * bt[p] for p in pairs}
    kk = {p: bdot_nt(kb[p], k[p]) for p in pairs}
    lmat = {p: jnp.where(strict, kk[p] * decay[p], 0.0) for p in pairs}
    pm = {p: eye - lmat[p] for p in pairs}
    mm = {p: hdot(lmat[p], lmat[p]) for p in pairs}
    s = 2
    while s < sub:
        pm = {p: pm[p] + hdot(pm[p], mm[p]) for p in pairs}
        s *= 2
        if s < sub:
            mm = {p: hdot(mm[p], mm[p]) for p in pairs}
    eg = {p: jnp.exp(gc[p]) for p in pairs}
    u = {p: hdot(pm[p], v[p] * bt[p]) for p in pairs}
    w = {p: hdot(pm[p], kb[p] * eg[p]) for p in pairs}
    attn = {p: bdot_nt(q[p], k[p]) * decay[p] for p in pairs}
    qd = {p: q[p] * eg[p] for p in pairs}
    g_last = {p: gc[p][sub - 1:] for p in pairs}
    kd = {p: k[p] * jnp.exp(g_last[p] - gc[p]) for p in pairs}
    last = {p: jnp.exp(g_last[p]) for p in pairs}
    s_h = [state[sl] for sl in heads]
    o = {}
    for c in range(nc):
        ws = [bdot(w[c, h], s_h[h]) for h in range(nh)]
        qs = [bdot(qd[c, h], s_h[h]) for h in range(nh)]
        v_new = [u[c, h] - ws[h] for h in range(nh)]
        av = [bdot(attn[c, h], v_new[h]) for h in range(nh)]
        kv = [bdot_tn(kd[c, h], v_new[h]) for h in range(nh)]
        for h in range(nh):
            o[c, h] = qs[h] + av[h]
        s_h = [s_h[h] * last[c, h] + kv[h] for h in range(nh)]
    out = jnp.concatenate([jnp.concatenate([rms_norm(o[c, h], norm_g) for h in range(nh)], axis=1)
                           for c in range(nc)], axis=0) * gate * m
    t0 = rows - CONV_TAIL
    return (out,), (u_q[t0:], u_k[t0:], u_v[t0:], jnp.concatenate(s_h, axis=0))


def ssd_multi(pad, dt_lane0, n_groups, sub, row0, params, seqs, carry):
    conv_w, conv_b, a_log, dt_bias, d_skip, norm_g = params
    u_z, u_xbc, small = seqs
    tail, state = carry
    rows = u_z.shape[0]
    width = u_z.shape[1]
    nh = a_log.shape[1]
    hd = width // nh
    ns = (u_xbc.shape[1] - width) // (2 * n_groups)
    hpg = nh // n_groups
    nc = rows // sub
    m = row_mask(row0, rows, pad)
    incl, _, _ = tri_masks(sub)
    tril = incl.astype(F32)
    triu = (lax.broadcasted_iota(jnp.int32, (sub, sub), 0) <= lax.broadcasted_iota(jnp.int32, (sub, sub), 1)).astype(F32)
    xbc = jax.nn.silu(conv4(tail, u_xbc, conv_w) + conv_b)
    xs = xbc[:, :width]
    dt = jax.nn.softplus(small[:, dt_lane0:dt_lane0 + nh] + dt_bias)
    a_all = dt * (-jnp.exp(a_log)) * m
    cs = lambda x, c: x[c * sub:(c + 1) * sub]
    heads = [slice(h * hd, (h + 1) * hd) for h in range(nh)]
    pairs = [(c, h) for c in range(nc) for h in range(nh)]
    grp = lambda h: h // hpg
    bm = {(c, g): cs(xbc[:, width + g * ns: width + (g + 1) * ns] * m, c) for c in range(nc) for g in range(n_groups)}
    cm = {(c, g): cs(xbc[:, width + (n_groups + g) * ns: width + (n_groups + g + 1) * ns] * m, c)
          for c in range(nc) for g in range(n_groups)}
    xh = {(c, h): cs(xs[:, heads[h]], c) for c, h in pairs}
    xdt = {(c, h): xh[c, h] * cs(dt[:, h:h + 1] * m, c) for c, h in pairs}
    acums = [hdot(tril, cs(a_all, c)) for c in range(nc)]
    acts = [hdot_tn(cs(a_all, c), triu) for c in range(nc)]
    acum = {(c, h): acums[c][:, h:h + 1] for c, h in pairs}
    a_last = {p: acum[p][sub - 1:] for p in pairs}
    lmat = {(c, h): jnp.exp(jnp.where(incl, acum[c, h] - acts[c][h:h + 1], NEG_BIG)) for c, h in pairs}
    cb = {cg: bdot_nt(cm[cg], bm[cg]) for cg in bm}
    y_diag = {(c, h): bdot(cb[c, grp(h)] * lmat[c, h], xdt[c, h]) for c, h in pairs}
    st = {(c, h): bdot_tn(xdt[c, h] * jnp.exp(a_last[c, h] - acum[c, h]), bm[c, grp(h)]) for c, h in pairs}
    e_in = {p: jnp.exp(acum[p]) for p in pairs}
    e_out = {p: jnp.exp(a_last[p]) for p in pairs}
    s_h = [state[sl] for sl in heads]
    y = {}
    for c in range(nc):
        off = [bdot_nt(cm[c, grp(h)], s_h[h]) for h in range(nh)]
        for h in range(nh):
            y[c, h] = y_diag[c, h] + off[h] * e_in[c, h] + d_skip[:, h:h + 1] * xh[c, h]
        s_h = [s_h[h] * e_out[c, h] + st[c, h] for h in range(nh)]
    yy = jnp.concatenate([jnp.concatenate([y[c, h] for h in range(nh)], axis=1) for c in range(nc)], axis=0)
    yy = yy * jax.nn.silu(u_z)
    gw = width // n_groups
    outs = [rms_norm(yy[:, g * gw:(g + 1) * gw], norm_g[:, g * gw:(g + 1) * gw]) for g in range(n_groups)]
    out = jnp.concatenate(outs, axis=1) * m
    return (out,), (u_xbc[rows - CONV_TAIL:], jnp.concatenate(s_h, axis=0))


def s5_chunk(pad, row0, params, seqs, carry):
    a_re, a_im, log_dt, b_re, b_im, c_re, c_im, d_skip = params
    (u,) = seqs
    s_re0, s_im0 = carry
    rows = u.shape[0]
    n_state = a_re.shape[1]
    n_grp = log_dt.shape[1]
    per = n_state // n_grp
    expand = (lax.broadcasted_iota(jnp.int32, (n_grp, n_state), 1) // per
              == lax.broadcasted_iota(jnp.int32, (n_grp, n_state), 0)).astype(F32)
    dt = jnp.exp(hdot(log_dt, expand))
    lam_re = jnp.minimum(a_re, -1e-4)
    lam_im = a_im
    mag = jnp.exp(dt * lam_re)
    ab_re = mag * jnp.cos(dt * lam_im)
    ab_im = mag * jnp.sin(dt * lam_im)
    den = lam_re * lam_re + lam_im * lam_im
    f_re = ((ab_re - 1.0) * lam_re + ab_im * lam_im) / den
    f_im = (ab_im * lam_re - (ab_re - 1.0) * lam_im) / den
    bb_re = f_re * b_re - f_im * b_im
    bb_im = f_re * b_im + f_im * b_re
    bu_re = bdot(u, bb_re)
    bu_im = bdot(u, bb_im)
    first = (lax.broadcasted_iota(jnp.int32, (rows, 1), 0) == 0).astype(F32)
    bu_re = bu_re + first * (ab_re * s_re0 - ab_im * s_im0)
    bu_im = bu_im + first * (ab_re * s_im0 + ab_im * s_re0)
    s_re, s_im = cscan_const(ab_re, ab_im, bu_re, bu_im)
    y = bdot(s_re, c_re) - bdot(s_im, c_im) + d_skip * u
    return (y,), (s_re[rows - 1:], s_im[rows - 1:])


def s5_post(pad, row0, params, seqs, carry):
    w_glu, norm_g = params
    (y,) = seqs
    y = jax.nn.gelu(y)
    y = y * jax.nn.sigmoid(bdot(y, w_glu))
    return (rms_norm(y, norm_g),), ()


def mix_out_delta(row0, params, seqs, carry):
    (w_out,) = params
    wd = seqs[0].shape[1]
    acc = bdot(seqs[0], w_out[0:wd])
    for k in range(1, len(seqs)):
        acc = acc + bdot(seqs[k], w_out[k * wd:(k + 1) * wd])
    return (acc,), ()


def blockdiag_expand(w):
    nh, a, b = w.shape
    eye = jnp.eye(nh, dtype=w.dtype)
    return (w[:, :, None, :] * eye[:, None, :, None]).reshape(nh * a, nh * b)


def blockdiag_extract(m, nh):
    a, b = m.shape[0] // nh, m.shape[1] // nh
    on_diag = jnp.eye(nh, dtype=bool)[:, None, :, None]
    return jnp.sum(jnp.where(on_diag, m.reshape(nh, a, nh, b), 0.0), axis=2)


S5_LANES = LANE


def s5_params_expand(a_re, a_im, log_dt, b_re, b_im, c_re, c_im, d_skip):
    n_grp, n_state = a_re.shape
    ch = b_re.shape[-1]
    gpl = S5_LANES // ch
    nq = n_grp // gpl
    eye = jnp.eye(gpl, dtype=F32)[None, :, None, :, None]

    def bexp(b):
        bt = jnp.swapaxes(b, 1, 2).reshape(nq, gpl, b.shape[2], 1, b.shape[1])
        return (bt * eye).reshape(nq, gpl * b.shape[2], gpl * b.shape[1])

    return (a_re.reshape(nq, 1, gpl * n_state), a_im.reshape(nq, 1, gpl * n_state), log_dt.reshape(nq, 1, gpl),
            bexp(b_re), bexp(b_im), bexp(c_re), bexp(c_im), d_skip.reshape(nq, 1, S5_LANES))


def s5_grads_extract(grads, n_grp, n_state, ch):
    da_re, da_im, dlog_dt, db_re, db_im, dc_re, dc_im, dd = grads
    gpl = S5_LANES // ch
    nq = n_grp // gpl
    on_diag = jnp.eye(gpl, dtype=bool)[None, :, None, :, None]

    def bext(b):
        r, c = b.shape[1] // gpl, b.shape[2] // gpl
        d = jnp.sum(jnp.where(on_diag, b.reshape(nq, gpl, r, gpl, c), 0.0), axis=3)
        return jnp.swapaxes(d.reshape(n_grp, r, c), 1, 2)

    return (da_re.reshape(n_grp, n_state), da_im.reshape(n_grp, n_state), dlog_dt.reshape(n_grp),
            bext(db_re), bext(db_im), bext(dc_re), bext(dc_im), dd.reshape(n_grp * ch))


def _cparams(**kw):
    return pltpu.CompilerParams(vmem_limit_bytes=VMEM_LIMIT, **kw)


def tiled_call(body_fn, name, *, n_steps, rows, n_groups=1, reverse=False,
               seq_in=(), whole_in=(), step_in=(), seq_out=(), acc_out=(), step_out=(), carry=(), a2a=(), ag=()):
    def step_of(i):
        return (n_steps - 1 - i) if reverse else i

    def col_of(col, g):
        return col(g) if callable(col) else col

    in_specs, operands = [], []
    for arr, width, col in seq_in:
        in_specs.append(pl.BlockSpec((rows, width), lambda g, i, col=col: (step_of(i), col_of(col, g))))
        operands.append(arr)
    for arr in whole_in:
        if arr.ndim == 2:
            in_specs.append(pl.BlockSpec(arr.shape, lambda g, i: (0, 0)))
        else:
            in_specs.append(pl.BlockSpec((None,) + arr.shape[1:], lambda g, i: (g, 0, 0)))
        operands.append(arr)
    for arr in step_in:
        in_specs.append(pl.BlockSpec((None, None) + arr.shape[2:], lambda g, i: (g, step_of(i), 0, 0)))
        operands.append(arr)
    out_shape, out_specs = [], []
    for total, width, col, dt in seq_out:
        out_shape.append(jax.ShapeDtypeStruct((n_steps * rows, total), dt))
        out_specs.append(pl.BlockSpec((rows, width), lambda g, i, col=col: (step_of(i), col_of(col, g))))
    for r, c in acc_out:
        out_shape.append(jax.ShapeDtypeStruct((n_groups, r, c), F32))
        out_specs.append(pl.BlockSpec((None, r, c), lambda g, i: (g, 0, 0)))
    for r, c in step_out:
        out_shape.append(jax.ShapeDtypeStruct((n_groups, n_steps, r, c), F32))
        out_specs.append(pl.BlockSpec((None, None, r, c), lambda g, i: (g, step_of(i), 0, 0)))
    n_seq, n_whole, n_step = len(seq_in), len(whole_in), len(step_in)
    n_so, n_ao, n_sto = len(seq_out), len(acc_out), len(step_out)
    assert not (a2a and ag)
    hosted = list(a2a) + list(ag)
    n_x = len(hosted)
    n_sem = AG_FIRST_COPIES if ag else N_DEV - 1
    hbm = pl.BlockSpec(memory_space=pl.ANY)
    in_specs += [hbm] * n_x
    operands += hosted
    out_specs += [hbm] * n_x
    out_shape += [jax.ShapeDtypeStruct((N_DEV,) + x.shape if ag else x.shape, x.dtype) for x in hosted]
    scratch = [pltpu.VMEM((r, c), F32) for r, c in carry]
    if n_x:
        scratch += [pltpu.SemaphoreType.DMA((n_x, n_sem)), pltpu.SemaphoreType.DMA((n_x, n_sem)),
                    pltpu.SemaphoreType.DMA((n_x,))]

    def body(*refs):
        pos = 0
        seq_refs = refs[pos:pos + n_seq]
        pos += n_seq
        whole_refs = refs[pos:pos + n_whole]
        pos += n_whole
        step_refs = refs[pos:pos + n_step]
        pos += n_step + n_x
        so_refs = refs[pos:pos + n_so]
        pos += n_so
        ao_refs = refs[pos:pos + n_ao]
        pos += n_ao
        sto_refs = refs[pos:pos + n_sto]
        pos += n_sto
        xo_refs = refs[pos:pos + n_x]
        pos += n_x
        carry_refs = refs[pos:pos + len(carry)]
        pos += len(carry)
        x_refs = refs[n_seq + n_whole + n_step:n_seq + n_whole + n_step + n_x]
        g, i = pl.program_id(0), pl.program_id(1)
        if n_x:
            locals_, sends, recvs = (ag_first_copies if ag else a2a_copies)(x_refs, xo_refs, *refs[pos:])

            @pl.when((g == 0) & (i == 0))
            def _():
                for cp in locals_ + sends:
                    cp.start()

        @pl.when(i == 0)
        def _():
            for r in carry_refs:
                r[...] = jnp.zeros(r.shape, r.dtype)
            for r in ao_refs:
                r[...] = jnp.zeros(r.shape, r.dtype)

        row0 = step_of(i) * rows
        seq_o, acc_o, step_o, new_c = body_fn(row0, [r[...] for r in whole_refs], [r[...] for r in seq_refs],
                                              [r[...] for r in step_refs], [r[...] for r in carry_refs])
        for r, val in zip(so_refs, seq_o, strict=True):
            r[...] = val.astype(r.dtype)
        for r, val in zip(ao_refs, acc_o, strict=True):
            r[...] += val
        for r, val in zip(sto_refs, step_o, strict=True):
            r[...] = val
        for r, val in zip(carry_refs, new_c, strict=True):
            r[...] = val
        if n_x:
            @pl.when((g == n_groups - 1) & (i == n_steps - 1))
            def _():
                for cp in recvs:
                    cp.wait_recv()
                for cp in sends:
                    cp.wait_send()
                for cp in locals_:
                    cp.wait()

    return pl.pallas_call(
        body, name=name, grid=(n_groups, n_steps), in_specs=in_specs, out_specs=out_specs, out_shape=out_shape,
        scratch_shapes=scratch,
        compiler_params=_cparams(dimension_semantics=("arbitrary", "arbitrary")),
    )(*operands)


def mixer_fwd(fn, name, *, n_steps, rows, seqs, params, out, carry, n_groups=1, ag=()):
    def body(row0, whole, seq_vals, steps, carry_vals):
        outs, new_c = fn(row0, tuple(whole), tuple(seq_vals), tuple(carry_vals))
        return list(outs), [], list(carry_vals), list(new_c)

    res = tiled_call(body, name, n_steps=n_steps, rows=rows, n_groups=n_groups, seq_in=seqs, whole_in=params,
                     seq_out=[out], step_out=carry, carry=carry, ag=ag)
    return res[0], list(res[1:1 + len(carry)]), list(res[1 + len(carry):])


def mixer_bwd(fn, name, *, n_steps, rows, seqs, params, dout, saved, carry, n_groups=1, a2a=()):
    n_seq = len(seqs)

    def body(row0, whole, seq_vals, steps, dcarry):
        params_f = tuple(p.astype(F32) for p in whole)
        _, vjp = jax.vjp(lambda p, s, c: fn(row0, p, s, c), params_f, tuple(seq_vals[:n_seq]), tuple(steps))
        dp, ds, dc = vjp(((seq_vals[n_seq],), tuple(dcarry)))
        return list(ds), list(dp), [], list(dc)

    seq_out = [(n_groups * w, w, (lambda g: g), F32) if callable(c) else (w, w, 0, F32) for a, w, c in seqs]
    acc_out = [p.shape[-2:] for p in params]
    res = tiled_call(body, name, n_steps=n_steps, rows=rows, n_groups=n_groups, reverse=True,
                     seq_in=list(seqs) + [dout], whole_in=params, step_in=saved,
                     seq_out=seq_out, acc_out=acc_out, carry=carry, a2a=a2a)
    n_p = len(params)
    if a2a:
        return list(res[:n_seq]), list(res[n_seq:n_seq + n_p]), list(res[n_seq + n_p:])
    return list(res[:n_seq]), list(res[n_seq:])


def rms_fwd(h, g, name, rows):
    def body(row0, whole, seqs, steps, carry):
        return [rms_norm(seqs[0], whole[0])], [], [], []
    d = h.shape[1]
    return tiled_call(body, name, n_steps=h.shape[0] // rows, rows=rows, seq_in=[(h, d, 0)], whole_in=[g],
                      seq_out=[(d, d, 0, BF16)])[0]


def rms_bwd_add(h, dxn, dh_out, g, name, rows):
    def body(row0, whole, seqs, steps, carry):
        _, vjp = jax.vjp(rms_norm, seqs[0], whole[0])
        dh, dg = vjp(seqs[1])
        return [seqs[2] + dh], [dg], [], []
    d = h.shape[1]
    dh_in, dg = tiled_call(body, name, n_steps=h.shape[0] // rows, rows=rows,
                           seq_in=[(h, d, 0), (dxn, d, 0), (dh_out, d, 0)], whole_in=[g],
                           seq_out=[(d, d, 0, F32)], acc_out=[(1, d)])
    return dh_in, dg[0]


def mix_out_fwd(h, ys, w_out, name, rows, ag=()):
    def body(row0, whole, seqs, steps, carry):
        (delta,), _ = mix_out_delta(row0, (whole[0],), tuple(seqs[1:]), ())
        return [seqs[0] + delta], [], [], []
    d, wd = h.shape[1], ys[0].shape[1]
    res = tiled_call(body, name, n_steps=h.shape[0] // rows, rows=rows,
                     seq_in=[(h, d, 0)] + [(y, wd, 0) for y in ys], whole_in=[w_out], seq_out=[(d, d, 0, F32)], ag=ag)
    return res[0], list(res[1:])


def loss_and_grad(h, target, g, name, rows, first_row):
    def body(row0, whole, seqs, steps, carry):
        hh, tt = seqs
        keep = row_mask(row0, hh.shape[0], first_row)

        def f(hv, gv):
            err = rms_norm(hv, gv) - tt
            return 0.5 * jnp.sum(jnp.mean(err * err, axis=-1, keepdims=True) * keep, axis=0, keepdims=True)

        val, vjp = jax.vjp(f, hh, whole[0])
        dh, dg = vjp(jnp.ones((1, 1), F32))
        return [dh], [jnp.broadcast_to(val, (1, LANE)), dg], [], []
    d = h.shape[1]
    dh, loss, dg = tiled_call(body, name, n_steps=h.shape[0] // rows, rows=rows,
                              seq_in=[(h, d, 0), (target, d, 0)], whole_in=[g],
                              seq_out=[(d, d, 0, F32)], acc_out=[(1, LANE), (1, d)])
    return loss[0, 0, 0], dh, dg[0]


def _pick(n, cands):
    for c in cands:
        if n % c == 0:
            return c
    raise ValueError(f"no tile for {n}")


ROW_TILES = (1056, 704, 352, 192, 96, 64)
COL_TILES = (256, 128)
NT_DIMS = (((1,), (1,)), ((), ()))
TN_DIMS = (((0,), (0,)), ((), ()))


def ffn_fwd(h, xn, wg, wu, wd, name):
    t, d = h.shape
    f = wg.shape[1]
    tm = _pick(t, ROW_TILES)
    tn = _pick(f, COL_TILES)
    n_j = f // tn

    def body(h_ref, xn_ref, wg_ref, wu_ref, wd_ref, o_ref, acc_ref):
        j = pl.program_id(1)

        @pl.when(j == 0)
        def _():
            acc_ref[...] = jnp.zeros(acc_ref.shape, F32)

        x = xn_ref[...]
        g = jnp.dot(x, wg_ref[...], preferred_element_type=F32)
        u = jnp.dot(x, wu_ref[...], preferred_element_type=F32)
        a = (jax.nn.silu(g) * u).astype(BF16)
        acc_ref[...] += jnp.dot(a, wd_ref[...], preferred_element_type=F32)

        @pl.when(j == n_j - 1)
        def _():
            o_ref[...] = h_ref[...] + 0.5 * acc_ref[...]

    return pl.pallas_call(
        body, name=name, grid=(t // tm, n_j),
        in_specs=[pl.BlockSpec((tm, d), lambda i, j: (i, 0)), pl.BlockSpec((tm, d), lambda i, j: (i, 0)),
                  pl.BlockSpec((d, tn), lambda i, j: (0, j)), pl.BlockSpec((d, tn), lambda i, j: (0, j)),
                  pl.BlockSpec((tn, d), lambda i, j: (j, 0))],
        out_specs=pl.BlockSpec((tm, d), lambda i, j: (i, 0)),
        out_shape=jax.ShapeDtypeStruct((t, d), F32),
        scratch_shapes=[pltpu.VMEM((tm, d), F32)],
        compiler_params=_cparams(dimension_semantics=("arbitrary", "arbitrary")),
    )(h, xn, wg, wu, wd)


def ffn_bwd(xn, dh, wg, wu, wd, name):
    t, d = dh.shape
    f = wg.shape[1]
    tm = _pick(t, ROW_TILES)
    tn = _pick(f, COL_TILES)

    def body(xn_ref, dh_ref, wg_ref, wu_ref, wd_ref, dxn_ref, dwg_ref, dwu_ref, dwd_ref):
        j, i = pl.program_id(0), pl.program_id(1)
        rows = pl.ds(pl.multiple_of(i * tm, 8), tm)
        x = xn_ref[rows, :]
        dhh = (0.5 * dh_ref[...]).astype(BF16)
        wgv, wuv = wg_ref[...], wu_ref[...]
        g = jnp.dot(x, wgv, preferred_element_type=F32)
        u = jnp.dot(x, wuv, preferred_element_type=F32)
        sg = jax.nn.sigmoid(g)
        s = g * sg
        da = lax.dot_general(dhh, wd_ref[...], NT_DIMS, preferred_element_type=F32)
        dwd = lax.dot_general((s * u).astype(BF16), dhh, TN_DIMS, preferred_element_type=F32)
        dg = (da * u * (sg * (1.0 + g * (1.0 - sg)))).astype(BF16)
        du = (da * s).astype(BF16)
        dwg = lax.dot_general(x, dg, TN_DIMS, preferred_element_type=F32)
        dwu = lax.dot_general(x, du, TN_DIMS, preferred_element_type=F32)
        dx = (lax.dot_general(dg, wgv, NT_DIMS, preferred_element_type=F32)
              + lax.dot_general(du, wuv, NT_DIMS, preferred_element_type=F32))

        @pl.when(i == 0)
        def _():
            dwg_ref[...] = dwg
            dwu_ref[...] = dwu
            dwd_ref[...] = dwd

        @pl.when(i > 0)
        def _():
            dwg_ref[...] += dwg
            dwu_ref[...] += dwu
            dwd_ref[...] += dwd

        @pl.when(j == 0)
        def _():
            dxn_ref[rows, :] = dx

        @pl.when(j > 0)
        def _():
            dxn_ref[rows, :] += dx

    return pl.pallas_call(
        body, name=name, grid=(f // tn, t // tm),
        in_specs=[pl.BlockSpec((t, d), lambda j, i: (0, 0)), pl.BlockSpec((tm, d), lambda j, i: (i, 0)),
                  pl.BlockSpec((d, tn), lambda j, i: (0, j)), pl.BlockSpec((d, tn), lambda j, i: (0, j)),
                  pl.BlockSpec((tn, d), lambda j, i: (j, 0))],
        out_specs=[pl.BlockSpec((t, d), lambda j, i: (0, 0)), pl.BlockSpec((d, tn), lambda j, i: (0, j)),
                   pl.BlockSpec((d, tn), lambda j, i: (0, j)), pl.BlockSpec((tn, d), lambda j, i: (j, 0))],
        out_shape=[jax.ShapeDtypeStruct((t, d), F32), jax.ShapeDtypeStruct((d, f), F32),
                   jax.ShapeDtypeStruct((d, f), F32), jax.ShapeDtypeStruct((f, d), F32)],
        compiler_params=_cparams(dimension_semantics=("arbitrary", "arbitrary")),
    )(xn, dh, wg, wu, wd)


def matmul_cols(xn, w, name):
    t, d = xn.shape
    n = w.shape[1]
    tn = _pick(n, COL_TILES)

    def body(x_ref, w_ref, o_ref):
        o_ref[...] = jnp.dot(x_ref[...], w_ref[...], preferred_element_type=F32)

    return pl.pallas_call(
        body, name=name, grid=(n // tn,),
        in_specs=[pl.BlockSpec((t, d), lambda j: (0, 0)), pl.BlockSpec((d, tn), lambda j: (0, j))],
        out_specs=pl.BlockSpec((t, tn), lambda j: (0, j)),
        out_shape=jax.ShapeDtypeStruct((t, n), F32),
        compiler_params=_cparams(dimension_semantics=("arbitrary",)),
    )(xn, w)


def matmul_cols_bwd(xn, dy, w, name):
    t, d = xn.shape
    n = w.shape[1]
    tn = _pick(n, COL_TILES)

    def body(x_ref, dy_ref, w_ref, dx_ref, dw_ref):
        j = pl.program_id(0)
        dyv = dy_ref[...].astype(BF16)
        dw_ref[...] = lax.dot_general(x_ref[...], dyv, TN_DIMS, preferred_element_type=F32)
        dx = lax.dot_general(dyv, w_ref[...], NT_DIMS, preferred_element_type=F32)

        @pl.when(j == 0)
        def _():
            dx_ref[...] = dx

        @pl.when(j > 0)
        def _():
            dx_ref[...] += dx

    return pl.pallas_call(
        body, name=name, grid=(n // tn,),
        in_specs=[pl.BlockSpec((t, d), lambda j: (0, 0)), pl.BlockSpec((t, tn), lambda j: (0, j)),
                  pl.BlockSpec((d, tn), lambda j: (0, j))],
        out_specs=[pl.BlockSpec((t, d), lambda j: (0, 0)), pl.BlockSpec((d, tn), lambda j: (0, j))],
        out_shape=[jax.ShapeDtypeStruct((t, d), F32), jax.ShapeDtypeStruct((d, n), F32)],
        compiler_params=_cparams(dimension_semantics=("arbitrary",)),
    )(xn, dy, w)


def _peer(mx, my, mc, k):
    px = 1 - mx if (k >> 2) & 1 else mx
    py = 1 - my if (k >> 1) & 1 else my
    pc = 1 - mc if k & 1 else mc
    return (px, py, pc), 4 * px + 2 * py + pc


def a2a_copies(x_refs, o_refs, send_sems, recv_sems, local_sems):
    mx, my, mc = lax.axis_index("x"), lax.axis_index("y"), lax.axis_index("c")
    me = 4 * mx + 2 * my + mc
    peers = [_peer(mx, my, mc, k) for k in range(1, N_DEV)]
    locals_, sends, recvs = [], [], []
    for a, (x_ref, o_ref) in enumerate(zip(x_refs, o_refs, strict=True)):
        locals_.append(pltpu.make_async_copy(x_ref.at[me], o_ref.at[me], local_sems.at[a]))
        for k, (dev, peer) in enumerate(peers):
            common = dict(send_sem=send_sems.at[a, k], recv_sem=recv_sems.at[a, k], device_id=dev,
                          device_id_type=pl.DeviceIdType.MESH)
            sends.append(pltpu.make_async_remote_copy(src_ref=x_ref.at[peer], dst_ref=o_ref.at[me], **common))
            recvs.append(pltpu.make_async_remote_copy(src_ref=x_ref.at[peer], dst_ref=o_ref.at[peer], **common))
    return locals_, sends, recvs


def all_gather(xs, name, with_a2a=()):
    n, n2 = len(xs), len(with_a2a)
    chip_flips = (4, 2, 6)

    def body(*refs):
        x_refs, t_refs = refs[:n], refs[n:n + n2]
        o_refs, r_refs = refs[n + n2:2 * n + n2], refs[2 * n + n2:2 * n + 2 * n2]
        send_sems, recv_sems, local_sems = refs[2 * n + 2 * n2:2 * n + 2 * n2 + 3]
        mx, my, mc = lax.axis_index("x"), lax.axis_index("y"), lax.axis_index("c")
        me = 4 * mx + 2 * my + mc
        sib_dev, sib = _peer(mx, my, mc, 1)

        def copy(a, k, row, to, src=None):
            return pltpu.make_async_remote_copy(
                src_ref=o_refs[a].at[row] if src is None else src, dst_ref=o_refs[a].at[row],
                send_sem=send_sems.at[a, k], recv_sem=recv_sems.at[a, k], device_id=to,
                device_id_type=pl.DeviceIdType.MESH)

        locals_, first, passed = [], [], []
        t_recvs = []
        if n2:
            t_locals, t_sends, t_recvs = a2a_copies(t_refs, r_refs, *refs[2 * n + 2 * n2 + 3:])
            locals_ += t_locals
            first += t_sends
        for a in range(n):
            locals_.append(pltpu.make_async_copy(x_refs[a], o_refs[a].at[me], local_sems.at[a]))
            first.append(copy(a, 0, me, sib_dev, src=x_refs[a]))
            for j, f in enumerate(chip_flips):
                first.append(copy(a, 1 + j, me, _peer(mx, my, mc, f)[0], src=x_refs[a]))
        for cp in locals_ + first:
            cp.start()
        for a in range(n):
            for j, f in enumerate(chip_flips):
                row = _peer(mx, my, mc, f)[1]
                copy(a, 1 + j, row, sib_dev).wait_recv()
                fwd = copy(a, 4 + j, row, sib_dev)
                fwd.start()
                passed.append(fwd)
        for a in range(n):
            copy(a, 0, sib, sib_dev).wait_recv()
            for j, f in enumerate(chip_flips):
                copy(a, 4 + j, _peer(mx, my, mc, f ^ 1)[1], sib_dev).wait_recv()
        for cp in t_recvs:
            cp.wait_recv()
        for cp in first + passed:
            cp.wait_send()
        for cp in locals_:
            cp.wait()

    hbm = pl.BlockSpec(memory_space=pl.ANY)
    scratch = [pltpu.SemaphoreType.DMA((n, N_DEV - 1)), pltpu.SemaphoreType.DMA((n, N_DEV - 1)),
               pltpu.SemaphoreType.DMA((n,))]
    if n2:
        scratch += [pltpu.SemaphoreType.DMA((n2, N_DEV - 1)), pltpu.SemaphoreType.DMA((n2, N_DEV - 1)),
                    pltpu.SemaphoreType.DMA((n2,))]
    return pl.pallas_call(
        body, name=name, in_specs=[hbm] * (n + n2), out_specs=[hbm] * (n + n2),
        out_shape=[jax.ShapeDtypeStruct((N_DEV,) + x.shape, x.dtype) for x in xs]
        + [jax.ShapeDtypeStruct(x.shape, x.dtype) for x in with_a2a],
        scratch_shapes=scratch,
    )(*xs, *with_a2a)


AG_FIRST_COPIES = 4


def ag_first_copies(x_refs, o_refs, send_sems, recv_sems, local_sems):
    mx, my, mc = lax.axis_index("x"), lax.axis_index("y"), lax.axis_index("c")
    me = 4 * mx + 2 * my + mc
    targets = [_peer(mx, my, mc, f) for f in (1, 4, 2, 6)]
    locals_, sends, recvs = [], [], []
    for a, (x_ref, o_ref) in enumerate(zip(x_refs, o_refs, strict=True)):
        locals_.append(pltpu.make_async_copy(x_ref, o_ref.at[me], local_sems.at[a]))
        for k, (dev, row) in enumerate(targets):
            common = dict(send_sem=send_sems.at[a, k], recv_sem=recv_sems.at[a, k], device_id=dev,
                          device_id_type=pl.DeviceIdType.MESH)
            sends.append(pltpu.make_async_remote_copy(src_ref=x_ref, dst_ref=o_ref.at[me], **common))
            recvs.append(pltpu.make_async_remote_copy(src_ref=x_ref, dst_ref=o_ref.at[row], **common))
    return locals_, sends, recvs


def ag_second_level(bufs, name):
    n = len(bufs)
    chip_flips = (4, 2, 6)

    def body(*refs):
        o_refs = refs[n:2 * n]
        send_sems, recv_sems = refs[2 * n:]
        mx, my, mc = lax.axis_index("x"), lax.axis_index("y"), lax.axis_index("c")
        sib_dev, _ = _peer(mx, my, mc, 1)
        sends, recvs = [], []
        for a in range(n):
            for j, f in enumerate(chip_flips):
                common = dict(send_sem=send_sems.at[a, j], recv_sem=recv_sems.at[a, j], device_id=sib_dev,
                              device_id_type=pl.DeviceIdType.MESH)
                row, sib_row = _peer(mx, my, mc, f)[1], _peer(mx, my, mc, f ^ 1)[1]
                sends.append(pltpu.make_async_remote_copy(src_ref=o_refs[a].at[row], dst_ref=o_refs[a].at[row], **common))
                recvs.append(pltpu.make_async_remote_copy(src_ref=o_refs[a].at[row], dst_ref=o_refs[a].at[sib_row],
                                                          **common))
        for cp in sends:
            cp.start()
        for cp in recvs:
            cp.wait_recv()
        for cp in sends:
            cp.wait_send()

    return pl.pallas_call(
        body, name=name,
        in_specs=[pl.BlockSpec(memory_space=pl.ANY)] * n, out_specs=[pl.BlockSpec(memory_space=pl.ANY)] * n,
        out_shape=[jax.ShapeDtypeStruct(x.shape, x.dtype) for x in bufs],
        input_output_aliases={a: a for a in range(n)},
        scratch_shapes=[pltpu.SemaphoreType.DMA((n, len(chip_flips))), pltpu.SemaphoreType.DMA((n, len(chip_flips)))],
    )(*bufs)


PACK_COLS = 1024
PACK_ROWS = 256
PARTS_TILE_BYTES = 4 * 1024 * 1024


def adamw_reduce(parts, w, m, v, name):
    r, c = w.shape
    fits = [t for t in (512, 352, 256, 128, 64, 32, 16, 8) if N_DEV * t * c * parts.dtype.itemsize <= PARTS_TILE_BYTES]
    tr = r if r < 2 * SUBLANE else _pick(r, fits)
    c1 = 1.0 - ADAM_B1 ** ADAM_STEP
    c2 = 1.0 - ADAM_B2 ** ADAM_STEP

    def body(p_ref, w_ref, m_ref, v_ref, g_ref, d_ref, mo_ref, vo_ref):
        g = p_ref[0].astype(F32)
        for k in range(1, N_DEV):
            g = g + p_ref[k].astype(F32)
        mn = ADAM_B1 * m_ref[...] + (1.0 - ADAM_B1) * g
        vn = ADAM_B2 * v_ref[...] + (1.0 - ADAM_B2) * (g * g)
        g_ref[...] = g
        mo_ref[...] = mn
        vo_ref[...] = vn
        d_ref[...] = -ADAM_LR * ((mn / c1) / (jnp.sqrt(vn / c2) + ADAM_EPS) + ADAM_WD * w_ref[...])

    spec = pl.BlockSpec((tr, c), lambda i: (i, 0))
    return pl.pallas_call(
        body, name=name, grid=(r // tr,),
        in_specs=[pl.BlockSpec((N_DEV, tr, c), lambda i: (0, i, 0)), spec, spec, spec],
        out_specs=[spec] * 4, out_shape=[jax.ShapeDtypeStruct((r, c), F32)] * 4,
        compiler_params=_cparams(dimension_semantics=("arbitrary",)),
    )(parts, w, m, v)


def pack_flat(arrs, dtype):
    parts = []
    for a in arrs:
        flat = a.reshape(-1).astype(dtype)
        k = -(-flat.shape[0] // PACK_COLS)
        parts.append(jnp.pad(flat, (0, k * PACK_COLS - flat.shape[0])).reshape(k, PACK_COLS))
    buf = jnp.concatenate(parts, axis=0)
    return jnp.pad(buf, ((0, -buf.shape[0] % PACK_ROWS), (0, 0)))


def unpack_flat(buf, shapes):
    out, r0 = [], 0
    for s in shapes:
        n = math.prod(s)
        k = -(-n // PACK_COLS)
        out.append(buf[r0:r0 + k].reshape(-1)[:n].reshape(tuple(s)))
        r0 += k
    return out


W_NAMES = ('meta_tokens', 'ffn1_norm', 'ffn1_w_gate', 'ffn1_w_up', 'ffn1_w_down', 'mix_norm', 'w_in', 'w_out',
           'lru_conv_w', 'lru_conv_b', 'lru_w_a', 'lru_b_a', 'lru_w_i', 'lru_b_i', 'lru_lambda', 'lru_norm',
           'gdn_conv_w', 'gdn_a_log', 'gdn_dt_bias', 'gdn_norm', 'ssd_conv_w', 'ssd_conv_b', 'ssd_a_log',
           'ssd_dt_bias', 'ssd_d', 'ssd_norm', 's5_a_re', 's5_a_im', 's5_log_dt', 's5_b_re', 's5_b_im', 's5_c_re',
           's5_c_im', 's5_d', 's5_w_glu', 's5_norm', 'ffn2_norm', 'ffn2_w_gate', 'ffn2_w_up', 'ffn2_w_down',
           'final_norm')
SHARD_AXIS = {'meta_tokens': 1, 'ffn1_w_gate': 2, 'ffn1_w_up': 2, 'ffn1_w_down': 1, 'w_in': 2, 'w_out': 1,
              'lru_conv_w': 2, 'gdn_conv_w': 2, 'ssd_conv_w': 2, 's5_w_glu': 1, 'ffn2_w_gate': 2, 'ffn2_w_up': 2,
              'ffn2_w_down': 1}
BIG_NAMES = ('ffn1_w_gate', 'ffn1_w_up', 'ffn1_w_down', 'w_in', 'w_out', 's5_w_glu', 'ffn2_w_gate', 'ffn2_w_up',
             'ffn2_w_down')
SHARD_NAMES = tuple(n for n in W_NAMES if n in SHARD_AXIS)
REP_NAMES = tuple(n for n in W_NAMES if n not in SHARD_AXIS)
SSD_GROUPS = 2
S5_CH = 16


def unshard(g, axis):
    return jnp.concatenate([g[p] for p in range(N_DEV)], axis=axis)


def kernel(*args):
    n_w = len(W_NAMES)
    x = args[0]
    w = dict(zip(W_NAMES, args[1:1 + n_w]))
    target = args[1 + n_w]
    m_in = dict(zip(W_NAMES, args[2 + n_w:2 + 2 * n_w]))
    v_in = dict(zip(W_NAMES, args[2 + 2 * n_w:2 + 3 * n_w]))

    depth, d = w['ffn1_norm'].shape
    seq = x.shape[1]
    n_meta = w['meta_tokens'].shape[0]
    pad = CHUNK - n_meta
    tp = pad + n_meta + seq
    wg = d // 2
    xbc_w = w['ssd_conv_w'].shape[-1] * N_DEV
    gdn_hd = w['gdn_norm'].shape[-1]
    gdn_h = wg // gdn_hd
    ssd_h = w['ssd_a_log'].shape[-1]
    lru_h = w['lru_w_a'].shape[1]
    s5_g, s5_n = w['s5_a_re'].shape[1:]
    s5_q = wg // S5_LANES
    row_tile = _pick(tp, (192, 96, 64))

    LAYER_NAMES = tuple(n for n in SHARD_NAMES if n != 'meta_tokens')

    def local_of(n, l):
        return w[n][l].astype(BF16 if n in BIG_NAMES else F32)

    def unshard_layer(gathered):
        return {n: unshard(g, SHARD_AXIS[n] - 1) for n, g in gathered.items()}

    first = all_gather([local_of(n, 0) for n in LAYER_NAMES] + [w['meta_tokens']], "gather_weights")
    meta_full = unshard(first[-1], SHARD_AXIS['meta_tokens'])
    full = [unshard_layer(dict(zip(LAYER_NAMES, first[:-1])))] + [None] * (depth - 1)

    segs = [('a_x', wg), ('a_gate', wg), ('b_q', wg), ('b_k', wg), ('b_v', wg), ('b_z', wg), ('c_xbc', xbc_w),
            ('c_z', wg), ('d_u', wg), ('small_b', LANE), ('small_c', LANE)]
    off, o = {}, 0
    for nme, wd_ in segs:
        assert o % wd_ == 0, (nme, o, wd_)
        off[nme] = o
        o += wd_
    o_beta = 6 * wg
    o_cz = o_beta + 2 * gdn_h
    o_xbc = o_cz + wg
    o_dt = o_xbc + xbc_w
    o_du = o_dt + ssd_h

    def pack_cols(a):
        z = lambda k: jnp.zeros(a.shape[:-1] + (k,), a.dtype)
        return jnp.concatenate([a[..., :o_beta], a[..., o_xbc:o_dt], a[..., o_cz:o_xbc], a[..., o_du:],
                                a[..., o_beta:o_cz], z(LANE - 2 * gdn_h), a[..., o_dt:o_du], z(LANE - ssd_h)], axis=-1)

    def unpack_cols(a):
        sb, sc = off['small_b'], off['small_c']
        return jnp.concatenate([a[..., :o_beta], a[..., sb:sb + 2 * gdn_h], a[..., off['c_z']:off['c_z'] + wg],
                                a[..., off['c_xbc']:off['c_xbc'] + xbc_w], a[..., sc:sc + ssd_h],
                                a[..., off['d_u']:off['d_u'] + wg]], axis=-1)

    w_in_p = [None] * depth
    w_in_p[0] = pack_cols(full[0]['w_in'])

    def col(name, width):
        return off[name] // width

    def row(a):
        return a.reshape(1, -1)

    def layer_params(l):
        gcw = full[l]['gdn_conv_w']
        lru = [full[l]['lru_conv_w'], row(w['lru_conv_b'][l]), blockdiag_expand(w['lru_w_a'][l]), row(w['lru_b_a'][l]),
               blockdiag_expand(w['lru_w_i'][l]), row(w['lru_b_i'][l]), row(w['lru_lambda'][l]), row(w['lru_norm'][l])]
        gdn = [gcw[:, :wg], gcw[:, wg:2 * wg], gcw[:, 2 * wg:], row(w['gdn_a_log'][l]), row(w['gdn_dt_bias'][l]),
               row(w['gdn_norm'][l])]
        ssd = [full[l]['ssd_conv_w'], row(w['ssd_conv_b'][l]), row(w['ssd_a_log'][l]), row(w['ssd_dt_bias'][l]),
               row(w['ssd_d'][l]), row(w['ssd_norm'][l])]
        s5 = list(s5_params_expand(*[w[n][l] for n in ('s5_a_re', 's5_a_im', 's5_log_dt', 's5_b_re', 's5_b_im',
                                                          's5_c_re', 's5_c_im', 's5_d')]))
        post = [full[l]['s5_w_glu'], row(w['s5_norm'][l])]
        return lru, gdn, ssd, s5, post

    lru_fn = functools.partial(lru_chunk, pad)
    gdn_fn = functools.partial(gdn_multi, pad, CHUNK)
    ssd_fn = functools.partial(ssd_multi, pad, 0, SSD_GROUPS, CHUNK)
    s5_fn = functools.partial(s5_chunk, pad)
    post_fn = functools.partial(s5_post, pad)
    n_state_lanes = (S5_LANES // S5_CH) * s5_n

    def mixer_specs(proj):
        lru_seqs = [(proj, wg, col('a_x', wg)), (proj, wg, col('a_gate', wg))]
        gdn_seqs = [(proj, wg, col('b_q', wg)), (proj, wg, col('b_k', wg)), (proj, wg, col('b_v', wg)),
                    (proj, wg, col('b_z', wg)), (proj, LANE, col('small_b', LANE))]
        ssd_seqs = [(proj, wg, col('c_z', wg)), (proj, xbc_w, col('c_xbc', xbc_w)), (proj, LANE, col('small_c', LANE))]
        base = col('d_u', S5_LANES)
        s5_seqs = [(proj, S5_LANES, lambda g: base + g)]
        return lru_seqs, gdn_seqs, ssd_seqs, s5_seqs

    lru_carry = [(CONV_TAIL, wg), (1, wg)]
    gdn_carry = [(CONV_TAIL, wg)] * 3 + [(wg, gdn_hd)]
    ssd_carry = [(CONV_TAIL, xbc_w), (wg, (xbc_w - wg) // (2 * SSD_GROUPS))]
    s5_carry = [(1, n_state_lanes)] * 2
    rk = dict(n_steps=tp // row_tile, rows=row_tile)
    mk = rk
    out_w = (wg, wg, 0, F32)

    h = jnp.concatenate([jnp.zeros((pad, d), F32), meta_full, x[0]], axis=0)
    target_p = jnp.concatenate([jnp.zeros((pad + n_meta, d), F32), target[0]], axis=0)
    saved = []
    for l in range(depth):
        lru_p, gdn_p, ssd_p, s5_p, post_p = layer_params(l)
        h0 = h
        xn1 = rms_fwd(h0, row(w['ffn1_norm'][l]), "rms_fwd", row_tile)
        nxt = l + 1 < depth
        take = lambda *names: [local_of(n, l + 1) for n in names] if nxt else []
        fw = full[l]
        h1 = ffn_fwd(h0, xn1, fw['ffn1_w_gate'], fw['ffn1_w_up'], fw['ffn1_w_down'], "ffn_fwd")
        xn2 = rms_fwd(h1, row(w['mix_norm'][l]), "rms_fwd", row_tile)
        proj = matmul_cols(xn2, w_in_p[l], "mix_in_fwd")
        lru_s, gdn_s, ssd_s, s5_s = mixer_specs(proj)
        got = {}
        names = ('ffn2_w_gate',)
        ya, lru_c, bufs = mixer_fwd(lru_fn, "lru_fwd", seqs=lru_s, params=lru_p, out=out_w, carry=lru_carry,
                                    ag=take(*names), **mk)
        got.update(zip(names, bufs))
        names = ('w_in',)
        yb, gdn_c, bufs = mixer_fwd(gdn_fn, "gdn_fwd", seqs=gdn_s, params=gdn_p, out=out_w, carry=gdn_carry,
                                    ag=take(*names), **mk)
        got.update(zip(names, bufs))
        names = ('w_out', 's5_w_glu', 'lru_conv_w', 'gdn_conv_w', 'ssd_conv_w')
        yc, ssd_c, bufs = mixer_fwd(ssd_fn, "ssd_fwd", seqs=ssd_s, params=ssd_p, out=out_w, carry=ssd_carry,
                                    ag=take(*names), **mk)
        got.update(zip(names, bufs))
        names = ('ffn1_w_gate', 'ffn1_w_up', 'ffn1_w_down')
        y1, s5_c, bufs = mixer_fwd(s5_fn, "s5_fwd", seqs=s5_s, params=s5_p, out=(wg, S5_LANES, lambda g: g, F32),
                                   carry=s5_carry, n_groups=s5_q, ag=take(*names), **mk)
        got.update(zip(names, bufs))
        names = ('ffn2_w_down',)
        yd, _, bufs = mixer_fwd(post_fn, "s5_post_fwd", seqs=[(y1, wg, 0)], params=post_p, out=out_w, carry=[],
                                ag=take(*names), **rk)
        got.update(zip(names, bufs))
        names = ('ffn2_w_up',)
        h2, bufs = mix_out_fwd(h1, [ya, yb, yc, yd], fw['w_out'], "mix_out_fwd", row_tile, ag=take(*names))
        got.update(zip(names, bufs))
        if nxt:
            full[l + 1] = unshard_layer(dict(zip(LAYER_NAMES, ag_second_level([got[n] for n in LAYER_NAMES],
                                                                                "gather_pass_on"))))
            w_in_p[l + 1] = pack_cols(full[l + 1]['w_in'])
        xn3 = rms_fwd(h2, row(w['ffn2_norm'][l]), "rms_fwd", row_tile)
        h3 = ffn_fwd(h2, xn3, fw['ffn2_w_gate'], fw['ffn2_w_up'], fw['ffn2_w_down'], "ffn_fwd")
        saved.append((h0, xn1, h1, xn2, proj, (ya, yb, yc, yd), y1, (lru_c, gdn_c, ssd_c, s5_c), h2, xn3))
        h = h3

    loss_part, dh, d_final = loss_and_grad(h, target_p, row(w['final_norm']), "loss", row_tile, pad + n_meta)
    loss = lax.psum(loss_part, ("x", "y", "c"))

    def shards_of(a, axis):
        sh = a.shape
        return jnp.moveaxis(a.reshape(sh[:axis] + (N_DEV, sh[axis] // N_DEV) + sh[axis + 1:]), axis, 0)

    def to_send(n, g):
        return shards_of(g, SHARD_AXIS[n] - 1).astype(BF16)

    received = {n: [None] * depth for n in SHARD_NAMES if n != 'meta_tokens'}
    pending = []

    def hosted(names_layers):
        keys = [k for k in pending if (k[0], k[1]) in names_layers]
        for k in keys:
            pending.remove(k)
        return [(k[0], k[1]) for k in keys], [k[2] for k in keys]

    def store(keys, arrays):
        for (n, l), a in zip(keys, arrays, strict=True):
            received[n][l] = a

    gw = {n: [None] * depth for n in W_NAMES if n not in ('meta_tokens', 'final_norm')}
    for l in reversed(range(depth)):
        lru_p, gdn_p, ssd_p, s5_p, post_p = layer_params(l)
        h0, xn1, h1, xn2, proj, ys, y1, (lru_c, gdn_c, ssd_c, s5_c), h2, xn3 = saved[l]
        dxn, gw['ffn2_w_gate'][l], gw['ffn2_w_up'][l], gw['ffn2_w_down'][l] = ffn_bwd(
            xn3, dh, full[l]['ffn2_w_gate'], full[l]['ffn2_w_up'], full[l]['ffn2_w_down'], "ffn_bwd")
        pending += [(n, l, to_send(n, gw[n][l])) for n in ('ffn2_w_gate', 'ffn2_w_up', 'ffn2_w_down')]
        dh, dg = rms_bwd_add(h2, dxn, dh, row(w['ffn2_norm'][l]), "rms_bwd", row_tile)
        gw['ffn2_norm'][l] = dg[0]

        dys, (d_wout,) = mixer_bwd(mix_out_delta, "mix_out_bwd", seqs=[(y, wg, 0) for y in ys],
                                   params=[full[l]['w_out']], dout=(dh, d, 0), saved=[], carry=[], **rk)
        gw['w_out'][l] = d_wout[0]
        pending.append(('w_out', l, to_send('w_out', gw['w_out'][l])))
        lru_s, gdn_s, ssd_s, s5_s = mixer_specs(proj)
        (dy1,), d_post = mixer_bwd(post_fn, "s5_post_bwd", seqs=[(y1, wg, 0)], params=post_p, dout=(dys[3], wg, 0),
                                   saved=[], carry=[], **rk)
        gw['s5_w_glu'][l], gw['s5_norm'][l] = d_post[0][0], d_post[1][0, 0]
        pending.append(('s5_w_glu', l, to_send('s5_w_glu', gw['s5_w_glu'][l])))
        keys, arrs = hosted({('ffn1_w_gate', l + 1), ('ffn1_w_up', l + 1), ('ffn1_w_down', l + 1)})
        (d_du,), d_s5, *got = mixer_bwd(s5_fn, "s5_bwd", seqs=s5_s, params=s5_p, dout=(dy1, S5_LANES, lambda g: g),
                                        saved=s5_c, carry=s5_carry, n_groups=s5_q, a2a=arrs, **mk)
        store(keys, got[0] if got else [])
        for n, g in zip(('s5_a_re', 's5_a_im', 's5_log_dt', 's5_b_re', 's5_b_im', 's5_c_re', 's5_c_im', 's5_d'),
                        s5_grads_extract(d_s5, s5_g, s5_n, S5_CH)):
            gw[n][l] = g
        keys, arrs = hosted({('w_in', l + 1), ('w_out', l), ('s5_w_glu', l)})
        (d_cz, d_cxbc, d_sc), d_ssd, *got = mixer_bwd(ssd_fn, "ssd_bwd", seqs=ssd_s, params=ssd_p,
                                                      dout=(dys[2], wg, 0), saved=ssd_c, carry=ssd_carry, a2a=arrs, **mk)
        store(keys, got[0] if got else [])
        for n, g in zip(('ssd_conv_w', 'ssd_conv_b', 'ssd_a_log', 'ssd_dt_bias', 'ssd_d', 'ssd_norm'), d_ssd):
            gw[n][l] = g[0] if n == 'ssd_conv_w' else g[0, 0]
        keys, arrs = hosted({('ffn2_w_gate', l), ('ffn2_w_up', l), ('ffn2_w_down', l)})
        (d_bq, d_bk, d_bv, d_bz, d_sb), d_gdn, *got = mixer_bwd(gdn_fn, "gdn_bwd", seqs=gdn_s, params=gdn_p,
                                                                dout=(dys[1], wg, 0), saved=gdn_c, carry=gdn_carry,
                                                                a2a=arrs, **mk)
        store(keys, got[0] if got else [])
        gw['gdn_conv_w'][l] = jnp.concatenate([d_gdn[0][0], d_gdn[1][0], d_gdn[2][0]], axis=1)
        gw['gdn_a_log'][l], gw['gdn_dt_bias'][l], gw['gdn_norm'][l] = d_gdn[3][0, 0], d_gdn[4][0, 0], d_gdn[5][0, 0]
        (d_ax, d_ag), d_lru = mixer_bwd(lru_fn, "lru_bwd", seqs=lru_s, params=lru_p, dout=(dys[0], wg, 0),
                                        saved=lru_c, carry=lru_carry, **mk)
        gw['lru_conv_w'][l], gw['lru_conv_b'][l] = d_lru[0][0], d_lru[1][0, 0]
        gw['lru_w_a'][l], gw['lru_b_a'][l] = blockdiag_extract(d_lru[2][0], lru_h), d_lru[3][0, 0]
        gw['lru_w_i'][l], gw['lru_b_i'][l] = blockdiag_extract(d_lru[4][0], lru_h), d_lru[5][0, 0]
        gw['lru_lambda'][l], gw['lru_norm'][l] = d_lru[6][0, 0], d_lru[7][0, 0]

        dproj = jnp.concatenate([d_ax, d_ag, d_bq, d_bk, d_bv, d_bz, d_cxbc, d_cz, d_du, d_sb, d_sc], axis=1)
        dxn, d_win_p = matmul_cols_bwd(xn2, dproj, w_in_p[l], "mix_in_bwd")
        gw['w_in'][l] = unpack_cols(d_win_p)
        pending.append(('w_in', l, to_send('w_in', gw['w_in'][l])))
        dh, dg = rms_bwd_add(h1, dxn, dh, row(w['mix_norm'][l]), "rms_bwd", row_tile)
        gw['mix_norm'][l] = dg[0]

        dxn, gw['ffn1_w_gate'][l], gw['ffn1_w_up'][l], gw['ffn1_w_down'][l] = ffn_bwd(
            xn1, dh, full[l]['ffn1_w_gate'], full[l]['ffn1_w_up'], full[l]['ffn1_w_down'], "ffn_bwd")
        dh, dg = rms_bwd_add(h0, dxn, dh, row(w['ffn1_norm'][l]), "rms_bwd", row_tile)
        gw['ffn1_norm'][l] = dg[0]
        pending += [(n, l, to_send(n, gw[n][l])) for n in ('ffn1_w_gate', 'ffn1_w_up', 'ffn1_w_down')]

    grad_x = dh[pad + n_meta:][None]
    grads = {n: jnp.stack(g, axis=0) for n, g in gw.items()}
    grads['meta_tokens'] = dh[pad:pad + n_meta]
    grads['final_norm'] = d_final[0]

    for n in ('lru_conv_w', 'gdn_conv_w', 'ssd_conv_w'):
        pending += [(n, l, to_send(n, gw[n][l])) for l in range(depth)]
    last = [k[2] for k in pending] + [shards_of(grads['meta_tokens'], SHARD_AXIS['meta_tokens']).astype(BF16)]
    recv_rep, *got, recv_meta = all_gather([pack_flat([grads[n] for n in REP_NAMES], F32)], "last_exchange",
                                           with_a2a=last)
    store([(k[0], k[1]) for k in pending], got)
    recv_sh = [recv_meta if n == 'meta_tokens' else jnp.stack(received[n], axis=1) for n in SHARD_NAMES]
    out = {}
    for n, recv in zip(SHARD_NAMES, recv_sh):
        c = w[n].shape[-1]
        res = adamw_reduce(recv.reshape(N_DEV, -1, c), w[n].reshape(-1, c), m_in[n].reshape(-1, c),
                           v_in[n].reshape(-1, c), "adamw_" + n)
        for kind, buf in zip(('grad', 'delta', 'new_m', 'new_v'), res):
            out[kind, n] = buf.reshape(w[n].shape)
    res = adamw_reduce(recv_rep, pack_flat([w[n] for n in REP_NAMES], F32), pack_flat([m_in[n] for n in REP_NAMES], F32),
                       pack_flat([v_in[n] for n in REP_NAMES], F32), "adamw_replicated")
    shapes = [w[n].shape for n in REP_NAMES]
    for kind, buf in zip(('grad', 'delta', 'new_m', 'new_v'), res):
        for n, a in zip(REP_NAMES, unpack_flat(buf, shapes)):
            out[kind, n] = a
    return (loss, grad_x) + tuple(out[k, n] for k in ('grad', 'delta', 'new_m', 'new_v') for n in W_NAMES)
```

```python
import functools
import math

import jax
import jax.numpy as jnp
from jax import lax
from jax.experimental import pallas as pl
from jax.experimental.pallas import tpu as pltpu

F32 = jnp.float32
BF16 = jnp.bfloat16

EPS = 1e-6
CHUNK = 64
CONV_K = 4
CONV_TAIL = 8
LRU_C = 8.0
LANE = 128
SUBLANE = 8
N_DEV = 8
NEG_BIG = -1e30

ADAM_LR = 0.001
ADAM_B1 = 0.9
ADAM_B2 = 0.999
ADAM_EPS = 1e-08
ADAM_WD = 0.01
ADAM_STEP = 10

VMEM_LIMIT = 56 * 1024 * 1024


def _dg(a, b, dims):
    return lax.dot_general(a.astype(BF16), b.astype(BF16), (dims, ((), ())), preferred_element_type=F32)


@jax.custom_vjp
def bdot(a, b):
    return _dg(a, b, ((1,), (0,)))


@jax.custom_vjp
def bdot_nt(a, b):
    return _dg(a, b, ((1,), (1,)))


@jax.custom_vjp
def bdot_tn(a, b):
    return _dg(a, b, ((0,), (0,)))


bdot.defvjp(lambda a, b: (bdot(a, b), (a, b)),
            lambda r, g: (bdot_nt(g, r[1]).astype(r[0].dtype), bdot_tn(r[0], g).astype(r[1].dtype)))
bdot_nt.defvjp(lambda a, b: (bdot_nt(a, b), (a, b)),
               lambda r, g: (bdot(g, r[1]).astype(r[0].dtype), bdot_tn(g, r[0]).astype(r[1].dtype)))
bdot_tn.defvjp(lambda a, b: (bdot_tn(a, b), (a, b)),
               lambda r, g: (bdot_nt(r[1], g).astype(r[0].dtype), bdot(r[0], g).astype(r[1].dtype)))


def _split_bf16(a):
    hi = a.astype(BF16)
    return hi, (a - hi.astype(F32)).astype(BF16)


def _dot3(a, b, dims):
    (ah, al), (bh, bl) = _split_bf16(a), _split_bf16(b)
    d = lambda x, y: lax.dot_general(x, y, (dims, ((), ())), preferred_element_type=F32)
    return d(ah, bh) + (d(ah, bl) + d(al, bh))


@jax.custom_vjp
def hdot(a, b):
    return _dot3(a, b, ((1,), (0,)))


@jax.custom_vjp
def hdot_tn(a, b):
    return _dot3(a, b, ((0,), (0,)))


hdot.defvjp(lambda a, b: (hdot(a, b), (a, b)),
            lambda r, g: (_dot3(g, r[1], ((1,), (1,))), _dot3(r[0], g, ((0,), (0,)))))
hdot_tn.defvjp(lambda a, b: (hdot_tn(a, b), (a, b)),
               lambda r, g: (_dot3(r[1], g, ((1,), (1,))), _dot3(r[0], g, ((1,), (0,)))))


def rms_norm(x, g):
    return x * lax.rsqrt(jnp.mean(x * x, axis=-1, keepdims=True) + EPS) * g


def row_mask(row0, rows, pad):
    r = row0 + lax.broadcasted_iota(jnp.int32, (rows, 1), 0)
    return (r >= pad).astype(F32)


def conv4(tail, u, w):
    rows = u.shape[0]
    xe = jnp.concatenate([tail, u], axis=0)
    y = w[0:1] * xe[CONV_TAIL - 3:CONV_TAIL - 3 + rows]
    for k in range(1, CONV_K):
        y = y + w[k:k + 1] * xe[CONV_TAIL - 3 + k:CONV_TAIL - 3 + k + rows]
    return y


def shift_rows(x, s, fill):
    rows = x.shape[0]
    return jnp.concatenate([jnp.full((s, x.shape[1]), fill, x.dtype), x[:rows - s]], axis=0)


def lin_scan(a, b):
    rows = a.shape[0]
    s = 1
    while s < rows:
        b = a * shift_rows(b, s, 0.0) + b
        a = a * shift_rows(a, s, 1.0)
        s *= 2
    return b


def cscan_const(ar, ai, br, bi):
    rows = br.shape[0]
    s = 1
    while s < rows:
        brs, bis = shift_rows(br, s, 0.0), shift_rows(bi, s, 0.0)
        br, bi = br + ar * brs - ai * bis, bi + ar * bis + ai * brs
        ar, ai = ar * ar - ai * ai, 2.0 * ar * ai
        s *= 2
    return br, bi


def neg_expm1(z):
    t = jnp.tanh(0.5 * z)
    return -2.0 * t / (1.0 - t)


def tri_masks(n):
    r = lax.broadcasted_iota(jnp.int32, (n, n), 0)
    c = lax.broadcasted_iota(jnp.int32, (n, n), 1)
    return r >= c, r > c, (r == c).astype(F32)


def lru_chunk(pad, row0, params, seqs, carry):
    conv_w, conv_b, w_a, b_a, w_i, b_i, lam, norm_g = params
    u_x, u_gate = seqs
    tail, h0 = carry
    rows = u_x.shape[0]
    m = row_mask(row0, rows, pad)
    xc = conv4(tail, u_x, conv_w) + conv_b
    r = jax.nn.sigmoid(bdot(xc, w_a) + b_a)
    ig = jax.nn.sigmoid(bdot(xc, w_i) + b_i)
    log_a = -LRU_C * r * jax.nn.softplus(-lam)
    a = jnp.exp(log_a)
    b = jnp.sqrt(neg_expm1(2.0 * log_a)) * (ig * xc) * m
    first = (lax.broadcasted_iota(jnp.int32, (rows, 1), 0) == 0).astype(F32)
    b = b + first * (a * h0)
    h = lin_scan(a, b)
    y = jax.nn.gelu(u_gate) * h
    out = rms_norm(y, norm_g) * m
    return (out,), (u_x[rows - CONV_TAIL:], h[rows - 1:])


def gdn_multi(pad, sub, row0, params, seqs, carry):
    wq, wk, wv, a_log, dt_bias, norm_g = params
    u_q, u_k, u_v, u_z, small = seqs
    tq, tk, tv, state = carry
    rows = u_q.shape[0]
    hd = norm_g.shape[1]
    nh = u_q.shape[1] // hd
    nc = rows // sub
    m = row_mask(row0, rows, pad)
    incl, strict, eye = tri_masks(sub)
    tril = incl.astype(F32)
    triu = (lax.broadcasted_iota(jnp.int32, (sub, sub), 0) <= lax.broadcasted_iota(jnp.int32, (sub, sub), 1)).astype(F32)
    qc = jax.nn.silu(conv4(tq, u_q, wq))
    kc = jax.nn.silu(conv4(tk, u_k, wk))
    vc = jax.nn.silu(conv4(tv, u_v, wv))
    beta = jax.nn.sigmoid(small[:, :nh]) * m
    g = -jnp.exp(a_log) * jax.nn.softplus(small[:, nh:2 * nh] + dt_bias) * m
    gate = jax.nn.silu(u_z)
    heads = [slice(h * hd, (h + 1) * hd) for h in range(nh)]
    q_h = [qc[:, sl] for sl in heads]
    k_h = [kc[:, sl] for sl in heads]
    q_h = [q * lax.rsqrt(jnp.sum(q * q, axis=-1, keepdims=True) + EPS) * (hd ** -0.5) * m for q in q_h]
    k_h = [k * lax.rsqrt(jnp.sum(k * k, axis=-1, keepdims=True) + EPS) * m for k in k_h]
    v_h = [vc[:, sl] * m for sl in heads]
    pairs = [(c, h) for c in range(nc) for h in range(nh)]
    cs = lambda x, c: x[c * sub:(c + 1) * sub]
    q = {(c, h): cs(q_h[h], c) for c, h in pairs}
    k = {(c, h): cs(k_h[h], c) for c, h in pairs}
    v = {(c, h): cs(v_h[h], c) for c, h in pairs}
    bt = {(c, h): cs(beta, c)[:, h:h + 1] for c, h in pairs}
    gcs = [hdot(tril, cs(g, c)) for c in range(nc)]
    gts = [hdot_tn(cs(g, c), triu) for c in range(nc)]
    gc = {(c, h): gcs[c][:, h:h + 1] for c, h in pairs}
    decay = {(c, h): jnp.exp(jnp.where(incl, gc[c, h] - gts[c][h:h + 1], NEG_BIG)) for c, h in pairs}
    kb = {p: k[p] * bt[p] for p in pairs}
    kk = {p: bdot_nt(kb[p], k[p]) for p in pairs}
    lmat = {p: jnp.where(strict, kk[p] * decay[p], 0.0) for p in pairs}
    pm = {p: eye - lmat[p] for p in pairs}
    mm = {p: hdot(lmat[p], lmat[p]) for p in pairs}
    s = 2
    while s < sub:
        pm = {p: pm[p] + hdot(pm[p], mm[p]) for p in pairs}
        s *= 2
        if s < sub:
            mm = {p: hdot(mm[p], mm[p]) for p in pairs}
    eg = {p: jnp.exp(gc[p]) for p in pairs}
    u = {p: hdot(pm[p], v[p] * bt[p]) for p in pairs}
    w = {p: hdot(pm[p], kb[p] * eg[p]) for p in pairs}
    attn = {p: bdot_nt(q[p], k[p]) * decay[p] for p in pairs}
    qd = {p: q[p] * eg[p] for p in pairs}
    g_last = {p: gc[p][sub - 1:] for p in pairs}
    kd = {p: k[p] * jnp.exp(g_last[p] - gc[p]) for p in pairs}
    last = {p: jnp.exp(g_last[p]) for p in pairs}
    s_h = [state[sl] for sl in heads]
    o = {}
    for c in range(nc):
        ws = [bdot(w[c, h], s_h[h]) for h in range(nh)]
        qs = [bdot(qd[c, h], s_h[h]) for h in range(nh)]
        v_new = [u[c, h] - ws[h] for h in range(nh)]
        av = [bdot(attn[c, h], v_new[h]) for h in range(nh)]
        kv = [bdot_tn(kd[c, h], v_new[h]) for h in range(nh)]
        for h in range(nh):
            o[c, h] = qs[h] + av[h]
        s_h = [s_h[h] * last[c, h] + kv[h] for h in range(nh)]
    out = jnp.concatenate([jnp.concatenate([rms_norm(o[c, h], norm_g) for h in range(nh)], axis=1)
                           for c in range(nc)], axis=0) * gate * m
    t0 = rows - CONV_TAIL
    return (out,), (u_q[t0:], u_k[t0:], u_v[t0:], jnp.concatenate(s_h, axis=0))


def ssd_multi(pad, dt_lane0, n_groups, sub, row0, params, seqs, carry):
    conv_w, conv_b, a_log, dt_bias, d_skip, norm_g = params
    u_z, u_xbc, small = seqs
    tail, state = carry
    rows = u_z.shape[0]
    width = u_z.shape[1]
    nh = a_log.shape[1]
    hd = width // nh
    ns = (u_xbc.shape[1] - width) // (2 * n_groups)
    hpg = nh // n_groups
    nc = rows // sub
    m = row_mask(row0, rows, pad)
    incl, _, _ = tri_masks(sub)
    tril = incl.astype(F32)
    triu = (lax.broadcasted_iota(jnp.int32, (sub, sub), 0) <= lax.broadcasted_iota(jnp.int32, (sub, sub), 1)).astype(F32)
    xbc = jax.nn.silu(conv4(tail, u_xbc, conv_w) + conv_b)
    xs = xbc[:, :width]
    dt = jax.nn.softplus(small[:, dt_lane0:dt_lane0 + nh] + dt_bias)
    a_all = dt * (-jnp.exp(a_log)) * m
    cs = lambda x, c: x[c * sub:(c + 1) * sub]
    heads = [slice(h * hd, (h + 1) * hd) for h in range(nh)]
    pairs = [(c, h) for c in range(nc) for h in range(nh)]
    grp = lambda h: h // hpg
    bm = {(c, g): cs(xbc[:, width + g * ns: width + (g + 1) * ns] * m, c) for c in range(nc) for g in range(n_groups)}
    cm = {(c, g): cs(xbc[:, width + (n_groups + g) * ns: width + (n_groups + g + 1) * ns] * m, c)
          for c in range(nc) for g in range(n_groups)}
    xh = {(c, h): cs(xs[:, heads[h]], c) for c, h in pairs}
    xdt = {(c, h): xh[c, h] * cs(dt[:, h:h + 1] * m, c) for c, h in pairs}
    acums = [hdot(tril, cs(a_all, c)) for c in range(nc)]
    acts = [hdot_tn(cs(a_all, c), triu) for c in range(nc)]
    acum = {(c, h): acums[c][:, h:h + 1] for c, h in pairs}
    a_last = {p: acum[p][sub - 1:] for p in pairs}
    lmat = {(c, h): jnp.exp(jnp.where(incl, acum[c, h] - acts[c][h:h + 1], NEG_BIG)) for c, h in pairs}
    cb = {cg: bdot_nt(cm[cg], bm[cg]) for cg in bm}
    y_diag = {(c, h): bdot(cb[c, grp(h)] * lmat[c, h], xdt[c, h]) for c, h in pairs}
    st = {(c, h): bdot_tn(xdt[c, h] * jnp.exp(a_last[c, h] - acum[c, h]), bm[c, grp(h)]) for c, h in pairs}
    e_in = {p: jnp.exp(acum[p]) for p in pairs}
    e_out = {p: jnp.exp(a_last[p]) for p in pairs}
    s_h = [state[sl] for sl in heads]
    y = {}
    for c in range(nc):
        off = [bdot_nt(cm[c, grp(h)], s_h[h]) for h in range(nh)]
        for h in range(nh):
            y[c, h] = y_diag[c, h] + off[h] * e_in[c, h] + d_skip[:, h:h + 1] * xh[c, h]
        s_h = [s_h[h] * e_out[c, h] + st[c, h] for h in range(nh)]
    yy = jnp.concatenate([jnp.concatenate([y[c, h] for h in range(nh)], axis=1) for c in range(nc)], axis=0)
    yy = yy * jax.nn.silu(u_z)
    gw = width // n_groups
    outs = [rms_norm(yy[:, g * gw:(g + 1) * gw], norm_g[:, g * gw:(g + 1) * gw]) for g in range(n_groups)]
    out = jnp.concatenate(outs, axis=1) * m
    return (out,), (u_xbc[rows - CONV_TAIL:], jnp.concatenate(s_h, axis=0))


def s5_chunk(pad, row0, params, seqs, carry):
    a_re, a_im, log_dt, b_re, b_im, c_re, c_im, d_skip = params
    (u,) = seqs
    s_re0, s_im0 = carry
    rows = u.shape[0]
    n_state = a_re.shape[1]
    n_grp = log_dt.shape[1]
    per = n_state // n_grp
    expand = (lax.broadcasted_iota(jnp.int32, (n_grp, n_state), 1) // per
              == lax.broadcasted_iota(jnp.int32, (n_grp, n_state), 0)).astype(F32)
    dt = jnp.exp(hdot(log_dt, expand))
    lam_re = jnp.minimum(a_re, -1e-4)
    lam_im = a_im
    mag = jnp.exp(dt * lam_re)
    ab_re = mag * jnp.cos(dt * lam_im)
    ab_im = mag * jnp.sin(dt * lam_im)
    den = lam_re * lam_re + lam_im * lam_im
    f_re = ((ab_re - 1.0) * lam_re + ab_im * lam_im) / den
    f_im = (ab_im * lam_re - (ab_re - 1.0) * lam_im) / den
    bb_re = f_re * b_re - f_im * b_im
    bb_im = f_re * b_im + f_im * b_re
    bu_re = bdot(u, bb_re)
    bu_im = bdot(u, bb_im)
    first = (lax.broadcasted_iota(jnp.int32, (rows, 1), 0) == 0).astype(F32)
    bu_re = bu_re + first * (ab_re * s_re0 - ab_im * s_im0)
    bu_im = bu_im + first * (ab_re * s_im0 + ab_im * s_re0)
    s_re, s_im = cscan_const(ab_re, ab_im, bu_re, bu_im)
    y = bdot(s_re, c_re) - bdot(s_im, c_im) + d_skip * u
    return (y,), (s_re[rows - 1:], s_im[rows - 1:])


def s5_post(pad, row0, params, seqs, carry):
    w_glu, norm_g = params
    (y,) = seqs
    y = jax.nn.gelu(y)
    y = y * jax.nn.sigmoid(bdot(y, w_glu))
    return (rms_norm(y, norm_g),), ()


def mix_out_delta(row0, params, seqs, carry):
    (w_out,) = params
    wd = seqs[0].shape[1]
    acc = bdot(seqs[0], w_out[0:wd])
    for k in range(1, len(seqs)):
        acc = acc + bdot(seqs[k], w_out[k * wd:(k + 1) * wd])
    return (acc,), ()


def blockdiag_expand(w):
    nh, a, b = w.shape
    eye = jnp.eye(nh, dtype=w.dtype)
    return (w[:, :, None, :] * eye[:, None, :, None]).reshape(nh * a, nh * b)


def blockdiag_extract(m, nh):
    a, b = m.shape[0] // nh, m.shape[1] // nh
    on_diag = jnp.eye(nh, dtype=bool)[:, None, :, None]
    return jnp.sum(jnp.where(on_diag, m.reshape(nh, a, nh, b), 0.0), axis=2)


S5_LANES = LANE


def s5_params_expand(a_re, a_im, log_dt, b_re, b_im, c_re, c_im, d_skip):
    n_grp, n_state = a_re.shape
    ch = b_re.shape[-1]
    gpl = S5_LANES // ch
    nq = n_grp // gpl
    eye = jnp.eye(gpl, dtype=F32)[None, :, None, :, None]

    def bexp(b):
        bt = jnp.swapaxes(b, 1, 2).reshape(nq, gpl, b.shape[2], 1, b.shape[1])
        return (bt * eye).reshape(nq, gpl * b.shape[2], gpl * b.shape[1])

    return (a_re.reshape(nq, 1, gpl * n_state), a_im.reshape(nq, 1, gpl * n_state), log_dt.reshape(nq, 1, gpl),
            bexp(b_re), bexp(b_im), bexp(c_re), bexp(c_im), d_skip.reshape(nq, 1, S5_LANES))


def s5_grads_extract(grads, n_grp, n_state, ch):
    da_re, da_im, dlog_dt, db_re, db_im, dc_re, dc_im, dd = grads
    gpl = S5_LANES // ch
    nq = n_grp // gpl
    on_diag = jnp.eye(gpl, dtype=bool)[None, :, None, :, None]

    def bext(b):
        r, c = b.shape[1] // gpl, b.shape[2] // gpl
        d = jnp.sum(jnp.where(on_diag, b.reshape(nq, gpl, r, gpl, c), 0.0), axis=3)
        return jnp.swapaxes(d.reshape(n_grp, r, c), 1, 2)

    return (da_re.reshape(n_grp, n_state), da_im.reshape(n_grp, n_state), dlog_dt.reshape(n_grp),
            bext(db_re), bext(db_im), bext(dc_re), bext(dc_im), dd.reshape(n_grp * ch))


def _cparams(**kw):
    return pltpu.CompilerParams(vmem_limit_bytes=VMEM_LIMIT, **kw)


def tiled_call(body_fn, name, *, n_steps, rows, n_groups=1, reverse=False,
               seq_in=(), whole_in=(), step_in=(), seq_out=(), acc_out=(), step_out=(), carry=(), a2a=(), ag=()):
    def step_of(i):
        return (n_steps - 1 - i) if reverse else i

    def col_of(col, g):
        return col(g) if callable(col) else col

    in_specs, operands = [], []
    for arr, width, col in seq_in:
        in_specs.append(pl.BlockSpec((rows, width), lambda g, i, col=col: (step_of(i), col_of(col, g))))
        operands.append(arr)
    for arr in whole_in:
        if arr.ndim == 2:
            in_specs.append(pl.BlockSpec(arr.shape, lambda g, i: (0, 0)))
        else:
            in_specs.append(pl.BlockSpec((None,) + arr.shape[1:], lambda g, i: (g, 0, 0)))
        operands.append(arr)
    for arr in step_in:
        in_specs.append(pl.BlockSpec((None, None) + arr.shape[2:], lambda g, i: (g, step_of(i), 0, 0)))
        operands.append(arr)
    out_shape, out_specs = [], []
    for total, width, col, dt in seq_out:
        out_shape.append(jax.ShapeDtypeStruct((n_steps * rows, total), dt))
        out_specs.append(pl.BlockSpec((rows, width), lambda g, i, col=col: (step_of(i), col_of(col, g))))
    for r, c in acc_out:
        out_shape.append(jax.ShapeDtypeStruct((n_groups, r, c), F32))
        out_specs.append(pl.BlockSpec((None, r, c), lambda g, i: (g, 0, 0)))
    for r, c in step_out:
        out_shape.append(jax.ShapeDtypeStruct((n_groups, n_steps, r, c), F32))
        out_specs.append(pl.BlockSpec((None, None, r, c), lambda g, i: (g, step_of(i), 0, 0)))
    n_seq, n_whole, n_step = len(seq_in), len(whole_in), len(step_in)
    n_so, n_ao, n_sto = len(seq_out), len(acc_out), len(step_out)
    assert not (a2a and ag)
    hosted = list(a2a) + list(ag)
    n_x = len(hosted)
    n_sem = AG_FIRST_COPIES if ag else N_DEV - 1
    hbm = pl.BlockSpec(memory_space=pl.ANY)
    in_specs += [hbm] * n_x
    operands += hosted
    out_specs += [hbm] * n_x
    out_shape += [jax.ShapeDtypeStruct((N_DEV,) + x.shape if ag else x.shape, x.dtype) for x in hosted]
    scratch = [pltpu.VMEM((r, c), F32) for r, c in carry]
    if n_x:
        scratch += [pltpu.SemaphoreType.DMA((n_x, n_sem)), pltpu.SemaphoreType.DMA((n_x, n_sem)),
                    pltpu.SemaphoreType.DMA((n_x,))]

    def body(*refs):
        pos = 0
        seq_refs = refs[pos:pos + n_seq]
        pos += n_seq
        whole_refs = refs[pos:pos + n_whole]
        pos += n_whole
        step_refs = refs[pos:pos + n_step]
        pos += n_step + n_x
        so_refs = refs[pos:pos + n_so]
        pos += n_so
        ao_refs = refs[pos:pos + n_ao]
        pos += n_ao
        sto_refs = refs[pos:pos + n_sto]
        pos += n_sto
        xo_refs = refs[pos:pos + n_x]
        pos += n_x
        carry_refs = refs[pos:pos + len(carry)]
        pos += len(carry)
        x_refs = refs[n_seq + n_whole + n_step:n_seq + n_whole + n_step + n_x]
        g, i = pl.program_id(0), pl.program_id(1)
        if n_x:
            locals_, sends, recvs = (ag_first_copies if ag else a2a_copies)(x_refs, xo_refs, *refs[pos:])

            @pl.when((g == 0) & (i == 0))
            def _():
                for cp in locals_ + sends:
                    cp.start()

        @pl.when(i == 0)
        def _():
            for r in carry_refs:
                r[...] = jnp.zeros(r.shape, r.dtype)
            for r in ao_refs:
                r[...] = jnp.zeros(r.shape, r.dtype)

        row0 = step_of(i) * rows
        seq_o, acc_o, step_o, new_c = body_fn(row0, [r[...] for r in whole_refs], [r[...] for r in seq_refs],
                                              [r[...] for r in step_refs], [r[...] for r in carry_refs])
        for r, val in zip(so_refs, seq_o, strict=True):
            r[...] = val.astype(r.dtype)
        for r, val in zip(ao_refs, acc_o, strict=True):
            r[...] += val
        for r, val in zip(sto_refs, step_o, strict=True):
            r[...] = val
        for r, val in zip(carry_refs, new_c, strict=True):
            r[...] = val
        if n_x:
            @pl.when((g == n_groups - 1) & (i == n_steps - 1))
            def _():
                for cp in recvs:
                    cp.wait_recv()
                for cp in sends:
                    cp.wait_send()
                for cp in locals_:
                    cp.wait()

    return pl.pallas_call(
        body, name=name, grid=(n_groups, n_steps), in_specs=in_specs, out_specs=out_specs, out_shape=out_shape,
        scratch_shapes=scratch,
        compiler_params=_cparams(dimension_semantics=("arbitrary", "arbitrary")),
    )(*operands)


def mixer_fwd(fn, name, *, n_steps, rows, seqs, params, out, carry, n_groups=1, ag=()):
    def body(row0, whole, seq_vals, steps, carry_vals):
        outs, new_c = fn(row0, tuple(whole), tuple(seq_vals), tuple(carry_vals))
        return list(outs), [], list(carry_vals), list(new_c)

    res = tiled_call(body, name, n_steps=n_steps, rows=rows, n_groups=n_groups, seq_in=seqs, whole_in=params,
                     seq_out=[out], step_out=carry, carry=carry, ag=ag)
    return res[0], list(res[1:1 + len(carry)]), list(res[1 + len(carry):])


def mixer_bwd(fn, name, *, n_steps, rows, seqs, params, dout, saved, carry, n_groups=1, a2a=()):
    n_seq = len(seqs)

    def body(row0, whole, seq_vals, steps, dcarry):
        params_f = tuple(p.astype(F32) for p in whole)
        _, vjp = jax.vjp(lambda p, s, c: fn(row0, p, s, c), params_f, tuple(seq_vals[:n_seq]), tuple(steps))
        dp, ds, dc = vjp(((seq_vals[n_seq],), tuple(dcarry)))
        return list(ds), list(dp), [], list(dc)

    seq_out = [(n_groups * w, w, (lambda g: g), F32) if callable(c) else (w, w, 0, F32) for a, w, c in seqs]
    acc_out = [p.shape[-2:] for p in params]
    res = tiled_call(body, name, n_steps=n_steps, rows=rows, n_groups=n_groups, reverse=True,
                     seq_in=list(seqs) + [dout], whole_in=params, step_in=saved,
                     seq_out=seq_out, acc_out=acc_out, carry=carry, a2a=a2a)
    n_p = len(params)
    if a2a:
        return list(res[:n_seq]), list(res[n_seq:n_seq + n_p]), list(res[n_seq + n_p:])
    return list(res[:n_seq]), list(res[n_seq:])


def rms_fwd(h, g, name, rows):
    def body(row0, whole, seqs, steps, carry):
        return [rms_norm(seqs[0], whole[0])], [], [], []
    d = h.shape[1]
    return tiled_call(body, name, n_steps=h.shape[0] // rows, rows=rows, seq_in=[(h, d, 0)], whole_in=[g],
                      seq_out=[(d, d, 0, BF16)])[0]


def rms_bwd_add(h, dxn, dh_out, g, name, rows):
    def body(row0, whole, seqs, steps, carry):
        _, vjp = jax.vjp(rms_norm, seqs[0], whole[0])
        dh, dg = vjp(seqs[1])
        return [seqs[2] + dh], [dg], [], []
    d = h.shape[1]
    dh_in, dg = tiled_call(body, name, n_steps=h.shape[0] // rows, rows=rows,
                           seq_in=[(h, d, 0), (dxn, d, 0), (dh_out, d, 0)], whole_in=[g],
                           seq_out=[(d, d, 0, F32)], acc_out=[(1, d)])
    return dh_in, dg[0]


def mix_out_fwd(h, ys, w_out, name, rows, ag=()):
    def body(row0, whole, seqs, steps, carry):
        (delta,), _ = mix_out_delta(row0, (whole[0],), tuple(seqs[1:]), ())
        return [seqs[0] + delta], [], [], []
    d, wd = h.shape[1], ys[0].shape[1]
    res = tiled_call(body, name, n_steps=h.shape[0] // rows, rows=rows,
                     seq_in=[(h, d, 0)] + [(y, wd, 0) for y in ys], whole_in=[w_out], seq_out=[(d, d, 0, F32)], ag=ag)
    return res[0], list(res[1:])


def loss_and_grad(h, target, g, name, rows, first_row):
    def body(row0, whole, seqs, steps, carry):
        hh, tt = seqs
        keep = row_mask(row0, hh.shape[0], first_row)

        def f(hv, gv):
            err = rms_norm(hv, gv) - tt
            return 0.5 * jnp.sum(jnp.mean(err * err, axis=-1, keepdims=True) * keep, axis=0, keepdims=True)

        val, vjp = jax.vjp(f, hh, whole[0])
        dh, dg = vjp(jnp.ones((1, 1), F32))
        return [dh], [jnp.broadcast_to(val, (1, LANE)), dg], [], []
    d = h.shape[1]
    dh, loss, dg = tiled_call(body, name, n_steps=h.shape[0] // rows, rows=rows,
                              seq_in=[(h, d, 0), (target, d, 0)], whole_in=[g],
                              seq_out=[(d, d, 0, F32)], acc_out=[(1, LANE), (1, d)])
    return loss[0, 0, 0], dh, dg[0]


def _pick(n, cands):
    for c in cands:
        if n % c == 0:
            return c
    raise ValueError(f"no tile for {n}")


ROW_TILES = (1056, 704, 352, 192, 96, 64)
COL_TILES = (256, 128)
NT_DIMS = (((1,), (1,)), ((), ()))
TN_DIMS = (((0,), (0,)), ((), ()))


def ffn_fwd(h, xn, wg, wu, wd, name):
    t, d = h.shape
    f = wg.shape[1]
    tm = _pick(t, ROW_TILES)
    tn = _pick(f, COL_TILES)
    n_j = f // tn

    def body(h_ref, xn_ref, wg_ref, wu_ref, wd_ref, o_ref, acc_ref):
        j = pl.program_id(1)

        @pl.when(j == 0)
        def _():
            acc_ref[...] = jnp.zeros(acc_ref.shape, F32)

        x = xn_ref[...]
        g = jnp.dot(x, wg_ref[...], preferred_element_type=F32)
        u = jnp.dot(x, wu_ref[...], preferred_element_type=F32)
        a = (jax.nn.silu(g) * u).astype(BF16)
        acc_ref[...] += jnp.dot(a, wd_ref[...], preferred_element_type=F32)

        @pl.when(j == n_j - 1)
        def _():
            o_ref[...] = h_ref[...] + 0.5 * acc_ref[...]

    return pl.pallas_call(
        body, name=name, grid=(t // tm, n_j),
        in_specs=[pl.BlockSpec((tm, d), lambda i, j: (i, 0)), pl.BlockSpec((tm, d), lambda i, j: (i, 0)),
                  pl.BlockSpec((d, tn), lambda i, j: (0, j)), pl.BlockSpec((d, tn), lambda i, j: (0, j)),
                  pl.BlockSpec((tn, d), lambda i, j: (j, 0))],
        out_specs=pl.BlockSpec((tm, d), lambda i, j: (i, 0)),
        out_shape=jax.ShapeDtypeStruct((t, d), F32),
        scratch_shapes=[pltpu.VMEM((tm, d), F32)],
        compiler_params=_cparams(dimension_semantics=("arbitrary", "arbitrary")),
    )(h, xn, wg, wu, wd)


def ffn_bwd(xn, dh, wg, wu, wd, name):
    t, d = dh.shape
    f = wg.shape[1]
    tm = _pick(t, ROW_TILES)
    tn = _pick(f, COL_TILES)
    n_i = t // tm

    def body(xn_ref, dh_ref, wg_ref, wu_ref, wd_ref, dxn_ref, dwg_ref, dwu_ref, dwd_ref, ag_ref, au_ref, ad_ref):
        j, i = pl.program_id(0), pl.program_id(1)
        rows = pl.ds(pl.multiple_of(i * tm, 8), tm)
        x = xn_ref[rows, :]
        dhh = (0.5 * dh_ref[...]).astype(BF16)
        wgv, wuv = wg_ref[...], wu_ref[...]
        g = jnp.dot(x, wgv, preferred_element_type=F32)
        u = jnp.dot(x, wuv, preferred_element_type=F32)
        sg = jax.nn.sigmoid(g)
        s = g * sg
        da = lax.dot_general(dhh, wd_ref[...], NT_DIMS, preferred_element_type=F32)
        dwd = lax.dot_general((s * u).astype(BF16), dhh, TN_DIMS, preferred_element_type=F32)
        dg = (da * u * (sg * (1.0 + g * (1.0 - sg)))).astype(BF16)
        du = (da * s).astype(BF16)
        dwg = lax.dot_general(x, dg, TN_DIMS, preferred_element_type=F32)
        dwu = lax.dot_general(x, du, TN_DIMS, preferred_element_type=F32)
        dx = (lax.dot_general(dg, wgv, NT_DIMS, preferred_element_type=F32)
              + lax.dot_general(du, wuv, NT_DIMS, preferred_element_type=F32))

        @pl.when(i == 0)
        def _():
            ag_ref[...] = dwg
            au_ref[...] = dwu
            ad_ref[...] = dwd

        @pl.when(i > 0)
        def _():
            ag_ref[...] += dwg
            au_ref[...] += dwu
            ad_ref[...] += dwd

        @pl.when(i == n_i - 1)
        def _():
            dwg_ref[...] = ag_ref[...].astype(BF16)
            dwu_ref[...] = au_ref[...].astype(BF16)
            dwd_ref[...] = ad_ref[...].astype(BF16)

        @pl.when(j == 0)
        def _():
            dxn_ref[rows, :] = dx

        @pl.when(j > 0)
        def _():
            dxn_ref[rows, :] += dx

    return pl.pallas_call(
        body, name=name, grid=(f // tn, t // tm),
        in_specs=[pl.BlockSpec((t, d), lambda j, i: (0, 0)), pl.BlockSpec((tm, d), lambda j, i: (i, 0)),
                  pl.BlockSpec((d, tn), lambda j, i: (0, j)), pl.BlockSpec((d, tn), lambda j, i: (0, j)),
                  pl.BlockSpec((tn, d), lambda j, i: (j, 0))],
        out_specs=[pl.BlockSpec((t, d), lambda j, i: (0, 0)), pl.BlockSpec((d, tn), lambda j, i: (0, j)),
                   pl.BlockSpec((d, tn), lambda j, i: (0, j)), pl.BlockSpec((tn, d), lambda j, i: (j, 0))],
        out_shape=[jax.ShapeDtypeStruct((t, d), F32), jax.ShapeDtypeStruct((d, f), BF16),
                   jax.ShapeDtypeStruct((d, f), BF16), jax.ShapeDtypeStruct((f, d), BF16)],
        scratch_shapes=[pltpu.VMEM((d, tn), F32), pltpu.VMEM((d, tn), F32), pltpu.VMEM((tn, d), F32)],
        compiler_params=_cparams(dimension_semantics=("arbitrary", "arbitrary")),
    )(xn, dh, wg, wu, wd)


def matmul_cols(xn, w, name):
    t, d = xn.shape
    n = w.shape[1]
    tn = _pick(n, COL_TILES)

    def body(x_ref, w_ref, o_ref):
        o_ref[...] = jnp.dot(x_ref[...], w_ref[...], preferred_element_type=F32)

    return pl.pallas_call(
        body, name=name, grid=(n // tn,),
        in_specs=[pl.BlockSpec((t, d), lambda j: (0, 0)), pl.BlockSpec((d, tn), lambda j: (0, j))],
        out_specs=pl.BlockSpec((t, tn), lambda j: (0, j)),
        out_shape=jax.ShapeDtypeStruct((t, n), F32),
        compiler_params=_cparams(dimension_semantics=("arbitrary",)),
    )(xn, w)


def matmul_cols_bwd(xn, dy, w, name):
    t, d = xn.shape
    n = w.shape[1]
    tn = _pick(n, COL_TILES)

    def body(x_ref, dy_ref, w_ref, dx_ref, dw_ref):
        j = pl.program_id(0)
        dyv = dy_ref[...].astype(BF16)
        dw_ref[...] = lax.dot_general(x_ref[...], dyv, TN_DIMS, preferred_element_type=F32).astype(BF16)
        dx = lax.dot_general(dyv, w_ref[...], NT_DIMS, preferred_element_type=F32)

        @pl.when(j == 0)
        def _():
            dx_ref[...] = dx

        @pl.when(j > 0)
        def _():
            dx_ref[...] += dx

    return pl.pallas_call(
        body, name=name, grid=(n // tn,),
        in_specs=[pl.BlockSpec((t, d), lambda j: (0, 0)), pl.BlockSpec((t, tn), lambda j: (0, j)),
                  pl.BlockSpec((d, tn), lambda j: (0, j))],
        out_specs=[pl.BlockSpec((t, d), lambda j: (0, 0)), pl.BlockSpec((d, tn), lambda j: (0, j))],
        out_shape=[jax.ShapeDtypeStruct((t, d), F32), jax.ShapeDtypeStruct((d, n), BF16)],
        compiler_params=_cparams(dimension_semantics=("arbitrary",)),
    )(xn, dy, w)


def _peer(mx, my, mc, k):
    px = 1 - mx if (k >> 2) & 1 else mx
    py = 1 - my if (k >> 1) & 1 else my
    pc = 1 - mc if k & 1 else mc
    return (px, py, pc), 4 * px + 2 * py + pc


def a2a_copies(x_refs, o_refs, send_sems, recv_sems, local_sems):
    mx, my, mc = lax.axis_index("x"), lax.axis_index("y"), lax.axis_index("c")
    me = 4 * mx + 2 * my + mc
    peers = [_peer(mx, my, mc, k) for k in range(1, N_DEV)]
    locals_, sends, recvs = [], [], []
    for a, (x_ref, o_ref) in enumerate(zip(x_refs, o_refs, strict=True)):
        locals_.append(pltpu.make_async_copy(x_ref.at[me], o_ref.at[me], local_sems.at[a]))
        for k, (dev, peer) in enumerate(peers):
            common = dict(send_sem=send_sems.at[a, k], recv_sem=recv_sems.at[a, k], device_id=dev,
                          device_id_type=pl.DeviceIdType.MESH)
            sends.append(pltpu.make_async_remote_copy(src_ref=x_ref.at[peer], dst_ref=o_ref.at[me], **common))
            recvs.append(pltpu.make_async_remote_copy(src_ref=x_ref.at[peer], dst_ref=o_ref.at[peer], **common))
    return locals_, sends, recvs


def all_gather(xs, name, with_a2a=()):
    n, n2 = len(xs), len(with_a2a)
    chip_flips = (4, 2, 6)

    def body(*refs):
        x_refs, t_refs = refs[:n], refs[n:n + n2]
        o_refs, r_refs = refs[n + n2:2 * n + n2], refs[2 * n + n2:2 * n + 2 * n2]
        send_sems, recv_sems, local_sems = refs[2 * n + 2 * n2:2 * n + 2 * n2 + 3]
        mx, my, mc = lax.axis_index("x"), lax.axis_index("y"), lax.axis_index("c")
        me = 4 * mx + 2 * my + mc
        sib_dev, sib = _peer(mx, my, mc, 1)

        def copy(a, k, row, to, src=None):
            return pltpu.make_async_remote_copy(
                src_ref=o_refs[a].at[row] if src is None else src, dst_ref=o_refs[a].at[row],
                send_sem=send_sems.at[a, k], recv_sem=recv_sems.at[a, k], device_id=to,
                device_id_type=pl.DeviceIdType.MESH)

        locals_, first, passed = [], [], []
        t_recvs = []
        if n2:
            t_locals, t_sends, t_recvs = a2a_copies(t_refs, r_refs, *refs[2 * n + 2 * n2 + 3:])
            locals_ += t_locals
            first += t_sends
        for a in range(n):
            locals_.append(pltpu.make_async_copy(x_refs[a], o_refs[a].at[me], local_sems.at[a]))
            first.append(copy(a, 0, me, sib_dev, src=x_refs[a]))
            for j, f in enumerate(chip_flips):
                first.append(copy(a, 1 + j, me, _peer(mx, my, mc, f)[0], src=x_refs[a]))
        for cp in locals_ + first:
            cp.start()
        for a in range(n):
            for j, f in enumerate(chip_flips):
                row = _peer(mx, my, mc, f)[1]
                copy(a, 1 + j, row, sib_dev).wait_recv()
                fwd = copy(a, 4 + j, row, sib_dev)
                fwd.start()
                passed.append(fwd)
        for a in range(n):
            copy(a, 0, sib, sib_dev).wait_recv()
            for j, f in enumerate(chip_flips):
                copy(a, 4 + j, _peer(mx, my, mc, f ^ 1)[1], sib_dev).wait_recv()
        for cp in t_recvs:
            cp.wait_recv()
        for cp in first + passed:
            cp.wait_send()
        for cp in locals_:
            cp.wait()

    hbm = pl.BlockSpec(memory_space=pl.ANY)
    scratch = [pltpu.SemaphoreType.DMA((n, N_DEV - 1)), pltpu.SemaphoreType.DMA((n, N_DEV - 1)),
               pltpu.SemaphoreType.DMA((n,))]
    if n2:
        scratch += [pltpu.SemaphoreType.DMA((n2, N_DEV - 1)), pltpu.SemaphoreType.DMA((n2, N_DEV - 1)),
                    pltpu.SemaphoreType.DMA((n2,))]
    return pl.pallas_call(
        body, name=name, in_specs=[hbm] * (n + n2), out_specs=[hbm] * (n + n2),
        out_shape=[jax.ShapeDtypeStruct((N_DEV,) + x.shape, x.dtype) for x in xs]
        + [jax.ShapeDtypeStruct(x.shape, x.dtype) for x in with_a2a],
        scratch_shapes=scratch,
    )(*xs, *with_a2a)


AG_FIRST_COPIES = 4


def ag_first_copies(x_refs, o_refs, send_sems, recv_sems, local_sems):
    mx, my, mc = lax.axis_index("x"), lax.axis_index("y"), lax.axis_index("c")
    me = 4 * mx + 2 * my + mc
    targets = [_peer(mx, my, mc, f) for f in (1, 4, 2, 6)]
    locals_, sends, recvs = [], [], []
    for a, (x_ref, o_ref) in enumerate(zip(x_refs, o_refs, strict=True)):
        locals_.append(pltpu.make_async_copy(x_ref, o_ref.at[me], local_sems.at[a]))
        for k, (dev, row) in enumerate(targets):
            common = dict(send_sem=send_sems.at[a, k], recv_sem=recv_sems.at[a, k], device_id=dev,
                          device_id_type=pl.DeviceIdType.MESH)
            sends.append(pltpu.make_async_remote_copy(src_ref=x_ref, dst_ref=o_ref.at[me], **common))
            recvs.append(pltpu.make_async_remote_copy(src_ref=x_ref, dst_ref=o_ref.at[row], **common))
    return locals_, sends, recvs


def ag_second_level(bufs, name):
    n = len(bufs)
    chip_flips = (4, 2, 6)

    def body(*refs):
        o_refs = refs[n:2 * n]
        send_sems, recv_sems = refs[2 * n:]
        mx, my, mc = lax.axis_index("x"), lax.axis_index("y"), lax.axis_index("c")
        sib_dev, _ = _peer(mx, my, mc, 1)
        sends, recvs = [], []
        for a in range(n):
            for j, f in enumerate(chip_flips):
                common = dict(send_sem=send_sems.at[a, j], recv_sem=recv_sems.at[a, j], device_id=sib_dev,
                              device_id_type=pl.DeviceIdType.MESH)
                row, sib_row = _peer(mx, my, mc, f)[1], _peer(mx, my, mc, f ^ 1)[1]
                sends.append(pltpu.make_async_remote_copy(src_ref=o_refs[a].at[row], dst_ref=o_refs[a].at[row], **common))
                recvs.append(pltpu.make_async_remote_copy(src_ref=o_refs[a].at[row], dst_ref=o_refs[a].at[sib_row],
                                                          **common))
        for cp in sends:
            cp.start()
        for cp in recvs:
            cp.wait_recv()
        for cp in sends:
            cp.wait_send()

    return pl.pallas_call(
        body, name=name,
        in_specs=[pl.BlockSpec(memory_space=pl.ANY)] * n, out_specs=[pl.BlockSpec(memory_space=pl.ANY)] * n,
        out_shape=[jax.ShapeDtypeStruct(x.shape, x.dtype) for x in bufs],
        input_output_aliases={a: a for a in range(n)},
        scratch_shapes=[pltpu.SemaphoreType.DMA((n, len(chip_flips))), pltpu.SemaphoreType.DMA((n, len(chip_flips)))],
    )(*bufs)


PACK_COLS = 1024
PACK_ROWS = 256
PARTS_TILE_BYTES = 4 * 1024 * 1024


def adamw_reduce(parts, w, m, v, name):
    r, c = w.shape
    fits = [t for t in (512, 352, 256, 128, 64, 32, 16, 8) if N_DEV * t * c * parts.dtype.itemsize <= PARTS_TILE_BYTES]
    tr = r if r < 2 * SUBLANE else _pick(r, fits)
    c1 = 1.0 - ADAM_B1 ** ADAM_STEP
    c2 = 1.0 - ADAM_B2 ** ADAM_STEP

    def body(p_ref, w_ref, m_ref, v_ref, g_ref, d_ref, mo_ref, vo_ref):
        g = p_ref[0].astype(F32)
        for k in range(1, N_DEV):
            g = g + p_ref[k].astype(F32)
        mn = ADAM_B1 * m_ref[...] + (1.0 - ADAM_B1) * g
        vn = ADAM_B2 * v_ref[...] + (1.0 - ADAM_B2) * (g * g)
        g_ref[...] = g
        mo_ref[...] = mn
        vo_ref[...] = vn
        d_ref[...] = -ADAM_LR * ((mn / c1) / (jnp.sqrt(vn / c2) + ADAM_EPS) + ADAM_WD * w_ref[...])

    spec = pl.BlockSpec((tr, c), lambda i: (i, 0))
    return pl.pallas_call(
        body, name=name, grid=(r // tr,),
        in_specs=[pl.BlockSpec((N_DEV, tr, c), lambda i: (0, i, 0)), spec, spec, spec],
        out_specs=[spec] * 4, out_shape=[jax.ShapeDtypeStruct((r, c), F32)] * 4,
        compiler_params=_cparams(dimension_semantics=("arbitrary",)),
    )(parts, w, m, v)


def pack_flat(arrs, dtype):
    parts = []
    for a in arrs:
        flat = a.reshape(-1).astype(dtype)
        k = -(-flat.shape[0] // PACK_COLS)
        parts.append(jnp.pad(flat, (0, k * PACK_COLS - flat.shape[0])).reshape(k, PACK_COLS))
    buf = jnp.concatenate(parts, axis=0)
    return jnp.pad(buf, ((0, -buf.shape[0] % PACK_ROWS), (0, 0)))


def unpack_flat(buf, shapes):
    out, r0 = [], 0
    for s in shapes:
        n = math.prod(s)
        k = -(-n // PACK_COLS)
        out.append(buf[r0:r0 + k].reshape(-1)[:n].reshape(tuple(s)))
        r0 += k
    return out


W_NAMES = ('meta_tokens', 'ffn1_norm', 'ffn1_w_gate', 'ffn1_w_up', 'ffn1_w_down', 'mix_norm', 'w_in', 'w_out',
           'lru_conv_w', 'lru_conv_b', 'lru_w_a', 'lru_b_a', 'lru_w_i', 'lru_b_i', 'lru_lambda', 'lru_norm',
           'gdn_conv_w', 'gdn_a_log', 'gdn_dt_bias', 'gdn_norm', 'ssd_conv_w', 'ssd_conv_b', 'ssd_a_log',
           'ssd_dt_bias', 'ssd_d', 'ssd_norm', 's5_a_re', 's5_a_im', 's5_log_dt', 's5_b_re', 's5_b_im', 's5_c_re',
           's5_c_im', 's5_d', 's5_w_glu', 's5_norm', 'ffn2_norm', 'ffn2_w_gate', 'ffn2_w_up', 'ffn2_w_down',
           'final_norm')
SHARD_AXIS = {'meta_tokens': 1, 'ffn1_w_gate': 2, 'ffn1_w_up': 2, 'ffn1_w_down': 1, 'w_in': 2, 'w_out': 1,
              'lru_conv_w': 2, 'gdn_conv_w': 2, 'ssd_conv_w': 2, 's5_w_glu': 1, 'ffn2_w_gate': 2, 'ffn2_w_up': 2,
              'ffn2_w_down': 1}
BIG_NAMES = ('ffn1_w_gate', 'ffn1_w_up', 'ffn1_w_down', 'w_in', 'w_out', 's5_w_glu', 'ffn2_w_gate', 'ffn2_w_up',
             'ffn2_w_down')
SHARD_NAMES = tuple(n for n in W_NAMES if n in SHARD_AXIS)
REP_NAMES = tuple(n for n in W_NAMES if n not in SHARD_AXIS)
SSD_GROUPS = 2
S5_CH = 16


def unshard(g, axis):
    if axis == 0:
        return g.reshape((-1,) + g.shape[2:])
    return jnp.concatenate([g[p] for p in range(N_DEV)], axis=axis)


def kernel(*args):
    n_w = len(W_NAMES)
    x = args[0]
    w = dict(zip(W_NAMES, args[1:1 + n_w]))
    target = args[1 + n_w]
    m_in = dict(zip(W_NAMES, args[2 + n_w:2 + 2 * n_w]))
    v_in = dict(zip(W_NAMES, args[2 + 2 * n_w:2 + 3 * n_w]))

    depth, d = w['ffn1_norm'].shape
    seq = x.shape[1]
    n_meta = w['meta_tokens'].shape[0]
    pad = CHUNK - n_meta
    tp = pad + n_meta + seq
    wg = d // 2
    xbc_w = w['ssd_conv_w'].shape[-1] * N_DEV
    gdn_hd = w['gdn_norm'].shape[-1]
    gdn_h = wg // gdn_hd
    ssd_h = w['ssd_a_log'].shape[-1]
    lru_h = w['lru_w_a'].shape[1]
    s5_g, s5_n = w['s5_a_re'].shape[1:]
    s5_q = wg // S5_LANES
    row_tile = _pick(tp, (192, 96, 64))
    norm_tile = _pick(tp, ROW_TILES)

    LAYER_NAMES = tuple(n for n in SHARD_NAMES if n != 'meta_tokens')

    def local_of(n, l):
        return w[n][l].astype(BF16 if n in BIG_NAMES else F32)

    def unshard_layer(gathered):
        return {n: unshard(g, SHARD_AXIS[n] - 1) for n, g in gathered.items()}

    first = all_gather([local_of(n, 0) for n in LAYER_NAMES] + [w['meta_tokens']], "gather_weights")
    meta_full = unshard(first[-1], SHARD_AXIS['meta_tokens'])
    full = [unshard_layer(dict(zip(LAYER_NAMES, first[:-1])))] + [None] * (depth - 1)

    segs = [('a_x', wg), ('a_gate', wg), ('b_q', wg), ('b_k', wg), ('b_v', wg), ('b_z', wg), ('c_xbc', xbc_w),
            ('c_z', wg), ('d_u', wg), ('small_b', LANE), ('small_c', LANE)]
    off, o = {}, 0
    for nme, wd_ in segs:
        assert o % wd_ == 0, (nme, o, wd_)
        off[nme] = o
        o += wd_
    o_beta = 6 * wg
    o_cz = o_beta + 2 * gdn_h
    o_xbc = o_cz + wg
    o_dt = o_xbc + xbc_w
    o_du = o_dt + ssd_h

    def pack_cols(a):
        z = lambda k: jnp.zeros(a.shape[:-1] + (k,), a.dtype)
        return jnp.concatenate([a[..., :o_beta], a[..., o_xbc:o_dt], a[..., o_cz:o_xbc], a[..., o_du:],
                                a[..., o_beta:o_cz], z(LANE - 2 * gdn_h), a[..., o_dt:o_du], z(LANE - ssd_h)], axis=-1)

    def unpack_cols(a):
        sb, sc = off['small_b'], off['small_c']
        return jnp.concatenate([a[..., :o_beta], a[..., sb:sb + 2 * gdn_h], a[..., off['c_z']:off['c_z'] + wg],
                                a[..., off['c_xbc']:off['c_xbc'] + xbc_w], a[..., sc:sc + ssd_h],
                                a[..., off['d_u']:off['d_u'] + wg]], axis=-1)

    w_in_p = [None] * depth
    w_in_p[0] = pack_cols(full[0]['w_in'])

    def col(name, width):
        return off[name] // width

    def row(a):
        return a.reshape(1, -1)

    def layer_params(l):
        gcw = full[l]['gdn_conv_w']
        lru = [full[l]['lru_conv_w'], row(w['lru_conv_b'][l]), blockdiag_expand(w['lru_w_a'][l]), row(w['lru_b_a'][l]),
               blockdiag_expand(w['lru_w_i'][l]), row(w['lru_b_i'][l]), row(w['lru_lambda'][l]), row(w['lru_norm'][l])]
        gdn = [gcw[:, :wg], gcw[:, wg:2 * wg], gcw[:, 2 * wg:], row(w['gdn_a_log'][l]), row(w['gdn_dt_bias'][l]),
               row(w['gdn_norm'][l])]
        ssd = [full[l]['ssd_conv_w'], row(w['ssd_conv_b'][l]), row(w['ssd_a_log'][l]), row(w['ssd_dt_bias'][l]),
               row(w['ssd_d'][l]), row(w['ssd_norm'][l])]
        s5 = list(s5_params_expand(*[w[n][l] for n in ('s5_a_re', 's5_a_im', 's5_log_dt', 's5_b_re', 's5_b_im',
                                                          's5_c_re', 's5_c_im', 's5_d')]))
        post = [full[l]['s5_w_glu'], row(w['s5_norm'][l])]
        return lru, gdn, ssd, s5, post

    lru_fn = functools.partial(lru_chunk, pad)
    gdn_fn = functools.partial(gdn_multi, pad, CHUNK)
    ssd_fn = functools.partial(ssd_multi, pad, 0, SSD_GROUPS, CHUNK)
    s5_fn = functools.partial(s5_chunk, pad)
    post_fn = functools.partial(s5_post, pad)
    n_state_lanes = (S5_LANES // S5_CH) * s5_n

    def mixer_specs(proj):
        lru_seqs = [(proj, wg, col('a_x', wg)), (proj, wg, col('a_gate', wg))]
        gdn_seqs = [(proj, wg, col('b_q', wg)), (proj, wg, col('b_k', wg)), (proj, wg, col('b_v', wg)),
                    (proj, wg, col('b_z', wg)), (proj, LANE, col('small_b', LANE))]
        ssd_seqs = [(proj, wg, col('c_z', wg)), (proj, xbc_w, col('c_xbc', xbc_w)), (proj, LANE, col('small_c', LANE))]
        base = col('d_u', S5_LANES)
        s5_seqs = [(proj, S5_LANES, lambda g: base + g)]
        return lru_seqs, gdn_seqs, ssd_seqs, s5_seqs

    lru_carry = [(CONV_TAIL, wg), (1, wg)]
    gdn_carry = [(CONV_TAIL, wg)] * 3 + [(wg, gdn_hd)]
    ssd_carry = [(CONV_TAIL, xbc_w), (wg, (xbc_w - wg) // (2 * SSD_GROUPS))]
    s5_carry = [(1, n_state_lanes)] * 2
    rk = dict(n_steps=tp // row_tile, rows=row_tile)
    mk = rk
    out_w = (wg, wg, 0, F32)

    h = jnp.concatenate([jnp.zeros((pad, d), F32), meta_full, x[0]], axis=0)
    target_p = jnp.concatenate([jnp.zeros((pad + n_meta, d), F32), target[0]], axis=0)
    saved = []
    for l in range(depth):
        lru_p, gdn_p, ssd_p, s5_p, post_p = layer_params(l)
        h0 = h
        xn1 = rms_fwd(h0, row(w['ffn1_norm'][l]), "rms_fwd", norm_tile)
        nxt = l + 1 < depth
        take = lambda *names: [local_of(n, l + 1) for n in names] if nxt else []
        fw = full[l]
        h1 = ffn_fwd(h0, xn1, fw['ffn1_w_gate'], fw['ffn1_w_up'], fw['ffn1_w_down'], "ffn_fwd")
        xn2 = rms_fwd(h1, row(w['mix_norm'][l]), "rms_fwd", norm_tile)
        proj = matmul_cols(xn2, w_in_p[l], "mix_in_fwd")
        lru_s, gdn_s, ssd_s, s5_s = mixer_specs(proj)
        got = {}
        names = ('ffn2_w_gate',)
        ya, lru_c, bufs = mixer_fwd(lru_fn, "lru_fwd", seqs=lru_s, params=lru_p, out=out_w, carry=lru_carry,
                                    ag=take(*names), **mk)
        got.update(zip(names, bufs))
        names = ('w_in',)
        yb, gdn_c, bufs = mixer_fwd(gdn_fn, "gdn_fwd", seqs=gdn_s, params=gdn_p, out=out_w, carry=gdn_carry,
                                    ag=take(*names), **mk)
        got.update(zip(names, bufs))
        names = ('w_out', 's5_w_glu', 'lru_conv_w', 'gdn_conv_w', 'ssd_conv_w')
        yc, ssd_c, bufs = mixer_fwd(ssd_fn, "ssd_fwd", seqs=ssd_s, params=ssd_p, out=out_w, carry=ssd_carry,
                                    ag=take(*names), **mk)
        got.update(zip(names, bufs))
        names = ('ffn1_w_gate', 'ffn1_w_up', 'ffn1_w_down')
        y1, s5_c, bufs = mixer_fwd(s5_fn, "s5_fwd", seqs=s5_s, params=s5_p, out=(wg, S5_LANES, lambda g: g, F32),
                                   carry=s5_carry, n_groups=s5_q, ag=take(*names), **mk)
        got.update(zip(names, bufs))
        names = ('ffn2_w_down',)
        yd, _, bufs = mixer_fwd(post_fn, "s5_post_fwd", seqs=[(y1, wg, 0)], params=post_p, out=out_w, carry=[],
                                ag=take(*names), **rk)
        got.update(zip(names, bufs))
        names = ('ffn2_w_up',)
        h2, bufs = mix_out_fwd(h1, [ya, yb, yc, yd], fw['w_out'], "mix_out_fwd", row_tile, ag=take(*names))
        got.update(zip(names, bufs))
        if nxt:
            full[l + 1] = unshard_layer(dict(zip(LAYER_NAMES, ag_second_level([got[n] for n in LAYER_NAMES],
                                                                                "gather_pass_on"))))
            w_in_p[l + 1] = pack_cols(full[l + 1]['w_in'])
        xn3 = rms_fwd(h2, row(w['ffn2_norm'][l]), "rms_fwd", norm_tile)
        h3 = ffn_fwd(h2, xn3, fw['ffn2_w_gate'], fw['ffn2_w_up'], fw['ffn2_w_down'], "ffn_fwd")
        saved.append((h0, xn1, h1, xn2, proj, (ya, yb, yc, yd), y1, (lru_c, gdn_c, ssd_c, s5_c), h2, xn3))
        h = h3

    loss_part, dh, d_final = loss_and_grad(h, target_p, row(w['final_norm']), "loss", norm_tile, pad + n_meta)
    loss = lax.psum(loss_part, ("x", "y", "c"))

    def shards_of(a, axis):
        sh = a.shape
        return jnp.moveaxis(a.reshape(sh[:axis] + (N_DEV, sh[axis] // N_DEV) + sh[axis + 1:]), axis, 0)

    def to_send(n, g):
        return shards_of(g, SHARD_AXIS[n] - 1).astype(BF16)

    received = {n: [None] * depth for n in SHARD_NAMES if n != 'meta_tokens'}
    pending = []

    def hosted(names_layers):
        keys = [k for k in pending if (k[0], k[1]) in names_layers]
        for k in keys:
            pending.remove(k)
        return [(k[0], k[1]) for k in keys], [k[2] for k in keys]

    def store(keys, arrays):
        for (n, l), a in zip(keys, arrays, strict=True):
            received[n][l] = a

    gw = {n: [None] * depth for n in W_NAMES if n not in ('meta_tokens', 'final_norm')}
    for l in reversed(range(depth)):
        lru_p, gdn_p, ssd_p, s5_p, post_p = layer_params(l)
        h0, xn1, h1, xn2, proj, ys, y1, (lru_c, gdn_c, ssd_c, s5_c), h2, xn3 = saved[l]
        dxn, gw['ffn2_w_gate'][l], gw['ffn2_w_up'][l], gw['ffn2_w_down'][l] = ffn_bwd(
            xn3, dh, full[l]['ffn2_w_gate'], full[l]['ffn2_w_up'], full[l]['ffn2_w_down'], "ffn_bwd")
        pending += [(n, l, to_send(n, gw[n][l])) for n in ('ffn2_w_gate', 'ffn2_w_up', 'ffn2_w_down')]
        dh, dg = rms_bwd_add(h2, dxn, dh, row(w['ffn2_norm'][l]), "rms_bwd", norm_tile)
        gw['ffn2_norm'][l] = dg[0]

        dys, (d_wout,) = mixer_bwd(mix_out_delta, "mix_out_bwd", seqs=[(y, wg, 0) for y in ys],
                                   params=[full[l]['w_out']], dout=(dh, d, 0), saved=[], carry=[], **rk)
        gw['w_out'][l] = d_wout[0]
        pending.append(('w_out', l, to_send('w_out', gw['w_out'][l])))
        lru_s, gdn_s, ssd_s, s5_s = mixer_specs(proj)
        (dy1,), d_post = mixer_bwd(post_fn, "s5_post_bwd", seqs=[(y1, wg, 0)], params=post_p, dout=(dys[3], wg, 0),
                                   saved=[], carry=[], **rk)
        gw['s5_w_glu'][l], gw['s5_norm'][l] = d_post[0][0], d_post[1][0, 0]
        pending.append(('s5_w_glu', l, to_send('s5_w_glu', gw['s5_w_glu'][l])))
        keys, arrs = hosted({('ffn1_w_gate', l + 1), ('ffn1_w_up', l + 1), ('ffn1_w_down', l + 1)})
        (d_du,), d_s5, *got = mixer_bwd(s5_fn, "s5_bwd", seqs=s5_s, params=s5_p, dout=(dy1, S5_LANES, lambda g: g),
                                        saved=s5_c, carry=s5_carry, n_groups=s5_q, a2a=arrs, **mk)
        store(keys, got[0] if got else [])
        for n, g in zip(('s5_a_re', 's5_a_im', 's5_log_dt', 's5_b_re', 's5_b_im', 's5_c_re', 's5_c_im', 's5_d'),
                        s5_grads_extract(d_s5, s5_g, s5_n, S5_CH)):
            gw[n][l] = g
        keys, arrs = hosted({('w_in', l + 1), ('w_out', l), ('s5_w_glu', l)})
        (d_cz, d_cxbc, d_sc), d_ssd, *got = mixer_bwd(ssd_fn, "ssd_bwd", seqs=ssd_s, params=ssd_p,
                                                      dout=(dys[2], wg, 0), saved=ssd_c, carry=ssd_carry, a2a=arrs, **mk)
        store(keys, got[0] if got else [])
        for n, g in zip(('ssd_conv_w', 'ssd_conv_b', 'ssd_a_log', 'ssd_dt_bias', 'ssd_d', 'ssd_norm'), d_ssd):
            gw[n][l] = g[0] if n == 'ssd_conv_w' else g[0, 0]
        keys, arrs = hosted({('ffn2_w_gate', l), ('ffn2_w_up', l), ('ffn2_w_down', l)})
        (d_bq, d_bk, d_bv, d_bz, d_sb), d_gdn, *got = mixer_bwd(gdn_fn, "gdn_bwd", seqs=gdn_s, params=gdn_p,
                                                                dout=(dys[1], wg, 0), saved=gdn_c, carry=gdn_carry,
                                                                a2a=arrs, **mk)
        store(keys, got[0] if got else [])
        gw['gdn_conv_w'][l] = jnp.concatenate([d_gdn[0][0], d_gdn[1][0], d_gdn[2][0]], axis=1)
        gw['gdn_a_log'][l], gw['gdn_dt_bias'][l], gw['gdn_norm'][l] = d_gdn[3][0, 0], d_gdn[4][0, 0], d_gdn[5][0, 0]
        (d_ax, d_ag), d_lru = mixer_bwd(lru_fn, "lru_bwd", seqs=lru_s, params=lru_p, dout=(dys[0], wg, 0),
                                        saved=lru_c, carry=lru_carry, **mk)
        gw['lru_conv_w'][l], gw['lru_conv_b'][l] = d_lru[0][0], d_lru[1][0, 0]
        gw['lru_w_a'][l], gw['lru_b_a'][l] = blockdiag_extract(d_lru[2][0], lru_h), d_lru[3][0, 0]
        gw['lru_w_i'][l], gw['lru_b_i'][l] = blockdiag_extract(d_lru[4][0], lru_h), d_lru[5][0, 0]
        gw['lru_lambda'][l], gw['lru_norm'][l] = d_lru[6][0, 0], d_lru[7][0, 0]

        dproj = jnp.concatenate([d_ax, d_ag, d_bq, d_bk, d_bv, d_bz, d_cxbc, d_cz, d_du, d_sb, d_sc], axis=1)
        dxn, d_win_p = matmul_cols_bwd(xn2, dproj, w_in_p[l], "mix_in_bwd")
        gw['w_in'][l] = unpack_cols(d_win_p)
        pending.append(('w_in', l, to_send('w_in', gw['w_in'][l])))
        dh, dg = rms_bwd_add(h1, dxn, dh, row(w['mix_norm'][l]), "rms_bwd", norm_tile)
        gw['mix_norm'][l] = dg[0]

        dxn, gw['ffn1_w_gate'][l], gw['ffn1_w_up'][l], gw['ffn1_w_down'][l] = ffn_bwd(
            xn1, dh, full[l]['ffn1_w_gate'], full[l]['ffn1_w_up'], full[l]['ffn1_w_down'], "ffn_bwd")
        dh, dg = rms_bwd_add(h0, dxn, dh, row(w['ffn1_norm'][l]), "rms_bwd", norm_tile)
        gw['ffn1_norm'][l] = dg[0]
        pending += [(n, l, to_send(n, gw[n][l])) for n in ('ffn1_w_gate', 'ffn1_w_up', 'ffn1_w_down')]

    grad_x = dh[pad + n_meta:][None]
    grads = {n: jnp.stack(gw[n], axis=0) for n in REP_NAMES if n != 'final_norm'}
    grads['meta_tokens'] = dh[pad:pad + n_meta]
    grads['final_norm'] = d_final[0]

    for n in ('lru_conv_w', 'gdn_conv_w', 'ssd_conv_w'):
        pending += [(n, l, to_send(n, gw[n][l])) for l in range(depth)]
    last = [k[2] for k in pending] + [shards_of(grads['meta_tokens'], SHARD_AXIS['meta_tokens']).astype(BF16)]
    recv_rep, *got, recv_meta = all_gather([pack_flat([grads[n] for n in REP_NAMES], BF16)], "last_exchange",
                                           with_a2a=last)
    store([(k[0], k[1]) for k in pending], got)
    recv_sh = [recv_meta if n == 'meta_tokens' else jnp.stack(received[n], axis=1) for n in SHARD_NAMES]
    out = {}
    for n, recv in zip(SHARD_NAMES, recv_sh):
        c = w[n].shape[-1]
        res = adamw_reduce(recv.reshape(N_DEV, -1, c), w[n].reshape(-1, c), m_in[n].reshape(-1, c),
                           v_in[n].reshape(-1, c), "adamw_" + n)
        for kind, buf in zip(('grad', 'delta', 'new_m', 'new_v'), res):
            out[kind, n] = buf.reshape(w[n].shape)
    res = adamw_reduce(recv_rep, pack_flat([w[n] for n in REP_NAMES], F32), pack_flat([m_in[n] for n in REP_NAMES], F32),
                       pack_flat([v_in[n] for n in REP_NAMES], F32), "adamw_replicated")
    shapes = [w[n].shape for n in REP_NAMES]
    for kind, buf in zip(('grad', 'delta', 'new_m', 'new_v'), res):
        for n, a in zip(REP_NAMES, unpack_flat(buf, shapes)):
            out[kind, n] = a
    return (loss, grad_x) + tuple(out[k, n] for k in ('grad', 'delta', 'new_m', 'new_v') for n in W_NAMES)
```

```python
import functools
import math

import jax
import jax.numpy as jnp
from jax import lax
from jax.experimental import pallas as pl
from jax.experimental.pallas import tpu as pltpu

F32 = jnp.float32
BF16 = jnp.bfloat16

EPS = 1e-6
CHUNK = 64
CONV_K = 4
CONV_TAIL = 8
LRU_C = 8.0
LANE = 128
SUBLANE = 8
N_DEV = 8
NEG_BIG = -1e30

ADAM_LR = 0.001
ADAM_B1 = 0.9
ADAM_B2 = 0.999
ADAM_EPS = 1e-08
ADAM_WD = 0.01
ADAM_STEP = 10

VMEM_LIMIT = 56 * 1024 * 1024


def _dg(a, b, dims):
    return lax.dot_general(a.astype(BF16), b.astype(BF16), (dims, ((), ())), preferred_element_type=F32)


@jax.custom_vjp
def bdot(a, b):
    return _dg(a, b, ((1,), (0,)))


@jax.custom_vjp
def bdot_nt(a, b):
    return _dg(a, b, ((1,), (1,)))


@jax.custom_vjp
def bdot_tn(a, b):
    return _dg(a, b, ((0,), (0,)))


bdot.defvjp(lambda a, b: (bdot(a, b), (a, b)),
            lambda r, g: (bdot_nt(g, r[1]).astype(r[0].dtype), bdot_tn(r[0], g).astype(r[1].dtype)))
bdot_nt.defvjp(lambda a, b: (bdot_nt(a, b), (a, b)),
               lambda r, g: (bdot(g, r[1]).astype(r[0].dtype), bdot_tn(g, r[0]).astype(r[1].dtype)))
bdot_tn.defvjp(lambda a, b: (bdot_tn(a, b), (a, b)),
               lambda r, g: (bdot_nt(r[1], g).astype(r[0].dtype), bdot(r[0], g).astype(r[1].dtype)))


def _split_bf16(a):
    hi = a.astype(BF16)
    return hi, (a - hi.astype(F32)).astype(BF16)


def _dot3(a, b, dims):
    (ah, al), (bh, bl) = _split_bf16(a), _split_bf16(b)
    d = lambda x, y: lax.dot_general(x, y, (dims, ((), ())), preferred_element_type=F32)
    return d(ah, bh) + (d(ah, bl) + d(al, bh))


@jax.custom_vjp
def hdot(a, b):
    return _dot3(a, b, ((1,), (0,)))


@jax.custom_vjp
def hdot_tn(a, b):
    return _dot3(a, b, ((0,), (0,)))


hdot.defvjp(lambda a, b: (hdot(a, b), (a, b)),
            lambda r, g: (_dot3(g, r[1], ((1,), (1,))), _dot3(r[0], g, ((0,), (0,)))))
hdot_tn.defvjp(lambda a, b: (hdot_tn(a, b), (a, b)),
               lambda r, g: (_dot3(r[1], g, ((1,), (1,))), _dot3(r[0], g, ((1,), (0,)))))


def rms_norm(x, g):
    return x * lax.rsqrt(jnp.mean(x * x, axis=-1, keepdims=True) + EPS) * g


def row_mask(row0, rows, pad):
    r = row0 + lax.broadcasted_iota(jnp.int32, (rows, 1), 0)
    return (r >= pad).astype(F32)


def conv4(tail, u, w):
    rows = u.shape[0]
    xe = jnp.concatenate([tail, u], axis=0)
    y = w[0:1] * xe[CONV_TAIL - 3:CONV_TAIL - 3 + rows]
    for k in range(1, CONV_K):
        y = y + w[k:k + 1] * xe[CONV_TAIL - 3 + k:CONV_TAIL - 3 + k + rows]
    return y


def shift_rows(x, s, fill):
    rows = x.shape[0]
    return jnp.concatenate([jnp.full((s, x.shape[1]), fill, x.dtype), x[:rows - s]], axis=0)


def lin_scan(a, b):
    rows = a.shape[0]
    s = 1
    while s < rows:
        b = a * shift_rows(b, s, 0.0) + b
        a = a * shift_rows(a, s, 1.0)
        s *= 2
    return b


def cscan_const(ar, ai, br, bi):
    rows = br.shape[0]
    s = 1
    while s < rows:
        brs, bis = shift_rows(br, s, 0.0), shift_rows(bi, s, 0.0)
        br, bi = br + ar * brs - ai * bis, bi + ar * bis + ai * brs
        ar, ai = ar * ar - ai * ai, 2.0 * ar * ai
        s *= 2
    return br, bi


def neg_expm1(z):
    t = jnp.tanh(0.5 * z)
    return -2.0 * t / (1.0 - t)


def tri_masks(n):
    r = lax.broadcasted_iota(jnp.int32, (n, n), 0)
    c = lax.broadcasted_iota(jnp.int32, (n, n), 1)
    return r >= c, r > c, (r == c).astype(F32)


def lru_chunk(pad, row0, params, seqs, carry):
    conv_w, conv_b, w_a, b_a, w_i, b_i, lam, norm_g = params
    u_x, u_gate = seqs
    tail, h0 = carry
    rows = u_x.shape[0]
    m = row_mask(row0, rows, pad)
    xc = conv4(tail, u_x, conv_w) + conv_b
    r = jax.nn.sigmoid(bdot(xc, w_a) + b_a)
    ig = jax.nn.sigmoid(bdot(xc, w_i) + b_i)
    log_a = -LRU_C * r * jax.nn.softplus(-lam)
    a = jnp.exp(log_a)
    b = jnp.sqrt(neg_expm1(2.0 * log_a)) * (ig * xc) * m
    first = (lax.broadcasted_iota(jnp.int32, (rows, 1), 0) == 0).astype(F32)
    b = b + first * (a * h0)
    h = lin_scan(a, b)
    y = jax.nn.gelu(u_gate) * h
    out = rms_norm(y, norm_g) * m
    return (out,), (u_x[rows - CONV_TAIL:], h[rows - 1:])


def gdn_multi(pad, sub, row0, params, seqs, carry):
    wq, wk, wv, a_log, dt_bias, norm_g = params
    u_q, u_k, u_v, u_z, small = seqs
    tq, tk, tv, state = carry
    rows = u_q.shape[0]
    hd = norm_g.shape[1]
    nh = u_q.shape[1] // hd
    nc = rows // sub
    m = row_mask(row0, rows, pad)
    incl, strict, eye = tri_masks(sub)
    tril = incl.astype(F32)
    triu = (lax.broadcasted_iota(jnp.int32, (sub, sub), 0) <= lax.broadcasted_iota(jnp.int32, (sub, sub), 1)).astype(F32)
    qc = jax.nn.silu(conv4(tq, u_q, wq))
    kc = jax.nn.silu(conv4(tk, u_k, wk))
    vc = jax.nn.silu(conv4(tv, u_v, wv))
    beta = jax.nn.sigmoid(small[:, :nh]) * m
    g = -jnp.exp(a_log) * jax.nn.softplus(small[:, nh:2 * nh] + dt_bias) * m
    gate = jax.nn.silu(u_z)
    heads = [slice(h * hd, (h + 1) * hd) for h in range(nh)]
    q_h = [qc[:, sl] for sl in heads]
    k_h = [kc[:, sl] for sl in heads]
    q_h = [q * lax.rsqrt(jnp.sum(q * q, axis=-1, keepdims=True) + EPS) * (hd ** -0.5) * m for q in q_h]
    k_h = [k * lax.rsqrt(jnp.sum(k * k, axis=-1, keepdims=True) + EPS) * m for k in k_h]
    v_h = [vc[:, sl] * m for sl in heads]
    pairs = [(c, h) for c in range(nc) for h in range(nh)]
    cs = lambda x, c: x[c * sub:(c + 1) * sub]
    q = {(c, h): cs(q_h[h], c) for c, h in pairs}
    k = {(c, h): cs(k_h[h], c) for c, h in pairs}
    v = {(c, h): cs(v_h[h], c) for c, h in pairs}
    bt = {(c, h): cs(beta, c)[:, h:h + 1] for c, h in pairs}
    gcs = [hdot(tril, cs(g, c)) for c in range(nc)]
    gts = [hdot_tn(cs(g, c), triu) for c in range(nc)]
    gc = {(c, h): gcs[c][:, h:h + 1] for c, h in pairs}
    decay = {(c, h): jnp.exp(jnp.where(incl, gc[c, h] - gts[c][h:h + 1], NEG_BIG)) for c, h in pairs}
    kb = {p: k[p] * bt[p] for p in pairs}
    kk = {p: bdot_nt(kb[p], k[p]) for p in pairs}
    lmat = {p: jnp.where(strict, kk[p] * decay[p], 0.0) for p in pairs}
    pm = {p: eye - lmat[p] for p in pairs}
    mm = {p: hdot(lmat[p], lmat[p]) for p in pairs}
    s = 2
    while s < sub:
        pm = {p: pm[p] + hdot(pm[p], mm[p]) for p in pairs}
        s *= 2
        if s < sub:
            mm = {p: hdot(mm[p], mm[p]) for p in pairs}
    eg = {p: jnp.exp(gc[p]) for p in pairs}
    u = {p: hdot(pm[p], v[p] * bt[p]) for p in pairs}
    w = {p: hdot(pm[p], kb[p] * eg[p]) for p in pairs}
    attn = {p: bdot_nt(q[p], k[p]) * decay[p] for p in pairs}
    qd = {p: q[p] * eg[p] for p in pairs}
    g_last = {p: gc[p][sub - 1:] for p in pairs}
    kd = {p: k[p] * jnp.exp(g_last[p] - gc[p]) for p in pairs}
    last = {p: jnp.exp(g_last[p]) for p in pairs}
    s_h = [state[sl] for sl in heads]
    o = {}
    for c in range(nc):
        ws = [bdot(w[c, h], s_h[h]) for h in range(nh)]
        qs = [bdot(qd[c, h], s_h[h]) for h in range(nh)]
        v_new = [u[c, h] - ws[h] for h in range(nh)]
        av = [bdot(attn[c, h], v_new[h]) for h in range(nh)]
        kv = [bdot_tn(kd[c, h], v_new[h]) for h in range(nh)]
        for h in range(nh):
            o[c, h] = qs[h] + av[h]
        s_h = [s_h[h] * last[c, h] + kv[h] for h in range(nh)]
    out = jnp.concatenate([jnp.concatenate([rms_norm(o[c, h], norm_g) for h in range(nh)], axis=1)
                           for c in range(nc)], axis=0) * gate * m
    t0 = rows - CONV_TAIL
    return (out,), (u_q[t0:], u_k[t0:], u_v[t0:], jnp.concatenate(s_h, axis=0))


def ssd_multi(pad, dt_lane0, n_groups, sub, row0, params, seqs, carry):
    conv_w, conv_b, a_log, dt_bias, d_skip, norm_g = params
    u_z, u_xbc, small = seqs
    tail, state = carry
    rows = u_z.shape[0]
    width = u_z.shape[1]
    nh = a_log.shape[1]
    hd = width // nh
    ns = (u_xbc.shape[1] - width) // (2 * n_groups)
    hpg = nh // n_groups
    nc = rows // sub
    m = row_mask(row0, rows, pad)
    incl, _, _ = tri_masks(sub)
    tril = incl.astype(F32)
    triu = (lax.broadcasted_iota(jnp.int32, (sub, sub), 0) <= lax.broadcasted_iota(jnp.int32, (sub, sub), 1)).astype(F32)
    xbc = jax.nn.silu(conv4(tail, u_xbc, conv_w) + conv_b)
    xs = xbc[:, :width]
    dt = jax.nn.softplus(small[:, dt_lane0:dt_lane0 + nh] + dt_bias)
    a_all = dt * (-jnp.exp(a_log)) * m
    cs = lambda x, c: x[c * sub:(c + 1) * sub]
    heads = [slice(h * hd, (h + 1) * hd) for h in range(nh)]
    pairs = [(c, h) for c in range(nc) for h in range(nh)]
    grp = lambda h: h // hpg
    bm = {(c, g): cs(xbc[:, width + g * ns: width + (g + 1) * ns] * m, c) for c in range(nc) for g in range(n_groups)}
    cm = {(c, g): cs(xbc[:, width + (n_groups + g) * ns: width + (n_groups + g + 1) * ns] * m, c)
          for c in range(nc) for g in range(n_groups)}
    xh = {(c, h): cs(xs[:, heads[h]], c) for c, h in pairs}
    xdt = {(c, h): xh[c, h] * cs(dt[:, h:h + 1] * m, c) for c, h in pairs}
    acums = [hdot(tril, cs(a_all, c)) for c in range(nc)]
    acts = [hdot_tn(cs(a_all, c), triu) for c in range(nc)]
    acum = {(c, h): acums[c][:, h:h + 1] for c, h in pairs}
    a_last = {p: acum[p][sub - 1:] for p in pairs}
    lmat = {(c, h): jnp.exp(jnp.where(incl, acum[c, h] - acts[c][h:h + 1], NEG_BIG)) for c, h in pairs}
    cb = {cg: bdot_nt(cm[cg], bm[cg]) for cg in bm}
    y_diag = {(c, h): bdot(cb[c, grp(h)] * lmat[c, h], xdt[c, h]) for c, h in pairs}
    st = {(c, h): bdot_tn(xdt[c, h] * jnp.exp(a_last[c, h] - acum[c, h]), bm[c, grp(h)]) for c, h in pairs}
    e_in = {p: jnp.exp(acum[p]) for p in pairs}
    e_out = {p: jnp.exp(a_last[p]) for p in pairs}
    s_h = [state[sl] for sl in heads]
    y = {}
    for c in range(nc):
        off = [bdot_nt(cm[c, grp(h)], s_h[h]) for h in range(nh)]
        for h in range(nh):
            y[c, h] = y_diag[c, h] + off[h] * e_in[c, h] + d_skip[:, h:h + 1] * xh[c, h]
        s_h = [s_h[h] * e_out[c, h] + st[c, h] for h in range(nh)]
    yy = jnp.concatenate([jnp.concatenate([y[c, h] for h in range(nh)], axis=1) for c in range(nc)], axis=0)
    yy = yy * jax.nn.silu(u_z)
    gw = width // n_groups
    outs = [rms_norm(yy[:, g * gw:(g + 1) * gw], norm_g[:, g * gw:(g + 1) * gw]) for g in range(n_groups)]
    out = jnp.concatenate(outs, axis=1) * m
    return (out,), (u_xbc[rows - CONV_TAIL:], jnp.concatenate(s_h, axis=0))


def s5_chunk(pad, row0, params, seqs, carry):
    a_re, a_im, log_dt, b_re, b_im, c_re, c_im, d_skip = params
    (u,) = seqs
    s_re0, s_im0 = carry
    rows = u.shape[0]
    n_state = a_re.shape[1]
    n_grp = log_dt.shape[1]
    per = n_state // n_grp
    expand = (lax.broadcasted_iota(jnp.int32, (n_grp, n_state), 1) // per
              == lax.broadcasted_iota(jnp.int32, (n_grp, n_state), 0)).astype(F32)
    dt = jnp.exp(hdot(log_dt, expand))
    lam_re = jnp.minimum(a_re, -1e-4)
    lam_im = a_im
    mag = jnp.exp(dt * lam_re)
    ab_re = mag * jnp.cos(dt * lam_im)
    ab_im = mag * jnp.sin(dt * lam_im)
    den = lam_re * lam_re + lam_im * lam_im
    f_re = ((ab_re - 1.0) * lam_re + ab_im * lam_im) / den
    f_im = (ab_im * lam_re - (ab_re - 1.0) * lam_im) / den
    bb_re = f_re * b_re - f_im * b_im
    bb_im = f_re * b_im + f_im * b_re
    bu_re = bdot(u, bb_re)
    bu_im = bdot(u, bb_im)
    first = (lax.broadcasted_iota(jnp.int32, (rows, 1), 0) == 0).astype(F32)
    bu_re = bu_re + first * (ab_re * s_re0 - ab_im * s_im0)
    bu_im = bu_im + first * (ab_re * s_im0 + ab_im * s_re0)
    s_re, s_im = cscan_const(ab_re, ab_im, bu_re, bu_im)
    y = bdot(s_re, c_re) - bdot(s_im, c_im) + d_skip * u
    return (y,), (s_re[rows - 1:], s_im[rows - 1:])


def s5_post(pad, row0, params, seqs, carry):
    w_glu, norm_g = params
    (y,) = seqs
    y = jax.nn.gelu(y)
    y = y * jax.nn.sigmoid(bdot(y, w_glu))
    return (rms_norm(y, norm_g),), ()


def mix_out_delta(row0, params, seqs, carry):
    (w_out,) = params
    wd = seqs[0].shape[1]
    acc = bdot(seqs[0], w_out[0:wd])
    for k in range(1, len(seqs)):
        acc = acc + bdot(seqs[k], w_out[k * wd:(k + 1) * wd])
    return (acc,), ()


def blockdiag_expand(w):
    nh, a, b = w.shape
    eye = jnp.eye(nh, dtype=w.dtype)
    return (w[:, :, None, :] * eye[:, None, :, None]).reshape(nh * a, nh * b)


def blockdiag_extract(m, nh):
    a, b = m.shape[0] // nh, m.shape[1] // nh
    on_diag = jnp.eye(nh, dtype=bool)[:, None, :, None]
    return jnp.sum(jnp.where(on_diag, m.reshape(nh, a, nh, b), 0.0), axis=2)


S5_LANES = LANE


def s5_params_expand(a_re, a_im, log_dt, b_re, b_im, c_re, c_im, d_skip):
    n_grp, n_state = a_re.shape
    ch = b_re.shape[-1]
    gpl = S5_LANES // ch
    nq = n_grp // gpl
    eye = jnp.eye(gpl, dtype=F32)[None, :, None, :, None]

    def bexp(b):
        bt = jnp.swapaxes(b, 1, 2).reshape(nq, gpl, b.shape[2], 1, b.shape[1])
        return (bt * eye).reshape(nq, gpl * b.shape[2], gpl * b.shape[1])

    return (a_re.reshape(nq, 1, gpl * n_state), a_im.reshape(nq, 1, gpl * n_state), log_dt.reshape(nq, 1, gpl),
            bexp(b_re), bexp(b_im), bexp(c_re), bexp(c_im), d_skip.reshape(nq, 1, S5_LANES))


def s5_grads_extract(grads, n_grp, n_state, ch):
    da_re, da_im, dlog_dt, db_re, db_im, dc_re, dc_im, dd = grads
    gpl = S5_LANES // ch
    nq = n_grp // gpl
    on_diag = jnp.eye(gpl, dtype=bool)[None, :, None, :, None]

    def bext(b):
        r, c = b.shape[1] // gpl, b.shape[2] // gpl
        d = jnp.sum(jnp.where(on_diag, b.reshape(nq, gpl, r, gpl, c), 0.0), axis=3)
        return jnp.swapaxes(d.reshape(n_grp, r, c), 1, 2)

    return (da_re.reshape(n_grp, n_state), da_im.reshape(n_grp, n_state), dlog_dt.reshape(n_grp),
            bext(db_re), bext(db_im), bext(dc_re), bext(dc_im), dd.reshape(n_grp * ch))


def _cparams(**kw):
    return pltpu.CompilerParams(vmem_limit_bytes=VMEM_LIMIT, **kw)


def tiled_call(body_fn, name, *, n_steps, rows, n_groups=1, reverse=False,
               seq_in=(), whole_in=(), step_in=(), seq_out=(), acc_out=(), step_out=(), carry=(), a2a=(), ag=()):
    def step_of(i):
        return (n_steps - 1 - i) if reverse else i

    def col_of(col, g):
        return col(g) if callable(col) else col

    in_specs, operands = [], []
    for arr, width, col in seq_in:
        in_specs.append(pl.BlockSpec((rows, width), lambda g, i, col=col: (step_of(i), col_of(col, g))))
        operands.append(arr)
    for arr in whole_in:
        if arr.ndim == 2:
            in_specs.append(pl.BlockSpec(arr.shape, lambda g, i: (0, 0)))
        else:
            in_specs.append(pl.BlockSpec((None,) + arr.shape[1:], lambda g, i: (g, 0, 0)))
        operands.append(arr)
    for arr in step_in:
        in_specs.append(pl.BlockSpec((None, None) + arr.shape[2:], lambda g, i: (g, step_of(i), 0, 0)))
        operands.append(arr)
    out_shape, out_specs = [], []
    for total, width, col, dt in seq_out:
        out_shape.append(jax.ShapeDtypeStruct((n_steps * rows, total), dt))
        out_specs.append(pl.BlockSpec((rows, width), lambda g, i, col=col: (step_of(i), col_of(col, g))))
    for r, c in acc_out:
        out_shape.append(jax.ShapeDtypeStruct((n_groups, r, c), F32))
        out_specs.append(pl.BlockSpec((None, r, c), lambda g, i: (g, 0, 0)))
    for r, c in step_out:
        out_shape.append(jax.ShapeDtypeStruct((n_groups, n_steps, r, c), F32))
        out_specs.append(pl.BlockSpec((None, None, r, c), lambda g, i: (g, step_of(i), 0, 0)))
    n_seq, n_whole, n_step = len(seq_in), len(whole_in), len(step_in)
    n_so, n_ao, n_sto = len(seq_out), len(acc_out), len(step_out)
    assert not (a2a and ag)
    hosted = list(a2a) + list(ag)
    n_x = len(hosted)
    n_sem = AG_FIRST_COPIES if ag else N_DEV - 1
    hbm = pl.BlockSpec(memory_space=pl.ANY)
    in_specs += [hbm] * n_x
    operands += hosted
    out_specs += [hbm] * n_x
    out_shape += [jax.ShapeDtypeStruct((N_DEV,) + x.shape if ag else x.shape, x.dtype) for x in hosted]
    scratch = [pltpu.VMEM((r, c), F32) for r, c in carry]
    if n_x:
        scratch += [pltpu.SemaphoreType.DMA((n_x, n_sem)), pltpu.SemaphoreType.DMA((n_x, n_sem)),
                    pltpu.SemaphoreType.DMA((n_x,))]

    def body(*refs):
        pos = 0
        seq_refs = refs[pos:pos + n_seq]
        pos += n_seq
        whole_refs = refs[pos:pos + n_whole]
        pos += n_whole
        step_refs = refs[pos:pos + n_step]
        pos += n_step + n_x
        so_refs = refs[pos:pos + n_so]
        pos += n_so
        ao_refs = refs[pos:pos + n_ao]
        pos += n_ao
        sto_refs = refs[pos:pos + n_sto]
        pos += n_sto
        xo_refs = refs[pos:pos + n_x]
        pos += n_x
        carry_refs = refs[pos:pos + len(carry)]
        pos += len(carry)
        x_refs = refs[n_seq + n_whole + n_step:n_seq + n_whole + n_step + n_x]
        g, i = pl.program_id(0), pl.program_id(1)
        if n_x:
            locals_, sends, recvs = (ag_first_copies if ag else a2a_copies)(x_refs, xo_refs, *refs[pos:])

            @pl.when((g == 0) & (i == 0))
            def _():
                for cp in locals_ + sends:
                    cp.start()

        @pl.when(i == 0)
        def _():
            for r in carry_refs:
                r[...] = jnp.zeros(r.shape, r.dtype)
            for r in ao_refs:
                r[...] = jnp.zeros(r.shape, r.dtype)

        row0 = step_of(i) * rows
        seq_o, acc_o, step_o, new_c = body_fn(row0, [r[...] for r in whole_refs], [r[...] for r in seq_refs],
                                              [r[...] for r in step_refs], [r[...] for r in carry_refs])
        for r, val in zip(so_refs, seq_o, strict=True):
            r[...] = val.astype(r.dtype)
        for r, val in zip(ao_refs, acc_o, strict=True):
            r[...] += val
        for r, val in zip(sto_refs, step_o, strict=True):
            r[...] = val
        for r, val in zip(carry_refs, new_c, strict=True):
            r[...] = val
        if n_x:
            @pl.when((g == n_groups - 1) & (i == n_steps - 1))
            def _():
                for cp in recvs:
                    cp.wait_recv()
                for cp in sends:
                    cp.wait_send()
                for cp in locals_:
                    cp.wait()

    return pl.pallas_call(
        body, name=name, grid=(n_groups, n_steps), in_specs=in_specs, out_specs=out_specs, out_shape=out_shape,
        scratch_shapes=scratch,
        compiler_params=_cparams(dimension_semantics=("arbitrary", "arbitrary")),
    )(*operands)


def mixer_fwd(fn, name, *, n_steps, rows, seqs, params, out, carry, n_groups=1, ag=()):
    def body(row0, whole, seq_vals, steps, carry_vals):
        outs, new_c = fn(row0, tuple(whole), tuple(seq_vals), tuple(carry_vals))
        return list(outs), [], list(carry_vals), list(new_c)

    res = tiled_call(body, name, n_steps=n_steps, rows=rows, n_groups=n_groups, seq_in=seqs, whole_in=params,
                     seq_out=[out], step_out=carry, carry=carry, ag=ag)
    return res[0], list(res[1:1 + len(carry)]), list(res[1 + len(carry):])


def mixer_bwd(fn, name, *, n_steps, rows, seqs, params, dout, saved, carry, n_groups=1, a2a=()):
    n_seq = len(seqs)

    def body(row0, whole, seq_vals, steps, dcarry):
        params_f = tuple(p.astype(F32) for p in whole)
        _, vjp = jax.vjp(lambda p, s, c: fn(row0, p, s, c), params_f, tuple(seq_vals[:n_seq]), tuple(steps))
        dp, ds, dc = vjp(((seq_vals[n_seq],), tuple(dcarry)))
        return list(ds), list(dp), [], list(dc)

    seq_out = [(n_groups * w, w, (lambda g: g), F32) if callable(c) else (w, w, 0, F32) for a, w, c in seqs]
    acc_out = [p.shape[-2:] for p in params]
    res = tiled_call(body, name, n_steps=n_steps, rows=rows, n_groups=n_groups, reverse=True,
                     seq_in=list(seqs) + [dout], whole_in=params, step_in=saved,
                     seq_out=seq_out, acc_out=acc_out, carry=carry, a2a=a2a)
    n_p = len(params)
    if a2a:
        return list(res[:n_seq]), list(res[n_seq:n_seq + n_p]), list(res[n_seq + n_p:])
    return list(res[:n_seq]), list(res[n_seq:])


def rms_fwd(h, g, name, rows):
    def body(row0, whole, seqs, steps, carry):
        return [rms_norm(seqs[0], whole[0])], [], [], []
    d = h.shape[1]
    return tiled_call(body, name, n_steps=h.shape[0] // rows, rows=rows, seq_in=[(h, d, 0)], whole_in=[g],
                      seq_out=[(d, d, 0, BF16)])[0]


def rms_bwd_add(h, dxn, dh_out, g, name, rows):
    def body(row0, whole, seqs, steps, carry):
        _, vjp = jax.vjp(rms_norm, seqs[0], whole[0])
        dh, dg = vjp(seqs[1])
        return [seqs[2] + dh], [dg], [], []
    d = h.shape[1]
    dh_in, dg = tiled_call(body, name, n_steps=h.shape[0] // rows, rows=rows,
                           seq_in=[(h, d, 0), (dxn, d, 0), (dh_out, d, 0)], whole_in=[g],
                           seq_out=[(d, d, 0, F32)], acc_out=[(1, d)])
    return dh_in, dg[0]


def mix_out_fwd(h, ys, w_out, name, rows, ag=()):
    def body(row0, whole, seqs, steps, carry):
        (delta,), _ = mix_out_delta(row0, (whole[0],), tuple(seqs[1:]), ())
        return [seqs[0] + delta], [], [], []
    d, wd = h.shape[1], ys[0].shape[1]
    res = tiled_call(body, name, n_steps=h.shape[0] // rows, rows=rows,
                     seq_in=[(h, d, 0)] + [(y, wd, 0) for y in ys], whole_in=[w_out], seq_out=[(d, d, 0, F32)], ag=ag)
    return res[0], list(res[1:])


def loss_and_grad(h, target, g, name, rows, first_row):
    def body(row0, whole, seqs, steps, carry):
        hh, tt = seqs
        keep = row_mask(row0, hh.shape[0], first_row)

        def f(hv, gv):
            err = rms_norm(hv, gv) - tt
            return 0.5 * jnp.sum(jnp.mean(err * err, axis=-1, keepdims=True) * keep, axis=0, keepdims=True)

        val, vjp = jax.vjp(f, hh, whole[0])
        dh, dg = vjp(jnp.ones((1, 1), F32))
        return [dh], [jnp.broadcast_to(val, (1, LANE)), dg], [], []
    d = h.shape[1]
    dh, loss, dg = tiled_call(body, name, n_steps=h.shape[0] // rows, rows=rows,
                              seq_in=[(h, d, 0), (target, d, 0)], whole_in=[g],
                              seq_out=[(d, d, 0, F32)], acc_out=[(1, LANE), (1, d)])
    return loss[0, 0, 0], dh, dg[0]


def _pick(n, cands):
    for c in cands:
        if n % c == 0:
            return c
    raise ValueError(f"no tile for {n}")


ROW_TILES = (1056, 704, 352, 192, 96, 64)
COL_TILES = (256, 128)
NT_DIMS = (((1,), (1,)), ((), ()))
TN_DIMS = (((0,), (0,)), ((), ()))


def ffn_fwd(h, xn, wg, wu, wd, name):
    t, d = h.shape
    f = wg.shape[1]
    tm = _pick(t, ROW_TILES)
    tn = _pick(f, COL_TILES)
    n_j = f // tn

    def body(h_ref, xn_ref, wg_ref, wu_ref, wd_ref, o_ref, acc_ref):
        j = pl.program_id(1)

        @pl.when(j == 0)
        def _():
            acc_ref[...] = jnp.zeros(acc_ref.shape, F32)

        x = xn_ref[...]
        g = jnp.dot(x, wg_ref[...], preferred_element_type=F32)
        u = jnp.dot(x, wu_ref[...], preferred_element_type=F32)
        a = (jax.nn.silu(g) * u).astype(BF16)
        acc_ref[...] += jnp.dot(a, wd_ref[...], preferred_element_type=F32)

        @pl.when(j == n_j - 1)
        def _():
            o_ref[...] = h_ref[...] + 0.5 * acc_ref[...]

    return pl.pallas_call(
        body, name=name, grid=(t // tm, n_j),
        in_specs=[pl.BlockSpec((tm, d), lambda i, j: (i, 0)), pl.BlockSpec((tm, d), lambda i, j: (i, 0)),
                  pl.BlockSpec((d, tn), lambda i, j: (0, j)), pl.BlockSpec((d, tn), lambda i, j: (0, j)),
                  pl.BlockSpec((tn, d), lambda i, j: (j, 0))],
        out_specs=pl.BlockSpec((tm, d), lambda i, j: (i, 0)),
        out_shape=jax.ShapeDtypeStruct((t, d), F32),
        scratch_shapes=[pltpu.VMEM((tm, d), F32)],
        compiler_params=_cparams(dimension_semantics=("arbitrary", "arbitrary")),
    )(h, xn, wg, wu, wd)


def ffn_bwd(xn, dh, wg, wu, wd, name):
    t, d = dh.shape
    f = wg.shape[1]
    tm = _pick(t, ROW_TILES)
    tn = _pick(f, COL_TILES)
    n_i = t // tm

    def body(xn_ref, dh_ref, wg_ref, wu_ref, wd_ref, dxn_ref, dwg_ref, dwu_ref, dwd_ref, ag_ref, au_ref, ad_ref):
        j, i = pl.program_id(0), pl.program_id(1)
        rows = pl.ds(pl.multiple_of(i * tm, 8), tm)
        x = xn_ref[rows, :]
        dhh = (0.5 * dh_ref[...]).astype(BF16)
        wgv, wuv = wg_ref[...], wu_ref[...]
        g = jnp.dot(x, wgv, preferred_element_type=F32)
        u = jnp.dot(x, wuv, preferred_element_type=F32)
        sg = jax.nn.sigmoid(g)
        s = g * sg
        da = lax.dot_general(dhh, wd_ref[...], NT_DIMS, preferred_element_type=F32)
        dwd = lax.dot_general((s * u).astype(BF16), dhh, TN_DIMS, preferred_element_type=F32)
        dg = (da * u * (sg * (1.0 + g * (1.0 - sg)))).astype(BF16)
        du = (da * s).astype(BF16)
        dwg = lax.dot_general(x, dg, TN_DIMS, preferred_element_type=F32)
        dwu = lax.dot_general(x, du, TN_DIMS, preferred_element_type=F32)
        dx = (lax.dot_general(dg, wgv, NT_DIMS, preferred_element_type=F32)
              + lax.dot_general(du, wuv, NT_DIMS, preferred_element_type=F32))

        @pl.when(i == 0)
        def _():
            ag_ref[...] = dwg
            au_ref[...] = dwu
            ad_ref[...] = dwd

        @pl.when(i > 0)
        def _():
            ag_ref[...] += dwg
            au_ref[...] += dwu
            ad_ref[...] += dwd

        @pl.when(i == n_i - 1)
        def _():
            dwg_ref[...] = ag_ref[...].astype(BF16)
            dwu_ref[...] = au_ref[...].astype(BF16)
            dwd_ref[...] = ad_ref[...].astype(BF16)

        @pl.when(j == 0)
        def _():
            dxn_ref[rows, :] = dx

        @pl.when(j > 0)
        def _():
            dxn_ref[rows, :] += dx

    return pl.pallas_call(
        body, name=name, grid=(f // tn, t // tm),
        in_specs=[pl.BlockSpec((t, d), lambda j, i: (0, 0)), pl.BlockSpec((tm, d), lambda j, i: (i, 0)),
                  pl.BlockSpec((d, tn), lambda j, i: (0, j)), pl.BlockSpec((d, tn), lambda j, i: (0, j)),
                  pl.BlockSpec((tn, d), lambda j, i: (j, 0))],
        out_specs=[pl.BlockSpec((t, d), lambda j, i: (0, 0)), pl.BlockSpec((d, tn), lambda j, i: (0, j)),
                   pl.BlockSpec((d, tn), lambda j, i: (0, j)), pl.BlockSpec((tn, d), lambda j, i: (j, 0))],
        out_shape=[jax.ShapeDtypeStruct((t, d), F32), jax.ShapeDtypeStruct((d, f), BF16),
                   jax.ShapeDtypeStruct((d, f), BF16), jax.ShapeDtypeStruct((f, d), BF16)],
        scratch_shapes=[pltpu.VMEM((d, tn), F32), pltpu.VMEM((d, tn), F32), pltpu.VMEM((tn, d), F32)],
        compiler_params=_cparams(dimension_semantics=("arbitrary", "arbitrary")),
    )(xn, dh, wg, wu, wd)


def matmul_cols(xn, w, name):
    t, d = xn.shape
    n = w.shape[1]
    tn = _pick(n, COL_TILES)

    def body(x_ref, w_ref, o_ref):
        o_ref[...] = jnp.dot(x_ref[...], w_ref[...], preferred_element_type=F32)

    return pl.pallas_call(
        body, name=name, grid=(n // tn,),
        in_specs=[pl.BlockSpec((t, d), lambda j: (0, 0)), pl.BlockSpec((d, tn), lambda j: (0, j))],
        out_specs=pl.BlockSpec((t, tn), lambda j: (0, j)),
        out_shape=jax.ShapeDtypeStruct((t, n), F32),
        compiler_params=_cparams(dimension_semantics=("arbitrary",)),
    )(xn, w)


def matmul_cols_bwd(xn, dy, w, name):
    t, d = xn.shape
    n = w.shape[1]
    tn = _pick(n, COL_TILES)

    def body(x_ref, dy_ref, w_ref, dx_ref, dw_ref):
        j = pl.program_id(0)
        dyv = dy_ref[...].astype(BF16)
        dw_ref[...] = lax.dot_general(x_ref[...], dyv, TN_DIMS, preferred_element_type=F32).astype(BF16)
        dx = lax.dot_general(dyv, w_ref[...], NT_DIMS, preferred_element_type=F32)

        @pl.when(j == 0)
        def _():
            dx_ref[...] = dx

        @pl.when(j > 0)
        def _():
            dx_ref[...] += dx

    return pl.pallas_call(
        body, name=name, grid=(n // tn,),
        in_specs=[pl.BlockSpec((t, d), lambda j: (0, 0)), pl.BlockSpec((t, tn), lambda j: (0, j)),
                  pl.BlockSpec((d, tn), lambda j: (0, j))],
        out_specs=[pl.BlockSpec((t, d), lambda j: (0, 0)), pl.BlockSpec((d, tn), lambda j: (0, j))],
        out_shape=[jax.ShapeDtypeStruct((t, d), F32), jax.ShapeDtypeStruct((d, n), BF16)],
        compiler_params=_cparams(dimension_semantics=("arbitrary",)),
    )(xn, dy, w)


def _peer(mx, my, mc, k):
    px = 1 - mx if (k >> 2) & 1 else mx
    py = 1 - my if (k >> 1) & 1 else my
    pc = 1 - mc if k & 1 else mc
    return (px, py, pc), 4 * px + 2 * py + pc


def a2a_copies(x_refs, o_refs, send_sems, recv_sems, local_sems):
    mx, my, mc = lax.axis_index("x"), lax.axis_index("y"), lax.axis_index("c")
    me = 4 * mx + 2 * my + mc
    peers = [_peer(mx, my, mc, k) for k in range(1, N_DEV)]
    locals_, sends, recvs = [], [], []
    for a, (x_ref, o_ref) in enumerate(zip(x_refs, o_refs, strict=True)):
        locals_.append(pltpu.make_async_copy(x_ref.at[me], o_ref.at[me], local_sems.at[a]))
        for k, (dev, peer) in enumerate(peers):
            common = dict(send_sem=send_sems.at[a, k], recv_sem=recv_sems.at[a, k], device_id=dev,
                          device_id_type=pl.DeviceIdType.MESH)
            sends.append(pltpu.make_async_remote_copy(src_ref=x_ref.at[peer], dst_ref=o_ref.at[me], **common))
            recvs.append(pltpu.make_async_remote_copy(src_ref=x_ref.at[peer], dst_ref=o_ref.at[peer], **common))
    return locals_, sends, recvs


def all_gather(xs, name, with_a2a=()):
    n, n2 = len(xs), len(with_a2a)
    chip_flips = (4, 2, 6)

    def body(*refs):
        x_refs, t_refs = refs[:n], refs[n:n + n2]
        o_refs, r_refs = refs[n + n2:2 * n + n2], refs[2 * n + n2:2 * n + 2 * n2]
        send_sems, recv_sems, local_sems = refs[2 * n + 2 * n2:2 * n + 2 * n2 + 3]
        mx, my, mc = lax.axis_index("x"), lax.axis_index("y"), lax.axis_index("c")
        me = 4 * mx + 2 * my + mc
        sib_dev, sib = _peer(mx, my, mc, 1)

        def copy(a, k, row, to, src=None):
            return pltpu.make_async_remote_copy(
                src_ref=o_refs[a].at[row] if src is None else src, dst_ref=o_refs[a].at[row],
                send_sem=send_sems.at[a, k], recv_sem=recv_sems.at[a, k], device_id=to,
                device_id_type=pl.DeviceIdType.MESH)

        locals_, first, passed = [], [], []
        t_recvs = []
        if n2:
            t_locals, t_sends, t_recvs = a2a_copies(t_refs, r_refs, *refs[2 * n + 2 * n2 + 3:])
            locals_ += t_locals
            first += t_sends
        for a in range(n):
            locals_.append(pltpu.make_async_copy(x_refs[a], o_refs[a].at[me], local_sems.at[a]))
            first.append(copy(a, 0, me, sib_dev, src=x_refs[a]))
            for j, f in enumerate(chip_flips):
                first.append(copy(a, 1 + j, me, _peer(mx, my, mc, f)[0], src=x_refs[a]))
        for cp in locals_ + first:
            cp.start()
        for a in range(n):
            for j, f in enumerate(chip_flips):
                row = _peer(mx, my, mc, f)[1]
                copy(a, 1 + j, row, sib_dev).wait_recv()
                fwd = copy(a, 4 + j, row, sib_dev)
                fwd.start()
                passed.append(fwd)
        for a in range(n):
            copy(a, 0, sib, sib_dev).wait_recv()
            for j, f in enumerate(chip_flips):
                copy(a, 4 + j, _peer(mx, my, mc, f ^ 1)[1], sib_dev).wait_recv()
        for cp in t_recvs:
            cp.wait_recv()
        for cp in first + passed:
            cp.wait_send()
        for cp in locals_:
            cp.wait()

    hbm = pl.BlockSpec(memory_space=pl.ANY)
    scratch = [pltpu.SemaphoreType.DMA((n, N_DEV - 1)), pltpu.SemaphoreType.DMA((n, N_DEV - 1)),
               pltpu.SemaphoreType.DMA((n,))]
    if n2:
        scratch += [pltpu.SemaphoreType.DMA((n2, N_DEV - 1)), pltpu.SemaphoreType.DMA((n2, N_DEV - 1)),
                    pltpu.SemaphoreType.DMA((n2,))]
    return pl.pallas_call(
        body, name=name, in_specs=[hbm] * (n + n2), out_specs=[hbm] * (n + n2),
        out_shape=[jax.ShapeDtypeStruct((N_DEV,) + x.shape, x.dtype) for x in xs]
        + [jax.ShapeDtypeStruct(x.shape, x.dtype) for x in with_a2a],
        scratch_shapes=scratch,
    )(*xs, *with_a2a)


AG_FIRST_COPIES = 4


def ag_first_copies(x_refs, o_refs, send_sems, recv_sems, local_sems):
    mx, my, mc = lax.axis_index("x"), lax.axis_index("y"), lax.axis_index("c")
    me = 4 * mx + 2 * my + mc
    targets = [_peer(mx, my, mc, f) for f in (1, 4, 2, 6)]
    locals_, sends, recvs = [], [], []
    for a, (x_ref, o_ref) in enumerate(zip(x_refs, o_refs, strict=True)):
        locals_.append(pltpu.make_async_copy(x_ref, o_ref.at[me], local_sems.at[a]))
        for k, (dev, row) in enumerate(targets):
            common = dict(send_sem=send_sems.at[a, k], recv_sem=recv_sems.at[a, k], device_id=dev,
                          device_id_type=pl.DeviceIdType.MESH)
            sends.append(pltpu.make_async_remote_copy(src_ref=x_ref, dst_ref=o_ref.at[me], **common))
            recvs.append(pltpu.make_async_remote_copy(src_ref=x_ref, dst_ref=o_ref.at[row], **common))
    return locals_, sends, recvs


def ag_second_level(bufs, name):
    n = len(bufs)
    chip_flips = (4, 2, 6)

    def body(*refs):
        o_refs = refs[n:2 * n]
        send_sems, recv_sems = refs[2 * n:]
        mx, my, mc = lax.axis_index("x"), lax.axis_index("y"), lax.axis_index("c")
        sib_dev, _ = _peer(mx, my, mc, 1)
        sends, recvs = [], []
        for a in range(n):
            for j, f in enumerate(chip_flips):
                common = dict(send_sem=send_sems.at[a, j], recv_sem=recv_sems.at[a, j], device_id=sib_dev,
                              device_id_type=pl.DeviceIdType.MESH)
                row, sib_row = _peer(mx, my, mc, f)[1], _peer(mx, my, mc, f ^ 1)[1]
                sends.append(pltpu.make_async_remote_copy(src_ref=o_refs[a].at[row], dst_ref=o_refs[a].at[row], **common))
                recvs.append(pltpu.make_async_remote_copy(src_ref=o_refs[a].at[row], dst_ref=o_refs[a].at[sib_row],
                                                          **common))
        for cp in sends:
            cp.start()
        for cp in recvs:
            cp.wait_recv()
        for cp in sends:
            cp.wait_send()

    return pl.pallas_call(
        body, name=name,
        in_specs=[pl.BlockSpec(memory_space=pl.ANY)] * n, out_specs=[pl.BlockSpec(memory_space=pl.ANY)] * n,
        out_shape=[jax.ShapeDtypeStruct(x.shape, x.dtype) for x in bufs],
        input_output_aliases={a: a for a in range(n)},
        scratch_shapes=[pltpu.SemaphoreType.DMA((n, len(chip_flips))), pltpu.SemaphoreType.DMA((n, len(chip_flips)))],
    )(*bufs)


PACK_COLS = 1024
PACK_ROWS = 256
PARTS_TILE_BYTES = 4 * 1024 * 1024


def adamw_reduce(parts, w, m, v, name):
    r, c = w.shape
    if (r // len(parts)) % SUBLANE:
        parts = [jnp.concatenate(parts, axis=1)]
    n_slab = len(parts)
    rs = r // n_slab
    fits = [t for t in (512, 352, 256, 128, 64, 32, 16, 8) if N_DEV * t * c * parts[0].dtype.itemsize <= PARTS_TILE_BYTES]
    tr = rs if rs < 2 * SUBLANE else _pick(rs, fits)
    n_t = rs // tr
    c1 = 1.0 - ADAM_B1 ** ADAM_STEP
    c2 = 1.0 - ADAM_B2 ** ADAM_STEP

    def body(*refs):
        p_refs = refs[:n_slab]
        w_ref, m_ref, v_ref, g_ref, d_ref, mo_ref, vo_ref = refs[n_slab:]
        slab = pl.program_id(0)
        for k, p_ref in enumerate(p_refs):
            @pl.when(slab == k)
            def _(p_ref=p_ref):
                g = p_ref[0].astype(F32)
                for dev in range(1, N_DEV):
                    g = g + p_ref[dev].astype(F32)
                mn = ADAM_B1 * m_ref[...] + (1.0 - ADAM_B1) * g
                vn = ADAM_B2 * v_ref[...] + (1.0 - ADAM_B2) * (g * g)
                g_ref[...] = g
                mo_ref[...] = mn
                vo_ref[...] = vn
                d_ref[...] = -ADAM_LR * ((mn / c1) / (jnp.sqrt(vn / c2) + ADAM_EPS) + ADAM_WD * w_ref[...])

    spec = pl.BlockSpec((tr, c), lambda s, i: (s * n_t + i, 0))
    p_specs = [pl.BlockSpec((N_DEV, tr, c), lambda s, i, k=k: (0, jnp.where(s == k, i, 0), 0)) for k in range(n_slab)]
    return pl.pallas_call(
        body, name=name, grid=(n_slab, n_t),
        in_specs=p_specs + [spec, spec, spec],
        out_specs=[spec] * 4, out_shape=[jax.ShapeDtypeStruct((r, c), F32)] * 4,
        compiler_params=_cparams(dimension_semantics=("arbitrary", "arbitrary")),
    )(*parts, w, m, v)


def pack_flat(arrs, dtype):
    parts = []
    for a in arrs:
        flat = a.reshape(-1).astype(dtype)
        k = -(-flat.shape[0] // PACK_COLS)
        parts.append(jnp.pad(flat, (0, k * PACK_COLS - flat.shape[0])).reshape(k, PACK_COLS))
    buf = jnp.concatenate(parts, axis=0)
    return jnp.pad(buf, ((0, -buf.shape[0] % PACK_ROWS), (0, 0)))


def unpack_flat(buf, shapes):
    out, r0 = [], 0
    for s in shapes:
        n = math.prod(s)
        k = -(-n // PACK_COLS)
        out.append(buf[r0:r0 + k].reshape(-1)[:n].reshape(tuple(s)))
        r0 += k
    return out


W_NAMES = ('meta_tokens', 'ffn1_norm', 'ffn1_w_gate', 'ffn1_w_up', 'ffn1_w_down', 'mix_norm', 'w_in', 'w_out',
           'lru_conv_w', 'lru_conv_b', 'lru_w_a', 'lru_b_a', 'lru_w_i', 'lru_b_i', 'lru_lambda', 'lru_norm',
           'gdn_conv_w', 'gdn_a_log', 'gdn_dt_bias', 'gdn_norm', 'ssd_conv_w', 'ssd_conv_b', 'ssd_a_log',
           'ssd_dt_bias', 'ssd_d', 'ssd_norm', 's5_a_re', 's5_a_im', 's5_log_dt', 's5_b_re', 's5_b_im', 's5_c_re',
           's5_c_im', 's5_d', 's5_w_glu', 's5_norm', 'ffn2_norm', 'ffn2_w_gate', 'ffn2_w_up', 'ffn2_w_down',
           'final_norm')
SHARD_AXIS = {'meta_tokens': 1, 'ffn1_w_gate': 2, 'ffn1_w_up': 2, 'ffn1_w_down': 1, 'w_in': 2, 'w_out': 1,
              'lru_conv_w': 2, 'gdn_conv_w': 2, 'ssd_conv_w': 2, 's5_w_glu': 1, 'ffn2_w_gate': 2, 'ffn2_w_up': 2,
              'ffn2_w_down': 1}
BIG_NAMES = ('ffn1_w_gate', 'ffn1_w_up', 'ffn1_w_down', 'w_in', 'w_out', 's5_w_glu', 'ffn2_w_gate', 'ffn2_w_up',
             'ffn2_w_down')
SHARD_NAMES = tuple(n for n in W_NAMES if n in SHARD_AXIS)
REP_NAMES = tuple(n for n in W_NAMES if n not in SHARD_AXIS)
SSD_GROUPS = 2
S5_CH = 16


def unshard(g, axis):
    if axis == 0:
        return g.reshape((-1,) + g.shape[2:])
    return jnp.concatenate([g[p] for p in range(N_DEV)], axis=axis)


def kernel(*args):
    n_w = len(W_NAMES)
    x = args[0]
    w = dict(zip(W_NAMES, args[1:1 + n_w]))
    target = args[1 + n_w]
    m_in = dict(zip(W_NAMES, args[2 + n_w:2 + 2 * n_w]))
    v_in = dict(zip(W_NAMES, args[2 + 2 * n_w:2 + 3 * n_w]))

    depth, d = w['ffn1_norm'].shape
    seq = x.shape[1]
    n_meta = w['meta_tokens'].shape[0]
    pad = CHUNK - n_meta
    tp = pad + n_meta + seq
    wg = d // 2
    xbc_w = w['ssd_conv_w'].shape[-1] * N_DEV
    gdn_hd = w['gdn_norm'].shape[-1]
    gdn_h = wg // gdn_hd
    ssd_h = w['ssd_a_log'].shape[-1]
    lru_h = w['lru_w_a'].shape[1]
    s5_g, s5_n = w['s5_a_re'].shape[1:]
    s5_q = wg // S5_LANES
    row_tile = _pick(tp, (192, 96, 64))
    norm_tile = _pick(tp, ROW_TILES)

    LAYER_NAMES = tuple(n for n in SHARD_NAMES if n != 'meta_tokens')

    def local_of(n, l):
        return w[n][l].astype(BF16 if n in BIG_NAMES else F32)

    def unshard_layer(gathered):
        return {n: unshard(g, SHARD_AXIS[n] - 1) for n, g in gathered.items()}

    first = all_gather([local_of(n, 0) for n in LAYER_NAMES] + [w['meta_tokens']], "gather_weights")
    meta_full = unshard(first[-1], SHARD_AXIS['meta_tokens'])
    full = [unshard_layer(dict(zip(LAYER_NAMES, first[:-1])))] + [None] * (depth - 1)

    segs = [('a_x', wg), ('a_gate', wg), ('b_q', wg), ('b_k', wg), ('b_v', wg), ('b_z', wg), ('c_xbc', xbc_w),
            ('c_z', wg), ('d_u', wg), ('small_b', LANE), ('small_c', LANE)]
    off, o = {}, 0
    for nme, wd_ in segs:
        assert o % wd_ == 0, (nme, o, wd_)
        off[nme] = o
        o += wd_
    o_beta = 6 * wg
    o_cz = o_beta + 2 * gdn_h
    o_xbc = o_cz + wg
    o_dt = o_xbc + xbc_w
    o_du = o_dt + ssd_h

    def pack_cols(a):
        z = lambda k: jnp.zeros(a.shape[:-1] + (k,), a.dtype)
        return jnp.concatenate([a[..., :o_beta], a[..., o_xbc:o_dt], a[..., o_cz:o_xbc], a[..., o_du:],
                                a[..., o_beta:o_cz], z(LANE - 2 * gdn_h), a[..., o_dt:o_du], z(LANE - ssd_h)], axis=-1)

    def unpack_cols(a):
        sb, sc = off['small_b'], off['small_c']
        return jnp.concatenate([a[..., :o_beta], a[..., sb:sb + 2 * gdn_h], a[..., off['c_z']:off['c_z'] + wg],
                                a[..., off['c_xbc']:off['c_xbc'] + xbc_w], a[..., sc:sc + ssd_h],
                                a[..., off['d_u']:off['d_u'] + wg]], axis=-1)

    w_in_p = [None] * depth
    w_in_p[0] = pack_cols(full[0]['w_in'])

    def col(name, width):
        return off[name] // width

    def row(a):
        return a.reshape(1, -1)

    def layer_params(l):
        gcw = full[l]['gdn_conv_w']
        lru = [full[l]['lru_conv_w'], row(w['lru_conv_b'][l]), blockdiag_expand(w['lru_w_a'][l]), row(w['lru_b_a'][l]),
               blockdiag_expand(w['lru_w_i'][l]), row(w['lru_b_i'][l]), row(w['lru_lambda'][l]), row(w['lru_norm'][l])]
        gdn = [gcw[:, :wg], gcw[:, wg:2 * wg], gcw[:, 2 * wg:], row(w['gdn_a_log'][l]), row(w['gdn_dt_bias'][l]),
               row(w['gdn_norm'][l])]
        ssd = [full[l]['ssd_conv_w'], row(w['ssd_conv_b'][l]), row(w['ssd_a_log'][l]), row(w['ssd_dt_bias'][l]),
               row(w['ssd_d'][l]), row(w['ssd_norm'][l])]
        s5 = list(s5_params_expand(*[w[n][l] for n in ('s5_a_re', 's5_a_im', 's5_log_dt', 's5_b_re', 's5_b_im',
                                                          's5_c_re', 's5_c_im', 's5_d')]))
        post = [full[l]['s5_w_glu'], row(w['s5_norm'][l])]
        return lru, gdn, ssd, s5, post

    lru_fn = functools.partial(lru_chunk, pad)
    gdn_fn = functools.partial(gdn_multi, pad, CHUNK)
    ssd_fn = functools.partial(ssd_multi, pad, 0, SSD_GROUPS, CHUNK)
    s5_fn = functools.partial(s5_chunk, pad)
    post_fn = functools.partial(s5_post, pad)
    n_state_lanes = (S5_LANES // S5_CH) * s5_n

    def mixer_specs(proj):
        lru_seqs = [(proj, wg, col('a_x', wg)), (proj, wg, col('a_gate', wg))]
        gdn_seqs = [(proj, wg, col('b_q', wg)), (proj, wg, col('b_k', wg)), (proj, wg, col('b_v', wg)),
                    (proj, wg, col('b_z', wg)), (proj, LANE, col('small_b', LANE))]
        ssd_seqs = [(proj, wg, col('c_z', wg)), (proj, xbc_w, col('c_xbc', xbc_w)), (proj, LANE, col('small_c', LANE))]
        base = col('d_u', S5_LANES)
        s5_seqs = [(proj, S5_LANES, lambda g: base + g)]
        return lru_seqs, gdn_seqs, ssd_seqs, s5_seqs

    lru_carry = [(CONV_TAIL, wg), (1, wg)]
    gdn_carry = [(CONV_TAIL, wg)] * 3 + [(wg, gdn_hd)]
    ssd_carry = [(CONV_TAIL, xbc_w), (wg, (xbc_w - wg) // (2 * SSD_GROUPS))]
    s5_carry = [(1, n_state_lanes)] * 2
    rk = dict(n_steps=tp // row_tile, rows=row_tile)
    mk = rk
    out_w = (wg, wg, 0, F32)

    h = jnp.concatenate([jnp.zeros((pad, d), F32), meta_full, x[0]], axis=0)
    target_p = jnp.concatenate([jnp.zeros((pad + n_meta, d), F32), target[0]], axis=0)
    saved = []
    for l in range(depth):
        lru_p, gdn_p, ssd_p, s5_p, post_p = layer_params(l)
        h0 = h
        xn1 = rms_fwd(h0, row(w['ffn1_norm'][l]), "rms_fwd", norm_tile)
        nxt = l + 1 < depth
        take = lambda *names: [local_of(n, l + 1) for n in names] if nxt else []
        fw = full[l]
        h1 = ffn_fwd(h0, xn1, fw['ffn1_w_gate'], fw['ffn1_w_up'], fw['ffn1_w_down'], "ffn_fwd")
        xn2 = rms_fwd(h1, row(w['mix_norm'][l]), "rms_fwd", norm_tile)
        proj = matmul_cols(xn2, w_in_p[l], "mix_in_fwd")
        lru_s, gdn_s, ssd_s, s5_s = mixer_specs(proj)
        got = {}
        names = ('ffn2_w_gate',)
        ya, lru_c, bufs = mixer_fwd(lru_fn, "lru_fwd", seqs=lru_s, params=lru_p, out=out_w, carry=lru_carry,
                                    ag=take(*names), **mk)
        got.update(zip(names, bufs))
        names = ('w_in',)
        yb, gdn_c, bufs = mixer_fwd(gdn_fn, "gdn_fwd", seqs=gdn_s, params=gdn_p, out=out_w, carry=gdn_carry,
                                    ag=take(*names), **mk)
        got.update(zip(names, bufs))
        names = ('w_out', 's5_w_glu', 'lru_conv_w', 'gdn_conv_w', 'ssd_conv_w')
        yc, ssd_c, bufs = mixer_fwd(ssd_fn, "ssd_fwd", seqs=ssd_s, params=ssd_p, out=out_w, carry=ssd_carry,
                                    ag=take(*names), **mk)
        got.update(zip(names, bufs))
        names = ('ffn1_w_gate', 'ffn1_w_up', 'ffn1_w_down')
        y1, s5_c, bufs = mixer_fwd(s5_fn, "s5_fwd", seqs=s5_s, params=s5_p, out=(wg, S5_LANES, lambda g: g, F32),
                                   carry=s5_carry, n_groups=s5_q, ag=take(*names), **mk)
        got.update(zip(names, bufs))
        names = ('ffn2_w_down',)
        yd, _, bufs = mixer_fwd(post_fn, "s5_post_fwd", seqs=[(y1, wg, 0)], params=post_p, out=out_w, carry=[],
                                ag=take(*names), **rk)
        got.update(zip(names, bufs))
        names = ('ffn2_w_up',)
        h2, bufs = mix_out_fwd(h1, [ya, yb, yc, yd], fw['w_out'], "mix_out_fwd", row_tile, ag=take(*names))
        got.update(zip(names, bufs))
        if nxt:
            full[l + 1] = unshard_layer(dict(zip(LAYER_NAMES, ag_second_level([got[n] for n in LAYER_NAMES],
                                                                                "gather_pass_on"))))
            w_in_p[l + 1] = pack_cols(full[l + 1]['w_in'])
        xn3 = rms_fwd(h2, row(w['ffn2_norm'][l]), "rms_fwd", norm_tile)
        h3 = ffn_fwd(h2, xn3, fw['ffn2_w_gate'], fw['ffn2_w_up'], fw['ffn2_w_down'], "ffn_fwd")
        saved.append((h0, xn1, h1, xn2, proj, (ya, yb, yc, yd), y1, (lru_c, gdn_c, ssd_c, s5_c), h2, xn3))
        h = h3

    loss_part, dh, d_final = loss_and_grad(h, target_p, row(w['final_norm']), "loss", norm_tile, pad + n_meta)
    loss = lax.psum(loss_part, ("x", "y", "c"))

    def shards_of(a, axis):
        sh = a.shape
        return jnp.moveaxis(a.reshape(sh[:axis] + (N_DEV, sh[axis] // N_DEV) + sh[axis + 1:]), axis, 0)

    def to_send(n, g):
        return shards_of(g, SHARD_AXIS[n] - 1).astype(BF16)

    received = {n: [None] * depth for n in SHARD_NAMES if n != 'meta_tokens'}
    pending = []

    def hosted(names_layers):
        keys = [k for k in pending if (k[0], k[1]) in names_layers]
        for k in keys:
            pending.remove(k)
        return [(k[0], k[1]) for k in keys], [k[2] for k in keys]

    def store(keys, arrays):
        for (n, l), a in zip(keys, arrays, strict=True):
            received[n][l] = a

    gw = {n: [None] * depth for n in W_NAMES if n not in ('meta_tokens', 'final_norm')}
    for l in reversed(range(depth)):
        lru_p, gdn_p, ssd_p, s5_p, post_p = layer_params(l)
        h0, xn1, h1, xn2, proj, ys, y1, (lru_c, gdn_c, ssd_c, s5_c), h2, xn3 = saved[l]
        dxn, gw['ffn2_w_gate'][l], gw['ffn2_w_up'][l], gw['ffn2_w_down'][l] = ffn_bwd(
            xn3, dh, full[l]['ffn2_w_gate'], full[l]['ffn2_w_up'], full[l]['ffn2_w_down'], "ffn_bwd")
        pending += [(n, l, to_send(n, gw[n][l])) for n in ('ffn2_w_gate', 'ffn2_w_up', 'ffn2_w_down')]
        dh, dg = rms_bwd_add(h2, dxn, dh, row(w['ffn2_norm'][l]), "rms_bwd", norm_tile)
        gw['ffn2_norm'][l] = dg[0]

        dys, (d_wout,) = mixer_bwd(mix_out_delta, "mix_out_bwd", seqs=[(y, wg, 0) for y in ys],
                                   params=[full[l]['w_out']], dout=(dh, d, 0), saved=[], carry=[], **rk)
        gw['w_out'][l] = d_wout[0]
        pending.append(('w_out', l, to_send('w_out', gw['w_out'][l])))
        lru_s, gdn_s, ssd_s, s5_s = mixer_specs(proj)
        (dy1,), d_post = mixer_bwd(post_fn, "s5_post_bwd", seqs=[(y1, wg, 0)], params=post_p, dout=(dys[3], wg, 0),
                                   saved=[], carry=[], **rk)
        gw['s5_w_glu'][l], gw['s5_norm'][l] = d_post[0][0], d_post[1][0, 0]
        pending.append(('s5_w_glu', l, to_send('s5_w_glu', gw['s5_w_glu'][l])))
        keys, arrs = hosted({('ffn1_w_gate', l + 1), ('ffn1_w_up', l + 1), ('ffn1_w_down', l + 1)})
        (d_du,), d_s5, *got = mixer_bwd(s5_fn, "s5_bwd", seqs=s5_s, params=s5_p, dout=(dy1, S5_LANES, lambda g: g),
                                        saved=s5_c, carry=s5_carry, n_groups=s5_q, a2a=arrs, **mk)
        store(keys, got[0] if got else [])
        for n, g in zip(('s5_a_re', 's5_a_im', 's5_log_dt', 's5_b_re', 's5_b_im', 's5_c_re', 's5_c_im', 's5_d'),
                        s5_grads_extract(d_s5, s5_g, s5_n, S5_CH)):
            gw[n][l] = g
        keys, arrs = hosted({('w_in', l + 1), ('w_out', l), ('s5_w_glu', l)})
        (d_cz, d_cxbc, d_sc), d_ssd, *got = mixer_bwd(ssd_fn, "ssd_bwd", seqs=ssd_s, params=ssd_p,
                                                      dout=(dys[2], wg, 0), saved=ssd_c, carry=ssd_carry, a2a=arrs, **mk)
        store(keys, got[0] if got else [])
        for n, g in zip(('ssd_conv_w', 'ssd_conv_b', 'ssd_a_log', 'ssd_dt_bias', 'ssd_d', 'ssd_norm'), d_ssd):
            gw[n][l] = g[0] if n == 'ssd_conv_w' else g[0, 0]
        keys, arrs = hosted({('ffn2_w_gate', l), ('ffn2_w_up', l), ('ffn2_w_down', l)})
        (d_bq, d_bk, d_bv, d_bz, d_sb), d_gdn, *got = mixer_bwd(gdn_fn, "gdn_bwd", seqs=gdn_s, params=gdn_p,
                                                                dout=(dys[1], wg, 0), saved=gdn_c, carry=gdn_carry,
                                                                a2a=arrs, **mk)
        store(keys, got[0] if got else [])
        gw['gdn_conv_w'][l] = jnp.concatenate([d_gdn[0][0], d_gdn[1][0], d_gdn[2][0]], axis=1)
        gw['gdn_a_log'][l], gw['gdn_dt_bias'][l], gw['gdn_norm'][l] = d_gdn[3][0, 0], d_gdn[4][0, 0], d_gdn[5][0, 0]
        (d_ax, d_ag), d_lru = mixer_bwd(lru_fn, "lru_bwd", seqs=lru_s, params=lru_p, dout=(dys[0], wg, 0),
                                        saved=lru_c, carry=lru_carry, **mk)
        gw['lru_conv_w'][l], gw['lru_conv_b'][l] = d_lru[0][0], d_lru[1][0, 0]
        gw['lru_w_a'][l], gw['lru_b_a'][l] = blockdiag_extract(d_lru[2][0], lru_h), d_lru[3][0, 0]
        gw['lru_w_i'][l], gw['lru_b_i'][l] = blockdiag_extract(d_lru[4][0], lru_h), d_lru[5][0, 0]
        gw['lru_lambda'][l], gw['lru_norm'][l] = d_lru[6][0, 0], d_lru[7][0, 0]

        dproj = jnp.concatenate([d_ax, d_ag, d_bq, d_bk, d_bv, d_bz, d_cxbc, d_cz, d_du, d_sb, d_sc], axis=1)
        dxn, d_win_p = matmul_cols_bwd(xn2, dproj, w_in_p[l], "mix_in_bwd")
        gw['w_in'][l] = unpack_cols(d_win_p)
        pending.append(('w_in', l, to_send('w_in', gw['w_in'][l])))
        dh, dg = rms_bwd_add(h1, dxn, dh, row(w['mix_norm'][l]), "rms_bwd", norm_tile)
        gw['mix_norm'][l] = dg[0]

        dxn, gw['ffn1_w_gate'][l], gw['ffn1_w_up'][l], gw['ffn1_w_down'][l] = ffn_bwd(
            xn1, dh, full[l]['ffn1_w_gate'], full[l]['ffn1_w_up'], full[l]['ffn1_w_down'], "ffn_bwd")
        dh, dg = rms_bwd_add(h0, dxn, dh, row(w['ffn1_norm'][l]), "rms_bwd", norm_tile)
        gw['ffn1_norm'][l] = dg[0]
        pending += [(n, l, to_send(n, gw[n][l])) for n in ('ffn1_w_gate', 'ffn1_w_up', 'ffn1_w_down')]

    grad_x = dh[pad + n_meta:][None]
    grads = {n: jnp.stack(gw[n], axis=0) for n in REP_NAMES if n != 'final_norm'}
    grads['meta_tokens'] = dh[pad:pad + n_meta]
    grads['final_norm'] = d_final[0]

    for n in ('lru_conv_w', 'gdn_conv_w', 'ssd_conv_w'):
        pending += [(n, l, to_send(n, gw[n][l])) for l in range(depth)]
    last = [k[2] for k in pending] + [shards_of(grads['meta_tokens'], SHARD_AXIS['meta_tokens']).astype(BF16)]
    recv_rep, *got, recv_meta = all_gather([pack_flat([grads[n] for n in REP_NAMES], BF16)], "last_exchange",
                                           with_a2a=last)
    store([(k[0], k[1]) for k in pending], got)
    out = {}
    for n in SHARD_NAMES:
        c = w[n].shape[-1]
        recv = [recv_meta] if n == 'meta_tokens' else received[n]
        res = adamw_reduce([a.reshape(N_DEV, -1, c) for a in recv], w[n].reshape(-1, c), m_in[n].reshape(-1, c),
                           v_in[n].reshape(-1, c), "adamw_" + n)
        for kind, buf in zip(('grad', 'delta', 'new_m', 'new_v'), res):
            out[kind, n] = buf.reshape(w[n].shape)
    res = adamw_reduce([recv_rep], pack_flat([w[n] for n in REP_NAMES], F32), pack_flat([m_in[n] for n in REP_NAMES], F32),
                       pack_flat([v_in[n] for n in REP_NAMES], F32), "adamw_replicated")
    shapes = [w[n].shape for n in REP_NAMES]
    for kind, buf in zip(('grad', 'delta', 'new_m', 'new_v'), res):
        for n, a in zip(REP_NAMES, unpack_flat(buf, shapes)):
            out[kind, n] = a
    return (loss, grad_x) + tuple(out[k, n] for k in ('grad', 'delta', 'new_m', 'new_v') for n in W_NAMES)
```

```python
import functools
import math

import jax
import jax.numpy as jnp
from jax import lax
from jax.experimental import pallas as pl
from jax.experimental.pallas import tpu as pltpu

F32 = jnp.float32
BF16 = jnp.bfloat16

EPS = 1e-6
CHUNK = 64
CONV_K = 4
CONV_TAIL = 8
LRU_C = 8.0
LANE = 128
SUBLANE = 8
N_DEV = 8
NEG_BIG = -1e30

ADAM_LR = 0.001
ADAM_B1 = 0.9
ADAM_B2 = 0.999
ADAM_EPS = 1e-08
ADAM_WD = 0.01
ADAM_STEP = 10

VMEM_LIMIT = 56 * 1024 * 1024


def _dg(a, b, dims):
    return lax.dot_general(a.astype(BF16), b.astype(BF16), (dims, ((), ())), preferred_element_type=F32)


@jax.custom_vjp
def bdot(a, b):
    return _dg(a, b, ((1,), (0,)))


@jax.custom_vjp
def bdot_nt(a, b):
    return _dg(a, b, ((1,), (1,)))


@jax.custom_vjp
def bdot_tn(a, b):
    return _dg(a, b, ((0,), (0,)))


bdot.defvjp(lambda a, b: (bdot(a, b), (a, b)),
            lambda r, g: (bdot_nt(g, r[1]).astype(r[0].dtype), bdot_tn(r[0], g).astype(r[1].dtype)))
bdot_nt.defvjp(lambda a, b: (bdot_nt(a, b), (a, b)),
               lambda r, g: (bdot(g, r[1]).astype(r[0].dtype), bdot_tn(g, r[0]).astype(r[1].dtype)))
bdot_tn.defvjp(lambda a, b: (bdot_tn(a, b), (a, b)),
               lambda r, g: (bdot_nt(r[1], g).astype(r[0].dtype), bdot(r[0], g).astype(r[1].dtype)))


def _split_bf16(a):
    hi = a.astype(BF16)
    return hi, (a - hi.astype(F32)).astype(BF16)


def _dot3(a, b, dims):
    (ah, al), (bh, bl) = _split_bf16(a), _split_bf16(b)
    d = lambda x, y: lax.dot_general(x, y, (dims, ((), ())), preferred_element_type=F32)
    return d(ah, bh) + (d(ah, bl) + d(al, bh))


@jax.custom_vjp
def hdot(a, b):
    return _dot3(a, b, ((1,), (0,)))


@jax.custom_vjp
def hdot_tn(a, b):
    return _dot3(a, b, ((0,), (0,)))


hdot.defvjp(lambda a, b: (hdot(a, b), (a, b)),
            lambda r, g: (_dot3(g, r[1], ((1,), (1,))), _dot3(r[0], g, ((0,), (0,)))))
hdot_tn.defvjp(lambda a, b: (hdot_tn(a, b), (a, b)),
               lambda r, g: (_dot3(r[1], g, ((1,), (1,))), _dot3(r[0], g, ((1,), (0,)))))


def rms_norm(x, g):
    return x * lax.rsqrt(jnp.mean(x * x, axis=-1, keepdims=True) + EPS) * g


def row_mask(row0, rows, pad):
    r = row0 + lax.broadcasted_iota(jnp.int32, (rows, 1), 0)
    return (r >= pad).astype(F32)


def conv4(tail, u, w):
    rows = u.shape[0]
    xe = jnp.concatenate([tail, u], axis=0)
    y = w[0:1] * xe[CONV_TAIL - 3:CONV_TAIL - 3 + rows]
    for k in range(1, CONV_K):
        y = y + w[k:k + 1] * xe[CONV_TAIL - 3 + k:CONV_TAIL - 3 + k + rows]
    return y


def shift_rows(x, s, fill):
    rows = x.shape[0]
    return jnp.concatenate([jnp.full((s, x.shape[1]), fill, x.dtype), x[:rows - s]], axis=0)


def lin_scan(a, b):
    rows = a.shape[0]
    s = 1
    while s < rows:
        b = a * shift_rows(b, s, 0.0) + b
        a = a * shift_rows(a, s, 1.0)
        s *= 2
    return b


def cscan_const(ar, ai, br, bi):
    rows = br.shape[0]
    s = 1
    while s < rows:
        brs, bis = shift_rows(br, s, 0.0), shift_rows(bi, s, 0.0)
        br, bi = br + ar * brs - ai * bis, bi + ar * bis + ai * brs
        ar, ai = ar * ar - ai * ai, 2.0 * ar * ai
        s *= 2
    return br, bi


def neg_expm1(z):
    t = jnp.tanh(0.5 * z)
    return -2.0 * t / (1.0 - t)


def tri_masks(n):
    r = lax.broadcasted_iota(jnp.int32, (n, n), 0)
    c = lax.broadcasted_iota(jnp.int32, (n, n), 1)
    return r >= c, r > c, (r == c).astype(F32)


def lru_chunk(pad, row0, params, seqs, carry):
    conv_w, conv_b, w_a, b_a, w_i, b_i, lam, norm_g = params
    u_x, u_gate = seqs
    tail, h0 = carry
    rows = u_x.shape[0]
    m = row_mask(row0, rows, pad)
    xc = conv4(tail, u_x, conv_w) + conv_b
    r = jax.nn.sigmoid(bdot(xc, w_a) + b_a)
    ig = jax.nn.sigmoid(bdot(xc, w_i) + b_i)
    log_a = -LRU_C * r * jax.nn.softplus(-lam)
    a = jnp.exp(log_a)
    b = jnp.sqrt(neg_expm1(2.0 * log_a)) * (ig * xc) * m
    first = (lax.broadcasted_iota(jnp.int32, (rows, 1), 0) == 0).astype(F32)
    b = b + first * (a * h0)
    h = lin_scan(a, b)
    y = jax.nn.gelu(u_gate) * h
    out = rms_norm(y, norm_g) * m
    return (out,), (u_x[rows - CONV_TAIL:], h[rows - 1:])


def gdn_multi(pad, sub, row0, params, seqs, carry):
    wq, wk, wv, a_log, dt_bias, norm_g = params
    u_q, u_k, u_v, u_z, small = seqs
    tq, tk, tv, state = carry
    rows = u_q.shape[0]
    hd = norm_g.shape[1]
    nh = u_q.shape[1] // hd
    nc = rows // sub
    m = row_mask(row0, rows, pad)
    incl, strict, eye = tri_masks(sub)
    tril = incl.astype(F32)
    triu = (lax.broadcasted_iota(jnp.int32, (sub, sub), 0) <= lax.broadcasted_iota(jnp.int32, (sub, sub), 1)).astype(F32)
    qc = jax.nn.silu(conv4(tq, u_q, wq))
    kc = jax.nn.silu(conv4(tk, u_k, wk))
    vc = jax.nn.silu(conv4(tv, u_v, wv))
    beta = jax.nn.sigmoid(small[:, :nh]) * m
    g = -jnp.exp(a_log) * jax.nn.softplus(small[:, nh:2 * nh] + dt_bias) * m
    gate = jax.nn.silu(u_z)
    heads = [slice(h * hd, (h + 1) * hd) for h in range(nh)]
    q_h = [qc[:, sl] for sl in heads]
    k_h = [kc[:, sl] for sl in heads]
    q_h = [q * lax.rsqrt(jnp.sum(q * q, axis=-1, keepdims=True) + EPS) * (hd ** -0.5) * m for q in q_h]
    k_h = [k * lax.rsqrt(jnp.sum(k * k, axis=-1, keepdims=True) + EPS) * m for k in k_h]
    v_h = [vc[:, sl] * m for sl in heads]
    pairs = [(c, h) for c in range(nc) for h in range(nh)]
    cs = lambda x, c: x[c * sub:(c + 1) * sub]
    q = {(c, h): cs(q_h[h], c) for c, h in pairs}
    k = {(c, h): cs(k_h[h], c) for c, h in pairs}
    v = {(c, h): cs(v_h[h], c) for c, h in pairs}
    bt = {(c, h): cs(beta, c)[:, h:h + 1] for c, h in pairs}
    gcs = [hdot(tril, cs(g, c)) for c in range(nc)]
    gts = [hdot_tn(cs(g, c), triu) for c in range(nc)]
    gc = {(c, h): gcs[c][:, h:h + 1] for c, h in pairs}
    decay = {(c, h): jnp.exp(jnp.where(incl, gc[c, h] - gts[c][h:h + 1], NEG_BIG)) for c, h in pairs}
    kb = {p: k[p] * bt[p] for p in pairs}
    kk = {p: bdot_nt(kb[p], k[p]) for p in pairs}
    lmat = {p: jnp.where(strict, kk[p] * decay[p], 0.0) for p in pairs}
    pm = {p: eye - lmat[p] for p in pairs}
    mm = {p: hdot(lmat[p], lmat[p]) for p in pairs}
    s = 2
    while s < sub:
        pm = {p: pm[p] + hdot(pm[p], mm[p]) for p in pairs}
        s *= 2
        if s < sub:
            mm = {p: hdot(mm[p], mm[p]) for p in pairs}
    eg = {p: jnp.exp(gc[p]) for p in pairs}
    u = {p: hdot(pm[p], v[p] * bt[p]) for p in pairs}
    w = {p: hdot(pm[p], kb[p] * eg[p]) for p in pairs}
    attn = {p: bdot_nt(q[p], k[p]) * decay[p] for p in pairs}
    qd = {p: q[p] * eg[p] for p in pairs}
    g_last = {p: gc[p][sub - 1:] for p in pairs}
    kd = {p: k[p] * jnp.exp(g_last[p] - gc[p]) for p in pairs}
    last = {p: jnp.exp(g_last[p]) for p in pairs}
    s_h = [state[sl] for sl in heads]
    o = {}
    for c in range(nc):
        ws = [bdot(w[c, h], s_h[h]) for h in range(nh)]
        qs = [bdot(qd[c, h], s_h[h]) for h in range(nh)]
        v_new = [u[c, h] - ws[h] for h in range(nh)]
        av = [bdot(attn[c, h], v_new[h]) for h in range(nh)]
        kv = [bdot_tn(kd[c, h], v_new[h]) for h in range(nh)]
        for h in range(nh):
            o[c, h] = qs[h] + av[h]
        s_h = [s_h[h] * last[c, h] + kv[h] for h in range(nh)]
    out = jnp.concatenate([jnp.concatenate([rms_norm(o[c, h], norm_g) for h in range(nh)], axis=1)
                           for c in range(nc)], axis=0) * gate * m
    t0 = rows - CONV_TAIL
    return (out,), (u_q[t0:], u_k[t0:], u_v[t0:], jnp.concatenate(s_h, axis=0))


def ssd_multi(pad, dt_lane0, n_groups, sub, row0, params, seqs, carry):
    conv_w, conv_b, a_log, dt_bias, d_skip, norm_g = params
    u_z, u_xbc, small = seqs
    tail, state = carry
    rows = u_z.shape[0]
    width = u_z.shape[1]
    nh = a_log.shape[1]
    hd = width // nh
    ns = (u_xbc.shape[1] - width) // (2 * n_groups)
    hpg = nh // n_groups
    nc = rows // sub
    m = row_mask(row0, rows, pad)
    incl, _, _ = tri_masks(sub)
    tril = incl.astype(F32)
    triu = (lax.broadcasted_iota(jnp.int32, (sub, sub), 0) <= lax.broadcasted_iota(jnp.int32, (sub, sub), 1)).astype(F32)
    xbc = jax.nn.silu(conv4(tail, u_xbc, conv_w) + conv_b)
    xs = xbc[:, :width]
    dt = jax.nn.softplus(small[:, dt_lane0:dt_lane0 + nh] + dt_bias)
    a_all = dt * (-jnp.exp(a_log)) * m
    cs = lambda x, c: x[c * sub:(c + 1) * sub]
    heads = [slice(h * hd, (h + 1) * hd) for h in range(nh)]
    pairs = [(c, h) for c in range(nc) for h in range(nh)]
    grp = lambda h: h // hpg
    bm = {(c, g): cs(xbc[:, width + g * ns: width + (g + 1) * ns] * m, c) for c in range(nc) for g in range(n_groups)}
    cm = {(c, g): cs(xbc[:, width + (n_groups + g) * ns: width + (n_groups + g + 1) * ns] * m, c)
          for c in range(nc) for g in range(n_groups)}
    xh = {(c, h): cs(xs[:, heads[h]], c) for c, h in pairs}
    xdt = {(c, h): xh[c, h] * cs(dt[:, h:h + 1] * m, c) for c, h in pairs}
    acums = [hdot(tril, cs(a_all, c)) for c in range(nc)]
    acts = [hdot_tn(cs(a_all, c), triu) for c in range(nc)]
    acum = {(c, h): acums[c][:, h:h + 1] for c, h in pairs}
    a_last = {p: acum[p][sub - 1:] for p in pairs}
    lmat = {(c, h): jnp.exp(jnp.where(incl, acum[c, h] - acts[c][h:h + 1], NEG_BIG)) for c, h in pairs}
    cb = {cg: bdot_nt(cm[cg], bm[cg]) for cg in bm}
    y_diag = {(c, h): bdot(cb[c, grp(h)] * lmat[c, h], xdt[c, h]) for c, h in pairs}
    st = {(c, h): bdot_tn(xdt[c, h] * jnp.exp(a_last[c, h] - acum[c, h]), bm[c, grp(h)]) for c, h in pairs}
    e_in = {p: jnp.exp(acum[p]) for p in pairs}
    e_out = {p: jnp.exp(a_last[p]) for p in pairs}
    s_h = [state[sl] for sl in heads]
    y = {}
    for c in range(nc):
        off = [bdot_nt(cm[c, grp(h)], s_h[h]) for h in range(nh)]
        for h in range(nh):
            y[c, h] = y_diag[c, h] + off[h] * e_in[c, h] + d_skip[:, h:h + 1] * xh[c, h]
        s_h = [s_h[h] * e_out[c, h] + st[c, h] for h in range(nh)]
    yy = jnp.concatenate([jnp.concatenate([y[c, h] for h in range(nh)], axis=1) for c in range(nc)], axis=0)
    yy = yy * jax.nn.silu(u_z)
    gw = width // n_groups
    outs = [rms_norm(yy[:, g * gw:(g + 1) * gw], norm_g[:, g * gw:(g + 1) * gw]) for g in range(n_groups)]
    out = jnp.concatenate(outs, axis=1) * m
    return (out,), (u_xbc[rows - CONV_TAIL:], jnp.concatenate(s_h, axis=0))


def s5_chunk(pad, row0, params, seqs, carry):
    a_re, a_im, log_dt, b_re, b_im, c_re, c_im, d_skip = params
    (u,) = seqs
    s_re0, s_im0 = carry
    rows = u.shape[0]
    n_state = a_re.shape[1]
    n_grp = log_dt.shape[1]
    per = n_state // n_grp
    expand = (lax.broadcasted_iota(jnp.int32, (n_grp, n_state), 1) // per
              == lax.broadcasted_iota(jnp.int32, (n_grp, n_state), 0)).astype(F32)
    dt = jnp.exp(hdot(log_dt, expand))
    lam_re = jnp.minimum(a_re, -1e-4)
    lam_im = a_im
    mag = jnp.exp(dt * lam_re)
    ab_re = mag * jnp.cos(dt * lam_im)
    ab_im = mag * jnp.sin(dt * lam_im)
    den = lam_re * lam_re + lam_im * lam_im
    f_re = ((ab_re - 1.0) * lam_re + ab_im * lam_im) / den
    f_im = (ab_im * lam_re - (ab_re - 1.0) * lam_im) / den
    bb_re = f_re * b_re - f_im * b_im
    bb_im = f_re * b_im + f_im * b_re
    bu_re = bdot(u, bb_re)
    bu_im = bdot(u, bb_im)
    first = (lax.broadcasted_iota(jnp.int32, (rows, 1), 0) == 0).astype(F32)
    bu_re = bu_re + first * (ab_re * s_re0 - ab_im * s_im0)
    bu_im = bu_im + first * (ab_re * s_im0 + ab_im * s_re0)
    s_re, s_im = cscan_const(ab_re, ab_im, bu_re, bu_im)
    y = bdot(s_re, c_re) - bdot(s_im, c_im) + d_skip * u
    return (y,), (s_re[rows - 1:], s_im[rows - 1:])


def s5_post(pad, row0, params, seqs, carry):
    w_glu, norm_g = params
    (y,) = seqs
    y = jax.nn.gelu(y)
    y = y * jax.nn.sigmoid(bdot(y, w_glu))
    return (rms_norm(y, norm_g),), ()


def mix_out_delta(row0, params, seqs, carry):
    (w_out,) = params
    wd = seqs[0].shape[1]
    acc = bdot(seqs[0], w_out[0:wd])
    for k in range(1, len(seqs)):
        acc = acc + bdot(seqs[k], w_out[k * wd:(k + 1) * wd])
    return (acc,), ()


def blockdiag_expand(w):
    nh, a, b = w.shape
    eye = jnp.eye(nh, dtype=w.dtype)
    return (w[:, :, None, :] * eye[:, None, :, None]).reshape(nh * a, nh * b)


def blockdiag_extract(m, nh):
    a, b = m.shape[0] // nh, m.shape[1] // nh
    on_diag = jnp.eye(nh, dtype=bool)[:, None, :, None]
    return jnp.sum(jnp.where(on_diag, m.reshape(nh, a, nh, b), 0.0), axis=2)


S5_LANES = LANE


def s5_params_expand(a_re, a_im, log_dt, b_re, b_im, c_re, c_im, d_skip):
    n_grp, n_state = a_re.shape
    ch = b_re.shape[-1]
    gpl = S5_LANES // ch
    nq = n_grp // gpl
    eye = jnp.eye(gpl, dtype=F32)[None, :, None, :, None]

    def bexp(b):
        bt = jnp.swapaxes(b, 1, 2).reshape(nq, gpl, b.shape[2], 1, b.shape[1])
        return (bt * eye).reshape(nq, gpl * b.shape[2], gpl * b.shape[1])

    return (a_re.reshape(nq, 1, gpl * n_state), a_im.reshape(nq, 1, gpl * n_state), log_dt.reshape(nq, 1, gpl),
            bexp(b_re), bexp(b_im), bexp(c_re), bexp(c_im), d_skip.reshape(nq, 1, S5_LANES))


def s5_grads_extract(grads, n_grp, n_state, ch):
    da_re, da_im, dlog_dt, db_re, db_im, dc_re, dc_im, dd = grads
    gpl = S5_LANES // ch
    nq = n_grp // gpl
    on_diag = jnp.eye(gpl, dtype=bool)[None, :, None, :, None]

    def bext(b):
        r, c = b.shape[1] // gpl, b.shape[2] // gpl
        d = jnp.sum(jnp.where(on_diag, b.reshape(nq, gpl, r, gpl, c), 0.0), axis=3)
        return jnp.swapaxes(d.reshape(n_grp, r, c), 1, 2)

    return (da_re.reshape(n_grp, n_state), da_im.reshape(n_grp, n_state), dlog_dt.reshape(n_grp),
            bext(db_re), bext(db_im), bext(dc_re), bext(dc_im), dd.reshape(n_grp * ch))


def _cparams(**kw):
    return pltpu.CompilerParams(vmem_limit_bytes=VMEM_LIMIT, **kw)


def tiled_call(body_fn, name, *, n_steps, rows, n_groups=1, reverse=False,
               seq_in=(), whole_in=(), step_in=(), seq_out=(), acc_out=(), step_out=(), carry=(), a2a=(), ag=()):
    def step_of(i):
        return (n_steps - 1 - i) if reverse else i

    def col_of(col, g):
        return col(g) if callable(col) else col

    in_specs, operands = [], []
    for arr, width, col in seq_in:
        in_specs.append(pl.BlockSpec((rows, width), lambda g, i, col=col: (step_of(i), col_of(col, g))))
        operands.append(arr)
    for arr in whole_in:
        if arr.ndim == 2:
            in_specs.append(pl.BlockSpec(arr.shape, lambda g, i: (0, 0)))
        else:
            in_specs.append(pl.BlockSpec((None,) + arr.shape[1:], lambda g, i: (g, 0, 0)))
        operands.append(arr)
    for arr in step_in:
        in_specs.append(pl.BlockSpec((None, None) + arr.shape[2:], lambda g, i: (g, step_of(i), 0, 0)))
        operands.append(arr)
    out_shape, out_specs = [], []
    for total, width, col, dt in seq_out:
        out_shape.append(jax.ShapeDtypeStruct((n_steps * rows, total), dt))
        out_specs.append(pl.BlockSpec((rows, width), lambda g, i, col=col: (step_of(i), col_of(col, g))))
    for r, c in acc_out:
        out_shape.append(jax.ShapeDtypeStruct((n_groups, r, c), F32))
        out_specs.append(pl.BlockSpec((None, r, c), lambda g, i: (g, 0, 0)))
    for r, c in step_out:
        out_shape.append(jax.ShapeDtypeStruct((n_groups, n_steps, r, c), F32))
        out_specs.append(pl.BlockSpec((None, None, r, c), lambda g, i: (g, step_of(i), 0, 0)))
    n_seq, n_whole, n_step = len(seq_in), len(whole_in), len(step_in)
    n_so, n_ao, n_sto = len(seq_out), len(acc_out), len(step_out)
    assert not (a2a and ag)
    hosted = list(a2a) + list(ag)
    n_x = len(hosted)
    n_sem = AG_FIRST_COPIES if ag else N_DEV - 1
    hbm = pl.BlockSpec(memory_space=pl.ANY)
    in_specs += [hbm] * n_x
    operands += hosted
    out_specs += [hbm] * n_x
    out_shape += [jax.ShapeDtypeStruct((N_DEV,) + x.shape if ag else x.shape, x.dtype) for x in hosted]
    scratch = [pltpu.VMEM((r, c), F32) for r, c in carry]
    if n_x:
        scratch += [pltpu.SemaphoreType.DMA((n_x, n_sem)), pltpu.SemaphoreType.DMA((n_x, n_sem)),
                    pltpu.SemaphoreType.DMA((n_x,))]

    def body(*refs):
        pos = 0
        seq_refs = refs[pos:pos + n_seq]
        pos += n_seq
        whole_refs = refs[pos:pos + n_whole]
        pos += n_whole
        step_refs = refs[pos:pos + n_step]
        pos += n_step + n_x
        so_refs = refs[pos:pos + n_so]
        pos += n_so
        ao_refs = refs[pos:pos + n_ao]
        pos += n_ao
        sto_refs = refs[pos:pos + n_sto]
        pos += n_sto
        xo_refs = refs[pos:pos + n_x]
        pos += n_x
        carry_refs = refs[pos:pos + len(carry)]
        pos += len(carry)
        x_refs = refs[n_seq + n_whole + n_step:n_seq + n_whole + n_step + n_x]
        g, i = pl.program_id(0), pl.program_id(1)
        if n_x:
            locals_, sends, recvs = (ag_first_copies if ag else a2a_copies)(x_refs, xo_refs, *refs[pos:])

            @pl.when((g == 0) & (i == 0))
            def _():
                for cp in locals_ + sends:
                    cp.start()

        @pl.when(i == 0)
        def _():
            for r in carry_refs:
                r[...] = jnp.zeros(r.shape, r.dtype)
            for r in ao_refs:
                r[...] = jnp.zeros(r.shape, r.dtype)

        row0 = step_of(i) * rows
        seq_o, acc_o, step_o, new_c = body_fn(row0, [r[...] for r in whole_refs], [r[...] for r in seq_refs],
                                              [r[...] for r in step_refs], [r[...] for r in carry_refs])
        for r, val in zip(so_refs, seq_o, strict=True):
            r[...] = val.astype(r.dtype)
        for r, val in zip(ao_refs, acc_o, strict=True):
            r[...] += val
        for r, val in zip(sto_refs, step_o, strict=True):
            r[...] = val
        for r, val in zip(carry_refs, new_c, strict=True):
            r[...] = val
        if n_x:
            @pl.when((g == n_groups - 1) & (i == n_steps - 1))
            def _():
                for cp in recvs:
                    cp.wait_recv()
                for cp in sends:
                    cp.wait_send()
                for cp in locals_:
                    cp.wait()

    return pl.pallas_call(
        body, name=name, grid=(n_groups, n_steps), in_specs=in_specs, out_specs=out_specs, out_shape=out_shape,
        scratch_shapes=scratch,
        compiler_params=_cparams(dimension_semantics=("arbitrary", "arbitrary")),
    )(*operands)


def mixer_fwd(fn, name, *, n_steps, rows, seqs, params, out, carry, n_groups=1, ag=()):
    def body(row0, whole, seq_vals, steps, carry_vals):
        outs, new_c = fn(row0, tuple(whole), tuple(seq_vals), tuple(carry_vals))
        return list(outs), [], list(carry_vals), list(new_c)

    res = tiled_call(body, name, n_steps=n_steps, rows=rows, n_groups=n_groups, seq_in=seqs, whole_in=params,
                     seq_out=[out], step_out=carry, carry=carry, ag=ag)
    return res[0], list(res[1:1 + len(carry)]), list(res[1 + len(carry):])


def mixer_bwd(fn, name, *, n_steps, rows, seqs, params, dout, saved, carry, n_groups=1, a2a=()):
    n_seq = len(seqs)

    def body(row0, whole, seq_vals, steps, dcarry):
        params_f = tuple(p.astype(F32) for p in whole)
        _, vjp = jax.vjp(lambda p, s, c: fn(row0, p, s, c), params_f, tuple(seq_vals[:n_seq]), tuple(steps))
        dp, ds, dc = vjp(((seq_vals[n_seq],), tuple(dcarry)))
        return list(ds), list(dp), [], list(dc)

    seq_out = [(n_groups * w, w, (lambda g: g), F32) if callable(c) else (w, w, 0, F32) for a, w, c in seqs]
    acc_out = [p.shape[-2:] for p in params]
    res = tiled_call(body, name, n_steps=n_steps, rows=rows, n_groups=n_groups, reverse=True,
                     seq_in=list(seqs) + [dout], whole_in=params, step_in=saved,
                     seq_out=seq_out, acc_out=acc_out, carry=carry, a2a=a2a)
    n_p = len(params)
    if a2a:
        return list(res[:n_seq]), list(res[n_seq:n_seq + n_p]), list(res[n_seq + n_p:])
    return list(res[:n_seq]), list(res[n_seq:])


def rms_fwd(h, g, name, rows):
    def body(row0, whole, seqs, steps, carry):
        return [rms_norm(seqs[0], whole[0])], [], [], []
    d = h.shape[1]
    return tiled_call(body, name, n_steps=h.shape[0] // rows, rows=rows, seq_in=[(h, d, 0)], whole_in=[g],
                      seq_out=[(d, d, 0, BF16)])[0]


def rms_bwd_add(h, dxn, dh_out, g, name, rows):
    def body(row0, whole, seqs, steps, carry):
        _, vjp = jax.vjp(rms_norm, seqs[0], whole[0])
        dh, dg = vjp(seqs[1])
        return [seqs[2] + dh], [dg], [], []
    d = h.shape[1]
    dh_in, dg = tiled_call(body, name, n_steps=h.shape[0] // rows, rows=rows,
                           seq_in=[(h, d, 0), (dxn, d, 0), (dh_out, d, 0)], whole_in=[g],
                           seq_out=[(d, d, 0, F32)], acc_out=[(1, d)])
    return dh_in, dg[0]


def mix_out_fwd(h, ys, w_out, name, rows, ag=()):
    def body(row0, whole, seqs, steps, carry):
        (delta,), _ = mix_out_delta(row0, (whole[0],), tuple(seqs[1:]), ())
        return [seqs[0] + delta], [], [], []
    d, wd = h.shape[1], ys[0].shape[1]
    res = tiled_call(body, name, n_steps=h.shape[0] // rows, rows=rows,
                     seq_in=[(h, d, 0)] + [(y, wd, 0) for y in ys], whole_in=[w_out], seq_out=[(d, d, 0, F32)], ag=ag)
    return res[0], list(res[1:])


def loss_and_grad(h, target, g, name, rows, first_row):
    def body(row0, whole, seqs, steps, carry):
        hh, tt = seqs
        keep = row_mask(row0, hh.shape[0], first_row)

        def f(hv, gv):
            err = rms_norm(hv, gv) - tt
            return 0.5 * jnp.sum(jnp.mean(err * err, axis=-1, keepdims=True) * keep, axis=0, keepdims=True)

        val, vjp = jax.vjp(f, hh, whole[0])
        dh, dg = vjp(jnp.ones((1, 1), F32))
        return [dh], [jnp.broadcast_to(val, (1, LANE)), dg], [], []
    d = h.shape[1]
    dh, loss, dg = tiled_call(body, name, n_steps=h.shape[0] // rows, rows=rows,
                              seq_in=[(h, d, 0), (target, d, 0)], whole_in=[g],
                              seq_out=[(d, d, 0, F32)], acc_out=[(1, LANE), (1, d)])
    return loss[0, 0, 0], dh, dg[0]


def _pick(n, cands):
    for c in cands:
        if n % c == 0:
            return c
    raise ValueError(f"no tile for {n}")


ROW_TILES = (1056, 704, 352, 192, 96, 64)
COL_TILES = (256, 128)
NT_DIMS = (((1,), (1,)), ((), ()))
TN_DIMS = (((0,), (0,)), ((), ()))


def ffn_fwd(h, xn, wg, wu, wd, name, ag=()):
    t, d = h.shape
    f = wg.shape[1]
    tm = _pick(t, ROW_TILES)
    tn = _pick(f, COL_TILES)
    n_i, n_j = t // tm, f // tn
    n_x = len(ag)

    def body(h_ref, xn_ref, wg_ref, wu_ref, wd_ref, *rest):
        x_refs, o_ref, xo_refs, acc_ref = rest[:n_x], rest[n_x], rest[n_x + 1:2 * n_x + 1], rest[2 * n_x + 1]
        i, j = pl.program_id(0), pl.program_id(1)
        if n_x:
            locals_, sends, recvs = ag_first_copies(x_refs, xo_refs, *rest[2 * n_x + 2:])

            @pl.when((i == 0) & (j == 0))
            def _():
                for cp in locals_ + sends:
                    cp.start()

        @pl.when(j == 0)
        def _():
            acc_ref[...] = jnp.zeros(acc_ref.shape, F32)

        x = xn_ref[...]
        g = jnp.dot(x, wg_ref[...], preferred_element_type=F32)
        u = jnp.dot(x, wu_ref[...], preferred_element_type=F32)
        a = (jax.nn.silu(g) * u).astype(BF16)
        acc_ref[...] += jnp.dot(a, wd_ref[...], preferred_element_type=F32)

        @pl.when(j == n_j - 1)
        def _():
            o_ref[...] = h_ref[...] + 0.5 * acc_ref[...]

        if n_x:
            @pl.when((i == n_i - 1) & (j == n_j - 1))
            def _():
                for cp in recvs:
                    cp.wait_recv()
                for cp in sends:
                    cp.wait_send()
                for cp in locals_:
                    cp.wait()

    hbm = pl.BlockSpec(memory_space=pl.ANY)
    scratch = [pltpu.VMEM((tm, d), F32)]
    if n_x:
        scratch += [pltpu.SemaphoreType.DMA((n_x, AG_FIRST_COPIES)), pltpu.SemaphoreType.DMA((n_x, AG_FIRST_COPIES)),
                    pltpu.SemaphoreType.DMA((n_x,))]
    res = pl.pallas_call(
        body, name=name, grid=(n_i, n_j),
        in_specs=[pl.BlockSpec((tm, d), lambda i, j: (i, 0)), pl.BlockSpec((tm, d), lambda i, j: (i, 0)),
                  pl.BlockSpec((d, tn), lambda i, j: (0, j)), pl.BlockSpec((d, tn), lambda i, j: (0, j)),
                  pl.BlockSpec((tn, d), lambda i, j: (j, 0))] + [hbm] * n_x,
        out_specs=[pl.BlockSpec((tm, d), lambda i, j: (i, 0))] + [hbm] * n_x,
        out_shape=[jax.ShapeDtypeStruct((t, d), F32)] + [jax.ShapeDtypeStruct((N_DEV,) + x.shape, x.dtype) for x in ag],
        scratch_shapes=scratch,
        compiler_params=_cparams(dimension_semantics=("arbitrary", "arbitrary")),
    )(h, xn, wg, wu, wd, *ag)
    return res[0], list(res[1:])


def ffn_bwd(xn, dh, wg, wu, wd, name):
    t, d = dh.shape
    f = wg.shape[1]
    tm = _pick(t, ROW_TILES)
    tn = _pick(f, COL_TILES)
    n_i = t // tm

    def body(xn_ref, dh_ref, wg_ref, wu_ref, wd_ref, dxn_ref, dwg_ref, dwu_ref, dwd_ref, ag_ref, au_ref, ad_ref):
        j, i = pl.program_id(0), pl.program_id(1)
        rows = pl.ds(pl.multiple_of(i * tm, 8), tm)
        x = xn_ref[rows, :]
        dhh = (0.5 * dh_ref[...]).astype(BF16)
        wgv, wuv = wg_ref[...], wu_ref[...]
        g = jnp.dot(x, wgv, preferred_element_type=F32)
        u = jnp.dot(x, wuv, preferred_element_type=F32)
        sg = jax.nn.sigmoid(g)
        s = g * sg
        da = lax.dot_general(dhh, wd_ref[...], NT_DIMS, preferred_element_type=F32)
        dwd = lax.dot_general((s * u).astype(BF16), dhh, TN_DIMS, preferred_element_type=F32)
        dg = (da * u * (sg * (1.0 + g * (1.0 - sg)))).astype(BF16)
        du = (da * s).astype(BF16)
        dwg = lax.dot_general(x, dg, TN_DIMS, preferred_element_type=F32)
        dwu = lax.dot_general(x, du, TN_DIMS, preferred_element_type=F32)
        dx = (lax.dot_general(dg, wgv, NT_DIMS, preferred_element_type=F32)
              + lax.dot_general(du, wuv, NT_DIMS, preferred_element_type=F32))

        @pl.when(i == 0)
        def _():
            ag_ref[...] = dwg
            au_ref[...] = dwu
            ad_ref[...] = dwd

        @pl.when(i > 0)
        def _():
            ag_ref[...] += dwg
            au_ref[...] += dwu
            ad_ref[...] += dwd

        @pl.when(i == n_i - 1)
        def _():
            dwg_ref[...] = ag_ref[...].astype(BF16)
            dwu_ref[...] = au_ref[...].astype(BF16)
            dwd_ref[...] = ad_ref[...].astype(BF16)

        @pl.when(j == 0)
        def _():
            dxn_ref[rows, :] = dx

        @pl.when(j > 0)
        def _():
            dxn_ref[rows, :] += dx

    return pl.pallas_call(
        body, name=name, grid=(f // tn, t // tm),
        in_specs=[pl.BlockSpec((t, d), lambda j, i: (0, 0)), pl.BlockSpec((tm, d), lambda j, i: (i, 0)),
                  pl.BlockSpec((d, tn), lambda j, i: (0, j)), pl.BlockSpec((d, tn), lambda j, i: (0, j)),
                  pl.BlockSpec((tn, d), lambda j, i: (j, 0))],
        out_specs=[pl.BlockSpec((t, d), lambda j, i: (0, 0)), pl.BlockSpec((d, tn), lambda j, i: (0, j)),
                   pl.BlockSpec((d, tn), lambda j, i: (0, j)), pl.BlockSpec((tn, d), lambda j, i: (j, 0))],
        out_shape=[jax.ShapeDtypeStruct((t, d), F32), jax.ShapeDtypeStruct((d, f), BF16),
                   jax.ShapeDtypeStruct((d, f), BF16), jax.ShapeDtypeStruct((f, d), BF16)],
        scratch_shapes=[pltpu.VMEM((d, tn), F32), pltpu.VMEM((d, tn), F32), pltpu.VMEM((tn, d), F32)],
        compiler_params=_cparams(dimension_semantics=("arbitrary", "arbitrary")),
    )(xn, dh, wg, wu, wd)


def matmul_cols(xn, w, name):
    t, d = xn.shape
    n = w.shape[1]
    tn = _pick(n, COL_TILES)

    def body(x_ref, w_ref, o_ref):
        o_ref[...] = jnp.dot(x_ref[...], w_ref[...], preferred_element_type=F32)

    return pl.pallas_call(
        body, name=name, grid=(n // tn,),
        in_specs=[pl.BlockSpec((t, d), lambda j: (0, 0)), pl.BlockSpec((d, tn), lambda j: (0, j))],
        out_specs=pl.BlockSpec((t, tn), lambda j: (0, j)),
        out_shape=jax.ShapeDtypeStruct((t, n), F32),
        compiler_params=_cparams(dimension_semantics=("arbitrary",)),
    )(xn, w)


def matmul_cols_bwd(xn, dy, w, name):
    t, d = xn.shape
    n = w.shape[1]
    tn = _pick(n, COL_TILES)

    def body(x_ref, dy_ref, w_ref, dx_ref, dw_ref):
        j = pl.program_id(0)
        dyv = dy_ref[...].astype(BF16)
        dw_ref[...] = lax.dot_general(x_ref[...], dyv, TN_DIMS, preferred_element_type=F32).astype(BF16)
        dx = lax.dot_general(dyv, w_ref[...], NT_DIMS, preferred_element_type=F32)

        @pl.when(j == 0)
        def _():
            dx_ref[...] = dx

        @pl.when(j > 0)
        def _():
            dx_ref[...] += dx

    return pl.pallas_call(
        body, name=name, grid=(n // tn,),
        in_specs=[pl.BlockSpec((t, d), lambda j: (0, 0)), pl.BlockSpec((t, tn), lambda j: (0, j)),
                  pl.BlockSpec((d, tn), lambda j: (0, j))],
        out_specs=[pl.BlockSpec((t, d), lambda j: (0, 0)), pl.BlockSpec((d, tn), lambda j: (0, j))],
        out_shape=[jax.ShapeDtypeStruct((t, d), F32), jax.ShapeDtypeStruct((d, n), BF16)],
        compiler_params=_cparams(dimension_semantics=("arbitrary",)),
    )(xn, dy, w)


def _peer(mx, my, mc, k):
    px = 1 - mx if (k >> 2) & 1 else mx
    py = 1 - my if (k >> 1) & 1 else my
    pc = 1 - mc if k & 1 else mc
    return (px, py, pc), 4 * px + 2 * py + pc


def a2a_copies(x_refs, o_refs, send_sems, recv_sems, local_sems):
    mx, my, mc = lax.axis_index("x"), lax.axis_index("y"), lax.axis_index("c")
    me = 4 * mx + 2 * my + mc
    peers = [_peer(mx, my, mc, k) for k in range(1, N_DEV)]
    locals_, sends, recvs = [], [], []
    for a, (x_ref, o_ref) in enumerate(zip(x_refs, o_refs, strict=True)):
        locals_.append(pltpu.make_async_copy(x_ref.at[me], o_ref.at[me], local_sems.at[a]))
        for k, (dev, peer) in enumerate(peers):
            common = dict(send_sem=send_sems.at[a, k], recv_sem=recv_sems.at[a, k], device_id=dev,
                          device_id_type=pl.DeviceIdType.MESH)
            sends.append(pltpu.make_async_remote_copy(src_ref=x_ref.at[peer], dst_ref=o_ref.at[me], **common))
            recvs.append(pltpu.make_async_remote_copy(src_ref=x_ref.at[peer], dst_ref=o_ref.at[peer], **common))
    return locals_, sends, recvs


def all_gather(xs, name, with_a2a=()):
    n, n2 = len(xs), len(with_a2a)
    chip_flips = (4, 2, 6)

    def body(*refs):
        x_refs, t_refs = refs[:n], refs[n:n + n2]
        o_refs, r_refs = refs[n + n2:2 * n + n2], refs[2 * n + n2:2 * n + 2 * n2]
        send_sems, recv_sems, local_sems = refs[2 * n + 2 * n2:2 * n + 2 * n2 + 3]
        mx, my, mc = lax.axis_index("x"), lax.axis_index("y"), lax.axis_index("c")
        me = 4 * mx + 2 * my + mc
        sib_dev, sib = _peer(mx, my, mc, 1)

        def copy(a, k, row, to, src=None):
            return pltpu.make_async_remote_copy(
                src_ref=o_refs[a].at[row] if src is None else src, dst_ref=o_refs[a].at[row],
                send_sem=send_sems.at[a, k], recv_sem=recv_sems.at[a, k], device_id=to,
                device_id_type=pl.DeviceIdType.MESH)

        locals_, first, passed = [], [], []
        t_recvs = []
        if n2:
            t_locals, t_sends, t_recvs = a2a_copies(t_refs, r_refs, *refs[2 * n + 2 * n2 + 3:])
            locals_ += t_locals
            first += t_sends
        for a in range(n):
            locals_.append(pltpu.make_async_copy(x_refs[a], o_refs[a].at[me], local_sems.at[a]))
            first.append(copy(a, 0, me, sib_dev, src=x_refs[a]))
            for j, f in enumerate(chip_flips):
                first.append(copy(a, 1 + j, me, _peer(mx, my, mc, f)[0], src=x_refs[a]))
        for cp in locals_ + first:
            cp.start()
        for a in range(n):
            for j, f in enumerate(chip_flips):
                row = _peer(mx, my, mc, f)[1]
                copy(a, 1 + j, row, sib_dev).wait_recv()
                fwd = copy(a, 4 + j, row, sib_dev)
                fwd.start()
                passed.append(fwd)
        for a in range(n):
            copy(a, 0, sib, sib_dev).wait_recv()
            for j, f in enumerate(chip_flips):
                copy(a, 4 + j, _peer(mx, my, mc, f ^ 1)[1], sib_dev).wait_recv()
        for cp in t_recvs:
            cp.wait_recv()
        for cp in first + passed:
            cp.wait_send()
        for cp in locals_:
            cp.wait()

    hbm = pl.BlockSpec(memory_space=pl.ANY)
    scratch = [pltpu.SemaphoreType.DMA((n, N_DEV - 1)), pltpu.SemaphoreType.DMA((n, N_DEV - 1)),
               pltpu.SemaphoreType.DMA((n,))]
    if n2:
        scratch += [pltpu.SemaphoreType.DMA((n2, N_DEV - 1)), pltpu.SemaphoreType.DMA((n2, N_DEV - 1)),
                    pltpu.SemaphoreType.DMA((n2,))]
    return pl.pallas_call(
        body, name=name, in_specs=[hbm] * (n + n2), out_specs=[hbm] * (n + n2),
        out_shape=[jax.ShapeDtypeStruct((N_DEV,) + x.shape, x.dtype) for x in xs]
        + [jax.ShapeDtypeStruct(x.shape, x.dtype) for x in with_a2a],
        scratch_shapes=scratch,
    )(*xs, *with_a2a)


AG_FIRST_COPIES = 4


def ag_first_copies(x_refs, o_refs, send_sems, recv_sems, local_sems):
    mx, my, mc = lax.axis_index("x"), lax.axis_index("y"), lax.axis_index("c")
    me = 4 * mx + 2 * my + mc
    targets = [_peer(mx, my, mc, f) for f in (1, 4, 2, 6)]
    locals_, sends, recvs = [], [], []
    for a, (x_ref, o_ref) in enumerate(zip(x_refs, o_refs, strict=True)):
        locals_.append(pltpu.make_async_copy(x_ref, o_ref.at[me], local_sems.at[a]))
        for k, (dev, row) in enumerate(targets):
            common = dict(send_sem=send_sems.at[a, k], recv_sem=recv_sems.at[a, k], device_id=dev,
                          device_id_type=pl.DeviceIdType.MESH)
            sends.append(pltpu.make_async_remote_copy(src_ref=x_ref, dst_ref=o_ref.at[me], **common))
            recvs.append(pltpu.make_async_remote_copy(src_ref=x_ref, dst_ref=o_ref.at[row], **common))
    return locals_, sends, recvs


def ag_second_level(bufs, name):
    n = len(bufs)
    chip_flips = (4, 2, 6)

    def body(*refs):
        o_refs = refs[n:2 * n]
        send_sems, recv_sems = refs[2 * n:]
        mx, my, mc = lax.axis_index("x"), lax.axis_index("y"), lax.axis_index("c")
        sib_dev, _ = _peer(mx, my, mc, 1)
        sends, recvs = [], []
        for a in range(n):
            for j, f in enumerate(chip_flips):
                common = dict(send_sem=send_sems.at[a, j], recv_sem=recv_sems.at[a, j], device_id=sib_dev,
                              device_id_type=pl.DeviceIdType.MESH)
                row, sib_row = _peer(mx, my, mc, f)[1], _peer(mx, my, mc, f ^ 1)[1]
                sends.append(pltpu.make_async_remote_copy(src_ref=o_refs[a].at[row], dst_ref=o_refs[a].at[row], **common))
                recvs.append(pltpu.make_async_remote_copy(src_ref=o_refs[a].at[row], dst_ref=o_refs[a].at[sib_row],
                                                          **common))
        for cp in sends:
            cp.start()
        for cp in recvs:
            cp.wait_recv()
        for cp in sends:
            cp.wait_send()

    return pl.pallas_call(
        body, name=name,
        in_specs=[pl.BlockSpec(memory_space=pl.ANY)] * n, out_specs=[pl.BlockSpec(memory_space=pl.ANY)] * n,
        out_shape=[jax.ShapeDtypeStruct(x.shape, x.dtype) for x in bufs],
        input_output_aliases={a: a for a in range(n)},
        scratch_shapes=[pltpu.SemaphoreType.DMA((n, len(chip_flips))), pltpu.SemaphoreType.DMA((n, len(chip_flips)))],
    )(*bufs)


PACK_COLS = 1024
PACK_ROWS = 256
PARTS_TILE_BYTES = 4 * 1024 * 1024


def adamw_reduce(parts, w, m, v, name):
    r, c = w.shape
    if (r // len(parts)) % SUBLANE:
        parts = [jnp.concatenate(parts, axis=1)]
    n_slab = len(parts)
    rs = r // n_slab
    fits = [t for t in (512, 352, 256, 128, 64, 32, 16, 8) if N_DEV * t * c * parts[0].dtype.itemsize <= PARTS_TILE_BYTES]
    tr = rs if rs < 2 * SUBLANE else _pick(rs, fits)
    n_t = rs // tr
    c1 = 1.0 - ADAM_B1 ** ADAM_STEP
    c2 = 1.0 - ADAM_B2 ** ADAM_STEP

    def body(*refs):
        p_refs = refs[:n_slab]
        w_ref, m_ref, v_ref, g_ref, d_ref, mo_ref, vo_ref = refs[n_slab:]
        slab = pl.program_id(0)
        for k, p_ref in enumerate(p_refs):
            @pl.when(slab == k)
            def _(p_ref=p_ref):
                g = p_ref[0].astype(F32)
                for dev in range(1, N_DEV):
                    g = g + p_ref[dev].astype(F32)
                mn = ADAM_B1 * m_ref[...] + (1.0 - ADAM_B1) * g
                vn = ADAM_B2 * v_ref[...] + (1.0 - ADAM_B2) * (g * g)
                g_ref[...] = g
                mo_ref[...] = mn
                vo_ref[...] = vn
                d_ref[...] = -ADAM_LR * ((mn / c1) / (jnp.sqrt(vn / c2) + ADAM_EPS) + ADAM_WD * w_ref[...])

    spec = pl.BlockSpec((tr, c), lambda s, i: (s * n_t + i, 0))
    p_specs = [pl.BlockSpec((N_DEV, tr, c), lambda s, i, k=k: (0, jnp.where(s == k, i, 0), 0)) for k in range(n_slab)]
    return pl.pallas_call(
        body, name=name, grid=(n_slab, n_t),
        in_specs=p_specs + [spec, spec, spec],
        out_specs=[spec] * 4, out_shape=[jax.ShapeDtypeStruct((r, c), F32)] * 4,
        compiler_params=_cparams(dimension_semantics=("arbitrary", "arbitrary")),
    )(*parts, w, m, v)


def pack_flat(arrs, dtype):
    parts = []
    for a in arrs:
        flat = a.reshape(-1).astype(dtype)
        k = -(-flat.shape[0] // PACK_COLS)
        parts.append(jnp.pad(flat, (0, k * PACK_COLS - flat.shape[0])).reshape(k, PACK_COLS))
    buf = jnp.concatenate(parts, axis=0)
    return jnp.pad(buf, ((0, -buf.shape[0] % PACK_ROWS), (0, 0)))


def unpack_flat(buf, shapes):
    out, r0 = [], 0
    for s in shapes:
        n = math.prod(s)
        k = -(-n // PACK_COLS)
        out.append(buf[r0:r0 + k].reshape(-1)[:n].reshape(tuple(s)))
        r0 += k
    return out


W_NAMES = ('meta_tokens', 'ffn1_norm', 'ffn1_w_gate', 'ffn1_w_up', 'ffn1_w_down', 'mix_norm', 'w_in', 'w_out',
           'lru_conv_w', 'lru_conv_b', 'lru_w_a', 'lru_b_a', 'lru_w_i', 'lru_b_i', 'lru_lambda', 'lru_norm',
           'gdn_conv_w', 'gdn_a_log', 'gdn_dt_bias', 'gdn_norm', 'ssd_conv_w', 'ssd_conv_b', 'ssd_a_log',
           'ssd_dt_bias', 'ssd_d', 'ssd_norm', 's5_a_re', 's5_a_im', 's5_log_dt', 's5_b_re', 's5_b_im', 's5_c_re',
           's5_c_im', 's5_d', 's5_w_glu', 's5_norm', 'ffn2_norm', 'ffn2_w_gate', 'ffn2_w_up', 'ffn2_w_down',
           'final_norm')
SHARD_AXIS = {'meta_tokens': 1, 'ffn1_w_gate': 2, 'ffn1_w_up': 2, 'ffn1_w_down': 1, 'w_in': 2, 'w_out': 1,
              'lru_conv_w': 2, 'gdn_conv_w': 2, 'ssd_conv_w': 2, 's5_w_glu': 1, 'ffn2_w_gate': 2, 'ffn2_w_up': 2,
              'ffn2_w_down': 1}
BIG_NAMES = ('ffn1_w_gate', 'ffn1_w_up', 'ffn1_w_down', 'w_in', 'w_out', 's5_w_glu', 'ffn2_w_gate', 'ffn2_w_up',
             'ffn2_w_down')
SHARD_NAMES = tuple(n for n in W_NAMES if n in SHARD_AXIS)
REP_NAMES = tuple(n for n in W_NAMES if n not in SHARD_AXIS)
SSD_GROUPS = 2
S5_CH = 16


def unshard(g, axis):
    if axis == 0:
        return g.reshape((-1,) + g.shape[2:])
    return jnp.concatenate([g[p] for p in range(N_DEV)], axis=axis)


def kernel(*args):
    n_w = len(W_NAMES)
    x = args[0]
    w = dict(zip(W_NAMES, args[1:1 + n_w]))
    target = args[1 + n_w]
    m_in = dict(zip(W_NAMES, args[2 + n_w:2 + 2 * n_w]))
    v_in = dict(zip(W_NAMES, args[2 + 2 * n_w:2 + 3 * n_w]))

    depth, d = w['ffn1_norm'].shape
    seq = x.shape[1]
    n_meta = w['meta_tokens'].shape[0]
    pad = CHUNK - n_meta
    tp = pad + n_meta + seq
    wg = d // 2
    xbc_w = w['ssd_conv_w'].shape[-1] * N_DEV
    gdn_hd = w['gdn_norm'].shape[-1]
    gdn_h = wg // gdn_hd
    ssd_h = w['ssd_a_log'].shape[-1]
    lru_h = w['lru_w_a'].shape[1]
    s5_g, s5_n = w['s5_a_re'].shape[1:]
    s5_q = wg // S5_LANES
    row_tile = _pick(tp, (192, 96, 64))
    norm_tile = _pick(tp, ROW_TILES)

    LAYER_NAMES = tuple(n for n in SHARD_NAMES if n != 'meta_tokens')

    def local_of(n, l):
        return w[n][l].astype(BF16 if n in BIG_NAMES else F32)

    def unshard_layer(gathered):
        return {n: unshard(g, SHARD_AXIS[n] - 1) for n, g in gathered.items()}

    first = all_gather([local_of(n, 0) for n in LAYER_NAMES] + [w['meta_tokens']], "gather_weights")
    meta_full = unshard(first[-1], SHARD_AXIS['meta_tokens'])
    full = [unshard_layer(dict(zip(LAYER_NAMES, first[:-1])))] + [None] * (depth - 1)

    segs = [('a_x', wg), ('a_gate', wg), ('b_q', wg), ('b_k', wg), ('b_v', wg), ('b_z', wg), ('c_xbc', xbc_w),
            ('c_z', wg), ('d_u', wg), ('small_b', LANE), ('small_c', LANE)]
    off, o = {}, 0
    for nme, wd_ in segs:
        assert o % wd_ == 0, (nme, o, wd_)
        off[nme] = o
        o += wd_
    o_beta = 6 * wg
    o_cz = o_beta + 2 * gdn_h
    o_xbc = o_cz + wg
    o_dt = o_xbc + xbc_w
    o_du = o_dt + ssd_h

    def pack_cols(a):
        z = lambda k: jnp.zeros(a.shape[:-1] + (k,), a.dtype)
        return jnp.concatenate([a[..., :o_beta], a[..., o_xbc:o_dt], a[..., o_cz:o_xbc], a[..., o_du:],
                                a[..., o_beta:o_cz], z(LANE - 2 * gdn_h), a[..., o_dt:o_du], z(LANE - ssd_h)], axis=-1)

    def unpack_cols(a):
        sb, sc = off['small_b'], off['small_c']
        return jnp.concatenate([a[..., :o_beta], a[..., sb:sb + 2 * gdn_h], a[..., off['c_z']:off['c_z'] + wg],
                                a[..., off['c_xbc']:off['c_xbc'] + xbc_w], a[..., sc:sc + ssd_h],
                                a[..., off['d_u']:off['d_u'] + wg]], axis=-1)

    w_in_p = [None] * depth
    w_in_p[0] = pack_cols(full[0]['w_in'])

    def col(name, width):
        return off[name] // width

    def row(a):
        return a.reshape(1, -1)

    def layer_params(l):
        gcw = full[l]['gdn_conv_w']
        lru = [full[l]['lru_conv_w'], row(w['lru_conv_b'][l]), blockdiag_expand(w['lru_w_a'][l]), row(w['lru_b_a'][l]),
               blockdiag_expand(w['lru_w_i'][l]), row(w['lru_b_i'][l]), row(w['lru_lambda'][l]), row(w['lru_norm'][l])]
        gdn = [gcw[:, :wg], gcw[:, wg:2 * wg], gcw[:, 2 * wg:], row(w['gdn_a_log'][l]), row(w['gdn_dt_bias'][l]),
               row(w['gdn_norm'][l])]
        ssd = [full[l]['ssd_conv_w'], row(w['ssd_conv_b'][l]), row(w['ssd_a_log'][l]), row(w['ssd_dt_bias'][l]),
               row(w['ssd_d'][l]), row(w['ssd_norm'][l])]
        s5 = list(s5_params_expand(*[w[n][l] for n in ('s5_a_re', 's5_a_im', 's5_log_dt', 's5_b_re', 's5_b_im',
                                                          's5_c_re', 's5_c_im', 's5_d')]))
        post = [full[l]['s5_w_glu'], row(w['s5_norm'][l])]
        return lru, gdn, ssd, s5, post

    lru_fn = functools.partial(lru_chunk, pad)
    gdn_fn = functools.partial(gdn_multi, pad, CHUNK)
    ssd_fn = functools.partial(ssd_multi, pad, 0, SSD_GROUPS, CHUNK)
    s5_fn = functools.partial(s5_chunk, pad)
    post_fn = functools.partial(s5_post, pad)
    n_state_lanes = (S5_LANES // S5_CH) * s5_n

    def mixer_specs(proj):
        lru_seqs = [(proj, wg, col('a_x', wg)), (proj, wg, col('a_gate', wg))]
        gdn_seqs = [(proj, wg, col('b_q', wg)), (proj, wg, col('b_k', wg)), (proj, wg, col('b_v', wg)),
                    (proj, wg, col('b_z', wg)), (proj, LANE, col('small_b', LANE))]
        ssd_seqs = [(proj, wg, col('c_z', wg)), (proj, xbc_w, col('c_xbc', xbc_w)), (proj, LANE, col('small_c', LANE))]
        base = col('d_u', S5_LANES)
        s5_seqs = [(proj, S5_LANES, lambda g: base + g)]
        return lru_seqs, gdn_seqs, ssd_seqs, s5_seqs

    lru_carry = [(CONV_TAIL, wg), (1, wg)]
    gdn_carry = [(CONV_TAIL, wg)] * 3 + [(wg, gdn_hd)]
    ssd_carry = [(CONV_TAIL, xbc_w), (wg, (xbc_w - wg) // (2 * SSD_GROUPS))]
    s5_carry = [(1, n_state_lanes)] * 2
    rk = dict(n_steps=tp // row_tile, rows=row_tile)
    mk = rk
    out_w = (wg, wg, 0, F32)

    h = jnp.concatenate([jnp.zeros((pad, d), F32), meta_full, x[0]], axis=0)
    target_p = jnp.concatenate([jnp.zeros((pad + n_meta, d), F32), target[0]], axis=0)
    saved = []
    for l in range(depth):
        lru_p, gdn_p, ssd_p, s5_p, post_p = layer_params(l)
        h0 = h
        xn1 = rms_fwd(h0, row(w['ffn1_norm'][l]), "rms_fwd", norm_tile)
        nxt = l + 1 < depth
        take = lambda *names: [local_of(n, l + 1) for n in names] if nxt else []
        fw = full[l]
        got = {}
        names = ('ffn1_w_gate', 'w_out')
        h1, bufs = ffn_fwd(h0, xn1, fw['ffn1_w_gate'], fw['ffn1_w_up'], fw['ffn1_w_down'], "ffn_fwd", ag=take(*names))
        got.update(zip(names, bufs))
        xn2 = rms_fwd(h1, row(w['mix_norm'][l]), "rms_fwd", norm_tile)
        proj = matmul_cols(xn2, w_in_p[l], "mix_in_fwd")
        lru_s, gdn_s, ssd_s, s5_s = mixer_specs(proj)
        names = ('s5_w_glu', 'lru_conv_w', 'gdn_conv_w', 'ssd_conv_w')
        ya, lru_c, bufs = mixer_fwd(lru_fn, "lru_fwd", seqs=lru_s, params=lru_p, out=out_w, carry=lru_carry,
                                    ag=take(*names), **mk)
        got.update(zip(names, bufs))
        names = ('w_in',)
        yb, gdn_c, bufs = mixer_fwd(gdn_fn, "gdn_fwd", seqs=gdn_s, params=gdn_p, out=out_w, carry=gdn_carry,
                                    ag=take(*names), **mk)
        got.update(zip(names, bufs))
        names = ('ffn1_w_up',)
        yc, ssd_c, bufs = mixer_fwd(ssd_fn, "ssd_fwd", seqs=ssd_s, params=ssd_p, out=out_w, carry=ssd_carry,
                                    ag=take(*names), **mk)
        got.update(zip(names, bufs))
        names = ('ffn1_w_down', 'ffn2_w_gate')
        y1, s5_c, bufs = mixer_fwd(s5_fn, "s5_fwd", seqs=s5_s, params=s5_p, out=(wg, S5_LANES, lambda g: g, F32),
                                   carry=s5_carry, n_groups=s5_q, ag=take(*names), **mk)
        got.update(zip(names, bufs))
        yd, _, _ = mixer_fwd(post_fn, "s5_post_fwd", seqs=[(y1, wg, 0)], params=post_p, out=out_w, carry=[], **rk)
        h2, _ = mix_out_fwd(h1, [ya, yb, yc, yd], fw['w_out'], "mix_out_fwd", row_tile)
        xn3 = rms_fwd(h2, row(w['ffn2_norm'][l]), "rms_fwd", norm_tile)
        names = ('ffn2_w_up', 'ffn2_w_down')
        h3, bufs = ffn_fwd(h2, xn3, fw['ffn2_w_gate'], fw['ffn2_w_up'], fw['ffn2_w_down'], "ffn_fwd", ag=take(*names))
        got.update(zip(names, bufs))
        if nxt:
            full[l + 1] = unshard_layer(dict(zip(LAYER_NAMES, ag_second_level([got[n] for n in LAYER_NAMES],
                                                                                "gather_pass_on"))))
            w_in_p[l + 1] = pack_cols(full[l + 1]['w_in'])
        saved.append((h0, xn1, h1, xn2, proj, (ya, yb, yc, yd), y1, (lru_c, gdn_c, ssd_c, s5_c), h2, xn3))
        h = h3

    loss_part, dh, d_final = loss_and_grad(h, target_p, row(w['final_norm']), "loss", norm_tile, pad + n_meta)
    loss = lax.psum(loss_part, ("x", "y", "c"))

    def shards_of(a, axis):
        sh = a.shape
        return jnp.moveaxis(a.reshape(sh[:axis] + (N_DEV, sh[axis] // N_DEV) + sh[axis + 1:]), axis, 0)

    def to_send(n, g):
        return shards_of(g, SHARD_AXIS[n] - 1).astype(BF16)

    received = {n: [None] * depth for n in SHARD_NAMES if n != 'meta_tokens'}
    pending = []

    def hosted(names_layers):
        keys = [k for k in pending if (k[0], k[1]) in names_layers]
        for k in keys:
            pending.remove(k)
        return [(k[0], k[1]) for k in keys], [k[2] for k in keys]

    def store(keys, arrays):
        for (n, l), a in zip(keys, arrays, strict=True):
            received[n][l] = a

    gw = {n: [None] * depth for n in W_NAMES if n not in ('meta_tokens', 'final_norm')}
    for l in reversed(range(depth)):
        lru_p, gdn_p, ssd_p, s5_p, post_p = layer_params(l)
        h0, xn1, h1, xn2, proj, ys, y1, (lru_c, gdn_c, ssd_c, s5_c), h2, xn3 = saved[l]
        dxn, gw['ffn2_w_gate'][l], gw['ffn2_w_up'][l], gw['ffn2_w_down'][l] = ffn_bwd(
            xn3, dh, full[l]['ffn2_w_gate'], full[l]['ffn2_w_up'], full[l]['ffn2_w_down'], "ffn_bwd")
        pending += [(n, l, to_send(n, gw[n][l])) for n in ('ffn2_w_gate', 'ffn2_w_up', 'ffn2_w_down')]
        dh, dg = rms_bwd_add(h2, dxn, dh, row(w['ffn2_norm'][l]), "rms_bwd", norm_tile)
        gw['ffn2_norm'][l] = dg[0]

        dys, (d_wout,) = mixer_bwd(mix_out_delta, "mix_out_bwd", seqs=[(y, wg, 0) for y in ys],
                                   params=[full[l]['w_out']], dout=(dh, d, 0), saved=[], carry=[], **rk)
        gw['w_out'][l] = d_wout[0]
        pending.append(('w_out', l, to_send('w_out', gw['w_out'][l])))
        lru_s, gdn_s, ssd_s, s5_s = mixer_specs(proj)
        (dy1,), d_post = mixer_bwd(post_fn, "s5_post_bwd", seqs=[(y1, wg, 0)], params=post_p, dout=(dys[3], wg, 0),
                                   saved=[], carry=[], **rk)
        gw['s5_w_glu'][l], gw['s5_norm'][l] = d_post[0][0], d_post[1][0, 0]
        pending.append(('s5_w_glu', l, to_send('s5_w_glu', gw['s5_w_glu'][l])))
        keys, arrs = hosted({('ffn1_w_gate', l + 1), ('ffn1_w_up', l + 1), ('ffn1_w_down', l + 1)})
        (d_du,), d_s5, *got = mixer_bwd(s5_fn, "s5_bwd", seqs=s5_s, params=s5_p, dout=(dy1, S5_LANES, lambda g: g),
                                        saved=s5_c, carry=s5_carry, n_groups=s5_q, a2a=arrs, **mk)
        store(keys, got[0] if got else [])
        for n, g in zip(('s5_a_re', 's5_a_im', 's5_log_dt', 's5_b_re', 's5_b_im', 's5_c_re', 's5_c_im', 's5_d'),
                        s5_grads_extract(d_s5, s5_g, s5_n, S5_CH)):
            gw[n][l] = g
        keys, arrs = hosted({('w_in', l + 1), ('w_out', l), ('s5_w_glu', l)})
        (d_cz, d_cxbc, d_sc), d_ssd, *got = mixer_bwd(ssd_fn, "ssd_bwd", seqs=ssd_s, params=ssd_p,
                                                      dout=(dys[2], wg, 0), saved=ssd_c, carry=ssd_carry, a2a=arrs, **mk)
        store(keys, got[0] if got else [])
        for n, g in zip(('ssd_conv_w', 'ssd_conv_b', 'ssd_a_log', 'ssd_dt_bias', 'ssd_d', 'ssd_norm'), d_ssd):
            gw[n][l] = g[0] if n == 'ssd_conv_w' else g[0, 0]
        keys, arrs = hosted({('ffn2_w_gate', l), ('ffn2_w_up', l), ('ffn2_w_down', l)})
        (d_bq, d_bk, d_bv, d_bz, d_sb), d_gdn, *got = mixer_bwd(gdn_fn, "gdn_bwd", seqs=gdn_s, params=gdn_p,
                                                                dout=(dys[1], wg, 0), saved=gdn_c, carry=gdn_carry,
                                                                a2a=arrs, **mk)
        store(keys, got[0] if got else [])
        gw['gdn_conv_w'][l] = jnp.concatenate([d_gdn[0][0], d_gdn[1][0], d_gdn[2][0]], axis=1)
        gw['gdn_a_log'][l], gw['gdn_dt_bias'][l], gw['gdn_norm'][l] = d_gdn[3][0, 0], d_gdn[4][0, 0], d_gdn[5][0, 0]
        (d_ax, d_ag), d_lru = mixer_bwd(lru_fn, "lru_bwd", seqs=lru_s, params=lru_p, dout=(dys[0], wg, 0),
                                        saved=lru_c, carry=lru_carry, **mk)
        gw['lru_conv_w'][l], gw['lru_conv_b'][l] = d_lru[0][0], d_lru[1][0, 0]
        gw['lru_w_a'][l], gw['lru_b_a'][l] = blockdiag_extract(d_lru[2][0], lru_h), d_lru[3][0, 0]
        gw['lru_w_i'][l], gw['lru_b_i'][l] = blockdiag_extract(d_lru[4][0], lru_h), d_lru[5][0, 0]
        gw['lru_lambda'][l], gw['lru_norm'][l] = d_lru[6][0, 0], d_lru[7][0, 0]

        dproj = jnp.concatenate([d_ax, d_ag, d_bq, d_bk, d_bv, d_bz, d_cxbc, d_cz, d_du, d_sb, d_sc], axis=1)
        dxn, d_win_p = matmul_cols_bwd(xn2, dproj, w_in_p[l], "mix_in_bwd")
        gw['w_in'][l] = unpack_cols(d_win_p)
        pending.append(('w_in', l, to_send('w_in', gw['w_in'][l])))
        dh, dg = rms_bwd_add(h1, dxn, dh, row(w['mix_norm'][l]), "rms_bwd", norm_tile)
        gw['mix_norm'][l] = dg[0]

        dxn, gw['ffn1_w_gate'][l], gw['ffn1_w_up'][l], gw['ffn1_w_down'][l] = ffn_bwd(
            xn1, dh, full[l]['ffn1_w_gate'], full[l]['ffn1_w_up'], full[l]['ffn1_w_down'], "ffn_bwd")
        dh, dg = rms_bwd_add(h0, dxn, dh, row(w['ffn1_norm'][l]), "rms_bwd", norm_tile)
        gw['ffn1_norm'][l] = dg[0]
        pending += [(n, l, to_send(n, gw[n][l])) for n in ('ffn1_w_gate', 'ffn1_w_up', 'ffn1_w_down')]

    grad_x = dh[pad + n_meta:][None]
    grads = {n: jnp.stack(gw[n], axis=0) for n in REP_NAMES if n != 'final_norm'}
    grads['meta_tokens'] = dh[pad:pad + n_meta]
    grads['final_norm'] = d_final[0]

    for n in ('lru_conv_w', 'gdn_conv_w', 'ssd_conv_w'):
        pending += [(n, l, to_send(n, gw[n][l])) for l in range(depth)]
    last = [k[2] for k in pending] + [shards_of(grads['meta_tokens'], SHARD_AXIS['meta_tokens']).astype(BF16)]
    recv_rep, *got, recv_meta = all_gather([pack_flat([grads[n] for n in REP_NAMES], BF16)], "last_exchange",
                                           with_a2a=last)
    store([(k[0], k[1]) for k in pending], got)
    out = {}
    for n in SHARD_NAMES:
        c = w[n].shape[-1]
        recv = [recv_meta] if n == 'meta_tokens' else received[n]
        res = adamw_reduce([a.reshape(N_DEV, -1, c) for a in recv], w[n].reshape(-1, c), m_in[n].reshape(-1, c),
                           v_in[n].reshape(-1, c), "adamw_" + n)
        for kind, buf in zip(('grad', 'delta', 'new_m', 'new_v'), res):
            out[kind, n] = buf.reshape(w[n].shape)
    res = adamw_reduce([recv_rep], pack_flat([w[n] for n in REP_NAMES], F32), pack_flat([m_in[n] for n in REP_NAMES], F32),
                       pack_flat([v_in[n] for n in REP_NAMES], F32), "adamw_replicated")
    shapes = [w[n].shape for n in REP_NAMES]
    for kind, buf in zip(('grad', 'delta', 'new_m', 'new_v'), res):
        for n, a in zip(REP_NAMES, unpack_flat(buf, shapes)):
            out[kind, n] = a
    return (loss, grad_x) + tuple(out[k, n] for k in ('grad', 'delta', 'new_m', 'new_v') for n in W_NAMES)
```

```python
import functools
import math

import jax
import jax.numpy as jnp
from jax import lax
from jax.experimental import pallas as pl
from jax.experimental.pallas import tpu as pltpu

F32 = jnp.float32
BF16 = jnp.bfloat16

EPS = 1e-6
CHUNK = 64
CONV_K = 4
CONV_TAIL = 8
LRU_C = 8.0
LANE = 128
SUBLANE = 8
N_DEV = 8
NEG_BIG = -1e30

ADAM_LR = 0.001
ADAM_B1 = 0.9
ADAM_B2 = 0.999
ADAM_EPS = 1e-08
ADAM_WD = 0.01
ADAM_STEP = 10

VMEM_LIMIT = 56 * 1024 * 1024


def _dg(a, b, dims):
    return lax.dot_general(a.astype(BF16), b.astype(BF16), (dims, ((), ())), preferred_element_type=F32)


@jax.custom_vjp
def bdot(a, b):
    return _dg(a, b, ((1,), (0,)))


@jax.custom_vjp
def bdot_nt(a, b):
    return _dg(a, b, ((1,), (1,)))


@jax.custom_vjp
def bdot_tn(a, b):
    return _dg(a, b, ((0,), (0,)))


bdot.defvjp(lambda a, b: (bdot(a, b), (a, b)),
            lambda r, g: (bdot_nt(g, r[1]).astype(r[0].dtype), bdot_tn(r[0], g).astype(r[1].dtype)))
bdot_nt.defvjp(lambda a, b: (bdot_nt(a, b), (a, b)),
               lambda r, g: (bdot(g, r[1]).astype(r[0].dtype), bdot_tn(g, r[0]).astype(r[1].dtype)))
bdot_tn.defvjp(lambda a, b: (bdot_tn(a, b), (a, b)),
               lambda r, g: (bdot_nt(r[1], g).astype(r[0].dtype), bdot(r[0], g).astype(r[1].dtype)))


def _split_bf16(a):
    hi = a.astype(BF16)
    return hi, (a - hi.astype(F32)).astype(BF16)


def _dot3(a, b, dims):
    (ah, al), (bh, bl) = _split_bf16(a), _split_bf16(b)
    d = lambda x, y: lax.dot_general(x, y, (dims, ((), ())), preferred_element_type=F32)
    return d(ah, bh) + (d(ah, bl) + d(al, bh))


@jax.custom_vjp
def hdot(a, b):
    return _dot3(a, b, ((1,), (0,)))


@jax.custom_vjp
def hdot_tn(a, b):
    return _dot3(a, b, ((0,), (0,)))


hdot.defvjp(lambda a, b: (hdot(a, b), (a, b)),
            lambda r, g: (_dot3(g, r[1], ((1,), (1,))), _dot3(r[0], g, ((0,), (0,)))))
hdot_tn.defvjp(lambda a, b: (hdot_tn(a, b), (a, b)),
               lambda r, g: (_dot3(r[1], g, ((1,), (1,))), _dot3(r[0], g, ((1,), (0,)))))


def rms_norm(x, g):
    return x * lax.rsqrt(jnp.mean(x * x, axis=-1, keepdims=True) + EPS) * g


def row_mask(row0, rows, pad):
    r = row0 + lax.broadcasted_iota(jnp.int32, (rows, 1), 0)
    return (r >= pad).astype(F32)


def conv4(tail, u, w):
    rows = u.shape[0]
    xe = jnp.concatenate([tail, u], axis=0)
    y = w[0:1] * xe[CONV_TAIL - 3:CONV_TAIL - 3 + rows]
    for k in range(1, CONV_K):
        y = y + w[k:k + 1] * xe[CONV_TAIL - 3 + k:CONV_TAIL - 3 + k + rows]
    return y


def shift_rows(x, s, fill):
    rows = x.shape[0]
    return jnp.concatenate([jnp.full((s, x.shape[1]), fill, x.dtype), x[:rows - s]], axis=0)


def lin_scan(a, b):
    rows = a.shape[0]
    s = 1
    while s < rows:
        b = a * shift_rows(b, s, 0.0) + b
        a = a * shift_rows(a, s, 1.0)
        s *= 2
    return b


def cscan_const(ar, ai, br, bi):
    rows = br.shape[0]
    s = 1
    while s < rows:
        brs, bis = shift_rows(br, s, 0.0), shift_rows(bi, s, 0.0)
        br, bi = br + ar * brs - ai * bis, bi + ar * bis + ai * brs
        ar, ai = ar * ar - ai * ai, 2.0 * ar * ai
        s *= 2
    return br, bi


def neg_expm1(z):
    t = jnp.tanh(0.5 * z)
    return -2.0 * t / (1.0 - t)


def tri_masks(n):
    r = lax.broadcasted_iota(jnp.int32, (n, n), 0)
    c = lax.broadcasted_iota(jnp.int32, (n, n), 1)
    return r >= c, r > c, (r == c).astype(F32)


def lru_chunk(pad, row0, params, seqs, carry):
    conv_w, conv_b, w_a, b_a, w_i, b_i, lam, norm_g = params
    u_x, u_gate = seqs
    tail, h0 = carry
    rows = u_x.shape[0]
    m = row_mask(row0, rows, pad)
    xc = conv4(tail, u_x, conv_w) + conv_b
    r = jax.nn.sigmoid(bdot(xc, w_a) + b_a)
    ig = jax.nn.sigmoid(bdot(xc, w_i) + b_i)
    log_a = -LRU_C * r * jax.nn.softplus(-lam)
    a = jnp.exp(log_a)
    b = jnp.sqrt(neg_expm1(2.0 * log_a)) * (ig * xc) * m
    first = (lax.broadcasted_iota(jnp.int32, (rows, 1), 0) == 0).astype(F32)
    b = b + first * (a * h0)
    h = lin_scan(a, b)
    y = jax.nn.gelu(u_gate) * h
    out = rms_norm(y, norm_g) * m
    return (out,), (u_x[rows - CONV_TAIL:], h[rows - 1:])


def gdn_multi(pad, sub, row0, params, seqs, carry):
    wq, wk, wv, a_log, dt_bias, norm_g = params
    u_q, u_k, u_v, u_z, small = seqs
    tq, tk, tv, state = carry
    rows = u_q.shape[0]
    hd = norm_g.shape[1]
    nh = u_q.shape[1] // hd
    nc = rows // sub
    m = row_mask(row0, rows, pad)
    incl, strict, eye = tri_masks(sub)
    tril = incl.astype(F32)
    triu = (lax.broadcasted_iota(jnp.int32, (sub, sub), 0) <= lax.broadcasted_iota(jnp.int32, (sub, sub), 1)).astype(F32)
    qc = jax.nn.silu(conv4(tq, u_q, wq))
    kc = jax.nn.silu(conv4(tk, u_k, wk))
    vc = jax.nn.silu(conv4(tv, u_v, wv))
    beta = jax.nn.sigmoid(small[:, :nh]) * m
    g = -jnp.exp(a_log) * jax.nn.softplus(small[:, nh:2 * nh] + dt_bias) * m
    gate = jax.nn.silu(u_z)
    heads = [slice(h * hd, (h + 1) * hd) for h in range(nh)]
    q_h = [qc[:, sl] for sl in heads]
    k_h = [kc[:, sl] for sl in heads]
    q_h = [q * lax.rsqrt(jnp.sum(q * q, axis=-1, keepdims=True) + EPS) * (hd ** -0.5) * m for q in q_h]
    k_h = [k * lax.rsqrt(jnp.sum(k * k, axis=-1, keepdims=True) + EPS) * m for k in k_h]
    v_h = [vc[:, sl] * m for sl in heads]
    pairs = [(c, h) for c in range(nc) for h in range(nh)]
    cs = lambda x, c: x[c * sub:(c + 1) * sub]
    q = {(c, h): cs(q_h[h], c) for c, h in pairs}
    k = {(c, h): cs(k_h[h], c) for c, h in pairs}
    v = {(c, h): cs(v_h[h], c) for c, h in pairs}
    bt = {(c, h): cs(beta, c)[:, h:h + 1] for c, h in pairs}
    gcs = [hdot(tril, cs(g, c)) for c in range(nc)]
    gts = [hdot_tn(cs(g, c), triu) for c in range(nc)]
    gc = {(c, h): gcs[c][:, h:h + 1] for c, h in pairs}
    decay = {(c, h): jnp.exp(jnp.where(incl, gc[c, h] - gts[c][h:h + 1], NEG_BIG)) for c, h in pairs}
    kb = {p: k[p] * bt[p] for p in pairs}
    kk = {p: bdot_nt(kb[p], k[p]) for p in pairs}
    lmat = {p: jnp.where(strict, kk[p] * decay[p], 0.0) for p in pairs}
    pm = {p: eye - lmat[p] for p in pairs}
    mm = {p: hdot(lmat[p], lmat[p]) for p in pairs}
    s = 2
    while s < sub:
        pm = {p: pm[p] + hdot(pm[p], mm[p]) for p in pairs}
        s *= 2
        if s < sub:
            mm = {p: hdot(mm[p], mm[p]) for p in pairs}
    eg = {p: jnp.exp(gc[p]) for p in pairs}
    u = {p: hdot(pm[p], v[p] * bt[p]) for p in pairs}
    w = {p: hdot(pm[p], kb[p] * eg[p]) for p in pairs}
    attn = {p: bdot_nt(q[p], k[p]) * decay[p] for p in pairs}
    qd = {p: q[p] * eg[p] for p in pairs}
    g_last = {p: gc[p][sub - 1:] for p in pairs}
    kd = {p: k[p] * jnp.exp(g_last[p] - gc[p]) for p in pairs}
    last = {p: jnp.exp(g_last[p]) for p in pairs}
    s_h = [state[sl] for sl in heads]
    o = {}
    for c in range(nc):
        ws = [bdot(w[c, h], s_h[h]) for h in range(nh)]
        qs = [bdot(qd[c, h], s_h[h]) for h in range(nh)]
        v_new = [u[c, h] - ws[h] for h in range(nh)]
        av = [bdot(attn[c, h], v_new[h]) for h in range(nh)]
        kv = [bdot_tn(kd[c, h], v_new[h]) for h in range(nh)]
        for h in range(nh):
            o[c, h] = qs[h] + av[h]
        s_h = [s_h[h] * last[c, h] + kv[h] for h in range(nh)]
    out = jnp.concatenate([jnp.concatenate([rms_norm(o[c, h], norm_g) for h in range(nh)], axis=1)
                           for c in range(nc)], axis=0) * gate * m
    t0 = rows - CONV_TAIL
    return (out,), (u_q[t0:], u_k[t0:], u_v[t0:], jnp.concatenate(s_h, axis=0))


def ssd_multi(pad, dt_lane0, n_groups, sub, row0, params, seqs, carry):
    conv_w, conv_b, a_log, dt_bias, d_skip, norm_g = params
    u_z, u_xbc, small = seqs
    tail, state = carry
    rows = u_z.shape[0]
    width = u_z.shape[1]
    nh = a_log.shape[1]
    hd = width // nh
    ns = (u_xbc.shape[1] - width) // (2 * n_groups)
    hpg = nh // n_groups
    nc = rows // sub
    m = row_mask(row0, rows, pad)
    incl, _, _ = tri_masks(sub)
    tril = incl.astype(F32)
    triu = (lax.broadcasted_iota(jnp.int32, (sub, sub), 0) <= lax.broadcasted_iota(jnp.int32, (sub, sub), 1)).astype(F32)
    xbc = jax.nn.silu(conv4(tail, u_xbc, conv_w) + conv_b)
    xs = xbc[:, :width]
    dt = jax.nn.softplus(small[:, dt_lane0:dt_lane0 + nh] + dt_bias)
    a_all = dt * (-jnp.exp(a_log)) * m
    cs = lambda x, c: x[c * sub:(c + 1) * sub]
    heads = [slice(h * hd, (h + 1) * hd) for h in range(nh)]
    pairs = [(c, h) for c in range(nc) for h in range(nh)]
    grp = lambda h: h // hpg
    bm = {(c, g): cs(xbc[:, width + g * ns: width + (g + 1) * ns] * m, c) for c in range(nc) for g in range(n_groups)}
    cm = {(c, g): cs(xbc[:, width + (n_groups + g) * ns: width + (n_groups + g + 1) * ns] * m, c)
          for c in range(nc) for g in range(n_groups)}
    xh = {(c, h): cs(xs[:, heads[h]], c) for c, h in pairs}
    xdt = {(c, h): xh[c, h] * cs(dt[:, h:h + 1] * m, c) for c, h in pairs}
    acums = [hdot(tril, cs(a_all, c)) for c in range(nc)]
    acts = [hdot_tn(cs(a_all, c), triu) for c in range(nc)]
    acum = {(c, h): acums[c][:, h:h + 1] for c, h in pairs}
    a_last = {p: acum[p][sub - 1:] for p in pairs}
    lmat = {(c, h): jnp.exp(jnp.where(incl, acum[c, h] - acts[c][h:h + 1], NEG_BIG)) for c, h in pairs}
    cb = {cg: bdot_nt(cm[cg], bm[cg]) for cg in bm}
    y_diag = {(c, h): bdot(cb[c, grp(h)] * lmat[c, h], xdt[c, h]) for c, h in pairs}
    st = {(c, h): bdot_tn(xdt[c, h] * jnp.exp(a_last[c, h] - acum[c, h]), bm[c, grp(h)]) for c, h in pairs}
    e_in = {p: jnp.exp(acum[p]) for p in pairs}
    e_out = {p: jnp.exp(a_last[p]) for p in pairs}
    s_h = [state[sl] for sl in heads]
    y = {}
    for c in range(nc):
        off = [bdot_nt(cm[c, grp(h)], s_h[h]) for h in range(nh)]
        for h in range(nh):
            y[c, h] = y_diag[c, h] + off[h] * e_in[c, h] + d_skip[:, h:h + 1] * xh[c, h]
        s_h = [s_h[h] * e_out[c, h] + st[c, h] for h in range(nh)]
    yy = jnp.concatenate([jnp.concatenate([y[c, h] for h in range(nh)], axis=1) for c in range(nc)], axis=0)
    yy = yy * jax.nn.silu(u_z)
    gw = width // n_groups
    outs = [rms_norm(yy[:, g * gw:(g + 1) * gw], norm_g[:, g * gw:(g + 1) * gw]) for g in range(n_groups)]
    out = jnp.concatenate(outs, axis=1) * m
    return (out,), (u_xbc[rows - CONV_TAIL:], jnp.concatenate(s_h, axis=0))


def s5_chunk(pad, row0, params, seqs, carry):
    a_re, a_im, log_dt, b_re, b_im, c_re, c_im, d_skip = params
    (u,) = seqs
    s_re0, s_im0 = carry
    rows = u.shape[0]
    n_state = a_re.shape[1]
    n_grp = log_dt.shape[1]
    per = n_state // n_grp
    expand = (lax.broadcasted_iota(jnp.int32, (n_grp, n_state), 1) // per
              == lax.broadcasted_iota(jnp.int32, (n_grp, n_state), 0)).astype(F32)
    dt = jnp.exp(hdot(log_dt, expand))
    lam_re = jnp.minimum(a_re, -1e-4)
    lam_im = a_im
    mag = jnp.exp(dt * lam_re)
    ab_re = mag * jnp.cos(dt * lam_im)
    ab_im = mag * jnp.sin(dt * lam_im)
    den = lam_re * lam_re + lam_im * lam_im
    f_re = ((ab_re - 1.0) * lam_re + ab_im * lam_im) / den
    f_im = (ab_im * lam_re - (ab_re - 1.0) * lam_im) / den
    bb_re = f_re * b_re - f_im * b_im
    bb_im = f_re * b_im + f_im * b_re
    bu_re = bdot(u, bb_re)
    bu_im = bdot(u, bb_im)
    first = (lax.broadcasted_iota(jnp.int32, (rows, 1), 0) == 0).astype(F32)
    bu_re = bu_re + first * (ab_re * s_re0 - ab_im * s_im0)
    bu_im = bu_im + first * (ab_re * s_im0 + ab_im * s_re0)
    s_re, s_im = cscan_const(ab_re, ab_im, bu_re, bu_im)
    y = bdot(s_re, c_re) - bdot(s_im, c_im) + d_skip * u
    return (y,), (s_re[rows - 1:], s_im[rows - 1:])


def s5_post(pad, row0, params, seqs, carry):
    w_glu, norm_g = params
    (y,) = seqs
    y = jax.nn.gelu(y)
    y = y * jax.nn.sigmoid(bdot(y, w_glu))
    return (rms_norm(y, norm_g),), ()


def mix_out_delta(row0, params, seqs, carry):
    (w_out,) = params
    wd = seqs[0].shape[1]
    acc = bdot(seqs[0], w_out[0:wd])
    for k in range(1, len(seqs)):
        acc = acc + bdot(seqs[k], w_out[k * wd:(k + 1) * wd])
    return (acc,), ()


def blockdiag_expand(w):
    nh, a, b = w.shape
    eye = jnp.eye(nh, dtype=w.dtype)
    return (w[:, :, None, :] * eye[:, None, :, None]).reshape(nh * a, nh * b)


def blockdiag_extract(m, nh):
    a, b = m.shape[0] // nh, m.shape[1] // nh
    on_diag = jnp.eye(nh, dtype=bool)[:, None, :, None]
    return jnp.sum(jnp.where(on_diag, m.reshape(nh, a, nh, b), 0.0), axis=2)


S5_LANES = LANE


def s5_params_expand(a_re, a_im, log_dt, b_re, b_im, c_re, c_im, d_skip):
    n_grp, n_state = a_re.shape
    ch = b_re.shape[-1]
    gpl = S5_LANES // ch
    nq = n_grp // gpl
    eye = jnp.eye(gpl, dtype=F32)[None, :, None, :, None]

    def bexp(b):
        bt = jnp.swapaxes(b, 1, 2).reshape(nq, gpl, b.shape[2], 1, b.shape[1])
        return (bt * eye).reshape(nq, gpl * b.shape[2], gpl * b.shape[1])

    return (a_re.reshape(nq, 1, gpl * n_state), a_im.reshape(nq, 1, gpl * n_state), log_dt.reshape(nq, 1, gpl),
            bexp(b_re), bexp(b_im), bexp(c_re), bexp(c_im), d_skip.reshape(nq, 1, S5_LANES))


def s5_grads_extract(grads, n_grp, n_state, ch):
    da_re, da_im, dlog_dt, db_re, db_im, dc_re, dc_im, dd = grads
    gpl = S5_LANES // ch
    nq = n_grp // gpl
    on_diag = jnp.eye(gpl, dtype=bool)[None, :, None, :, None]

    def bext(b):
        r, c = b.shape[1] // gpl, b.shape[2] // gpl
        d = jnp.sum(jnp.where(on_diag, b.reshape(nq, gpl, r, gpl, c), 0.0), axis=3)
        return jnp.swapaxes(d.reshape(n_grp, r, c), 1, 2)

    return (da_re.reshape(n_grp, n_state), da_im.reshape(n_grp, n_state), dlog_dt.reshape(n_grp),
            bext(db_re), bext(db_im), bext(dc_re), bext(dc_im), dd.reshape(n_grp * ch))


def _cparams(**kw):
    return pltpu.CompilerParams(vmem_limit_bytes=VMEM_LIMIT, **kw)


def tiled_call(body_fn, name, *, n_steps, rows, n_groups=1, reverse=False,
               seq_in=(), whole_in=(), step_in=(), seq_out=(), acc_out=(), step_out=(), carry=(), a2a=(), ag=()):
    def step_of(i):
        return (n_steps - 1 - i) if reverse else i

    def col_of(col, g):
        return col(g) if callable(col) else col

    in_specs, operands = [], []
    for arr, width, col in seq_in:
        in_specs.append(pl.BlockSpec((rows, width), lambda g, i, col=col: (step_of(i), col_of(col, g))))
        operands.append(arr)
    for arr in whole_in:
        if arr.ndim == 2:
            in_specs.append(pl.BlockSpec(arr.shape, lambda g, i: (0, 0)))
        else:
            in_specs.append(pl.BlockSpec((None,) + arr.shape[1:], lambda g, i: (g, 0, 0)))
        operands.append(arr)
    for arr in step_in:
        in_specs.append(pl.BlockSpec((None, None) + arr.shape[2:], lambda g, i: (g, step_of(i), 0, 0)))
        operands.append(arr)
    out_shape, out_specs = [], []
    for total, width, col, dt in seq_out:
        out_shape.append(jax.ShapeDtypeStruct((n_steps * rows, total), dt))
        out_specs.append(pl.BlockSpec((rows, width), lambda g, i, col=col: (step_of(i), col_of(col, g))))
    for r, c in acc_out:
        out_shape.append(jax.ShapeDtypeStruct((n_groups, r, c), F32))
        out_specs.append(pl.BlockSpec((None, r, c), lambda g, i: (g, 0, 0)))
    for r, c in step_out:
        out_shape.append(jax.ShapeDtypeStruct((n_groups, n_steps, r, c), F32))
        out_specs.append(pl.BlockSpec((None, None, r, c), lambda g, i: (g, step_of(i), 0, 0)))
    n_seq, n_whole, n_step = len(seq_in), len(whole_in), len(step_in)
    n_so, n_ao, n_sto = len(seq_out), len(acc_out), len(step_out)
    assert not (a2a and ag)
    hosted = list(a2a) + list(ag)
    n_x = len(hosted)
    n_sem = AG_FIRST_COPIES if ag else N_DEV - 1
    hbm = pl.BlockSpec(memory_space=pl.ANY)
    in_specs += [hbm] * n_x
    operands += hosted
    out_specs += [hbm] * n_x
    out_shape += [jax.ShapeDtypeStruct((N_DEV,) + x.shape if ag else x.shape, x.dtype) for x in hosted]
    scratch = [pltpu.VMEM((r, c), F32) for r, c in carry]
    if n_x:
        scratch += [pltpu.SemaphoreType.DMA((n_x, n_sem)), pltpu.SemaphoreType.DMA((n_x, n_sem)),
                    pltpu.SemaphoreType.DMA((n_x,))]

    def body(*refs):
        pos = 0
        seq_refs = refs[pos:pos + n_seq]
        pos += n_seq
        whole_refs = refs[pos:pos + n_whole]
        pos += n_whole
        step_refs = refs[pos:pos + n_step]
        pos += n_step + n_x
        so_refs = refs[pos:pos + n_so]
        pos += n_so
        ao_refs = refs[pos:pos + n_ao]
        pos += n_ao
        sto_refs = refs[pos:pos + n_sto]
        pos += n_sto
        xo_refs = refs[pos:pos + n_x]
        pos += n_x
        carry_refs = refs[pos:pos + len(carry)]
        pos += len(carry)
        x_refs = refs[n_seq + n_whole + n_step:n_seq + n_whole + n_step + n_x]
        g, i = pl.program_id(0), pl.program_id(1)
        if n_x:
            locals_, sends, recvs = (ag_first_copies if ag else a2a_copies)(x_refs, xo_refs, *refs[pos:])

            @pl.when((g == 0) & (i == 0))
            def _():
                for cp in locals_ + sends:
                    cp.start()

        @pl.when(i == 0)
        def _():
            for r in carry_refs:
                r[...] = jnp.zeros(r.shape, r.dtype)
            for r in ao_refs:
                r[...] = jnp.zeros(r.shape, r.dtype)

        row0 = step_of(i) * rows
        seq_o, acc_o, step_o, new_c = body_fn(row0, [r[...] for r in whole_refs], [r[...] for r in seq_refs],
                                              [r[...] for r in step_refs], [r[...] for r in carry_refs])
        for r, val in zip(so_refs, seq_o, strict=True):
            r[...] = val.astype(r.dtype)
        for r, val in zip(ao_refs, acc_o, strict=True):
            r[...] += val
        for r, val in zip(sto_refs, step_o, strict=True):
            r[...] = val
        for r, val in zip(carry_refs, new_c, strict=True):
            r[...] = val
        if n_x:
            @pl.when((g == n_groups - 1) & (i == n_steps - 1))
            def _():
                for cp in recvs:
                    cp.wait_recv()
                for cp in sends:
                    cp.wait_send()
                for cp in locals_:
                    cp.wait()

    return pl.pallas_call(
        body, name=name, grid=(n_groups, n_steps), in_specs=in_specs, out_specs=out_specs, out_shape=out_shape,
        scratch_shapes=scratch,
        compiler_params=_cparams(dimension_semantics=("arbitrary", "arbitrary")),
    )(*operands)


def mixer_fwd(fn, name, *, n_steps, rows, seqs, params, out, carry, n_groups=1, ag=()):
    def body(row0, whole, seq_vals, steps, carry_vals):
        outs, new_c = fn(row0, tuple(whole), tuple(seq_vals), tuple(carry_vals))
        return list(outs), [], list(carry_vals), list(new_c)

    res = tiled_call(body, name, n_steps=n_steps, rows=rows, n_groups=n_groups, seq_in=seqs, whole_in=params,
                     seq_out=[out], step_out=carry, carry=carry, ag=ag)
    return res[0], list(res[1:1 + len(carry)]), list(res[1 + len(carry):])


def mixer_bwd(fn, name, *, n_steps, rows, seqs, params, dout, saved, carry, n_groups=1, a2a=()):
    n_seq = len(seqs)

    def body(row0, whole, seq_vals, steps, dcarry):
        params_f = tuple(p.astype(F32) for p in whole)
        _, vjp = jax.vjp(lambda p, s, c: fn(row0, p, s, c), params_f, tuple(seq_vals[:n_seq]), tuple(steps))
        dp, ds, dc = vjp(((seq_vals[n_seq],), tuple(dcarry)))
        return list(ds), list(dp), [], list(dc)

    seq_out = [(n_groups * w, w, (lambda g: g), F32) if callable(c) else (w, w, 0, F32) for a, w, c in seqs]
    acc_out = [p.shape[-2:] for p in params]
    res = tiled_call(body, name, n_steps=n_steps, rows=rows, n_groups=n_groups, reverse=True,
                     seq_in=list(seqs) + [dout], whole_in=params, step_in=saved,
                     seq_out=seq_out, acc_out=acc_out, carry=carry, a2a=a2a)
    n_p = len(params)
    if a2a:
        return list(res[:n_seq]), list(res[n_seq:n_seq + n_p]), list(res[n_seq + n_p:])
    return list(res[:n_seq]), list(res[n_seq:])


def rms_fwd(h, g, name, rows):
    def body(row0, whole, seqs, steps, carry):
        return [rms_norm(seqs[0], whole[0])], [], [], []
    d = h.shape[1]
    return tiled_call(body, name, n_steps=h.shape[0] // rows, rows=rows, seq_in=[(h, d, 0)], whole_in=[g],
                      seq_out=[(d, d, 0, BF16)])[0]


def rms_bwd_add(h, dxn, dh_out, g, name, rows):
    def body(row0, whole, seqs, steps, carry):
        _, vjp = jax.vjp(rms_norm, seqs[0], whole[0])
        dh, dg = vjp(seqs[1])
        return [seqs[2] + dh], [dg], [], []
    d = h.shape[1]
    dh_in, dg = tiled_call(body, name, n_steps=h.shape[0] // rows, rows=rows,
                           seq_in=[(h, d, 0), (dxn, d, 0), (dh_out, d, 0)], whole_in=[g],
                           seq_out=[(d, d, 0, F32)], acc_out=[(1, d)])
    return dh_in, dg[0]


def mix_out_fwd(h, ys, w_out, name, rows, ag=()):
    def body(row0, whole, seqs, steps, carry):
        (delta,), _ = mix_out_delta(row0, (whole[0],), tuple(seqs[1:]), ())
        return [seqs[0] + delta], [], [], []
    d, wd = h.shape[1], ys[0].shape[1]
    res = tiled_call(body, name, n_steps=h.shape[0] // rows, rows=rows,
                     seq_in=[(h, d, 0)] + [(y, wd, 0) for y in ys], whole_in=[w_out], seq_out=[(d, d, 0, F32)], ag=ag)
    return res[0], list(res[1:])


def loss_and_grad(h, target, g, name, rows, first_row):
    def body(row0, whole, seqs, steps, carry):
        hh, tt = seqs
        keep = row_mask(row0, hh.shape[0], first_row)

        def f(hv, gv):
            err = rms_norm(hv, gv) - tt
            return 0.5 * jnp.sum(jnp.mean(err * err, axis=-1, keepdims=True) * keep, axis=0, keepdims=True)

        val, vjp = jax.vjp(f, hh, whole[0])
        dh, dg = vjp(jnp.ones((1, 1), F32))
        return [dh], [jnp.broadcast_to(val, (1, LANE)), dg], [], []
    d = h.shape[1]
    dh, loss, dg = tiled_call(body, name, n_steps=h.shape[0] // rows, rows=rows,
                              seq_in=[(h, d, 0), (target, d, 0)], whole_in=[g],
                              seq_out=[(d, d, 0, F32)], acc_out=[(1, LANE), (1, d)])
    return loss[0, 0, 0], dh, dg[0]


def _pick(n, cands):
    for c in cands:
        if n % c == 0:
            return c
    raise ValueError(f"no tile for {n}")


ROW_TILES = (1056, 704, 352, 192, 96, 64)
COL_TILES = (256, 128)
NT_DIMS = (((1,), (1,)), ((), ()))
TN_DIMS = (((0,), (0,)), ((), ()))


def ffn_fwd(h, xn, wg, wu, wd, name, ag=()):
    t, d = h.shape
    f = wg.shape[1]
    tm = _pick(t, ROW_TILES)
    tn = _pick(f, COL_TILES)
    n_i, n_j = t // tm, f // tn
    n_x = len(ag)

    def body(h_ref, xn_ref, wg_ref, wu_ref, wd_ref, *rest):
        x_refs, o_ref, xo_refs, acc_ref = rest[:n_x], rest[n_x], rest[n_x + 1:2 * n_x + 1], rest[2 * n_x + 1]
        i, j = pl.program_id(0), pl.program_id(1)
        if n_x:
            locals_, sends, recvs = ag_first_copies(x_refs, xo_refs, *rest[2 * n_x + 2:])

            @pl.when((i == 0) & (j == 0))
            def _():
                for cp in locals_ + sends:
                    cp.start()

        @pl.when(j == 0)
        def _():
            acc_ref[...] = jnp.zeros(acc_ref.shape, F32)

        x = xn_ref[...]
        g = jnp.dot(x, wg_ref[...], preferred_element_type=F32)
        u = jnp.dot(x, wu_ref[...], preferred_element_type=F32)
        a = (jax.nn.silu(g) * u).astype(BF16)
        acc_ref[...] += jnp.dot(a, wd_ref[...], preferred_element_type=F32)

        @pl.when(j == n_j - 1)
        def _():
            o_ref[...] = h_ref[...] + 0.5 * acc_ref[...]

        if n_x:
            @pl.when((i == n_i - 1) & (j == n_j - 1))
            def _():
                for cp in recvs:
                    cp.wait_recv()
                for cp in sends:
                    cp.wait_send()
                for cp in locals_:
                    cp.wait()

    hbm = pl.BlockSpec(memory_space=pl.ANY)
    scratch = [pltpu.VMEM((tm, d), F32)]
    if n_x:
        scratch += [pltpu.SemaphoreType.DMA((n_x, AG_FIRST_COPIES)), pltpu.SemaphoreType.DMA((n_x, AG_FIRST_COPIES)),
                    pltpu.SemaphoreType.DMA((n_x,))]
    res = pl.pallas_call(
        body, name=name, grid=(n_i, n_j),
        in_specs=[pl.BlockSpec((tm, d), lambda i, j: (i, 0)), pl.BlockSpec((tm, d), lambda i, j: (i, 0)),
                  pl.BlockSpec((d, tn), lambda i, j: (0, j)), pl.BlockSpec((d, tn), lambda i, j: (0, j)),
                  pl.BlockSpec((tn, d), lambda i, j: (j, 0))] + [hbm] * n_x,
        out_specs=[pl.BlockSpec((tm, d), lambda i, j: (i, 0))] + [hbm] * n_x,
        out_shape=[jax.ShapeDtypeStruct((t, d), F32)] + [jax.ShapeDtypeStruct((N_DEV,) + x.shape, x.dtype) for x in ag],
        scratch_shapes=scratch,
        compiler_params=_cparams(dimension_semantics=("arbitrary", "arbitrary")),
    )(h, xn, wg, wu, wd, *ag)
    return res[0], list(res[1:])


def ffn_bwd(xn, dh, wg, wu, wd, name, a2a=()):
    t, d = dh.shape
    f = wg.shape[1]
    tm = _pick(t, ROW_TILES)
    tn = _pick(f, COL_TILES)
    n_i, n_j = t // tm, f // tn
    n_x = len(a2a)

    def body(xn_ref, dh_ref, wg_ref, wu_ref, wd_ref, *rest):
        x_refs = rest[:n_x]
        dxn_ref, dwg_ref, dwu_ref, dwd_ref = rest[n_x:n_x + 4]
        xo_refs = rest[n_x + 4:2 * n_x + 4]
        ag_ref, au_ref, ad_ref = rest[2 * n_x + 4:2 * n_x + 7]
        j, i = pl.program_id(0), pl.program_id(1)
        if n_x:
            locals_, sends, recvs = a2a_copies(x_refs, xo_refs, *rest[2 * n_x + 7:])

            @pl.when((j == 0) & (i == 0))
            def _():
                for cp in locals_ + sends:
                    cp.start()
        rows = pl.ds(pl.multiple_of(i * tm, 8), tm)
        x = xn_ref[rows, :]
        dhh = (0.5 * dh_ref[...]).astype(BF16)
        wgv, wuv = wg_ref[...], wu_ref[...]
        g = jnp.dot(x, wgv, preferred_element_type=F32)
        u = jnp.dot(x, wuv, preferred_element_type=F32)
        sg = jax.nn.sigmoid(g)
        s = g * sg
        da = lax.dot_general(dhh, wd_ref[...], NT_DIMS, preferred_element_type=F32)
        dwd = lax.dot_general((s * u).astype(BF16), dhh, TN_DIMS, preferred_element_type=F32)
        dg = (da * u * (sg * (1.0 + g * (1.0 - sg)))).astype(BF16)
        du = (da * s).astype(BF16)
        dwg = lax.dot_general(x, dg, TN_DIMS, preferred_element_type=F32)
        dwu = lax.dot_general(x, du, TN_DIMS, preferred_element_type=F32)
        dx = (lax.dot_general(dg, wgv, NT_DIMS, preferred_element_type=F32)
              + lax.dot_general(du, wuv, NT_DIMS, preferred_element_type=F32))

        @pl.when(i == 0)
        def _():
            ag_ref[...] = dwg
            au_ref[...] = dwu
            ad_ref[...] = dwd

        @pl.when(i > 0)
        def _():
            ag_ref[...] += dwg
            au_ref[...] += dwu
            ad_ref[...] += dwd

        @pl.when(i == n_i - 1)
        def _():
            dwg_ref[...] = ag_ref[...].astype(BF16)
            dwu_ref[...] = au_ref[...].astype(BF16)
            dwd_ref[...] = ad_ref[...].astype(BF16)

        @pl.when(j == 0)
        def _():
            dxn_ref[rows, :] = dx

        @pl.when(j > 0)
        def _():
            dxn_ref[rows, :] += dx

        if n_x:
            @pl.when((j == n_j - 1) & (i == n_i - 1))
            def _():
                for cp in recvs:
                    cp.wait_recv()
                for cp in sends:
                    cp.wait_send()
                for cp in locals_:
                    cp.wait()

    hbm = pl.BlockSpec(memory_space=pl.ANY)
    scratch = [pltpu.VMEM((d, tn), F32), pltpu.VMEM((d, tn), F32), pltpu.VMEM((tn, d), F32)]
    if n_x:
        scratch += [pltpu.SemaphoreType.DMA((n_x, N_DEV - 1)), pltpu.SemaphoreType.DMA((n_x, N_DEV - 1)),
                    pltpu.SemaphoreType.DMA((n_x,))]
    res = pl.pallas_call(
        body, name=name, grid=(n_j, n_i),
        in_specs=[pl.BlockSpec((t, d), lambda j, i: (0, 0)), pl.BlockSpec((tm, d), lambda j, i: (i, 0)),
                  pl.BlockSpec((d, tn), lambda j, i: (0, j)), pl.BlockSpec((d, tn), lambda j, i: (0, j)),
                  pl.BlockSpec((tn, d), lambda j, i: (j, 0))] + [hbm] * n_x,
        out_specs=[pl.BlockSpec((t, d), lambda j, i: (0, 0)), pl.BlockSpec((d, tn), lambda j, i: (0, j)),
                   pl.BlockSpec((d, tn), lambda j, i: (0, j)), pl.BlockSpec((tn, d), lambda j, i: (j, 0))] + [hbm] * n_x,
        out_shape=[jax.ShapeDtypeStruct((t, d), F32), jax.ShapeDtypeStruct((d, f), BF16),
                   jax.ShapeDtypeStruct((d, f), BF16), jax.ShapeDtypeStruct((f, d), BF16)]
        + [jax.ShapeDtypeStruct(x.shape, x.dtype) for x in a2a],
        scratch_shapes=scratch,
        compiler_params=_cparams(dimension_semantics=("arbitrary", "arbitrary")),
    )(xn, dh, wg, wu, wd, *a2a)
    return res[:4], list(res[4:])


def matmul_cols(xn, w, name):
    t, d = xn.shape
    n = w.shape[1]
    tn = _pick(n, COL_TILES)

    def body(x_ref, w_ref, o_ref):
        o_ref[...] = jnp.dot(x_ref[...], w_ref[...], preferred_element_type=F32)

    return pl.pallas_call(
        body, name=name, grid=(n // tn,),
        in_specs=[pl.BlockSpec((t, d), lambda j: (0, 0)), pl.BlockSpec((d, tn), lambda j: (0, j))],
        out_specs=pl.BlockSpec((t, tn), lambda j: (0, j)),
        out_shape=jax.ShapeDtypeStruct((t, n), F32),
        compiler_params=_cparams(dimension_semantics=("arbitrary",)),
    )(xn, w)


def matmul_cols_bwd(xn, dy, w, name):
    t, d = xn.shape
    n = w.shape[1]
    tn = _pick(n, COL_TILES)

    def body(x_ref, dy_ref, w_ref, dx_ref, dw_ref):
        j = pl.program_id(0)
        dyv = dy_ref[...].astype(BF16)
        dw_ref[...] = lax.dot_general(x_ref[...], dyv, TN_DIMS, preferred_element_type=F32).astype(BF16)
        dx = lax.dot_general(dyv, w_ref[...], NT_DIMS, preferred_element_type=F32)

        @pl.when(j == 0)
        def _():
            dx_ref[...] = dx

        @pl.when(j > 0)
        def _():
            dx_ref[...] += dx

    return pl.pallas_call(
        body, name=name, grid=(n // tn,),
        in_specs=[pl.BlockSpec((t, d), lambda j: (0, 0)), pl.BlockSpec((t, tn), lambda j: (0, j)),
                  pl.BlockSpec((d, tn), lambda j: (0, j))],
        out_specs=[pl.BlockSpec((t, d), lambda j: (0, 0)), pl.BlockSpec((d, tn), lambda j: (0, j))],
        out_shape=[jax.ShapeDtypeStruct((t, d), F32), jax.ShapeDtypeStruct((d, n), BF16)],
        compiler_params=_cparams(dimension_semantics=("arbitrary",)),
    )(xn, dy, w)


def _peer(mx, my, mc, k):
    px = 1 - mx if (k >> 2) & 1 else mx
    py = 1 - my if (k >> 1) & 1 else my
    pc = 1 - mc if k & 1 else mc
    return (px, py, pc), 4 * px + 2 * py + pc


def a2a_copies(x_refs, o_refs, send_sems, recv_sems, local_sems):
    mx, my, mc = lax.axis_index("x"), lax.axis_index("y"), lax.axis_index("c")
    me = 4 * mx + 2 * my + mc
    peers = [_peer(mx, my, mc, k) for k in range(1, N_DEV)]
    locals_, sends, recvs = [], [], []
    for a, (x_ref, o_ref) in enumerate(zip(x_refs, o_refs, strict=True)):
        locals_.append(pltpu.make_async_copy(x_ref.at[me], o_ref.at[me], local_sems.at[a]))
        for k, (dev, peer) in enumerate(peers):
            common = dict(send_sem=send_sems.at[a, k], recv_sem=recv_sems.at[a, k], device_id=dev,
                          device_id_type=pl.DeviceIdType.MESH)
            sends.append(pltpu.make_async_remote_copy(src_ref=x_ref.at[peer], dst_ref=o_ref.at[me], **common))
            recvs.append(pltpu.make_async_remote_copy(src_ref=x_ref.at[peer], dst_ref=o_ref.at[peer], **common))
    return locals_, sends, recvs


def all_gather(xs, name, with_a2a=()):
    n, n2 = len(xs), len(with_a2a)
    chip_flips = (4, 2, 6)

    def body(*refs):
        x_refs, t_refs = refs[:n], refs[n:n + n2]
        o_refs, r_refs = refs[n + n2:2 * n + n2], refs[2 * n + n2:2 * n + 2 * n2]
        send_sems, recv_sems, local_sems = refs[2 * n + 2 * n2:2 * n + 2 * n2 + 3]
        mx, my, mc = lax.axis_index("x"), lax.axis_index("y"), lax.axis_index("c")
        me = 4 * mx + 2 * my + mc
        sib_dev, sib = _peer(mx, my, mc, 1)

        def copy(a, k, row, to, src=None):
            return pltpu.make_async_remote_copy(
                src_ref=o_refs[a].at[row] if src is None else src, dst_ref=o_refs[a].at[row],
                send_sem=send_sems.at[a, k], recv_sem=recv_sems.at[a, k], device_id=to,
                device_id_type=pl.DeviceIdType.MESH)

        locals_, first, passed = [], [], []
        t_recvs = []
        if n2:
            t_locals, t_sends, t_recvs = a2a_copies(t_refs, r_refs, *refs[2 * n + 2 * n2 + 3:])
            locals_ += t_locals
            first += t_sends
        for a in range(n):
            locals_.append(pltpu.make_async_copy(x_refs[a], o_refs[a].at[me], local_sems.at[a]))
            first.append(copy(a, 0, me, sib_dev, src=x_refs[a]))
            for j, f in enumerate(chip_flips):
                first.append(copy(a, 1 + j, me, _peer(mx, my, mc, f)[0], src=x_refs[a]))
        for cp in locals_ + first:
            cp.start()
        for a in range(n):
            for j, f in enumerate(chip_flips):
                row = _peer(mx, my, mc, f)[1]
                copy(a, 1 + j, row, sib_dev).wait_recv()
                fwd = copy(a, 4 + j, row, sib_dev)
                fwd.start()
                passed.append(fwd)
        for a in range(n):
            copy(a, 0, sib, sib_dev).wait_recv()
            for j, f in enumerate(chip_flips):
                copy(a, 4 + j, _peer(mx, my, mc, f ^ 1)[1], sib_dev).wait_recv()
        for cp in t_recvs:
            cp.wait_recv()
        for cp in first + passed:
            cp.wait_send()
        for cp in locals_:
            cp.wait()

    hbm = pl.BlockSpec(memory_space=pl.ANY)
    scratch = [pltpu.SemaphoreType.DMA((n, N_DEV - 1)), pltpu.SemaphoreType.DMA((n, N_DEV - 1)),
               pltpu.SemaphoreType.DMA((n,))]
    if n2:
        scratch += [pltpu.SemaphoreType.DMA((n2, N_DEV - 1)), pltpu.SemaphoreType.DMA((n2, N_DEV - 1)),
                    pltpu.SemaphoreType.DMA((n2,))]
    return pl.pallas_call(
        body, name=name, in_specs=[hbm] * (n + n2), out_specs=[hbm] * (n + n2),
        out_shape=[jax.ShapeDtypeStruct((N_DEV,) + x.shape, x.dtype) for x in xs]
        + [jax.ShapeDtypeStruct(x.shape, x.dtype) for x in with_a2a],
        scratch_shapes=scratch,
    )(*xs, *with_a2a)


AG_FIRST_COPIES = 4


def ag_first_copies(x_refs, o_refs, send_sems, recv_sems, local_sems):
    mx, my, mc = lax.axis_index("x"), lax.axis_index("y"), lax.axis_index("c")
    me = 4 * mx + 2 * my + mc
    targets = [_peer(mx, my, mc, f) for f in (1, 4, 2, 6)]
    locals_, sends, recvs = [], [], []
    for a, (x_ref, o_ref) in enumerate(zip(x_refs, o_refs, strict=True)):
        locals_.append(pltpu.make_async_copy(x_ref, o_ref.at[me], local_sems.at[a]))
        for k, (dev, row) in enumerate(targets):
            common = dict(send_sem=send_sems.at[a, k], recv_sem=recv_sems.at[a, k], device_id=dev,
                          device_id_type=pl.DeviceIdType.MESH)
            sends.append(pltpu.make_async_remote_copy(src_ref=x_ref, dst_ref=o_ref.at[me], **common))
            recvs.append(pltpu.make_async_remote_copy(src_ref=x_ref, dst_ref=o_ref.at[row], **common))
    return locals_, sends, recvs


def ag_second_level(bufs, name):
    n = len(bufs)
    chip_flips = (4, 2, 6)

    def body(*refs):
        o_refs = refs[n:2 * n]
        send_sems, recv_sems = refs[2 * n:]
        mx, my, mc = lax.axis_index("x"), lax.axis_index("y"), lax.axis_index("c")
        sib_dev, _ = _peer(mx, my, mc, 1)
        sends, recvs = [], []
        for a in range(n):
            for j, f in enumerate(chip_flips):
                common = dict(send_sem=send_sems.at[a, j], recv_sem=recv_sems.at[a, j], device_id=sib_dev,
                              device_id_type=pl.DeviceIdType.MESH)
                row, sib_row = _peer(mx, my, mc, f)[1], _peer(mx, my, mc, f ^ 1)[1]
                sends.append(pltpu.make_async_remote_copy(src_ref=o_refs[a].at[row], dst_ref=o_refs[a].at[row], **common))
                recvs.append(pltpu.make_async_remote_copy(src_ref=o_refs[a].at[row], dst_ref=o_refs[a].at[sib_row],
                                                          **common))
        for cp in sends:
            cp.start()
        for cp in recvs:
            cp.wait_recv()
        for cp in sends:
            cp.wait_send()

    return pl.pallas_call(
        body, name=name,
        in_specs=[pl.BlockSpec(memory_space=pl.ANY)] * n, out_specs=[pl.BlockSpec(memory_space=pl.ANY)] * n,
        out_shape=[jax.ShapeDtypeStruct(x.shape, x.dtype) for x in bufs],
        input_output_aliases={a: a for a in range(n)},
        scratch_shapes=[pltpu.SemaphoreType.DMA((n, len(chip_flips))), pltpu.SemaphoreType.DMA((n, len(chip_flips)))],
    )(*bufs)


PACK_COLS = 1024
PACK_ROWS = 256
PARTS_TILE_BYTES = 4 * 1024 * 1024


def adamw_reduce(parts, w, m, v, name):
    r, c = w.shape
    if (r // len(parts)) % SUBLANE:
        parts = [jnp.concatenate(parts, axis=1)]
    n_slab = len(parts)
    rs = r // n_slab
    fits = [t for t in (512, 352, 256, 128, 64, 32, 16, 8) if N_DEV * t * c * parts[0].dtype.itemsize <= PARTS_TILE_BYTES]
    tr = rs if rs < 2 * SUBLANE else _pick(rs, fits)
    n_t = rs // tr
    c1 = 1.0 - ADAM_B1 ** ADAM_STEP
    c2 = 1.0 - ADAM_B2 ** ADAM_STEP

    def body(*refs):
        p_refs = refs[:n_slab]
        w_ref, m_ref, v_ref, g_ref, d_ref, mo_ref, vo_ref = refs[n_slab:]
        slab = pl.program_id(0)
        for k, p_ref in enumerate(p_refs):
            @pl.when(slab == k)
            def _(p_ref=p_ref):
                g = p_ref[0].astype(F32)
                for dev in range(1, N_DEV):
                    g = g + p_ref[dev].astype(F32)
                mn = ADAM_B1 * m_ref[...] + (1.0 - ADAM_B1) * g
                vn = ADAM_B2 * v_ref[...] + (1.0 - ADAM_B2) * (g * g)
                g_ref[...] = g
                mo_ref[...] = mn
                vo_ref[...] = vn
                d_ref[...] = -ADAM_LR * ((mn / c1) / (jnp.sqrt(vn / c2) + ADAM_EPS) + ADAM_WD * w_ref[...])

    spec = pl.BlockSpec((tr, c), lambda s, i: (s * n_t + i, 0))
    p_specs = [pl.BlockSpec((N_DEV, tr, c), lambda s, i, k=k: (0, jnp.where(s == k, i, 0), 0)) for k in range(n_slab)]
    return pl.pallas_call(
        body, name=name, grid=(n_slab, n_t),
        in_specs=p_specs + [spec, spec, spec],
        out_specs=[spec] * 4, out_shape=[jax.ShapeDtypeStruct((r, c), F32)] * 4,
        compiler_params=_cparams(dimension_semantics=("arbitrary", "arbitrary")),
    )(*parts, w, m, v)


def pack_flat(arrs, dtype):
    parts = []
    for a in arrs:
        flat = a.reshape(-1).astype(dtype)
        k = -(-flat.shape[0] // PACK_COLS)
        parts.append(jnp.pad(flat, (0, k * PACK_COLS - flat.shape[0])).reshape(k, PACK_COLS))
    buf = jnp.concatenate(parts, axis=0)
    return jnp.pad(buf, ((0, -buf.shape[0] % PACK_ROWS), (0, 0)))


def unpack_flat(buf, shapes):
    out, r0 = [], 0
    for s in shapes:
        n = math.prod(s)
        k = -(-n // PACK_COLS)
        out.append(buf[r0:r0 + k].reshape(-1)[:n].reshape(tuple(s)))
        r0 += k
    return out


W_NAMES = ('meta_tokens', 'ffn1_norm', 'ffn1_w_gate', 'ffn1_w_up', 'ffn1_w_down', 'mix_norm', 'w_in', 'w_out',
           'lru_conv_w', 'lru_conv_b', 'lru_w_a', 'lru_b_a', 'lru_w_i', 'lru_b_i', 'lru_lambda', 'lru_norm',
           'gdn_conv_w', 'gdn_a_log', 'gdn_dt_bias', 'gdn_norm', 'ssd_conv_w', 'ssd_conv_b', 'ssd_a_log',
           'ssd_dt_bias', 'ssd_d', 'ssd_norm', 's5_a_re', 's5_a_im', 's5_log_dt', 's5_b_re', 's5_b_im', 's5_c_re',
           's5_c_im', 's5_d', 's5_w_glu', 's5_norm', 'ffn2_norm', 'ffn2_w_gate', 'ffn2_w_up', 'ffn2_w_down',
           'final_norm')
SHARD_AXIS = {'meta_tokens': 1, 'ffn1_w_gate': 2, 'ffn1_w_up': 2, 'ffn1_w_down': 1, 'w_in': 2, 'w_out': 1,
              'lru_conv_w': 2, 'gdn_conv_w': 2, 'ssd_conv_w': 2, 's5_w_glu': 1, 'ffn2_w_gate': 2, 'ffn2_w_up': 2,
              'ffn2_w_down': 1}
BIG_NAMES = ('ffn1_w_gate', 'ffn1_w_up', 'ffn1_w_down', 'w_in', 'w_out', 's5_w_glu', 'ffn2_w_gate', 'ffn2_w_up',
             'ffn2_w_down')
SHARD_NAMES = tuple(n for n in W_NAMES if n in SHARD_AXIS)
REP_NAMES = tuple(n for n in W_NAMES if n not in SHARD_AXIS)
SSD_GROUPS = 2
S5_CH = 16


def unshard(g, axis):
    if axis == 0:
        return g.reshape((-1,) + g.shape[2:])
    return jnp.concatenate([g[p] for p in range(N_DEV)], axis=axis)


def kernel(*args):
    n_w = len(W_NAMES)
    x = args[0]
    w = dict(zip(W_NAMES, args[1:1 + n_w]))
    target = args[1 + n_w]
    m_in = dict(zip(W_NAMES, args[2 + n_w:2 + 2 * n_w]))
    v_in = dict(zip(W_NAMES, args[2 + 2 * n_w:2 + 3 * n_w]))

    depth, d = w['ffn1_norm'].shape
    seq = x.shape[1]
    n_meta = w['meta_tokens'].shape[0]
    pad = CHUNK - n_meta
    tp = pad + n_meta + seq
    wg = d // 2
    xbc_w = w['ssd_conv_w'].shape[-1] * N_DEV
    gdn_hd = w['gdn_norm'].shape[-1]
    gdn_h = wg // gdn_hd
    ssd_h = w['ssd_a_log'].shape[-1]
    lru_h = w['lru_w_a'].shape[1]
    s5_g, s5_n = w['s5_a_re'].shape[1:]
    s5_q = wg // S5_LANES
    row_tile = _pick(tp, (192, 96, 64))
    norm_tile = _pick(tp, ROW_TILES)

    LAYER_NAMES = tuple(n for n in SHARD_NAMES if n != 'meta_tokens')

    def local_of(n, l):
        return w[n][l].astype(BF16 if n in BIG_NAMES else F32)

    def unshard_layer(gathered):
        return {n: unshard(g, SHARD_AXIS[n] - 1) for n, g in gathered.items()}

    first = all_gather([local_of(n, 0) for n in LAYER_NAMES] + [w['meta_tokens']], "gather_weights")
    meta_full = unshard(first[-1], SHARD_AXIS['meta_tokens'])
    full = [unshard_layer(dict(zip(LAYER_NAMES, first[:-1])))] + [None] * (depth - 1)

    segs = [('a_x', wg), ('a_gate', wg), ('b_q', wg), ('b_k', wg), ('b_v', wg), ('b_z', wg), ('c_xbc', xbc_w),
            ('c_z', wg), ('d_u', wg), ('small_b', LANE), ('small_c', LANE)]
    off, o = {}, 0
    for nme, wd_ in segs:
        assert o % wd_ == 0, (nme, o, wd_)
        off[nme] = o
        o += wd_
    o_beta = 6 * wg
    o_cz = o_beta + 2 * gdn_h
    o_xbc = o_cz + wg
    o_dt = o_xbc + xbc_w
    o_du = o_dt + ssd_h

    def pack_cols(a):
        z = lambda k: jnp.zeros(a.shape[:-1] + (k,), a.dtype)
        return jnp.concatenate([a[..., :o_beta], a[..., o_xbc:o_dt], a[..., o_cz:o_xbc], a[..., o_du:],
                                a[..., o_beta:o_cz], z(LANE - 2 * gdn_h), a[..., o_dt:o_du], z(LANE - ssd_h)], axis=-1)

    def unpack_cols(a):
        sb, sc = off['small_b'], off['small_c']
        return jnp.concatenate([a[..., :o_beta], a[..., sb:sb + 2 * gdn_h], a[..., off['c_z']:off['c_z'] + wg],
                                a[..., off['c_xbc']:off['c_xbc'] + xbc_w], a[..., sc:sc + ssd_h],
                                a[..., off['d_u']:off['d_u'] + wg]], axis=-1)

    w_in_p = [None] * depth
    w_in_p[0] = pack_cols(full[0]['w_in'])

    def col(name, width):
        return off[name] // width

    def row(a):
        return a.reshape(1, -1)

    def layer_params(l):
        gcw = full[l]['gdn_conv_w']
        lru = [full[l]['lru_conv_w'], row(w['lru_conv_b'][l]), blockdiag_expand(w['lru_w_a'][l]), row(w['lru_b_a'][l]),
               blockdiag_expand(w['lru_w_i'][l]), row(w['lru_b_i'][l]), row(w['lru_lambda'][l]), row(w['lru_norm'][l])]
        gdn = [gcw[:, :wg], gcw[:, wg:2 * wg], gcw[:, 2 * wg:], row(w['gdn_a_log'][l]), row(w['gdn_dt_bias'][l]),
               row(w['gdn_norm'][l])]
        ssd = [full[l]['ssd_conv_w'], row(w['ssd_conv_b'][l]), row(w['ssd_a_log'][l]), row(w['ssd_dt_bias'][l]),
               row(w['ssd_d'][l]), row(w['ssd_norm'][l])]
        s5 = list(s5_params_expand(*[w[n][l] for n in ('s5_a_re', 's5_a_im', 's5_log_dt', 's5_b_re', 's5_b_im',
                                                          's5_c_re', 's5_c_im', 's5_d')]))
        post = [full[l]['s5_w_glu'], row(w['s5_norm'][l])]
        return lru, gdn, ssd, s5, post

    lru_fn = functools.partial(lru_chunk, pad)
    gdn_fn = functools.partial(gdn_multi, pad, CHUNK)
    ssd_fn = functools.partial(ssd_multi, pad, 0, SSD_GROUPS, CHUNK)
    s5_fn = functools.partial(s5_chunk, pad)
    post_fn = functools.partial(s5_post, pad)
    n_state_lanes = (S5_LANES // S5_CH) * s5_n

    def mixer_specs(proj):
        lru_seqs = [(proj, wg, col('a_x', wg)), (proj, wg, col('a_gate', wg))]
        gdn_seqs = [(proj, wg, col('b_q', wg)), (proj, wg, col('b_k', wg)), (proj, wg, col('b_v', wg)),
                    (proj, wg, col('b_z', wg)), (proj, LANE, col('small_b', LANE))]
        ssd_seqs = [(proj, wg, col('c_z', wg)), (proj, xbc_w, col('c_xbc', xbc_w)), (proj, LANE, col('small_c', LANE))]
        base = col('d_u', S5_LANES)
        s5_seqs = [(proj, S5_LANES, lambda g: base + g)]
        return lru_seqs, gdn_seqs, ssd_seqs, s5_seqs

    lru_carry = [(CONV_TAIL, wg), (1, wg)]
    gdn_carry = [(CONV_TAIL, wg)] * 3 + [(wg, gdn_hd)]
    ssd_carry = [(CONV_TAIL, xbc_w), (wg, (xbc_w - wg) // (2 * SSD_GROUPS))]
    s5_carry = [(1, n_state_lanes)] * 2
    rk = dict(n_steps=tp // row_tile, rows=row_tile)
    mk = rk
    out_w = (wg, wg, 0, F32)

    h = jnp.concatenate([jnp.zeros((pad, d), F32), meta_full, x[0]], axis=0)
    target_p = jnp.concatenate([jnp.zeros((pad + n_meta, d), F32), target[0]], axis=0)
    saved = []
    for l in range(depth):
        lru_p, gdn_p, ssd_p, s5_p, post_p = layer_params(l)
        h0 = h
        xn1 = rms_fwd(h0, row(w['ffn1_norm'][l]), "rms_fwd", norm_tile)
        nxt = l + 1 < depth
        take = lambda *names: [local_of(n, l + 1) for n in names] if nxt else []
        fw = full[l]
        got = {}
        names = ('ffn1_w_gate', 'w_out')
        h1, bufs = ffn_fwd(h0, xn1, fw['ffn1_w_gate'], fw['ffn1_w_up'], fw['ffn1_w_down'], "ffn_fwd", ag=take(*names))
        got.update(zip(names, bufs))
        xn2 = rms_fwd(h1, row(w['mix_norm'][l]), "rms_fwd", norm_tile)
        proj = matmul_cols(xn2, w_in_p[l], "mix_in_fwd")
        lru_s, gdn_s, ssd_s, s5_s = mixer_specs(proj)
        names = ('s5_w_glu', 'lru_conv_w', 'gdn_conv_w', 'ssd_conv_w')
        ya, lru_c, bufs = mixer_fwd(lru_fn, "lru_fwd", seqs=lru_s, params=lru_p, out=out_w, carry=lru_carry,
                                    ag=take(*names), **mk)
        got.update(zip(names, bufs))
        names = ('w_in',)
        yb, gdn_c, bufs = mixer_fwd(gdn_fn, "gdn_fwd", seqs=gdn_s, params=gdn_p, out=out_w, carry=gdn_carry,
                                    ag=take(*names), **mk)
        got.update(zip(names, bufs))
        names = ('ffn1_w_up',)
        yc, ssd_c, bufs = mixer_fwd(ssd_fn, "ssd_fwd", seqs=ssd_s, params=ssd_p, out=out_w, carry=ssd_carry,
                                    ag=take(*names), **mk)
        got.update(zip(names, bufs))
        names = ('ffn1_w_down', 'ffn2_w_gate')
        y1, s5_c, bufs = mixer_fwd(s5_fn, "s5_fwd", seqs=s5_s, params=s5_p, out=(wg, S5_LANES, lambda g: g, F32),
                                   carry=s5_carry, n_groups=s5_q, ag=take(*names), **mk)
        got.update(zip(names, bufs))
        yd, _, _ = mixer_fwd(post_fn, "s5_post_fwd", seqs=[(y1, wg, 0)], params=post_p, out=out_w, carry=[], **rk)
        h2, _ = mix_out_fwd(h1, [ya, yb, yc, yd], fw['w_out'], "mix_out_fwd", row_tile)
        xn3 = rms_fwd(h2, row(w['ffn2_norm'][l]), "rms_fwd", norm_tile)
        names = ('ffn2_w_up', 'ffn2_w_down')
        h3, bufs = ffn_fwd(h2, xn3, fw['ffn2_w_gate'], fw['ffn2_w_up'], fw['ffn2_w_down'], "ffn_fwd", ag=take(*names))
        got.update(zip(names, bufs))
        if nxt:
            full[l + 1] = unshard_layer(dict(zip(LAYER_NAMES, ag_second_level([got[n] for n in LAYER_NAMES],
                                                                                "gather_pass_on"))))
            w_in_p[l + 1] = pack_cols(full[l + 1]['w_in'])
        saved.append((h0, xn1, h1, xn2, proj, (ya, yb, yc, yd), y1, (lru_c, gdn_c, ssd_c, s5_c), h2, xn3))
        h = h3

    loss_part, dh, d_final = loss_and_grad(h, target_p, row(w['final_norm']), "loss", norm_tile, pad + n_meta)
    loss = lax.psum(loss_part, ("x", "y", "c"))

    def shards_of(a, axis):
        sh = a.shape
        return jnp.moveaxis(a.reshape(sh[:axis] + (N_DEV, sh[axis] // N_DEV) + sh[axis + 1:]), axis, 0)

    def to_send(n, g):
        return shards_of(g, SHARD_AXIS[n] - 1).astype(BF16)

    received = {n: [None] * depth for n in SHARD_NAMES if n != 'meta_tokens'}
    pending = []

    def hosted(names_layers):
        keys = [k for k in pending if (k[0], k[1]) in names_layers]
        for k in keys:
            pending.remove(k)
        return [(k[0], k[1]) for k in keys], [k[2] for k in keys]

    def store(keys, arrays):
        for (n, l), a in zip(keys, arrays, strict=True):
            received[n][l] = a

    gw = {n: [None] * depth for n in W_NAMES if n not in ('meta_tokens', 'final_norm')}
    for l in reversed(range(depth)):
        lru_p, gdn_p, ssd_p, s5_p, post_p = layer_params(l)
        h0, xn1, h1, xn2, proj, ys, y1, (lru_c, gdn_c, ssd_c, s5_c), h2, xn3 = saved[l]
        (dxn, gw['ffn2_w_gate'][l], gw['ffn2_w_up'][l], gw['ffn2_w_down'][l]), _ = ffn_bwd(
            xn3, dh, full[l]['ffn2_w_gate'], full[l]['ffn2_w_up'], full[l]['ffn2_w_down'], "ffn_bwd")
        pending += [(n, l, to_send(n, gw[n][l])) for n in ('ffn2_w_gate', 'ffn2_w_up', 'ffn2_w_down')]
        dh, dg = rms_bwd_add(h2, dxn, dh, row(w['ffn2_norm'][l]), "rms_bwd", norm_tile)
        gw['ffn2_norm'][l] = dg[0]

        dys, (d_wout,) = mixer_bwd(mix_out_delta, "mix_out_bwd", seqs=[(y, wg, 0) for y in ys],
                                   params=[full[l]['w_out']], dout=(dh, d, 0), saved=[], carry=[], **rk)
        gw['w_out'][l] = d_wout[0]
        pending.append(('w_out', l, to_send('w_out', gw['w_out'][l])))
        lru_s, gdn_s, ssd_s, s5_s = mixer_specs(proj)
        (dy1,), d_post = mixer_bwd(post_fn, "s5_post_bwd", seqs=[(y1, wg, 0)], params=post_p, dout=(dys[3], wg, 0),
                                   saved=[], carry=[], **rk)
        gw['s5_w_glu'][l], gw['s5_norm'][l] = d_post[0][0], d_post[1][0, 0]
        pending.append(('s5_w_glu', l, to_send('s5_w_glu', gw['s5_w_glu'][l])))
        keys, arrs = hosted({('ffn1_w_gate', l + 1), ('ffn1_w_up', l + 1), ('ffn1_w_down', l + 1)})
        (d_du,), d_s5, *got = mixer_bwd(s5_fn, "s5_bwd", seqs=s5_s, params=s5_p, dout=(dy1, S5_LANES, lambda g: g),
                                        saved=s5_c, carry=s5_carry, n_groups=s5_q, a2a=arrs, **mk)
        store(keys, got[0] if got else [])
        for n, g in zip(('s5_a_re', 's5_a_im', 's5_log_dt', 's5_b_re', 's5_b_im', 's5_c_re', 's5_c_im', 's5_d'),
                        s5_grads_extract(d_s5, s5_g, s5_n, S5_CH)):
            gw[n][l] = g
        keys, arrs = hosted({('w_out', l), ('s5_w_glu', l)})
        (d_cz, d_cxbc, d_sc), d_ssd, *got = mixer_bwd(ssd_fn, "ssd_bwd", seqs=ssd_s, params=ssd_p,
                                                      dout=(dys[2], wg, 0), saved=ssd_c, carry=ssd_carry, a2a=arrs, **mk)
        store(keys, got[0] if got else [])
        for n, g in zip(('ssd_conv_w', 'ssd_conv_b', 'ssd_a_log', 'ssd_dt_bias', 'ssd_d', 'ssd_norm'), d_ssd):
            gw[n][l] = g[0] if n == 'ssd_conv_w' else g[0, 0]
        keys, arrs = hosted({('ffn2_w_gate', l), ('ffn2_w_up', l), ('ffn2_w_down', l)})
        (d_bq, d_bk, d_bv, d_bz, d_sb), d_gdn, *got = mixer_bwd(gdn_fn, "gdn_bwd", seqs=gdn_s, params=gdn_p,
                                                                dout=(dys[1], wg, 0), saved=gdn_c, carry=gdn_carry,
                                                                a2a=arrs, **mk)
        store(keys, got[0] if got else [])
        gw['gdn_conv_w'][l] = jnp.concatenate([d_gdn[0][0], d_gdn[1][0], d_gdn[2][0]], axis=1)
        gw['gdn_a_log'][l], gw['gdn_dt_bias'][l], gw['gdn_norm'][l] = d_gdn[3][0, 0], d_gdn[4][0, 0], d_gdn[5][0, 0]
        (d_ax, d_ag), d_lru = mixer_bwd(lru_fn, "lru_bwd", seqs=lru_s, params=lru_p, dout=(dys[0], wg, 0),
                                        saved=lru_c, carry=lru_carry, **mk)
        gw['lru_conv_w'][l], gw['lru_conv_b'][l] = d_lru[0][0], d_lru[1][0, 0]
        gw['lru_w_a'][l], gw['lru_b_a'][l] = blockdiag_extract(d_lru[2][0], lru_h), d_lru[3][0, 0]
        gw['lru_w_i'][l], gw['lru_b_i'][l] = blockdiag_extract(d_lru[4][0], lru_h), d_lru[5][0, 0]
        gw['lru_lambda'][l], gw['lru_norm'][l] = d_lru[6][0, 0], d_lru[7][0, 0]

        dproj = jnp.concatenate([d_ax, d_ag, d_bq, d_bk, d_bv, d_bz, d_cxbc, d_cz, d_du, d_sb, d_sc], axis=1)
        dxn, d_win_p = matmul_cols_bwd(xn2, dproj, w_in_p[l], "mix_in_bwd")
        gw['w_in'][l] = unpack_cols(d_win_p)
        pending.append(('w_in', l, to_send('w_in', gw['w_in'][l])))
        dh, dg = rms_bwd_add(h1, dxn, dh, row(w['mix_norm'][l]), "rms_bwd", norm_tile)
        gw['mix_norm'][l] = dg[0]

        keys, arrs = hosted({('w_in', l)})
        (dxn, gw['ffn1_w_gate'][l], gw['ffn1_w_up'][l], gw['ffn1_w_down'][l]), got = ffn_bwd(
            xn1, dh, full[l]['ffn1_w_gate'], full[l]['ffn1_w_up'], full[l]['ffn1_w_down'], "ffn_bwd", a2a=arrs)
        store(keys, got)
        dh, dg = rms_bwd_add(h0, dxn, dh, row(w['ffn1_norm'][l]), "rms_bwd", norm_tile)
        gw['ffn1_norm'][l] = dg[0]
        pending += [(n, l, to_send(n, gw[n][l])) for n in ('ffn1_w_gate', 'ffn1_w_up', 'ffn1_w_down')]

    grad_x = dh[pad + n_meta:][None]
    grads = {n: jnp.stack(gw[n], axis=0) for n in REP_NAMES if n != 'final_norm'}
    grads['meta_tokens'] = dh[pad:pad + n_meta]
    grads['final_norm'] = d_final[0]

    for n in ('lru_conv_w', 'gdn_conv_w', 'ssd_conv_w'):
        pending += [(n, l, to_send(n, gw[n][l])) for l in range(depth)]
    last = [k[2] for k in pending] + [shards_of(grads['meta_tokens'], SHARD_AXIS['meta_tokens']).astype(BF16)]
    recv_rep, *got, recv_meta = all_gather([pack_flat([grads[n] for n in REP_NAMES], BF16)], "last_exchange",
                                           with_a2a=last)
    store([(k[0], k[1]) for k in pending], got)
    out = {}
    for n in SHARD_NAMES:
        c = w[n].shape[-1]
        recv = [recv_meta] if n == 'meta_tokens' else received[n]
        res = adamw_reduce([a.reshape(N_DEV, -1, c) for a in recv], w[n].reshape(-1, c), m_in[n].reshape(-1, c),
                           v_in[n].reshape(-1, c), "adamw_" + n)
        for kind, buf in zip(('grad', 'delta', 'new_m', 'new_v'), res):
            out[kind, n] = buf.reshape(w[n].shape)
    res = adamw_reduce([recv_rep], pack_flat([w[n] for n in REP_NAMES], F32), pack_flat([m_in[n] for n in REP_NAMES], F32),
                       pack_flat([v_in[n] for n in REP_NAMES], F32), "adamw_replicated")
    shapes = [w[n].shape for n in REP_NAMES]
    for kind, buf in zip(('grad', 'delta', 'new_m', 'new_v'), res):
        for n, a in zip(REP_NAMES, unpack_flat(buf, shapes)):
            out[kind, n] = a
    return (loss, grad_x) + tuple(out[k, n] for k in ('grad', 'delta', 'new_m', 'new_v') for n in W_NAMES)
```

```python
import functools
import math

import jax
import jax.numpy as jnp
from jax import lax
from jax.experimental import pallas as pl
from jax.experimental.pallas import tpu as pltpu

F32 = jnp.float32
BF16 = jnp.bfloat16

EPS = 1e-6
CHUNK = 64
CONV_K = 4
CONV_TAIL = 8
LRU_C = 8.0
LANE = 128
SUBLANE = 8
N_DEV = 8
NEG_BIG = -1e30

ADAM_LR = 0.001
ADAM_B1 = 0.9
ADAM_B2 = 0.999
ADAM_EPS = 1e-08
ADAM_WD = 0.01
ADAM_STEP = 10

VMEM_LIMIT = 56 * 1024 * 1024


def _dg(a, b, dims):
    return lax.dot_general(a.astype(BF16), b.astype(BF16), (dims, ((), ())), preferred_element_type=F32)


@jax.custom_vjp
def bdot(a, b):
    return _dg(a, b, ((1,), (0,)))


@jax.custom_vjp
def bdot_nt(a, b):
    return _dg(a, b, ((1,), (1,)))


@jax.custom_vjp
def bdot_tn(a, b):
    return _dg(a, b, ((0,), (0,)))


bdot.defvjp(lambda a, b: (bdot(a, b), (a, b)),
            lambda r, g: (bdot_nt(g, r[1]).astype(r[0].dtype), bdot_tn(r[0], g).astype(r[1].dtype)))
bdot_nt.defvjp(lambda a, b: (bdot_nt(a, b), (a, b)),
               lambda r, g: (bdot(g, r[1]).astype(r[0].dtype), bdot_tn(g, r[0]).astype(r[1].dtype)))
bdot_tn.defvjp(lambda a, b: (bdot_tn(a, b), (a, b)),
               lambda r, g: (bdot_nt(r[1], g).astype(r[0].dtype), bdot(r[0], g).astype(r[1].dtype)))


def _split_bf16(a):
    hi = a.astype(BF16)
    return hi, (a - hi.astype(F32)).astype(BF16)


def _dot3(a, b, dims):
    (ah, al), (bh, bl) = _split_bf16(a), _split_bf16(b)
    d = lambda x, y: lax.dot_general(x, y, (dims, ((), ())), preferred_element_type=F32)
    return d(ah, bh) + (d(ah, bl) + d(al, bh))


@jax.custom_vjp
def hdot(a, b):
    return _dot3(a, b, ((1,), (0,)))


@jax.custom_vjp
def hdot_tn(a, b):
    return _dot3(a, b, ((0,), (0,)))


hdot.defvjp(lambda a, b: (hdot(a, b), (a, b)),
            lambda r, g: (_dot3(g, r[1], ((1,), (1,))), _dot3(r[0], g, ((0,), (0,)))))
hdot_tn.defvjp(lambda a, b: (hdot_tn(a, b), (a, b)),
               lambda r, g: (_dot3(r[1], g, ((1,), (1,))), _dot3(r[0], g, ((1,), (0,)))))


def rms_norm(x, g):
    return x * lax.rsqrt(jnp.mean(x * x, axis=-1, keepdims=True) + EPS) * g


def row_mask(row0, rows, pad):
    r = row0 + lax.broadcasted_iota(jnp.int32, (rows, 1), 0)
    return (r >= pad).astype(F32)


def conv4(tail, u, w):
    rows = u.shape[0]
    xe = jnp.concatenate([tail, u], axis=0)
    y = w[0:1] * xe[CONV_TAIL - 3:CONV_TAIL - 3 + rows]
    for k in range(1, CONV_K):
        y = y + w[k:k + 1] * xe[CONV_TAIL - 3 + k:CONV_TAIL - 3 + k + rows]
    return y


def shift_rows(x, s, fill):
    rows = x.shape[0]
    return jnp.concatenate([jnp.full((s, x.shape[1]), fill, x.dtype), x[:rows - s]], axis=0)


def lin_scan(a, b):
    rows = a.shape[0]
    s = 1
    while s < rows:
        b = a * shift_rows(b, s, 0.0) + b
        a = a * shift_rows(a, s, 1.0)
        s *= 2
    return b


def cscan_const(ar, ai, br, bi):
    rows = br.shape[0]
    s = 1
    while s < rows:
        brs, bis = shift_rows(br, s, 0.0), shift_rows(bi, s, 0.0)
        br, bi = br + ar * brs - ai * bis, bi + ar * bis + ai * brs
        ar, ai = ar * ar - ai * ai, 2.0 * ar * ai
        s *= 2
    return br, bi


def neg_expm1(z):
    t = jnp.tanh(0.5 * z)
    return -2.0 * t / (1.0 - t)


def tri_masks(n):
    r = lax.broadcasted_iota(jnp.int32, (n, n), 0)
    c = lax.broadcasted_iota(jnp.int32, (n, n), 1)
    return r >= c, r > c, (r == c).astype(F32)


def lru_chunk(pad, row0, params, seqs, carry):
    conv_w, conv_b, w_a, b_a, w_i, b_i, lam, norm_g = params
    u_x, u_gate = seqs
    tail, h0 = carry
    rows = u_x.shape[0]
    m = row_mask(row0, rows, pad)
    xc = conv4(tail, u_x, conv_w) + conv_b
    r = jax.nn.sigmoid(bdot(xc, w_a) + b_a)
    ig = jax.nn.sigmoid(bdot(xc, w_i) + b_i)
    log_a = -LRU_C * r * jax.nn.softplus(-lam)
    a = jnp.exp(log_a)
    b = jnp.sqrt(neg_expm1(2.0 * log_a)) * (ig * xc) * m
    first = (lax.broadcasted_iota(jnp.int32, (rows, 1), 0) == 0).astype(F32)
    b = b + first * (a * h0)
    h = lin_scan(a, b)
    y = jax.nn.gelu(u_gate) * h
    out = rms_norm(y, norm_g) * m
    return (out,), (u_x[rows - CONV_TAIL:], h[rows - 1:])


def gdn_multi(pad, sub, row0, params, seqs, carry):
    wq, wk, wv, a_log, dt_bias, norm_g = params
    u_q, u_k, u_v, u_z, small = seqs
    tq, tk, tv, state = carry
    rows = u_q.shape[0]
    hd = norm_g.shape[1]
    nh = u_q.shape[1] // hd
    nc = rows // sub
    m = row_mask(row0, rows, pad)
    incl, strict, eye = tri_masks(sub)
    tril = incl.astype(F32)
    triu = (lax.broadcasted_iota(jnp.int32, (sub, sub), 0) <= lax.broadcasted_iota(jnp.int32, (sub, sub), 1)).astype(F32)
    qc = jax.nn.silu(conv4(tq, u_q, wq))
    kc = jax.nn.silu(conv4(tk, u_k, wk))
    vc = jax.nn.silu(conv4(tv, u_v, wv))
    beta = jax.nn.sigmoid(small[:, :nh]) * m
    g = -jnp.exp(a_log) * jax.nn.softplus(small[:, nh:2 * nh] + dt_bias) * m
    gate = jax.nn.silu(u_z)
    heads = [slice(h * hd, (h + 1) * hd) for h in range(nh)]
    q_h = [qc[:, sl] for sl in heads]
    k_h = [kc[:, sl] for sl in heads]
    q_h = [q * lax.rsqrt(jnp.sum(q * q, axis=-1, keepdims=True) + EPS) * (hd ** -0.5) * m for q in q_h]
    k_h = [k * lax.rsqrt(jnp.sum(k * k, axis=-1, keepdims=True) + EPS) * m for k in k_h]
    v_h = [vc[:, sl] * m for sl in heads]
    pairs = [(c, h) for c in range(nc) for h in range(nh)]
    cs = lambda x, c: x[c * sub:(c + 1) * sub]
    q = {(c, h): cs(q_h[h], c) for c, h in pairs}
    k = {(c, h): cs(k_h[h], c) for c, h in pairs}
    v = {(c, h): cs(v_h[h], c) for c, h in pairs}
    bt = {(c, h): cs(beta, c)[:, h:h + 1] for c, h in pairs}
    gcs = [hdot(tril, cs(g, c)) for c in range(nc)]
    gts = [hdot_tn(cs(g, c), triu) for c in range(nc)]
    gc = {(c, h): gcs[c][:, h:h + 1] for c, h in pairs}
    decay = {(c, h): jnp.exp(jnp.where(incl, gc[c, h] - gts[c][h:h + 1], NEG_BIG)) for c, h in pairs}
    kb = {p: k[p] * bt[p] for p in pairs}
    kk = {p: bdot_nt(kb[p], k[p]) for p in pairs}
    lmat = {p: jnp.where(strict, kk[p] * decay[p], 0.0) for p in pairs}
    pm = {p: eye - lmat[p] for p in pairs}
    mm = {p: hdot(lmat[p], lmat[p]) for p in pairs}
    s = 2
    while s < sub:
        pm = {p: pm[p] + hdot(pm[p], mm[p]) for p in pairs}
        s *= 2
        if s < sub:
            mm = {p: hdot(mm[p], mm[p]) for p in pairs}
    eg = {p: jnp.exp(gc[p]) for p in pairs}
    u = {p: hdot(pm[p], v[p] * bt[p]) for p in pairs}
    w = {p: hdot(pm[p], kb[p] * eg[p]) for p in pairs}
    attn = {p: bdot_nt(q[p], k[p]) * decay[p] for p in pairs}
    qd = {p: q[p] * eg[p] for p in pairs}
    g_last = {p: gc[p][sub - 1:] for p in pairs}
    kd = {p: k[p] * jnp.exp(g_last[p] - gc[p]) for p in pairs}
    last = {p: jnp.exp(g_last[p]) for p in pairs}
    s_h = [state[sl] for sl in heads]
    o = {}
    for c in range(nc):
        ws = [bdot(w[c, h], s_h[h]) for h in range(nh)]
        qs = [bdot(qd[c, h], s_h[h]) for h in range(nh)]
        v_new = [u[c, h] - ws[h] for h in range(nh)]
        av = [bdot(attn[c, h], v_new[h]) for h in range(nh)]
        kv = [bdot_tn(kd[c, h], v_new[h]) for h in range(nh)]
        for h in range(nh):
            o[c, h] = qs[h] + av[h]
        s_h = [s_h[h] * last[c, h] + kv[h] for h in range(nh)]
    out = jnp.concatenate([jnp.concatenate([rms_norm(o[c, h], norm_g) for h in range(nh)], axis=1)
                           for c in range(nc)], axis=0) * gate * m
    t0 = rows - CONV_TAIL
    return (out,), (u_q[t0:], u_k[t0:], u_v[t0:], jnp.concatenate(s_h, axis=0))


def ssd_multi(pad, dt_lane0, n_groups, sub, row0, params, seqs, carry):
    conv_w, conv_b, a_log, dt_bias, d_skip, norm_g = params
    u_z, u_xbc, small = seqs
    tail, state = carry
    rows = u_z.shape[0]
    width = u_z.shape[1]
    nh = a_log.shape[1]
    hd = width // nh
    ns = (u_xbc.shape[1] - width) // (2 * n_groups)
    hpg = nh // n_groups
    nc = rows // sub
    m = row_mask(row0, rows, pad)
    incl, _, _ = tri_masks(sub)
    tril = incl.astype(F32)
    triu = (lax.broadcasted_iota(jnp.int32, (sub, sub), 0) <= lax.broadcasted_iota(jnp.int32, (sub, sub), 1)).astype(F32)
    xbc = jax.nn.silu(conv4(tail, u_xbc, conv_w) + conv_b)
    xs = xbc[:, :width]
    dt = jax.nn.softplus(small[:, dt_lane0:dt_lane0 + nh] + dt_bias)
    a_all = dt * (-jnp.exp(a_log)) * m
    cs = lambda x, c: x[c * sub:(c + 1) * sub]
    heads = [slice(h * hd, (h + 1) * hd) for h in range(nh)]
    pairs = [(c, h) for c in range(nc) for h in range(nh)]
    grp = lambda h: h // hpg
    bm = {(c, g): cs(xbc[:, width + g * ns: width + (g + 1) * ns] * m, c) for c in range(nc) for g in range(n_groups)}
    cm = {(c, g): cs(xbc[:, width + (n_groups + g) * ns: width + (n_groups + g + 1) * ns] * m, c)
          for c in range(nc) for g in range(n_groups)}
    xh = {(c, h): cs(xs[:, heads[h]], c) for c, h in pairs}
    xdt = {(c, h): xh[c, h] * cs(dt[:, h:h + 1] * m, c) for c, h in pairs}
    acums = [hdot(tril, cs(a_all, c)) for c in range(nc)]
    acts = [hdot_tn(cs(a_all, c), triu) for c in range(nc)]
    acum = {(c, h): acums[c][:, h:h + 1] for c, h in pairs}
    a_last = {p: acum[p][sub - 1:] for p in pairs}
    lmat = {(c, h): jnp.exp(jnp.where(incl, acum[c, h] - acts[c][h:h + 1], NEG_BIG)) for c, h in pairs}
    cb = {cg: bdot_nt(cm[cg], bm[cg]) for cg in bm}
    y_diag = {(c, h): bdot(cb[c, grp(h)] * lmat[c, h], xdt[c, h]) for c, h in pairs}
    st = {(c, h): bdot_tn(xdt[c, h] * jnp.exp(a_last[c, h] - acum[c, h]), bm[c, grp(h)]) for c, h in pairs}
    e_in = {p: jnp.exp(acum[p]) for p in pairs}
    e_out = {p: jnp.exp(a_last[p]) for p in pairs}
    s_h = [state[sl] for sl in heads]
    y = {}
    for c in range(nc):
        off = [bdot_nt(cm[c, grp(h)], s_h[h]) for h in range(nh)]
        for h in range(nh):
            y[c, h] = y_diag[c, h] + off[h] * e_in[c, h] + d_skip[:, h:h + 1] * xh[c, h]
        s_h = [s_h[h] * e_out[c, h] + st[c, h] for h in range(nh)]
    yy = jnp.concatenate([jnp.concatenate([y[c, h] for h in range(nh)], axis=1) for c in range(nc)], axis=0)
    yy = yy * jax.nn.silu(u_z)
    gw = width // n_groups
    outs = [rms_norm(yy[:, g * gw:(g + 1) * gw], norm_g[:, g * gw:(g + 1) * gw]) for g in range(n_groups)]
    out = jnp.concatenate(outs, axis=1) * m
    return (out,), (u_xbc[rows - CONV_TAIL:], jnp.concatenate(s_h, axis=0))


def s5_chunk(pad, row0, params, seqs, carry):
    a_re, a_im, log_dt, b_re, b_im, c_re, c_im, d_skip = params
    (u,) = seqs
    s_re0, s_im0 = carry
    rows = u.shape[0]
    n_state = a_re.shape[1]
    n_grp = log_dt.shape[1]
    per = n_state // n_grp
    expand = (lax.broadcasted_iota(jnp.int32, (n_grp, n_state), 1) // per
              == lax.broadcasted_iota(jnp.int32, (n_grp, n_state), 0)).astype(F32)
    dt = jnp.exp(hdot(log_dt, expand))
    lam_re = jnp.minimum(a_re, -1e-4)
    lam_im = a_im
    mag = jnp.exp(dt * lam_re)
    ab_re = mag * jnp.cos(dt * lam_im)
    ab_im = mag * jnp.sin(dt * lam_im)
    den = lam_re * lam_re + lam_im * lam_im
    f_re = ((ab_re - 1.0) * lam_re + ab_im * lam_im) / den
    f_im = (ab_im * lam_re - (ab_re - 1.0) * lam_im) / den
    bb_re = f_re * b_re - f_im * b_im
    bb_im = f_re * b_im + f_im * b_re
    bu_re = bdot(u, bb_re)
    bu_im = bdot(u, bb_im)
    first = (lax.broadcasted_iota(jnp.int32, (rows, 1), 0) == 0).astype(F32)
    bu_re = bu_re + first * (ab_re * s_re0 - ab_im * s_im0)
    bu_im = bu_im + first * (ab_re * s_im0 + ab_im * s_re0)
    s_re, s_im = cscan_const(ab_re, ab_im, bu_re, bu_im)
    y = bdot(s_re, c_re) - bdot(s_im, c_im) + d_skip * u
    return (y,), (s_re[rows - 1:], s_im[rows - 1:])


def s5_post(pad, row0, params, seqs, carry):
    w_glu, norm_g = params
    (y,) = seqs
    y = jax.nn.gelu(y)
    y = y * jax.nn.sigmoid(bdot(y, w_glu))
    return (rms_norm(y, norm_g),), ()


def mix_out_delta(row0, params, seqs, carry):
    (w_out,) = params
    wd = seqs[0].shape[1]
    acc = bdot(seqs[0], w_out[0:wd])
    for k in range(1, len(seqs)):
        acc = acc + bdot(seqs[k], w_out[k * wd:(k + 1) * wd])
    return (acc,), ()


def blockdiag_expand(w):
    nh, a, b = w.shape
    eye = jnp.eye(nh, dtype=w.dtype)
    return (w[:, :, None, :] * eye[:, None, :, None]).reshape(nh * a, nh * b)


def blockdiag_extract(m, nh):
    a, b = m.shape[0] // nh, m.shape[1] // nh
    on_diag = jnp.eye(nh, dtype=bool)[:, None, :, None]
    return jnp.sum(jnp.where(on_diag, m.reshape(nh, a, nh, b), 0.0), axis=2)


S5_LANES = LANE


def s5_params_expand(a_re, a_im, log_dt, b_re, b_im, c_re, c_im, d_skip):
    n_grp, n_state = a_re.shape
    ch = b_re.shape[-1]
    gpl = S5_LANES // ch
    nq = n_grp // gpl
    eye = jnp.eye(gpl, dtype=F32)[None, :, None, :, None]

    def bexp(b):
        bt = jnp.swapaxes(b, 1, 2).reshape(nq, gpl, b.shape[2], 1, b.shape[1])
        return (bt * eye).reshape(nq, gpl * b.shape[2], gpl * b.shape[1])

    return (a_re.reshape(nq, 1, gpl * n_state), a_im.reshape(nq, 1, gpl * n_state), log_dt.reshape(nq, 1, gpl),
            bexp(b_re), bexp(b_im), bexp(c_re), bexp(c_im), d_skip.reshape(nq, 1, S5_LANES))


def s5_grads_extract(grads, n_grp, n_state, ch):
    da_re, da_im, dlog_dt, db_re, db_im, dc_re, dc_im, dd = grads
    gpl = S5_LANES // ch
    nq = n_grp // gpl
    on_diag = jnp.eye(gpl, dtype=bool)[None, :, None, :, None]

    def bext(b):
        r, c = b.shape[1] // gpl, b.shape[2] // gpl
        d = jnp.sum(jnp.where(on_diag, b.reshape(nq, gpl, r, gpl, c), 0.0), axis=3)
        return jnp.swapaxes(d.reshape(n_grp, r, c), 1, 2)

    return (da_re.reshape(n_grp, n_state), da_im.reshape(n_grp, n_state), dlog_dt.reshape(n_grp),
            bext(db_re), bext(db_im), bext(dc_re), bext(dc_im), dd.reshape(n_grp * ch))


def _cparams(**kw):
    return pltpu.CompilerParams(vmem_limit_bytes=VMEM_LIMIT, **kw)


def tiled_call(body_fn, name, *, n_steps, rows, n_groups=1, reverse=False,
               seq_in=(), whole_in=(), step_in=(), seq_out=(), acc_out=(), step_out=(), carry=(), a2a=(), ag=()):
    def step_of(i):
        return (n_steps - 1 - i) if reverse else i

    def col_of(col, g):
        return col(g) if callable(col) else col

    in_specs, operands = [], []
    for arr, width, col in seq_in:
        in_specs.append(pl.BlockSpec((rows, width), lambda g, i, col=col: (step_of(i), col_of(col, g))))
        operands.append(arr)
    for arr in whole_in:
        if arr.ndim == 2:
            in_specs.append(pl.BlockSpec(arr.shape, lambda g, i: (0, 0)))
        else:
            in_specs.append(pl.BlockSpec((None,) + arr.shape[1:], lambda g, i: (g, 0, 0)))
        operands.append(arr)
    for arr in step_in:
        in_specs.append(pl.BlockSpec((None, None) + arr.shape[2:], lambda g, i: (g, step_of(i), 0, 0)))
        operands.append(arr)
    out_shape, out_specs = [], []
    for total, width, col, dt in seq_out:
        out_shape.append(jax.ShapeDtypeStruct((n_steps * rows, total), dt))
        out_specs.append(pl.BlockSpec((rows, width), lambda g, i, col=col: (step_of(i), col_of(col, g))))
    for r, c in acc_out:
        out_shape.append(jax.ShapeDtypeStruct((n_groups, r, c), F32))
        out_specs.append(pl.BlockSpec((None, r, c), lambda g, i: (g, 0, 0)))
    for r, c in step_out:
        out_shape.append(jax.ShapeDtypeStruct((n_groups, n_steps, r, c), F32))
        out_specs.append(pl.BlockSpec((None, None, r, c), lambda g, i: (g, step_of(i), 0, 0)))
    n_seq, n_whole, n_step = len(seq_in), len(whole_in), len(step_in)
    n_so, n_ao, n_sto = len(seq_out), len(acc_out), len(step_out)
    assert not (a2a and ag)
    hosted = list(a2a) + list(ag)
    n_x = len(hosted)
    n_sem = AG_FIRST_COPIES if ag else N_DEV - 1
    hbm = pl.BlockSpec(memory_space=pl.ANY)
    in_specs += [hbm] * n_x
    operands += hosted
    out_specs += [hbm] * n_x
    out_shape += [jax.ShapeDtypeStruct((N_DEV,) + x.shape if ag else x.shape, x.dtype) for x in hosted]
    scratch = [pltpu.VMEM((r, c), F32) for r, c in carry]
    if n_x:
        scratch += [pltpu.SemaphoreType.DMA((n_x, n_sem)), pltpu.SemaphoreType.DMA((n_x, n_sem)),
                    pltpu.SemaphoreType.DMA((n_x,))]

    def body(*refs):
        pos = 0
        seq_refs = refs[pos:pos + n_seq]
        pos += n_seq
        whole_refs = refs[pos:pos + n_whole]
        pos += n_whole
        step_refs = refs[pos:pos + n_step]
        pos += n_step + n_x
        so_refs = refs[pos:pos + n_so]
        pos += n_so
        ao_refs = refs[pos:pos + n_ao]
        pos += n_ao
        sto_refs = refs[pos:pos + n_sto]
        pos += n_sto
        xo_refs = refs[pos:pos + n_x]
        pos += n_x
        carry_refs = refs[pos:pos + len(carry)]
        pos += len(carry)
        x_refs = refs[n_seq + n_whole + n_step:n_seq + n_whole + n_step + n_x]
        g, i = pl.program_id(0), pl.program_id(1)
        if n_x:
            locals_, sends, recvs = (ag_first_copies if ag else a2a_copies)(x_refs, xo_refs, *refs[pos:])

            @pl.when((g == 0) & (i == 0))
            def _():
                for cp in locals_ + sends:
                    cp.start()

        @pl.when(i == 0)
        def _():
            for r in carry_refs:
                r[...] = jnp.zeros(r.shape, r.dtype)
            for r in ao_refs:
                r[...] = jnp.zeros(r.shape, r.dtype)

        row0 = step_of(i) * rows
        seq_o, acc_o, step_o, new_c = body_fn(row0, [r[...] for r in whole_refs], [r[...] for r in seq_refs],
                                              [r[...] for r in step_refs], [r[...] for r in carry_refs])
        for r, val in zip(so_refs, seq_o, strict=True):
            r[...] = val.astype(r.dtype)
        for r, val in zip(ao_refs, acc_o, strict=True):
            r[...] += val
        for r, val in zip(sto_refs, step_o, strict=True):
            r[...] = val
        for r, val in zip(carry_refs, new_c, strict=True):
            r[...] = val
        if n_x:
            @pl.when((g == n_groups - 1) & (i == n_steps - 1))
            def _():
                for cp in recvs:
                    cp.wait_recv()
                for cp in sends:
                    cp.wait_send()
                for cp in locals_:
                    cp.wait()

    return pl.pallas_call(
        body, name=name, grid=(n_groups, n_steps), in_specs=in_specs, out_specs=out_specs, out_shape=out_shape,
        scratch_shapes=scratch,
        compiler_params=_cparams(dimension_semantics=("arbitrary", "arbitrary")),
    )(*operands)


def mixer_fwd(fn, name, *, n_steps, rows, seqs, params, out, carry, n_groups=1, ag=()):
    def body(row0, whole, seq_vals, steps, carry_vals):
        outs, new_c = fn(row0, tuple(whole), tuple(seq_vals), tuple(carry_vals))
        return list(outs), [], list(carry_vals), list(new_c)

    res = tiled_call(body, name, n_steps=n_steps, rows=rows, n_groups=n_groups, seq_in=seqs, whole_in=params,
                     seq_out=[out], step_out=carry, carry=carry, ag=ag)
    return res[0], list(res[1:1 + len(carry)]), list(res[1 + len(carry):])


def mixer_bwd(fn, name, *, n_steps, rows, seqs, params, dout, saved, carry, n_groups=1, a2a=()):
    n_seq = len(seqs)

    def body(row0, whole, seq_vals, steps, dcarry):
        params_f = tuple(p.astype(F32) for p in whole)
        _, vjp = jax.vjp(lambda p, s, c: fn(row0, p, s, c), params_f, tuple(seq_vals[:n_seq]), tuple(steps))
        dp, ds, dc = vjp(((seq_vals[n_seq],), tuple(dcarry)))
        return list(ds), list(dp), [], list(dc)

    seq_out = [(n_groups * w, w, (lambda g: g), F32) if callable(c) else (w, w, 0, F32) for a, w, c in seqs]
    acc_out = [p.shape[-2:] for p in params]
    res = tiled_call(body, name, n_steps=n_steps, rows=rows, n_groups=n_groups, reverse=True,
                     seq_in=list(seqs) + [dout], whole_in=params, step_in=saved,
                     seq_out=seq_out, acc_out=acc_out, carry=carry, a2a=a2a)
    n_p = len(params)
    if a2a:
        return list(res[:n_seq]), list(res[n_seq:n_seq + n_p]), list(res[n_seq + n_p:])
    return list(res[:n_seq]), list(res[n_seq:])


def rms_fwd(h, g, name, rows):
    def body(row0, whole, seqs, steps, carry):
        return [rms_norm(seqs[0], whole[0])], [], [], []
    d = h.shape[1]
    return tiled_call(body, name, n_steps=h.shape[0] // rows, rows=rows, seq_in=[(h, d, 0)], whole_in=[g],
                      seq_out=[(d, d, 0, BF16)])[0]


def rms_bwd_add(h, dxn, dh_out, g, name, rows):
    def body(row0, whole, seqs, steps, carry):
        _, vjp = jax.vjp(rms_norm, seqs[0], whole[0])
        dh, dg = vjp(seqs[1])
        return [seqs[2] + dh], [dg], [], []
    d = h.shape[1]
    dh_in, dg = tiled_call(body, name, n_steps=h.shape[0] // rows, rows=rows,
                           seq_in=[(h, d, 0), (dxn, d, 0), (dh_out, d, 0)], whole_in=[g],
                           seq_out=[(d, d, 0, F32)], acc_out=[(1, d)])
    return dh_in, dg[0]


def mix_out_fwd(h, ys, w_out, name, rows, ag=()):
    def body(row0, whole, seqs, steps, carry):
        (delta,), _ = mix_out_delta(row0, (whole[0],), tuple(seqs[1:]), ())
        return [seqs[0] + delta], [], [], []
    d, wd = h.shape[1], ys[0].shape[1]
    res = tiled_call(body, name, n_steps=h.shape[0] // rows, rows=rows,
                     seq_in=[(h, d, 0)] + [(y, wd, 0) for y in ys], whole_in=[w_out], seq_out=[(d, d, 0, F32)], ag=ag)
    return res[0], list(res[1:])


def loss_and_grad(h, target, g, name, rows, first_row):
    def body(row0, whole, seqs, steps, carry):
        hh, tt = seqs
        keep = row_mask(row0, hh.shape[0], first_row)

        def f(hv, gv):
            err = rms_norm(hv, gv) - tt
            return 0.5 * jnp.sum(jnp.mean(err * err, axis=-1, keepdims=True) * keep, axis=0, keepdims=True)

        val, vjp = jax.vjp(f, hh, whole[0])
        dh, dg = vjp(jnp.ones((1, 1), F32))
        return [dh], [jnp.broadcast_to(val, (1, LANE)), dg], [], []
    d = h.shape[1]
    dh, loss, dg = tiled_call(body, name, n_steps=h.shape[0] // rows, rows=rows,
                              seq_in=[(h, d, 0), (target, d, 0)], whole_in=[g],
                              seq_out=[(d, d, 0, F32)], acc_out=[(1, LANE), (1, d)])
    return loss[0, 0, 0], dh, dg[0]


def _pick(n, cands):
    for c in cands:
        if n % c == 0:
            return c
    raise ValueError(f"no tile for {n}")


ROW_TILES = (1056, 704, 352, 192, 96, 64)
COL_TILES = (256, 128)
NT_DIMS = (((1,), (1,)), ((), ()))
TN_DIMS = (((0,), (0,)), ((), ()))


def ffn_fwd(h, xn, wg, wu, wd, name, ag=()):
    t, d = h.shape
    f = wg.shape[1]
    tm = _pick(t, ROW_TILES)
    tn = _pick(f, COL_TILES)
    n_i, n_j = t // tm, f // tn
    n_x = len(ag)

    def body(h_ref, xn_ref, wg_ref, wu_ref, wd_ref, *rest):
        x_refs, o_ref, xo_refs, acc_ref = rest[:n_x], rest[n_x], rest[n_x + 1:2 * n_x + 1], rest[2 * n_x + 1]
        i, j = pl.program_id(0), pl.program_id(1)
        if n_x:
            locals_, sends, recvs = ag_first_copies(x_refs, xo_refs, *rest[2 * n_x + 2:])

            @pl.when((i == 0) & (j == 0))
            def _():
                for cp in locals_ + sends:
                    cp.start()

        @pl.when(j == 0)
        def _():
            acc_ref[...] = jnp.zeros(acc_ref.shape, F32)

        x = xn_ref[...]
        g = jnp.dot(x, wg_ref[...], preferred_element_type=F32)
        u = jnp.dot(x, wu_ref[...], preferred_element_type=F32)
        a = (jax.nn.silu(g) * u).astype(BF16)
        acc_ref[...] += jnp.dot(a, wd_ref[...], preferred_element_type=F32)

        @pl.when(j == n_j - 1)
        def _():
            o_ref[...] = h_ref[...] + 0.5 * acc_ref[...]

        if n_x:
            @pl.when((i == n_i - 1) & (j == n_j - 1))
            def _():
                for cp in recvs:
                    cp.wait_recv()
                for cp in sends:
                    cp.wait_send()
                for cp in locals_:
                    cp.wait()

    hbm = pl.BlockSpec(memory_space=pl.ANY)
    scratch = [pltpu.VMEM((tm, d), F32)]
    if n_x:
        scratch += [pltpu.SemaphoreType.DMA((n_x, AG_FIRST_COPIES)), pltpu.SemaphoreType.DMA((n_x, AG_FIRST_COPIES)),
                    pltpu.SemaphoreType.DMA((n_x,))]
    res = pl.pallas_call(
        body, name=name, grid=(n_i, n_j),
        in_specs=[pl.BlockSpec((tm, d), lambda i, j: (i, 0)), pl.BlockSpec((tm, d), lambda i, j: (i, 0)),
                  pl.BlockSpec((d, tn), lambda i, j: (0, j)), pl.BlockSpec((d, tn), lambda i, j: (0, j)),
                  pl.BlockSpec((tn, d), lambda i, j: (j, 0))] + [hbm] * n_x,
        out_specs=[pl.BlockSpec((tm, d), lambda i, j: (i, 0))] + [hbm] * n_x,
        out_shape=[jax.ShapeDtypeStruct((t, d), F32)] + [jax.ShapeDtypeStruct((N_DEV,) + x.shape, x.dtype) for x in ag],
        scratch_shapes=scratch,
        compiler_params=_cparams(dimension_semantics=("arbitrary", "arbitrary")),
    )(h, xn, wg, wu, wd, *ag)
    return res[0], list(res[1:])


def ffn_bwd(xn, dh, wg, wu, wd, name, a2a=()):
    t, d = dh.shape
    f = wg.shape[1]
    tm = _pick(t, ROW_TILES)
    tn = _pick(f, COL_TILES)
    n_i, n_j = t // tm, f // tn
    n_x = len(a2a)

    def body(xn_ref, dh_ref, wg_ref, wu_ref, wd_ref, *rest):
        x_refs = rest[:n_x]
        dxn_ref, dwg_ref, dwu_ref, dwd_ref = rest[n_x:n_x + 4]
        xo_refs = rest[n_x + 4:2 * n_x + 4]
        ag_ref, au_ref, ad_ref = rest[2 * n_x + 4:2 * n_x + 7]
        j, i = pl.program_id(0), pl.program_id(1)
        if n_x:
            locals_, sends, recvs = a2a_copies(x_refs, xo_refs, *rest[2 * n_x + 7:])

            @pl.when((j == 0) & (i == 0))
            def _():
                for cp in locals_ + sends:
                    cp.start()
        rows = pl.ds(pl.multiple_of(i * tm, 8), tm)
        x = xn_ref[rows, :]
        dhh = (0.5 * dh_ref[...]).astype(BF16)
        wgv, wuv = wg_ref[...], wu_ref[...]
        g = jnp.dot(x, wgv, preferred_element_type=F32)
        u = jnp.dot(x, wuv, preferred_element_type=F32)
        sg = jax.nn.sigmoid(g)
        s = g * sg
        da = lax.dot_general(dhh, wd_ref[...], NT_DIMS, preferred_element_type=F32)
        dwd = lax.dot_general((s * u).astype(BF16), dhh, TN_DIMS, preferred_element_type=F32)
        dg = (da * u * (sg * (1.0 + g * (1.0 - sg)))).astype(BF16)
        du = (da * s).astype(BF16)
        dwg = lax.dot_general(x, dg, TN_DIMS, preferred_element_type=F32)
        dwu = lax.dot_general(x, du, TN_DIMS, preferred_element_type=F32)
        dx = (lax.dot_general(dg, wgv, NT_DIMS, preferred_element_type=F32)
              + lax.dot_general(du, wuv, NT_DIMS, preferred_element_type=F32))

        @pl.when(i == 0)
        def _():
            ag_ref[...] = dwg
            au_ref[...] = dwu
            ad_ref[...] = dwd

        @pl.when(i > 0)
        def _():
            ag_ref[...] += dwg
            au_ref[...] += dwu
            ad_ref[...] += dwd

        @pl.when(i == n_i - 1)
        def _():
            dwg_ref[...] = ag_ref[...].astype(BF16)
            dwu_ref[...] = au_ref[...].astype(BF16)
            dwd_ref[...] = ad_ref[...].astype(BF16)

        @pl.when(j == 0)
        def _():
            dxn_ref[rows, :] = dx

        @pl.when(j > 0)
        def _():
            dxn_ref[rows, :] += dx

        if n_x:
            @pl.when((j == n_j - 1) & (i == n_i - 1))
            def _():
                for cp in recvs:
                    cp.wait_recv()
                for cp in sends:
                    cp.wait_send()
                for cp in locals_:
                    cp.wait()

    hbm = pl.BlockSpec(memory_space=pl.ANY)
    scratch = [pltpu.VMEM((d, tn), F32), pltpu.VMEM((d, tn), F32), pltpu.VMEM((tn, d), F32)]
    if n_x:
        scratch += [pltpu.SemaphoreType.DMA((n_x, N_DEV - 1)), pltpu.SemaphoreType.DMA((n_x, N_DEV - 1)),
                    pltpu.SemaphoreType.DMA((n_x,))]
    res = pl.pallas_call(
        body, name=name, grid=(n_j, n_i),
        in_specs=[pl.BlockSpec((t, d), lambda j, i: (0, 0)), pl.BlockSpec((tm, d), lambda j, i: (i, 0)),
                  pl.BlockSpec((d, tn), lambda j, i: (0, j)), pl.BlockSpec((d, tn), lambda j, i: (0, j)),
                  pl.BlockSpec((tn, d), lambda j, i: (j, 0))] + [hbm] * n_x,
        out_specs=[pl.BlockSpec((t, d), lambda j, i: (0, 0)), pl.BlockSpec((d, tn), lambda j, i: (0, j)),
                   pl.BlockSpec((d, tn), lambda j, i: (0, j)), pl.BlockSpec((tn, d), lambda j, i: (j, 0))] + [hbm] * n_x,
        out_shape=[jax.ShapeDtypeStruct((t, d), F32), jax.ShapeDtypeStruct((d, f), BF16),
                   jax.ShapeDtypeStruct((d, f), BF16), jax.ShapeDtypeStruct((f, d), BF16)]
        + [jax.ShapeDtypeStruct(x.shape, x.dtype) for x in a2a],
        scratch_shapes=scratch,
        compiler_params=_cparams(dimension_semantics=("arbitrary", "arbitrary")),
    )(xn, dh, wg, wu, wd, *a2a)
    return res[:4], list(res[4:])


def matmul_cols(xn, w, name):
    t, d = xn.shape
    n = w.shape[1]
    tn = _pick(n, COL_TILES)

    def body(x_ref, w_ref, o_ref):
        o_ref[...] = jnp.dot(x_ref[...], w_ref[...], preferred_element_type=F32)

    return pl.pallas_call(
        body, name=name, grid=(n // tn,),
        in_specs=[pl.BlockSpec((t, d), lambda j: (0, 0)), pl.BlockSpec((d, tn), lambda j: (0, j))],
        out_specs=pl.BlockSpec((t, tn), lambda j: (0, j)),
        out_shape=jax.ShapeDtypeStruct((t, n), F32),
        compiler_params=_cparams(dimension_semantics=("arbitrary",)),
    )(xn, w)


def matmul_cols_bwd(xn, dy, w, name):
    t, d = xn.shape
    n = w.shape[1]
    tn = _pick(n, COL_TILES)

    def body(x_ref, dy_ref, w_ref, dx_ref, dw_ref):
        j = pl.program_id(0)
        dyv = dy_ref[...].astype(BF16)
        dw_ref[...] = lax.dot_general(x_ref[...], dyv, TN_DIMS, preferred_element_type=F32).astype(BF16)
        dx = lax.dot_general(dyv, w_ref[...], NT_DIMS, preferred_element_type=F32)

        @pl.when(j == 0)
        def _():
            dx_ref[...] = dx

        @pl.when(j > 0)
        def _():
            dx_ref[...] += dx

    return pl.pallas_call(
        body, name=name, grid=(n // tn,),
        in_specs=[pl.BlockSpec((t, d), lambda j: (0, 0)), pl.BlockSpec((t, tn), lambda j: (0, j)),
                  pl.BlockSpec((d, tn), lambda j: (0, j))],
        out_specs=[pl.BlockSpec((t, d), lambda j: (0, 0)), pl.BlockSpec((d, tn), lambda j: (0, j))],
        out_shape=[jax.ShapeDtypeStruct((t, d), F32), jax.ShapeDtypeStruct((d, n), BF16)],
        compiler_params=_cparams(dimension_semantics=("arbitrary",)),
    )(xn, dy, w)


def _peer(mx, my, mc, k):
    px = 1 - mx if (k >> 2) & 1 else mx
    py = 1 - my if (k >> 1) & 1 else my
    pc = 1 - mc if k & 1 else mc
    return (px, py, pc), 4 * px + 2 * py + pc


def a2a_copies(x_refs, o_refs, send_sems, recv_sems, local_sems):
    mx, my, mc = lax.axis_index("x"), lax.axis_index("y"), lax.axis_index("c")
    me = 4 * mx + 2 * my + mc
    peers = [_peer(mx, my, mc, k) for k in range(1, N_DEV)]
    locals_, sends, recvs = [], [], []
    for a, (x_ref, o_ref) in enumerate(zip(x_refs, o_refs, strict=True)):
        locals_.append(pltpu.make_async_copy(x_ref.at[me], o_ref.at[me], local_sems.at[a]))
        for k, (dev, peer) in enumerate(peers):
            common = dict(send_sem=send_sems.at[a, k], recv_sem=recv_sems.at[a, k], device_id=dev,
                          device_id_type=pl.DeviceIdType.MESH)
            sends.append(pltpu.make_async_remote_copy(src_ref=x_ref.at[peer], dst_ref=o_ref.at[me], **common))
            recvs.append(pltpu.make_async_remote_copy(src_ref=x_ref.at[peer], dst_ref=o_ref.at[peer], **common))
    return locals_, sends, recvs


def all_gather(xs, name, with_a2a=()):
    n, n2 = len(xs), len(with_a2a)
    chip_flips = (4, 2, 6)

    def body(*refs):
        x_refs, t_refs = refs[:n], refs[n:n + n2]
        o_refs, r_refs = refs[n + n2:2 * n + n2], refs[2 * n + n2:2 * n + 2 * n2]
        send_sems, recv_sems, local_sems = refs[2 * n + 2 * n2:2 * n + 2 * n2 + 3]
        mx, my, mc = lax.axis_index("x"), lax.axis_index("y"), lax.axis_index("c")
        me = 4 * mx + 2 * my + mc
        sib_dev, sib = _peer(mx, my, mc, 1)

        def copy(a, k, row, to, src=None):
            return pltpu.make_async_remote_copy(
                src_ref=o_refs[a].at[row] if src is None else src, dst_ref=o_refs[a].at[row],
                send_sem=send_sems.at[a, k], recv_sem=recv_sems.at[a, k], device_id=to,
                device_id_type=pl.DeviceIdType.MESH)

        locals_, first, passed = [], [], []
        t_recvs = []
        if n2:
            t_locals, t_sends, t_recvs = a2a_copies(t_refs, r_refs, *refs[2 * n + 2 * n2 + 3:])
            locals_ += t_locals
            first += t_sends
        for a in range(n):
            locals_.append(pltpu.make_async_copy(x_refs[a], o_refs[a].at[me], local_sems.at[a]))
            first.append(copy(a, 0, me, sib_dev, src=x_refs[a]))
            for j, f in enumerate(chip_flips):
                first.append(copy(a, 1 + j, me, _peer(mx, my, mc, f)[0], src=x_refs[a]))
        for cp in locals_ + first:
            cp.start()
        for a in range(n):
            for j, f in enumerate(chip_flips):
                row = _peer(mx, my, mc, f)[1]
                copy(a, 1 + j, row, sib_dev).wait_recv()
                fwd = copy(a, 4 + j, row, sib_dev)
                fwd.start()
                passed.append(fwd)
        for a in range(n):
            copy(a, 0, sib, sib_dev).wait_recv()
            for j, f in enumerate(chip_flips):
                copy(a, 4 + j, _peer(mx, my, mc, f ^ 1)[1], sib_dev).wait_recv()
        for cp in t_recvs:
            cp.wait_recv()
        for cp in first + passed:
            cp.wait_send()
        for cp in locals_:
            cp.wait()

    hbm = pl.BlockSpec(memory_space=pl.ANY)
    scratch = [pltpu.SemaphoreType.DMA((n, N_DEV - 1)), pltpu.SemaphoreType.DMA((n, N_DEV - 1)),
               pltpu.SemaphoreType.DMA((n,))]
    if n2:
        scratch += [pltpu.SemaphoreType.DMA((n2, N_DEV - 1)), pltpu.SemaphoreType.DMA((n2, N_DEV - 1)),
                    pltpu.SemaphoreType.DMA((n2,))]
    return pl.pallas_call(
        body, name=name, in_specs=[hbm] * (n + n2), out_specs=[hbm] * (n + n2),
        out_shape=[jax.ShapeDtypeStruct((N_DEV,) + x.shape, x.dtype) for x in xs]
        + [jax.ShapeDtypeStruct(x.shape, x.dtype) for x in with_a2a],
        scratch_shapes=scratch,
    )(*xs, *with_a2a)


AG_FIRST_COPIES = 4


def ag_first_copies(x_refs, o_refs, send_sems, recv_sems, local_sems):
    mx, my, mc = lax.axis_index("x"), lax.axis_index("y"), lax.axis_index("c")
    me = 4 * mx + 2 * my + mc
    targets = [_peer(mx, my, mc, f) for f in (1, 4, 2, 6)]
    locals_, sends, recvs = [], [], []
    for a, (x_ref, o_ref) in enumerate(zip(x_refs, o_refs, strict=True)):
        locals_.append(pltpu.make_async_copy(x_ref, o_ref.at[me], local_sems.at[a]))
        for k, (dev, row) in enumerate(targets):
            common = dict(send_sem=send_sems.at[a, k], recv_sem=recv_sems.at[a, k], device_id=dev,
                          device_id_type=pl.DeviceIdType.MESH)
            sends.append(pltpu.make_async_remote_copy(src_ref=x_ref, dst_ref=o_ref.at[me], **common))
            recvs.append(pltpu.make_async_remote_copy(src_ref=x_ref, dst_ref=o_ref.at[row], **common))
    return locals_, sends, recvs


def ag_second_level(bufs, name):
    n = len(bufs)
    chip_flips = (4, 2, 6)

    def body(*refs):
        o_refs = refs[n:2 * n]
        send_sems, recv_sems = refs[2 * n:]
        mx, my, mc = lax.axis_index("x"), lax.axis_index("y"), lax.axis_index("c")
        sib_dev, _ = _peer(mx, my, mc, 1)
        sends, recvs = [], []
        for a in range(n):
            for j, f in enumerate(chip_flips):
                common = dict(send_sem=send_sems.at[a, j], recv_sem=recv_sems.at[a, j], device_id=sib_dev,
                              device_id_type=pl.DeviceIdType.MESH)
                row, sib_row = _peer(mx, my, mc, f)[1], _peer(mx, my, mc, f ^ 1)[1]
                sends.append(pltpu.make_async_remote_copy(src_ref=o_refs[a].at[row], dst_ref=o_refs[a].at[row], **common))
                recvs.append(pltpu.make_async_remote_copy(src_ref=o_refs[a].at[row], dst_ref=o_refs[a].at[sib_row],
                                                          **common))
        for cp in sends:
            cp.start()
        for cp in recvs:
            cp.wait_recv()
        for cp in sends:
            cp.wait_send()

    return pl.pallas_call(
        body, name=name,
        in_specs=[pl.BlockSpec(memory_space=pl.ANY)] * n, out_specs=[pl.BlockSpec(memory_space=pl.ANY)] * n,
        out_shape=[jax.ShapeDtypeStruct(x.shape, x.dtype) for x in bufs],
        input_output_aliases={a: a for a in range(n)},
        scratch_shapes=[pltpu.SemaphoreType.DMA((n, len(chip_flips))), pltpu.SemaphoreType.DMA((n, len(chip_flips)))],
    )(*bufs)


PACK_COLS = 1024
PACK_ROWS = 256
PARTS_TILE_BYTES = 4 * 1024 * 1024


def adamw_reduce(parts, w, m, v, name):
    r, c = w.shape
    if (r // len(parts)) % SUBLANE:
        parts = [jnp.concatenate(parts, axis=1)]
    n_slab = len(parts)
    rs = r // n_slab
    fits = [t for t in (512, 352, 256, 128, 64, 32, 16, 8) if N_DEV * t * c * parts[0].dtype.itemsize <= PARTS_TILE_BYTES]
    tr = rs if rs < 2 * SUBLANE else _pick(rs, fits)
    n_t = rs // tr
    c1 = 1.0 - ADAM_B1 ** ADAM_STEP
    c2 = 1.0 - ADAM_B2 ** ADAM_STEP

    def body(*refs):
        p_refs = refs[:n_slab]
        w_ref, m_ref, v_ref, g_ref, d_ref, mo_ref, vo_ref = refs[n_slab:]
        slab = pl.program_id(0)
        for k, p_ref in enumerate(p_refs):
            @pl.when(slab == k)
            def _(p_ref=p_ref):
                g = p_ref[0].astype(F32)
                for dev in range(1, N_DEV):
                    g = g + p_ref[dev].astype(F32)
                mn = ADAM_B1 * m_ref[...] + (1.0 - ADAM_B1) * g
                vn = ADAM_B2 * v_ref[...] + (1.0 - ADAM_B2) * (g * g)
                g_ref[...] = g
                mo_ref[...] = mn
                vo_ref[...] = vn
                d_ref[...] = -ADAM_LR * ((mn / c1) / (jnp.sqrt(vn / c2) + ADAM_EPS) + ADAM_WD * w_ref[...])

    spec = pl.BlockSpec((tr, c), lambda s, i: (s * n_t + i, 0))
    p_specs = [pl.BlockSpec((N_DEV, tr, c), lambda s, i, k=k: (0, jnp.where(s == k, i, 0), 0)) for k in range(n_slab)]
    return pl.pallas_call(
        body, name=name, grid=(n_slab, n_t),
        in_specs=p_specs + [spec, spec, spec],
        out_specs=[spec] * 4, out_shape=[jax.ShapeDtypeStruct((r, c), F32)] * 4,
        compiler_params=_cparams(dimension_semantics=("arbitrary", "arbitrary")),
    )(*parts, w, m, v)


def pack_flat(arrs, dtype):
    parts = []
    for a in arrs:
        flat = a.reshape(-1).astype(dtype)
        k = -(-flat.shape[0] // PACK_COLS)
        parts.append(jnp.pad(flat, (0, k * PACK_COLS - flat.shape[0])).reshape(k, PACK_COLS))
    buf = jnp.concatenate(parts, axis=0)
    return jnp.pad(buf, ((0, -buf.shape[0] % PACK_ROWS), (0, 0)))


def unpack_flat(buf, shapes):
    out, r0 = [], 0
    for s in shapes:
        n = math.prod(s)
        k = -(-n // PACK_COLS)
        out.append(buf[r0:r0 + k].reshape(-1)[:n].reshape(tuple(s)))
        r0 += k
    return out


W_NAMES = ('meta_tokens', 'ffn1_norm', 'ffn1_w_gate', 'ffn1_w_up', 'ffn1_w_down', 'mix_norm', 'w_in', 'w_out',
           'lru_conv_w', 'lru_conv_b', 'lru_w_a', 'lru_b_a', 'lru_w_i', 'lru_b_i', 'lru_lambda', 'lru_norm',
           'gdn_conv_w', 'gdn_a_log', 'gdn_dt_bias', 'gdn_norm', 'ssd_conv_w', 'ssd_conv_b', 'ssd_a_log',
           'ssd_dt_bias', 'ssd_d', 'ssd_norm', 's5_a_re', 's5_a_im', 's5_log_dt', 's5_b_re', 's5_b_im', 's5_c_re',
           's5_c_im', 's5_d', 's5_w_glu', 's5_norm', 'ffn2_norm', 'ffn2_w_gate', 'ffn2_w_up', 'ffn2_w_down',
           'final_norm')
SHARD_AXIS = {'meta_tokens': 1, 'ffn1_w_gate': 2, 'ffn1_w_up': 2, 'ffn1_w_down': 1, 'w_in': 2, 'w_out': 1,
              'lru_conv_w': 2, 'gdn_conv_w': 2, 'ssd_conv_w': 2, 's5_w_glu': 1, 'ffn2_w_gate': 2, 'ffn2_w_up': 2,
              'ffn2_w_down': 1}
BIG_NAMES = ('ffn1_w_gate', 'ffn1_w_up', 'ffn1_w_down', 'w_in', 'w_out', 's5_w_glu', 'ffn2_w_gate', 'ffn2_w_up',
             'ffn2_w_down')
SHARD_NAMES = tuple(n for n in W_NAMES if n in SHARD_AXIS)
REP_NAMES = tuple(n for n in W_NAMES if n not in SHARD_AXIS)
SSD_GROUPS = 2
S5_CH = 16


def unshard(g, axis):
    if axis == 0:
        return g.reshape((-1,) + g.shape[2:])
    return jnp.concatenate([g[p] for p in range(N_DEV)], axis=axis)


def kernel(*args):
    n_w = len(W_NAMES)
    x = args[0]
    w = dict(zip(W_NAMES, args[1:1 + n_w]))
    target = args[1 + n_w]
    m_in = dict(zip(W_NAMES, args[2 + n_w:2 + 2 * n_w]))
    v_in = dict(zip(W_NAMES, args[2 + 2 * n_w:2 + 3 * n_w]))

    depth, d = w['ffn1_norm'].shape
    seq = x.shape[1]
    n_meta = w['meta_tokens'].shape[0]
    pad = CHUNK - n_meta
    tp = pad + n_meta + seq
    wg = d // 2
    xbc_w = w['ssd_conv_w'].shape[-1] * N_DEV
    gdn_hd = w['gdn_norm'].shape[-1]
    gdn_h = wg // gdn_hd
    ssd_h = w['ssd_a_log'].shape[-1]
    lru_h = w['lru_w_a'].shape[1]
    s5_g, s5_n = w['s5_a_re'].shape[1:]
    s5_q = wg // S5_LANES
    row_tile = _pick(tp, (192, 96, 64))
    norm_tile = _pick(tp, ROW_TILES)

    LAYER_NAMES = tuple(n for n in SHARD_NAMES if n != 'meta_tokens')

    def local_of(n, l):
        return w[n][l].astype(BF16 if n in BIG_NAMES else F32)

    def unshard_layer(gathered):
        return {n: unshard(g, SHARD_AXIS[n] - 1) for n, g in gathered.items()}

    first = all_gather([local_of(n, 0) for n in LAYER_NAMES] + [w['meta_tokens']], "gather_weights")
    meta_full = unshard(first[-1], SHARD_AXIS['meta_tokens'])
    full = [unshard_layer(dict(zip(LAYER_NAMES, first[:-1])))] + [None] * (depth - 1)

    segs = [('a_x', wg), ('a_gate', wg), ('b_q', wg), ('b_k', wg), ('b_v', wg), ('b_z', wg), ('c_xbc', xbc_w),
            ('c_z', wg), ('d_u', wg), ('small_b', LANE), ('small_c', LANE)]
    off, o = {}, 0
    for nme, wd_ in segs:
        assert o % wd_ == 0, (nme, o, wd_)
        off[nme] = o
        o += wd_
    o_beta = 6 * wg
    o_cz = o_beta + 2 * gdn_h
    o_xbc = o_cz + wg
    o_dt = o_xbc + xbc_w
    o_du = o_dt + ssd_h

    def pack_cols(a):
        z = lambda k: jnp.zeros(a.shape[:-1] + (k,), a.dtype)
        return jnp.concatenate([a[..., :o_beta], a[..., o_xbc:o_dt], a[..., o_cz:o_xbc], a[..., o_du:],
                                a[..., o_beta:o_cz], z(LANE - 2 * gdn_h), a[..., o_dt:o_du], z(LANE - ssd_h)], axis=-1)

    def unpack_cols(a):
        sb, sc = off['small_b'], off['small_c']
        return jnp.concatenate([a[..., :o_beta], a[..., sb:sb + 2 * gdn_h], a[..., off['c_z']:off['c_z'] + wg],
                                a[..., off['c_xbc']:off['c_xbc'] + xbc_w], a[..., sc:sc + ssd_h],
                                a[..., off['d_u']:off['d_u'] + wg]], axis=-1)

    w_in_p = [None] * depth
    w_in_p[0] = pack_cols(full[0]['w_in'])

    def col(name, width):
        return off[name] // width

    def row(a):
        return a.reshape(1, -1)

    def layer_params(l):
        gcw = full[l]['gdn_conv_w']
        lru = [full[l]['lru_conv_w'], row(w['lru_conv_b'][l]), blockdiag_expand(w['lru_w_a'][l]), row(w['lru_b_a'][l]),
               blockdiag_expand(w['lru_w_i'][l]), row(w['lru_b_i'][l]), row(w['lru_lambda'][l]), row(w['lru_norm'][l])]
        gdn = [gcw[:, :wg], gcw[:, wg:2 * wg], gcw[:, 2 * wg:], row(w['gdn_a_log'][l]), row(w['gdn_dt_bias'][l]),
               row(w['gdn_norm'][l])]
        ssd = [full[l]['ssd_conv_w'], row(w['ssd_conv_b'][l]), row(w['ssd_a_log'][l]), row(w['ssd_dt_bias'][l]),
               row(w['ssd_d'][l]), row(w['ssd_norm'][l])]
        s5 = list(s5_params_expand(*[w[n][l] for n in ('s5_a_re', 's5_a_im', 's5_log_dt', 's5_b_re', 's5_b_im',
                                                          's5_c_re', 's5_c_im', 's5_d')]))
        post = [full[l]['s5_w_glu'], row(w['s5_norm'][l])]
        return lru, gdn, ssd, s5, post

    lru_fn = functools.partial(lru_chunk, pad)
    gdn_fn = functools.partial(gdn_multi, pad, CHUNK)
    ssd_fn = functools.partial(ssd_multi, pad, 0, SSD_GROUPS, CHUNK)
    s5_fn = functools.partial(s5_chunk, pad)
    post_fn = functools.partial(s5_post, pad)
    n_state_lanes = (S5_LANES // S5_CH) * s5_n

    def mixer_specs(proj):
        lru_seqs = [(proj, wg, col('a_x', wg)), (proj, wg, col('a_gate', wg))]
        gdn_seqs = [(proj, wg, col('b_q', wg)), (proj, wg, col('b_k', wg)), (proj, wg, col('b_v', wg)),
                    (proj, wg, col('b_z', wg)), (proj, LANE, col('small_b', LANE))]
        ssd_seqs = [(proj, wg, col('c_z', wg)), (proj, xbc_w, col('c_xbc', xbc_w)), (proj, LANE, col('small_c', LANE))]
        base = col('d_u', S5_LANES)
        s5_seqs = [(proj, S5_LANES, lambda g: base + g)]
        return lru_seqs, gdn_seqs, ssd_seqs, s5_seqs

    lru_carry = [(CONV_TAIL, wg), (1, wg)]
    gdn_carry = [(CONV_TAIL, wg)] * 3 + [(wg, gdn_hd)]
    ssd_carry = [(CONV_TAIL, xbc_w), (wg, (xbc_w - wg) // (2 * SSD_GROUPS))]
    s5_carry = [(1, n_state_lanes)] * 2
    rk = dict(n_steps=tp // row_tile, rows=row_tile)
    mk = rk
    mk_ssd = dict(n_steps=tp // CHUNK, rows=CHUNK)
    out_w = (wg, wg, 0, F32)

    h = jnp.concatenate([jnp.zeros((pad, d), F32), meta_full, x[0]], axis=0)
    target_p = jnp.concatenate([jnp.zeros((pad + n_meta, d), F32), target[0]], axis=0)
    saved = []
    for l in range(depth):
        lru_p, gdn_p, ssd_p, s5_p, post_p = layer_params(l)
        h0 = h
        xn1 = rms_fwd(h0, row(w['ffn1_norm'][l]), "rms_fwd", norm_tile)
        nxt = l + 1 < depth
        take = lambda *names: [local_of(n, l + 1) for n in names] if nxt else []
        fw = full[l]
        got = {}
        names = ('ffn1_w_gate', 'w_out')
        h1, bufs = ffn_fwd(h0, xn1, fw['ffn1_w_gate'], fw['ffn1_w_up'], fw['ffn1_w_down'], "ffn_fwd", ag=take(*names))
        got.update(zip(names, bufs))
        xn2 = rms_fwd(h1, row(w['mix_norm'][l]), "rms_fwd", norm_tile)
        proj = matmul_cols(xn2, w_in_p[l], "mix_in_fwd")
        lru_s, gdn_s, ssd_s, s5_s = mixer_specs(proj)
        names = ('s5_w_glu', 'lru_conv_w', 'gdn_conv_w', 'ssd_conv_w')
        ya, lru_c, bufs = mixer_fwd(lru_fn, "lru_fwd", seqs=lru_s, params=lru_p, out=out_w, carry=lru_carry,
                                    ag=take(*names), **mk)
        got.update(zip(names, bufs))
        names = ('w_in',)
        yb, gdn_c, bufs = mixer_fwd(gdn_fn, "gdn_fwd", seqs=gdn_s, params=gdn_p, out=out_w, carry=gdn_carry,
                                    ag=take(*names), **mk)
        got.update(zip(names, bufs))
        names = ('ffn1_w_up',)
        yc, ssd_c, bufs = mixer_fwd(ssd_fn, "ssd_fwd", seqs=ssd_s, params=ssd_p, out=out_w, carry=ssd_carry,
                                    ag=take(*names), **mk_ssd)
        got.update(zip(names, bufs))
        names = ('ffn1_w_down', 'ffn2_w_gate')
        y1, s5_c, bufs = mixer_fwd(s5_fn, "s5_fwd", seqs=s5_s, params=s5_p, out=(wg, S5_LANES, lambda g: g, F32),
                                   carry=s5_carry, n_groups=s5_q, ag=take(*names), **mk)
        got.update(zip(names, bufs))
        yd, _, _ = mixer_fwd(post_fn, "s5_post_fwd", seqs=[(y1, wg, 0)], params=post_p, out=out_w, carry=[], **rk)
        h2, _ = mix_out_fwd(h1, [ya, yb, yc, yd], fw['w_out'], "mix_out_fwd", row_tile)
        xn3 = rms_fwd(h2, row(w['ffn2_norm'][l]), "rms_fwd", norm_tile)
        names = ('ffn2_w_up', 'ffn2_w_down')
        h3, bufs = ffn_fwd(h2, xn3, fw['ffn2_w_gate'], fw['ffn2_w_up'], fw['ffn2_w_down'], "ffn_fwd", ag=take(*names))
        got.update(zip(names, bufs))
        if nxt:
            full[l + 1] = unshard_layer(dict(zip(LAYER_NAMES, ag_second_level([got[n] for n in LAYER_NAMES],
                                                                                "gather_pass_on"))))
            w_in_p[l + 1] = pack_cols(full[l + 1]['w_in'])
        saved.append((h0, xn1, h1, xn2, proj, (ya, yb, yc, yd), y1, (lru_c, gdn_c, ssd_c, s5_c), h2, xn3))
        h = h3

    loss_part, dh, d_final = loss_and_grad(h, target_p, row(w['final_norm']), "loss", norm_tile, pad + n_meta)
    loss = lax.psum(loss_part, ("x", "y", "c"))

    def shards_of(a, axis):
        sh = a.shape
        return jnp.moveaxis(a.reshape(sh[:axis] + (N_DEV, sh[axis] // N_DEV) + sh[axis + 1:]), axis, 0)

    def to_send(n, g):
        return shards_of(g, SHARD_AXIS[n] - 1).astype(BF16)

    received = {n: [None] * depth for n in SHARD_NAMES if n != 'meta_tokens'}
    pending = []

    def hosted(names_layers):
        keys = [k for k in pending if (k[0], k[1]) in names_layers]
        for k in keys:
            pending.remove(k)
        return [(k[0], k[1]) for k in keys], [k[2] for k in keys]

    def store(keys, arrays):
        for (n, l), a in zip(keys, arrays, strict=True):
            received[n][l] = a

    gw = {n: [None] * depth for n in W_NAMES if n not in ('meta_tokens', 'final_norm')}
    for l in reversed(range(depth)):
        lru_p, gdn_p, ssd_p, s5_p, post_p = layer_params(l)
        h0, xn1, h1, xn2, proj, ys, y1, (lru_c, gdn_c, ssd_c, s5_c), h2, xn3 = saved[l]
        (dxn, gw['ffn2_w_gate'][l], gw['ffn2_w_up'][l], gw['ffn2_w_down'][l]), _ = ffn_bwd(
            xn3, dh, full[l]['ffn2_w_gate'], full[l]['ffn2_w_up'], full[l]['ffn2_w_down'], "ffn_bwd")
        pending += [(n, l, to_send(n, gw[n][l])) for n in ('ffn2_w_gate', 'ffn2_w_up', 'ffn2_w_down')]
        dh, dg = rms_bwd_add(h2, dxn, dh, row(w['ffn2_norm'][l]), "rms_bwd", norm_tile)
        gw['ffn2_norm'][l] = dg[0]

        dys, (d_wout,) = mixer_bwd(mix_out_delta, "mix_out_bwd", seqs=[(y, wg, 0) for y in ys],
                                   params=[full[l]['w_out']], dout=(dh, d, 0), saved=[], carry=[], **rk)
        gw['w_out'][l] = d_wout[0]
        pending.append(('w_out', l, to_send('w_out', gw['w_out'][l])))
        lru_s, gdn_s, ssd_s, s5_s = mixer_specs(proj)
        (dy1,), d_post = mixer_bwd(post_fn, "s5_post_bwd", seqs=[(y1, wg, 0)], params=post_p, dout=(dys[3], wg, 0),
                                   saved=[], carry=[], **rk)
        gw['s5_w_glu'][l], gw['s5_norm'][l] = d_post[0][0], d_post[1][0, 0]
        pending.append(('s5_w_glu', l, to_send('s5_w_glu', gw['s5_w_glu'][l])))
        keys, arrs = hosted({('ffn1_w_gate', l + 1), ('ffn1_w_up', l + 1), ('ffn1_w_down', l + 1)})
        (d_du,), d_s5, *got = mixer_bwd(s5_fn, "s5_bwd", seqs=s5_s, params=s5_p, dout=(dy1, S5_LANES, lambda g: g),
                                        saved=s5_c, carry=s5_carry, n_groups=s5_q, a2a=arrs, **mk)
        store(keys, got[0] if got else [])
        for n, g in zip(('s5_a_re', 's5_a_im', 's5_log_dt', 's5_b_re', 's5_b_im', 's5_c_re', 's5_c_im', 's5_d'),
                        s5_grads_extract(d_s5, s5_g, s5_n, S5_CH)):
            gw[n][l] = g
        keys, arrs = hosted({('w_out', l), ('s5_w_glu', l)})
        (d_cz, d_cxbc, d_sc), d_ssd, *got = mixer_bwd(ssd_fn, "ssd_bwd", seqs=ssd_s, params=ssd_p,
                                                      dout=(dys[2], wg, 0), saved=ssd_c, carry=ssd_carry, a2a=arrs, **mk_ssd)
        store(keys, got[0] if got else [])
        for n, g in zip(('ssd_conv_w', 'ssd_conv_b', 'ssd_a_log', 'ssd_dt_bias', 'ssd_d', 'ssd_norm'), d_ssd):
            gw[n][l] = g[0] if n == 'ssd_conv_w' else g[0, 0]
        keys, arrs = hosted({('ffn2_w_gate', l), ('ffn2_w_up', l), ('ffn2_w_down', l)})
        (d_bq, d_bk, d_bv, d_bz, d_sb), d_gdn, *got = mixer_bwd(gdn_fn, "gdn_bwd", seqs=gdn_s, params=gdn_p,
                                                                dout=(dys[1], wg, 0), saved=gdn_c, carry=gdn_carry,
                                                                a2a=arrs, **mk)
        store(keys, got[0] if got else [])
        gw['gdn_conv_w'][l] = jnp.concatenate([d_gdn[0][0], d_gdn[1][0], d_gdn[2][0]], axis=1)
        gw['gdn_a_log'][l], gw['gdn_dt_bias'][l], gw['gdn_norm'][l] = d_gdn[3][0, 0], d_gdn[4][0, 0], d_gdn[5][0, 0]
        (d_ax, d_ag), d_lru = mixer_bwd(lru_fn, "lru_bwd", seqs=lru_s, params=lru_p, dout=(dys[0], wg, 0),
                                        saved=lru_c, carry=lru_carry, **mk)
        gw['lru_conv_w'][l], gw['lru_conv_b'][l] = d_lru[0][0], d_lru[1][0, 0]
        gw['lru_w_a'][l], gw['lru_b_a'][l] = blockdiag_extract(d_lru[2][0], lru_h), d_lru[3][0, 0]
        gw['lru_w_i'][l], gw['lru_b_i'][l] = blockdiag_extract(d_lru[4][0], lru_h), d_lru[5][0, 0]
        gw['lru_lambda'][l], gw['lru_norm'][l] = d_lru[6][0, 0], d_lru[7][0, 0]

        dproj = jnp.concatenate([d_ax, d_ag, d_bq, d_bk, d_bv, d_bz, d_cxbc, d_cz, d_du, d_sb, d_sc], axis=1)
        dxn, d_win_p = matmul_cols_bwd(xn2, dproj, w_in_p[l], "mix_in_bwd")
        gw['w_in'][l] = unpack_cols(d_win_p)
        pending.append(('w_in', l, to_send('w_in', gw['w_in'][l])))
        dh, dg = rms_bwd_add(h1, dxn, dh, row(w['mix_norm'][l]), "rms_bwd", norm_tile)
        gw['mix_norm'][l] = dg[0]

        keys, arrs = hosted({('w_in', l)})
        (dxn, gw['ffn1_w_gate'][l], gw['ffn1_w_up'][l], gw['ffn1_w_down'][l]), got = ffn_bwd(
            xn1, dh, full[l]['ffn1_w_gate'], full[l]['ffn1_w_up'], full[l]['ffn1_w_down'], "ffn_bwd", a2a=arrs)
        store(keys, got)
        dh, dg = rms_bwd_add(h0, dxn, dh, row(w['ffn1_norm'][l]), "rms_bwd", norm_tile)
        gw['ffn1_norm'][l] = dg[0]
        pending += [(n, l, to_send(n, gw[n][l])) for n in ('ffn1_w_gate', 'ffn1_w_up', 'ffn1_w_down')]

    grad_x = dh[pad + n_meta:][None]
    grads = {n: jnp.stack(gw[n], axis=0) for n in REP_NAMES if n != 'final_norm'}
    grads['meta_tokens'] = dh[pad:pad + n_meta]
    grads['final_norm'] = d_final[0]

    for n in ('lru_conv_w', 'gdn_conv_w', 'ssd_conv_w'):
        pending += [(n, l, to_send(n, gw[n][l])) for l in range(depth)]
    last = [k[2] for k in pending] + [shards_of(grads['meta_tokens'], SHARD_AXIS['meta_tokens']).astype(BF16)]
    recv_rep, *got, recv_meta = all_gather([pack_flat([grads[n] for n in REP_NAMES], BF16)], "last_exchange",
                                           with_a2a=last)
    store([(k[0], k[1]) for k in pending], got)
    out = {}
    for n in SHARD_NAMES:
        c = w[n].shape[-1]
        recv = [recv_meta] if n == 'meta_tokens' else received[n]
        res = adamw_reduce([a.reshape(N_DEV, -1, c) for a in recv], w[n].reshape(-1, c), m_in[n].reshape(-1, c),
                           v_in[n].reshape(-1, c), "adamw_" + n)
        for kind, buf in zip(('grad', 'delta', 'new_m', 'new_v'), res):
            out[kind, n] = buf.reshape(w[n].shape)
    res = adamw_reduce([recv_rep], pack_flat([w[n] for n in REP_NAMES], F32), pack_flat([m_in[n] for n in REP_NAMES], F32),
                       pack_flat([v_in[n] for n in REP_NAMES], F32), "adamw_replicated")
    shapes = [w[n].shape for n in REP_NAMES]
    for kind, buf in zip(('grad', 'delta', 'new_m', 'new_v'), res):
        for n, a in zip(REP_NAMES, unpack_flat(buf, shapes)):
            out[kind, n] = a
    return (loss, grad_x) + tuple(out[k, n] for k in ('grad', 'delta', 'new_m', 'new_v') for n in W_NAMES)
```
